```python
import math
import jax, jax.numpy as jnp
from jax import lax
import numpy as np

D_MODEL = 4096
BATCH = 8
SEQ = 4096
DEPTH = 1

HEAD_SIZE = 64
D_A = D_MODEL // 2
D_B = D_MODEL - D_A
H_A = D_A // HEAD_SIZE
H_Q = D_B // HEAD_SIZE
GQA_RATIO = 8
H_KV = H_Q // GQA_RATIO
GQA_GROUP = H_Q // H_KV
WINDOW = 128
BLOCK = 128
RPB_BUCKETS = 32
RPB_MAX_EXACT = RPB_BUCKETS // 2
RPB_MAX_DIST = 128
DECAY_LORA = max(32, int(round(D_A ** 0.5 * 1.8 / 32)) * 32)
ICLR_LORA = max(32, int(round(D_A ** 0.5 * 1.8 / 32)) * 32)
GATE_LORA = max(32, int(round(D_A ** 0.6 * 0.8 / 32)) * 32)
D_FF = 4 * D_MODEL
ALPHA = (2.0 * DEPTH) ** 0.25
BETA = (8.0 * DEPTH) ** -0.25
LN_EPS = 1e-5
LNX_EPS = 64e-5
OFF_R = 0
OFF_K = OFF_R + D_A
OFF_V = OFF_K + D_A
OFF_W = OFF_V + D_A
OFF_A = OFF_W + DECAY_LORA
OFF_G = OFF_A + ICLR_LORA
RWKV_COLS = OFF_G + GATE_LORA
OFF_Q = RWKV_COLS
OFF_KB = OFF_Q + D_B
OFF_VB = OFF_KB + H_KV * HEAD_SIZE
N_IN = OFF_VB + H_KV * HEAD_SIZE

kernel_name = "hybrid_rwkv7_swa_sink_block"


def layer_norm(x, g, b, eps=LN_EPS):
    xf = x.astype(jnp.float32)
    mu = xf.mean(-1, keepdims=True)
    var = jnp.square(xf - mu).mean(-1, keepdims=True)
    return ((xf - mu) * lax.rsqrt(var + eps) * g + b).astype(x.dtype)


def token_shift(p, mu):
    prev = jnp.pad(p, ((0, 0), (1, 0), (0, 0)))[:, :-1]
    return p + (prev - p) * mu


def rwkv7_recurrence(r, decay, k, v, kk, a):
    b, _, h, n = r.shape

    def step(state, inp):
        r_t, w_t, k_t, v_t, kk_t, a_t = inp
        sa = jnp.einsum('bhvk,bhk->bhv', state, -kk_t)
        state = (state * w_t[:, :, None, :]
                 + sa[..., None] * (kk_t * a_t)[:, :, None, :]
                 + v_t[..., None] * k_t[:, :, None, :])
        return state, jnp.einsum('bhvk,bhk->bhv', state, r_t)

    xs = tuple(jnp.moveaxis(t, 1, 0) for t in (r, decay, k, v, kk, a))
    s0 = jnp.zeros((b, h, n, n), jnp.float32)
    _, ys = lax.scan(step, s0, xs)
    return jnp.moveaxis(ys, 0, 1)


def rwkv7_mixer(p, mu, w0, w_decay_up, a0, w_iclr_up, w_gate_up, k_k, k_a, r_k, lnx_g, lnx_b):
    bsz, seq, _ = p.shape
    f32 = jnp.float32
    p = token_shift(p, mu)
    r = p[..., OFF_R:OFF_K]
    k = p[..., OFF_K:OFF_V]
    v = p[..., OFF_V:OFF_W]
    xw = p[..., OFF_W:OFF_A]
    xa = p[..., OFF_A:OFF_G]
    xg = p[..., OFF_G:RWKV_COLS]
    w = -jax.nn.softplus(-(w0 + jnp.tanh(xw) @ w_decay_up)) - 0.5
    decay = jnp.exp(-jnp.exp(w.astype(f32)))
    a = jax.nn.sigmoid(a0 + xa @ w_iclr_up)
    g = jax.nn.sigmoid(xg) @ w_gate_up
    heads = lambda t: t.astype(f32).reshape(bsz, seq, H_A, HEAD_SIZE)
    kk = heads(k * k_k)
    kk = kk / jnp.maximum(jnp.linalg.norm(kk, axis=-1, keepdims=True), 1e-12)
    k = k * (1 + (a - 1) * k_a)
    r_h, k_h, v_h, a_h = heads(r), heads(k), heads(v), heads(a)
    y = rwkv7_recurrence(r_h, heads(decay), k_h, v_h, kk, a_h)
    mu_y = y.mean(-1, keepdims=True)
    var_y = jnp.square(y - mu_y).mean(-1, keepdims=True)
    y = ((y - mu_y) * lax.rsqrt(var_y + LNX_EPS)).reshape(bsz, seq, D_A) * lnx_g + lnx_b
    bonus = jnp.sum(r_h * k_h * r_k, axis=-1, keepdims=True) * v_h
    y = (y + bonus.reshape(bsz, seq, D_A)) * g
    return y.astype(p.dtype)


def t5_causal_bucket(dist):
    n = jnp.maximum(dist, 0)
    nf = jnp.maximum(n, 1).astype(jnp.float32)
    large = RPB_MAX_EXACT + (jnp.log(nf / RPB_MAX_EXACT)
                             / math.log(RPB_MAX_DIST / RPB_MAX_EXACT)
                             * (RPB_BUCKETS - RPB_MAX_EXACT)).astype(jnp.int32)
    large = jnp.minimum(large, RPB_BUCKETS - 1)
    return jnp.where(n < RPB_MAX_EXACT, n, large)


def swa_sink_attention(q, k, v, rpb_table, sinks):
    bsz, seq = q.shape[:2]
    nb = seq // BLOCK
    f32 = jnp.float32
    qb = q.reshape(bsz, nb, BLOCK, H_KV, GQA_GROUP, HEAD_SIZE)

    def band(t):
        tp = jnp.pad(t, ((0, 0), (BLOCK, 0), (0, 0), (0, 0)))
        prev = tp[:, :seq].reshape(bsz, nb, BLOCK, H_KV, HEAD_SIZE)
        cur = t.reshape(bsz, nb, BLOCK, H_KV, HEAD_SIZE)
        return jnp.concatenate([prev, cur], axis=2)

    kb, vb = band(k), band(v)
    qi = jnp.arange(BLOCK)[:, None]
    kj = jnp.arange(2 * BLOCK)[None, :]
    dist = qi + BLOCK - kj
    bias = rpb_table[t5_causal_bucket(dist)]
    bias = jnp.transpose(bias, (2, 0, 1)).reshape(H_KV, GQA_GROUP, BLOCK, 2 * BLOCK).astype(f32)
    blk = jnp.arange(nb)[:, None, None]
    valid = (dist >= 0) & (dist < WINDOW) & ((blk > 0) | (kj >= BLOCK))
    s = jnp.einsum('bnqhgd,bnkhd->bnhgqk', qb, kb).astype(f32) * (HEAD_SIZE ** -0.5) + bias
    s = jnp.where(valid[None, :, None, None], s, -1e30)
    sink = sinks.astype(f32).reshape(H_KV, GQA_GROUP)[None, None, :, :, None, None]
    m = jnp.maximum(s.max(-1, keepdims=True), sink)
    e = jnp.exp(s - m)
    probs = e / (e.sum(-1, keepdims=True) + jnp.exp(sink - m))
    o = jnp.einsum('bnhgqk,bnkhd->bnqhgd', probs.astype(v.dtype), vb)
    return o.reshape(bsz, seq, H_Q * HEAD_SIZE)


def _fwd_setup_inputs(seed: int = 0) -> dict:
    key = jax.random.key(seed)
    ks = jax.random.split(key, 28)
    nrm = lambda k, shape, s: s * jax.random.normal(k, shape, jnp.float32)
    L = DEPTH
    col_scale = np.ones((N_IN,), np.float32)
    col_scale[OFF_V:OFF_W] = BETA
    col_scale[OFF_VB:N_IN] = BETA
    return {
        "x": nrm(ks[0], (BATCH, SEQ, D_MODEL), 1.0),
        "c": nrm(ks[1], (BATCH, D_MODEL), 1.0),
        "ln_emb_g": 1.0 + nrm(ks[2], (D_MODEL,), 0.02),
        "ln_emb_b": nrm(ks[3], (D_MODEL,), 0.02),
        "rpb_table": nrm(ks[4], (RPB_BUCKETS, H_Q), 0.5),
        "w_mod": nrm(ks[5], (L, D_MODEL, 6 * D_MODEL), 0.2 * D_MODEL ** -0.5),
        "b_mod": nrm(ks[6], (L, 6 * D_MODEL), 0.01),
        "w_in": nrm(ks[7], (L, D_MODEL, N_IN), D_MODEL ** -0.5) * jnp.asarray(col_scale),
        "mu_shift": jax.random.uniform(ks[8], (L, RWKV_COLS), jnp.float32),
        "w0": jax.random.uniform(ks[9], (L, D_A), jnp.float32, -6.0, -1.0),
        "w_decay_up": nrm(ks[10], (L, DECAY_LORA, D_A), 0.5 * DECAY_LORA ** -0.5),
        "a0": nrm(ks[11], (L, D_A), 0.1),
        "w_iclr_up": nrm(ks[12], (L, ICLR_LORA, D_A), 0.5 * ICLR_LORA ** -0.5),
        "w_gate_up": nrm(ks[13], (L, GATE_LORA, D_A), GATE_LORA ** -0.5),
        "k_k": 0.85 + nrm(ks[14], (L, D_A), 0.05),
        "k_a": 1.0 + nrm(ks[15], (L, D_A), 0.05),
        "r_k": nrm(ks[16], (L, H_A, HEAD_SIZE), 0.1),
        "lnx_g": 1.0 + nrm(ks[17], (L, D_A), 0.02),
        "lnx_b": nrm(ks[18], (L, D_A), 0.02),
        "attn_sinks": nrm(ks[19], (L, H_Q), 1.0),
        "w_out": nrm(ks[20], (L, D_MODEL, D_MODEL), BETA * D_MODEL ** -0.5),
        "ln1_g": 1.0 + nrm(ks[21], (L, D_MODEL), 0.02),
        "ln1_b": nrm(ks[22], (L, D_MODEL), 0.02),
        "w_up": nrm(ks[23], (L, D_MODEL, D_FF), D_MODEL ** -0.5),
        "w_down": nrm(ks[24], (L, D_FF, D_MODEL), BETA * D_FF ** -0.5),
        "ln2_g": 1.0 + nrm(ks[25], (L, D_MODEL), 0.02),
        "ln2_b": nrm(ks[26], (L, D_MODEL), 0.02),
    }


def _fwd_reference(x, c, ln_emb_g, ln_emb_b, rpb_table, w_mod, b_mod, w_in, mu_shift, w0,
              w_decay_up, a0, w_iclr_up, w_gate_up, k_k, k_a, r_k, lnx_g, lnx_b,
              attn_sinks, w_out, ln1_g, ln1_b, w_up, w_down, ln2_g, ln2_b):
    bsz, seq, _ = x.shape
    cond = jax.nn.silu(c)
    x = layer_norm(x, ln_emb_g, ln_emb_b)
    for l in range(DEPTH):
        mod = (cond @ w_mod[l] + b_mod[l]).reshape(bsz, 6, D_MODEL)[:, :, None, :]
        sh1, sc1, g1, sh2, sc2, g2 = (mod[:, i] for i in range(6))
        u = x * (1 + sc1) + sh1
        p = u @ w_in[l]
        y_a = rwkv7_mixer(p[..., :RWKV_COLS], mu_shift[l], w0[l], w_decay_up[l], a0[l],
                          w_iclr_up[l], w_gate_up[l], k_k[l], k_a[l], r_k[l],
                          lnx_g[l], lnx_b[l])
        q = p[..., OFF_Q:OFF_KB].reshape(bsz, seq, H_Q, HEAD_SIZE)
        kb = p[..., OFF_KB:OFF_VB].reshape(bsz, seq, H_KV, HEAD_SIZE)
        vb = p[..., OFF_VB:N_IN].reshape(bsz, seq, H_KV, HEAD_SIZE)
        y_b = swa_sink_attention(q, kb, vb, rpb_table, attn_sinks[l])
        mix = jnp.concatenate([y_a, y_b], axis=-1) @ w_out[l]
        x = layer_norm(ALPHA * x + (1 + g1) * mix, ln1_g[l], ln1_b[l])
        u = x * (1 + sc2) + sh2
        h = jnp.square(jax.nn.relu(u @ w_up[l])) @ w_down[l]
        x = layer_norm(ALPHA * x + (1 + g2) * h, ln2_g[l], ln2_b[l])
    return x


import jax as _jax
import jax.numpy as _jnp

TWIN_FORMAT = 'train_step'
FWD_PARAMS = ['x', 'c', 'ln_emb_g', 'ln_emb_b', 'rpb_table', 'w_mod', 'b_mod', 'w_in', 'mu_shift', 'w0', 'w_decay_up', 'a0', 'w_iclr_up', 'w_gate_up', 'k_k', 'k_a', 'r_k', 'lnx_g', 'lnx_b', 'attn_sinks', 'w_out', 'ln1_g', 'ln1_b', 'w_up', 'w_down', 'ln2_g', 'ln2_b']
TWIN_WEIGHTS = ['ln_emb_g', 'ln_emb_b', 'rpb_table', 'w_mod', 'b_mod', 'w_in', 'mu_shift', 'w0', 'w_decay_up', 'a0', 'w_iclr_up', 'w_gate_up', 'k_k', 'k_a', 'r_k', 'lnx_g', 'lnx_b', 'attn_sinks', 'w_out', 'ln1_g', 'ln1_b', 'w_up', 'w_down', 'ln2_g', 'ln2_b']
TWIN_DIFF_INPUT = 'x'
TWIN_INPUTS = ['x', 'c', 'ln_emb_g', 'ln_emb_b', 'rpb_table', 'w_mod', 'b_mod', 'w_in', 'mu_shift', 'w0', 'w_decay_up', 'a0', 'w_iclr_up', 'w_gate_up', 'k_k', 'k_a', 'r_k', 'lnx_g', 'lnx_b', 'attn_sinks', 'w_out', 'ln1_g', 'ln1_b', 'w_up', 'w_down', 'ln2_g', 'ln2_b', 'loss_target', 'm_ln_emb_g', 'm_ln_emb_b', 'm_rpb_table', 'm_w_mod', 'm_b_mod', 'm_w_in', 'm_mu_shift', 'm_w0', 'm_w_decay_up', 'm_a0', 'm_w_iclr_up', 'm_w_gate_up', 'm_k_k', 'm_k_a', 'm_r_k', 'm_lnx_g', 'm_lnx_b', 'm_attn_sinks', 'm_w_out', 'm_ln1_g', 'm_ln1_b', 'm_w_up', 'm_w_down', 'm_ln2_g', 'm_ln2_b', 'v_ln_emb_g', 'v_ln_emb_b', 'v_rpb_table', 'v_w_mod', 'v_b_mod', 'v_w_in', 'v_mu_shift', 'v_w0', 'v_w_decay_up', 'v_a0', 'v_w_iclr_up', 'v_w_gate_up', 'v_k_k', 'v_k_a', 'v_r_k', 'v_lnx_g', 'v_lnx_b', 'v_attn_sinks', 'v_w_out', 'v_ln1_g', 'v_ln1_b', 'v_w_up', 'v_w_down', 'v_ln2_g', 'v_ln2_b']
TWIN_OUTPUTS = ['loss', 'grad_x', 'grad_ln_emb_g', 'grad_ln_emb_b', 'grad_rpb_table', 'grad_w_mod', 'grad_b_mod', 'grad_w_in', 'grad_mu_shift', 'grad_w0', 'grad_w_decay_up', 'grad_a0', 'grad_w_iclr_up', 'grad_w_gate_up', 'grad_k_k', 'grad_k_a', 'grad_r_k', 'grad_lnx_g', 'grad_lnx_b', 'grad_attn_sinks', 'grad_w_out', 'grad_ln1_g', 'grad_ln1_b', 'grad_w_up', 'grad_w_down', 'grad_ln2_g', 'grad_ln2_b', 'delta_ln_emb_g', 'delta_ln_emb_b', 'delta_rpb_table', 'delta_w_mod', 'delta_b_mod', 'delta_w_in', 'delta_mu_shift', 'delta_w0', 'delta_w_decay_up', 'delta_a0', 'delta_w_iclr_up', 'delta_w_gate_up', 'delta_k_k', 'delta_k_a', 'delta_r_k', 'delta_lnx_g', 'delta_lnx_b', 'delta_attn_sinks', 'delta_w_out', 'delta_ln1_g', 'delta_ln1_b', 'delta_w_up', 'delta_w_down', 'delta_ln2_g', 'delta_ln2_b', 'new_m_ln_emb_g', 'new_m_ln_emb_b', 'new_m_rpb_table', 'new_m_w_mod', 'new_m_b_mod', 'new_m_w_in', 'new_m_mu_shift', 'new_m_w0', 'new_m_w_decay_up', 'new_m_a0', 'new_m_w_iclr_up', 'new_m_w_gate_up', 'new_m_k_k', 'new_m_k_a', 'new_m_r_k', 'new_m_lnx_g', 'new_m_lnx_b', 'new_m_attn_sinks', 'new_m_w_out', 'new_m_ln1_g', 'new_m_ln1_b', 'new_m_w_up', 'new_m_w_down', 'new_m_ln2_g', 'new_m_ln2_b', 'new_v_ln_emb_g', 'new_v_ln_emb_b', 'new_v_rpb_table', 'new_v_w_mod', 'new_v_b_mod', 'new_v_w_in', 'new_v_mu_shift', 'new_v_w0', 'new_v_w_decay_up', 'new_v_a0', 'new_v_w_iclr_up', 'new_v_w_gate_up', 'new_v_k_k', 'new_v_k_a', 'new_v_r_k', 'new_v_lnx_g', 'new_v_lnx_b', 'new_v_attn_sinks', 'new_v_w_out', 'new_v_ln1_g', 'new_v_ln1_b', 'new_v_w_up', 'new_v_w_down', 'new_v_ln2_g', 'new_v_ln2_b']
TWIN_LEAF_KINDS = {'loss': 'loss', 'grad_x': 'grad_x', 'grad_ln_emb_g': 'grad_w', 'grad_ln_emb_b': 'grad_w', 'grad_rpb_table': 'grad_w', 'grad_w_mod': 'grad_w', 'grad_b_mod': 'grad_w', 'grad_w_in': 'grad_w', 'grad_mu_shift': 'grad_w', 'grad_w0': 'grad_w', 'grad_w_decay_up': 'grad_w', 'grad_a0': 'grad_w', 'grad_w_iclr_up': 'grad_w', 'grad_w_gate_up': 'grad_w', 'grad_k_k': 'grad_w', 'grad_k_a': 'grad_w', 'grad_r_k': 'grad_w', 'grad_lnx_g': 'grad_w', 'grad_lnx_b': 'grad_w', 'grad_attn_sinks': 'grad_w', 'grad_w_out': 'grad_w', 'grad_ln1_g': 'grad_w', 'grad_ln1_b': 'grad_w', 'grad_w_up': 'grad_w', 'grad_w_down': 'grad_w', 'grad_ln2_g': 'grad_w', 'grad_ln2_b': 'grad_w', 'delta_ln_emb_g': 'delta_w', 'delta_ln_emb_b': 'delta_w', 'delta_rpb_table': 'delta_w', 'delta_w_mod': 'delta_w', 'delta_b_mod': 'delta_w', 'delta_w_in': 'delta_w', 'delta_mu_shift': 'delta_w', 'delta_w0': 'delta_w', 'delta_w_decay_up': 'delta_w', 'delta_a0': 'delta_w', 'delta_w_iclr_up': 'delta_w', 'delta_w_gate_up': 'delta_w', 'delta_k_k': 'delta_w', 'delta_k_a': 'delta_w', 'delta_r_k': 'delta_w', 'delta_lnx_g': 'delta_w', 'delta_lnx_b': 'delta_w', 'delta_attn_sinks': 'delta_w', 'delta_w_out': 'delta_w', 'delta_ln1_g': 'delta_w', 'delta_ln1_b': 'delta_w', 'delta_w_up': 'delta_w', 'delta_w_down': 'delta_w', 'delta_ln2_g': 'delta_w', 'delta_ln2_b': 'delta_w', 'new_m_ln_emb_g': 'new_m', 'new_m_ln_emb_b': 'new_m', 'new_m_rpb_table': 'new_m', 'new_m_w_mod': 'new_m', 'new_m_b_mod': 'new_m', 'new_m_w_in': 'new_m', 'new_m_mu_shift': 'new_m', 'new_m_w0': 'new_m', 'new_m_w_decay_up': 'new_m', 'new_m_a0': 'new_m', 'new_m_w_iclr_up': 'new_m', 'new_m_w_gate_up': 'new_m', 'new_m_k_k': 'new_m', 'new_m_k_a': 'new_m', 'new_m_r_k': 'new_m', 'new_m_lnx_g': 'new_m', 'new_m_lnx_b': 'new_m', 'new_m_attn_sinks': 'new_m', 'new_m_w_out': 'new_m', 'new_m_ln1_g': 'new_m', 'new_m_ln1_b': 'new_m', 'new_m_w_up': 'new_m', 'new_m_w_down': 'new_m', 'new_m_ln2_g': 'new_m', 'new_m_ln2_b': 'new_m', 'new_v_ln_emb_g': 'new_v', 'new_v_ln_emb_b': 'new_v', 'new_v_rpb_table': 'new_v', 'new_v_w_mod': 'new_v', 'new_v_b_mod': 'new_v', 'new_v_w_in': 'new_v', 'new_v_mu_shift': 'new_v', 'new_v_w0': 'new_v', 'new_v_w_decay_up': 'new_v', 'new_v_a0': 'new_v', 'new_v_w_iclr_up': 'new_v', 'new_v_w_gate_up': 'new_v', 'new_v_k_k': 'new_v', 'new_v_k_a': 'new_v', 'new_v_r_k': 'new_v', 'new_v_lnx_g': 'new_v', 'new_v_lnx_b': 'new_v', 'new_v_attn_sinks': 'new_v', 'new_v_w_out': 'new_v', 'new_v_ln1_g': 'new_v', 'new_v_ln1_b': 'new_v', 'new_v_w_up': 'new_v', 'new_v_w_down': 'new_v', 'new_v_ln2_g': 'new_v', 'new_v_ln2_b': 'new_v'}


def _forward(args):
    return _fwd_reference(*[args[k] for k in FWD_PARAMS])


def _output_shape():
    out = _jax.eval_shape(lambda: _forward(_fwd_setup_inputs(0)))
    return out.shape, out.dtype

N_MICROBATCH = 1
ADAM_LR = 0.001
ADAM_B1 = 0.9
ADAM_B2 = 0.999
ADAM_EPS = 1e-08
ADAM_WD = 0.01
ADAM_STEP = 10
PER_EXAMPLE_BATCH_AXIS = {'x': 0, 'c': 0, 'loss_target': 0}
SHARED_INPUTS = []
_WEIGHT_DTYPES = {'ln_emb_g': _jnp.float32, 'ln_emb_b': _jnp.float32, 'rpb_table': _jnp.float32, 'w_mod': _jnp.float32, 'b_mod': _jnp.float32, 'w_in': _jnp.float32, 'mu_shift': _jnp.float32, 'w0': _jnp.float32, 'w_decay_up': _jnp.float32, 'a0': _jnp.float32, 'w_iclr_up': _jnp.float32, 'w_gate_up': _jnp.float32, 'k_k': _jnp.float32, 'k_a': _jnp.float32, 'r_k': _jnp.float32, 'lnx_g': _jnp.float32, 'lnx_b': _jnp.float32, 'attn_sinks': _jnp.float32, 'w_out': _jnp.float32, 'ln1_g': _jnp.float32, 'ln1_b': _jnp.float32, 'w_up': _jnp.float32, 'w_down': _jnp.float32, 'ln2_g': _jnp.float32, 'ln2_b': _jnp.float32}
MOMENT_SCALE = {'ln_emb_g': 1.630884e-01, 'ln_emb_b': 1.081142e-01, 'rpb_table': 3.361346e-03, 'w_mod': 1.603692e-02, 'b_mod': 3.516085e-02, 'w_in': 1.457253e-02, 'mu_shift': 2.158697e-02, 'w0': 5.359731e-03, 'w_decay_up': 6.945932e-04, 'a0': 5.617018e-03, 'w_iclr_up': 5.163314e-03, 'w_gate_up': 1.262308e-02, 'k_k': 2.010216e-02, 'k_a': 1.433762e-02, 'r_k': 1.784781e-02, 'lnx_g': 1.278192e-02, 'lnx_b': 2.496847e-02, 'attn_sinks': 2.346779e-03, 'w_out': 1.534370e-02, 'ln1_g': 1.710032e-01, 'ln1_b': 1.070192e-01, 'w_up': 1.376204e-02, 'w_down': 5.306229e-02, 'ln2_g': 8.016224e+00, 'ln2_b': 1.747757e+00}


def _to_microbatches(a, axis):
    t = _jnp.moveaxis(a, axis, 0)
    t = t.reshape((N_MICROBATCH, t.shape[0] // N_MICROBATCH) + t.shape[1:])
    return _jnp.moveaxis(t, 1, axis + 1)


def setup_inputs(seed: int = 0) -> dict:
    inp = _fwd_setup_inputs(seed)
    key = _jax.random.fold_in(_jax.random.key(seed), 7919)
    shape, _ = _output_shape()
    out = dict(inp)
    out["loss_target"] = _jax.random.normal(_jax.random.fold_in(key, 0), shape, _jnp.float32)
    for i, name in enumerate(TWIN_WEIGHTS):
        w = inp[name].astype(_jnp.float32)
        if MOMENT_SCALE is None:
            s = _jnp.sqrt(_jnp.mean(_jnp.square(w)) + 1e-30)
        else:
            s = MOMENT_SCALE[name]
        km, kv = _jax.random.split(_jax.random.fold_in(key, i + 1))
        out[name] = w
        out["m_" + name] = s * _jax.random.normal(km, w.shape, _jnp.float32)
        out["v_" + name] = (s * s) * _jax.random.uniform(kv, w.shape, _jnp.float32, 0.5, 1.5)
    if N_MICROBATCH > 1:
        for name, axis in PER_EXAMPLE_BATCH_AXIS.items():
            out[name] = _to_microbatches(out[name], axis)
    return {'x': out['x'], 'c': out['c'], 'ln_emb_g': out['ln_emb_g'], 'ln_emb_b': out['ln_emb_b'], 'rpb_table': out['rpb_table'], 'w_mod': out['w_mod'], 'b_mod': out['b_mod'], 'w_in': out['w_in'], 'mu_shift': out['mu_shift'], 'w0': out['w0'], 'w_decay_up': out['w_decay_up'], 'a0': out['a0'], 'w_iclr_up': out['w_iclr_up'], 'w_gate_up': out['w_gate_up'], 'k_k': out['k_k'], 'k_a': out['k_a'], 'r_k': out['r_k'], 'lnx_g': out['lnx_g'], 'lnx_b': out['lnx_b'], 'attn_sinks': out['attn_sinks'], 'w_out': out['w_out'], 'ln1_g': out['ln1_g'], 'ln1_b': out['ln1_b'], 'w_up': out['w_up'], 'w_down': out['w_down'], 'ln2_g': out['ln2_g'], 'ln2_b': out['ln2_b'], 'loss_target': out['loss_target'], 'm_ln_emb_g': out['m_ln_emb_g'], 'm_ln_emb_b': out['m_ln_emb_b'], 'm_rpb_table': out['m_rpb_table'], 'm_w_mod': out['m_w_mod'], 'm_b_mod': out['m_b_mod'], 'm_w_in': out['m_w_in'], 'm_mu_shift': out['m_mu_shift'], 'm_w0': out['m_w0'], 'm_w_decay_up': out['m_w_decay_up'], 'm_a0': out['m_a0'], 'm_w_iclr_up': out['m_w_iclr_up'], 'm_w_gate_up': out['m_w_gate_up'], 'm_k_k': out['m_k_k'], 'm_k_a': out['m_k_a'], 'm_r_k': out['m_r_k'], 'm_lnx_g': out['m_lnx_g'], 'm_lnx_b': out['m_lnx_b'], 'm_attn_sinks': out['m_attn_sinks'], 'm_w_out': out['m_w_out'], 'm_ln1_g': out['m_ln1_g'], 'm_ln1_b': out['m_ln1_b'], 'm_w_up': out['m_w_up'], 'm_w_down': out['m_w_down'], 'm_ln2_g': out['m_ln2_g'], 'm_ln2_b': out['m_ln2_b'], 'v_ln_emb_g': out['v_ln_emb_g'], 'v_ln_emb_b': out['v_ln_emb_b'], 'v_rpb_table': out['v_rpb_table'], 'v_w_mod': out['v_w_mod'], 'v_b_mod': out['v_b_mod'], 'v_w_in': out['v_w_in'], 'v_mu_shift': out['v_mu_shift'], 'v_w0': out['v_w0'], 'v_w_decay_up': out['v_w_decay_up'], 'v_a0': out['v_a0'], 'v_w_iclr_up': out['v_w_iclr_up'], 'v_w_gate_up': out['v_w_gate_up'], 'v_k_k': out['v_k_k'], 'v_k_a': out['v_k_a'], 'v_r_k': out['v_r_k'], 'v_lnx_g': out['v_lnx_g'], 'v_lnx_b': out['v_lnx_b'], 'v_attn_sinks': out['v_attn_sinks'], 'v_w_out': out['v_w_out'], 'v_ln1_g': out['v_ln1_g'], 'v_ln1_b': out['v_ln1_b'], 'v_w_up': out['v_w_up'], 'v_w_down': out['v_w_down'], 'v_ln2_g': out['v_ln2_g'], 'v_ln2_b': out['v_ln2_b']}


def _loss(weights, diff, rest, loss_target):
    with _jax.named_scope("forward"):
        args = {**rest, TWIN_DIFF_INPUT: diff, **{k: w.astype(_WEIGHT_DTYPES[k]) for k, w in weights.items()}}
        y = _forward(args)
    with _jax.named_scope("loss_head"):
        err = _jnp.square(y.astype(_jnp.float32) - loss_target)
        return 0.5 * _jnp.sum(_jnp.mean(err, axis=-1)) if err.ndim else 0.5 * err


def _adamw(w, g, m, v):
    m = ADAM_B1 * m + (1.0 - ADAM_B1) * g
    v = ADAM_B2 * v + (1.0 - ADAM_B2) * _jnp.square(g)
    m_hat = m / (1.0 - ADAM_B1 ** ADAM_STEP)
    v_hat = v / (1.0 - ADAM_B2 ** ADAM_STEP)
    delta = -ADAM_LR * (m_hat / (_jnp.sqrt(v_hat) + ADAM_EPS) + ADAM_WD * w)
    return delta, m, v


def reference(x, c, ln_emb_g, ln_emb_b, rpb_table, w_mod, b_mod, w_in, mu_shift, w0, w_decay_up, a0, w_iclr_up, w_gate_up, k_k, k_a, r_k, lnx_g, lnx_b, attn_sinks, w_out, ln1_g, ln1_b, w_up, w_down, ln2_g, ln2_b, loss_target, m_ln_emb_g, m_ln_emb_b, m_rpb_table, m_w_mod, m_b_mod, m_w_in, m_mu_shift, m_w0, m_w_decay_up, m_a0, m_w_iclr_up, m_w_gate_up, m_k_k, m_k_a, m_r_k, m_lnx_g, m_lnx_b, m_attn_sinks, m_w_out, m_ln1_g, m_ln1_b, m_w_up, m_w_down, m_ln2_g, m_ln2_b, v_ln_emb_g, v_ln_emb_b, v_rpb_table, v_w_mod, v_b_mod, v_w_in, v_mu_shift, v_w0, v_w_decay_up, v_a0, v_w_iclr_up, v_w_gate_up, v_k_k, v_k_a, v_r_k, v_lnx_g, v_lnx_b, v_attn_sinks, v_w_out, v_ln1_g, v_ln1_b, v_w_up, v_w_down, v_ln2_g, v_ln2_b):
    given = dict(x=x, c=c, ln_emb_g=ln_emb_g, ln_emb_b=ln_emb_b, rpb_table=rpb_table, w_mod=w_mod, b_mod=b_mod, w_in=w_in, mu_shift=mu_shift, w0=w0, w_decay_up=w_decay_up, a0=a0, w_iclr_up=w_iclr_up, w_gate_up=w_gate_up, k_k=k_k, k_a=k_a, r_k=r_k, lnx_g=lnx_g, lnx_b=lnx_b, attn_sinks=attn_sinks, w_out=w_out, ln1_g=ln1_g, ln1_b=ln1_b, w_up=w_up, w_down=w_down, ln2_g=ln2_g, ln2_b=ln2_b, loss_target=loss_target, m_ln_emb_g=m_ln_emb_g, m_ln_emb_b=m_ln_emb_b, m_rpb_table=m_rpb_table, m_w_mod=m_w_mod, m_b_mod=m_b_mod, m_w_in=m_w_in, m_mu_shift=m_mu_shift, m_w0=m_w0, m_w_decay_up=m_w_decay_up, m_a0=m_a0, m_w_iclr_up=m_w_iclr_up, m_w_gate_up=m_w_gate_up, m_k_k=m_k_k, m_k_a=m_k_a, m_r_k=m_r_k, m_lnx_g=m_lnx_g, m_lnx_b=m_lnx_b, m_attn_sinks=m_attn_sinks, m_w_out=m_w_out, m_ln1_g=m_ln1_g, m_ln1_b=m_ln1_b, m_w_up=m_w_up, m_w_down=m_w_down, m_ln2_g=m_ln2_g, m_ln2_b=m_ln2_b, v_ln_emb_g=v_ln_emb_g, v_ln_emb_b=v_ln_emb_b, v_rpb_table=v_rpb_table, v_w_mod=v_w_mod, v_b_mod=v_b_mod, v_w_in=v_w_in, v_mu_shift=v_mu_shift, v_w0=v_w0, v_w_decay_up=v_w_decay_up, v_a0=v_a0, v_w_iclr_up=v_w_iclr_up, v_w_gate_up=v_w_gate_up, v_k_k=v_k_k, v_k_a=v_k_a, v_r_k=v_r_k, v_lnx_g=v_lnx_g, v_lnx_b=v_lnx_b, v_attn_sinks=v_attn_sinks, v_w_out=v_w_out, v_ln1_g=v_ln1_g, v_ln1_b=v_ln1_b, v_w_up=v_w_up, v_w_down=v_w_down, v_ln2_g=v_ln2_g, v_ln2_b=v_ln2_b)
    weights = {n: given[n] for n in TWIN_WEIGHTS}
    shared = {n: given[n] for n in SHARED_INPUTS}
    per_example = {n: given[n] for n in ['x', 'c']}
    grad_fn = _jax.value_and_grad(_loss, argnums=(0, 1))

    def one_microbatch(ex, loss_target):
        ex = dict(ex)
        diff = ex.pop(TWIN_DIFF_INPUT)
        return grad_fn(weights, diff, {**shared, **ex}, loss_target)

    if N_MICROBATCH == 1:
        loss, (grad_w, grad_x) = one_microbatch(per_example, given["loss_target"])
    else:
        def body(carry, xs):
            loss_sum, grad_sum = carry
            l_k, (gw_k, gx_k) = one_microbatch(xs[0], xs[1])
            with _jax.named_scope("update"):
                return (loss_sum + l_k, _jax.tree.map(_jnp.add, grad_sum, gw_k)), gx_k

        init = (_jnp.zeros((), _jnp.float32), _jax.tree.map(_jnp.zeros_like, weights))
        (loss, grad_w), grad_x = _jax.lax.scan(body, init, (per_example, given["loss_target"]))
    with _jax.named_scope("update"):
        delta_w, new_m, new_v = {}, {}, {}
        for n in TWIN_WEIGHTS:
            delta_w[n], new_m[n], new_v[n] = _adamw(weights[n], grad_w[n], given["m_" + n], given["v_" + n])
    return (loss, grad_x, *[grad_w[n] for n in TWIN_WEIGHTS], *[delta_w[n] for n in TWIN_WEIGHTS],
            *[new_m[n] for n in TWIN_WEIGHTS], *[new_v[n] for n in TWIN_WEIGHTS])
```

```python
import functools
import math

import jax
import jax.numpy as jnp
from jax import lax
from jax.experimental import pallas as pl
from jax.experimental.pallas import tpu as pltpu

F32 = jnp.float32
BF16 = jnp.bfloat16
HI = lax.Precision.HIGHEST
MESH_AXES = ("x", "y", "c")
N_DEV = 8

HEAD = 64
GQA_RATIO = 8
ATT_BLOCK = 128
RPB_MAX_DIST = 128
LN_EPS = 1e-5
LNX_EPS = 64e-5
DEPTH = 1
ALPHA = (2.0 * DEPTH) ** 0.25
CHUNK = 64
REC_HEADS = 4

ADAM_LR = 0.001
ADAM_B1 = 0.9
ADAM_B2 = 0.999
ADAM_EPS = 1e-08
ADAM_WD = 0.01
ADAM_STEP = 10

VMEM_LIMIT = 56 * 1024 * 1024


def _cparams(sem=None):
    return pltpu.CompilerParams(dimension_semantics=sem, vmem_limit_bytes=VMEM_LIMIT)


def _tile(dim, cap):
    best = None
    t = 128
    while t <= min(dim, cap):
        if dim % t == 0:
            best = t
        t += 128
    return best or dim


def _rtile(dim, cap):
    best = None
    t = 8
    while t <= min(dim, cap):
        if dim % t == 0:
            best = t
        t += 8
    return best or dim


def _raw_dot(a, b, ca, cb, prec):
    dims = (((ca,), (cb,)), ((), ()))
    if prec == "bf16":
        return lax.dot_general(a.astype(BF16), b.astype(BF16), dims, preferred_element_type=F32)
    return lax.dot_general(a, b, dims, precision=HI, preferred_element_type=F32)


@functools.partial(jax.custom_vjp, nondiff_argnums=(2, 3, 4))
def _bf16_dot(a, b, ca, cb, prec):
    return _raw_dot(a, b, ca, cb, prec)


def _bf16_dot_fwd(a, b, ca, cb, prec):
    return _raw_dot(a, b, ca, cb, prec), (a, b)


def _bf16_dot_bwd(ca, cb, prec, res, g):
    a, b = res
    if ca == 1:
        da = _bf16_dot(g, b, 1, 1 - cb, prec)
    else:
        da = _bf16_dot(b, g, 1 - cb, 1, prec)
    if cb == 0:
        db = _bf16_dot(a, g, 1 - ca, 0, prec)
    else:
        db = _bf16_dot(g, a, 0, 1 - ca, prec)
    return da, db


_bf16_dot.defvjp(_bf16_dot_fwd, _bf16_dot_bwd)


def _dot(a, b, ca, cb, prec):
    return _raw_dot(a, b, ca, cb, prec) if prec == "hi" else _bf16_dot(a, b, ca, cb, prec)


def _sigmoid(z):
    return 1.0 / (1.0 + jnp.exp(-z))


def _softplus(z):
    return jnp.maximum(z, 0.0) + jnp.log(1.0 + jnp.exp(-jnp.abs(z)))


def _matmul(name, a, b, mode, out_dtypes, epilogue=None, extras=(), caps=(1024, 1024, 512)):
    if mode == "nn":
        (m, k), n = a.shape, b.shape[1]
    elif mode == "nt":
        (m, k), n = a.shape, b.shape[0]
    else:
        (k, m), n = a.shape, b.shape[1]
    tm, tn, tk = _tile(m, caps[0]), _tile(n, caps[1]), _tile(k, caps[2])
    nk = k // tk
    ne, no = len(extras), len(out_dtypes)
    ca, cb = {"nn": (1, 0), "nt": (1, 1), "tn": (0, 0)}[mode]

    def body(a_ref, b_ref, *rest):
        extra_refs, out_refs, acc = rest[:ne], rest[ne:ne + no], rest[-1]
        kk = pl.program_id(2)

        @pl.when(kk == 0)
        def _():
            acc[...] = jnp.zeros_like(acc)

        acc[...] += _raw_dot(a_ref[...], b_ref[...], ca, cb, "bf16")

        @pl.when(kk == nk - 1)
        def _():
            res = epilogue(acc[...], *[e[...] for e in extra_refs]) if epilogue else (acc[...],)
            for o, v in zip(out_refs, res):
                o[...] = v.astype(o.dtype)

    a_spec = (pl.BlockSpec((tk, tm), lambda i, j, kk: (kk, i)) if mode == "tn"
              else pl.BlockSpec((tm, tk), lambda i, j, kk: (i, kk)))
    b_spec = (pl.BlockSpec((tn, tk), lambda i, j, kk: (j, kk)) if mode == "nt"
              else pl.BlockSpec((tk, tn), lambda i, j, kk: (kk, j)))
    mn_spec = pl.BlockSpec((tm, tn), lambda i, j, kk: (i, j))
    outs = pl.pallas_call(
        body, name=name, grid=(m // tm, n // tn, nk),
        in_specs=[a_spec, b_spec] + [mn_spec] * ne,
        out_specs=[mn_spec] * no,
        out_shape=[jax.ShapeDtypeStruct((m, n), dt) for dt in out_dtypes],
        scratch_shapes=[pltpu.VMEM((tm, tn), F32)],
        compiler_params=_cparams(("parallel", "parallel", "arbitrary")),
    )(a, b, *extras)
    return outs


def _rowwise(name, fn, rows, bcasts, out_rows, out_accs, tr, halos=()):
    t = rows[0].shape[0]
    nb = t // tr
    n_in = len(rows) + len(halos) + len(bcasts)
    n_ro = len(out_rows)

    def body(*refs):
        ins = [r[...] for r in refs[:n_in]]
        o_refs = refs[n_in:]
        i = pl.program_id(0)
        routs, aouts = fn(i, nb, *ins)
        for ref, v in zip(o_refs[:n_ro], routs):
            ref[...] = v.astype(ref.dtype)
        for ref, v in zip(o_refs[n_ro:], aouts):
            @pl.when(i == 0)
            def _(ref=ref):
                ref[...] = jnp.zeros_like(ref)
            ref[...] += v.reshape(ref.shape)

    in_specs = [pl.BlockSpec((tr, r.shape[1]), lambda i: (i, 0)) for r in rows]
    for arr, which in halos:
        if which == "prev":
            in_specs.append(pl.BlockSpec((8, arr.shape[1]), lambda i: (jnp.maximum(i * (tr // 8) - 1, 0), 0)))
        else:
            in_specs.append(pl.BlockSpec((8, arr.shape[1]),
                                         lambda i: (jnp.minimum((i + 1) * (tr // 8), t // 8 - 1), 0)))
    for bc in bcasts:
        in_specs.append(pl.BlockSpec(bc.shape, lambda i, nd=bc.ndim: (0,) * nd))
    out_specs = [pl.BlockSpec((tr, c), lambda i: (i, 0)) for c, _ in out_rows]
    out_specs += [pl.BlockSpec(s, lambda i, nd=len(s): (0,) * nd) for s in out_accs]
    out_shape = [jax.ShapeDtypeStruct((t, c), dt) for c, dt in out_rows]
    out_shape += [jax.ShapeDtypeStruct(s, F32) for s in out_accs]
    return pl.pallas_call(
        body, name=name, grid=(nb,), in_specs=in_specs, out_specs=out_specs, out_shape=out_shape,
        compiler_params=_cparams(("arbitrary",)),
    )(*rows, *[h[0] for h in halos], *bcasts)


def _shift_prev(x, halo, i):
    rolled = pltpu.roll(x, 1, 0)
    first = jnp.where(i == 0, 0.0, halo[7:8, :])
    row = lax.broadcasted_iota(jnp.int32, x.shape, 0)
    return jnp.where(row == 0, first, rolled)


def _shift_next(x, halo, i, nb):
    rolled = pltpu.roll(x, x.shape[0] - 1, 0)
    last = jnp.where(i == nb - 1, 0.0, halo[0:1, :])
    row = lax.broadcasted_iota(jnp.int32, x.shape, 0)
    return jnp.where(row == x.shape[0] - 1, last, rolled)


def _ln(x, g, b, eps=LN_EPS):
    mu = jnp.mean(x, axis=-1, keepdims=True)
    xc = x - mu
    var = jnp.mean(xc * xc, axis=-1, keepdims=True)
    return xc * lax.rsqrt(var + eps) * g + b


def _embed_math(x, g, b, sc, sh):
    x0 = _ln(x, g, b)
    return x0, x0 * (1.0 + sc) + sh


def _post_math(xin, y, gate, g, b, sc, sh):
    x1 = _ln(ALPHA * xin + (1.0 + gate) * y, g, b)
    return x1, x1 * (1.0 + sc) + sh


def _loss_math(xin, h, tgt, gate, g, b):
    x2 = _ln(ALPHA * xin + (1.0 + gate) * h, g, b)
    err = x2 - tgt
    return 0.5 * jnp.sum(jnp.mean(err * err, axis=-1))


def _rwkv_pre_math(r, k, v, xw, xa, xg, w0, wd, a0, wa, wg, k_k, k_a, seg, seg_t):
    wpre = -_softplus(-(w0 + _dot(jnp.tanh(xw), wd, 1, 0, "hi"))) - 0.5
    lw = -jnp.exp(wpre)
    a = _sigmoid(a0 + _dot(xa, wa, 1, 0, "hi"))
    g = _dot(_sigmoid(xg), wg, 1, 0, "hi")
    kk = k * k_k
    norm = jnp.sqrt(_dot(kk * kk, seg, 1, 0, "hi"))
    kkn = kk * _dot(1.0 / jnp.maximum(norm, 1e-12), seg_t, 1, 0, "hi")
    k2 = k * (1.0 + (a - 1.0) * k_a)
    return r, lw, k2, v, kkn, kkn * a, g


def _rwkv_post_math(y, r, k2, v, g, lnx_g, lnx_b, r_k, seg, seg_t):
    inv = 1.0 / HEAD
    mu = _dot(_dot(y, seg, 1, 0, "hi"), seg_t, 1, 0, "hi") * inv
    yc = y - mu
    var = _dot(_dot(yc * yc, seg, 1, 0, "hi"), seg_t, 1, 0, "hi") * inv
    yn = yc * lax.rsqrt(var + LNX_EPS) * lnx_g + lnx_b
    bonus = _dot(_dot(r * k2 * r_k, seg, 1, 0, "hi"), seg_t, 1, 0, "hi") * v
    return (yn + bonus) * g


def _chunk_math(s0, r, lw, k, v, kk, b):
    c = r.shape[0]
    ti = lax.broadcasted_iota(jnp.int32, (c, c), 0)
    tj = lax.broadcasted_iota(jnp.int32, (c, c), 1)
    incl = (ti >= tj).astype(F32)
    strict = (ti > tj).astype(F32)
    eye = (ti == tj).astype(F32)
    cl = _dot(incl, lw, 1, 0, "hi")
    ge = jnp.exp(cl)
    gi = jnp.exp(-cl)
    khat, bhat = k * gi, b * gi
    ahat = -kk * jnp.exp(cl - lw)
    rhat = r * ge
    m_ak = _dot(ahat, khat, 1, 1, "hi") * strict
    m_ab = _dot(ahat, bhat, 1, 1, "hi") * strict
    m_rk = _dot(rhat, khat, 1, 1, "hi") * incl
    m_rb = _dot(rhat, bhat, 1, 1, "hi") * incl
    inv = eye + m_ab
    pw = m_ab
    for _ in range(int(math.log2(c)) - 1):
        pw = _dot(pw, pw, 1, 0, "hi")
        inv = inv + _dot(inv, pw, 1, 0, "hi")
    u = _dot(inv, _dot(ahat, s0, 1, 1, "hi") + _dot(m_ak, v, 1, 0, "hi"), 1, 0, "hi")
    y = _dot(rhat, s0, 1, 1, "hi") + _dot(m_rk, v, 1, 0, "hi") + _dot(m_rb, u, 1, 0, "hi")
    s1 = (s0 + _dot(v, khat, 0, 0, "hi") + _dot(u, bhat, 0, 0, "hi")) * ge[c - 1:c, :]
    return y, s1


def _attn_math(q, kp, kc, vp, vc, bias, sinks, first, dot=_dot):
    hq = q.shape[1] // HEAD
    hkv = kc.shape[1] // HEAD
    group = hq // hkv
    qi = lax.broadcasted_iota(jnp.int32, (ATT_BLOCK, 2 * ATT_BLOCK), 0)
    kj = lax.broadcasted_iota(jnp.int32, (ATT_BLOCK, 2 * ATT_BLOCK), 1)
    dist = qi + ATT_BLOCK - kj
    valid = (dist >= 0) & (dist < ATT_BLOCK) & (jnp.logical_not(first) | (kj >= ATT_BLOCK))
    outs = []
    for j in range(hkv):
        kband = jnp.concatenate([kp[:, j * HEAD:(j + 1) * HEAD], kc[:, j * HEAD:(j + 1) * HEAD]], axis=0)
        vband = jnp.concatenate([vp[:, j * HEAD:(j + 1) * HEAD], vc[:, j * HEAD:(j + 1) * HEAD]], axis=0)
        for gq in range(group):
            h = j * group + gq
            s = dot(q[:, h * HEAD:(h + 1) * HEAD], kband, 1, 1, "bf16") * (HEAD ** -0.5) + bias[h]
            s = jnp.where(valid, s, -1e30)
            sink = sinks[0:1, h:h + 1]
            m = jnp.maximum(jnp.max(s, axis=-1, keepdims=True), sink)
            e = jnp.exp(s - m)
            p = e / (jnp.sum(e, axis=-1, keepdims=True) + jnp.exp(sink - m))
            outs.append(dot(p, vband, 1, 0, "bf16"))
    return jnp.concatenate(outs, axis=1)


def _rec_specs(t, da, gh, reverse):
    nc = t // CHUNK
    if reverse:
        return pl.BlockSpec((CHUNK, gh * HEAD), lambda hg, c: (nc - 1 - c, hg))
    return pl.BlockSpec((CHUNK, gh * HEAD), lambda hg, c: (c, hg))


def _rec_fwd(r, lw, k, v, kk, b):
    t, da = r.shape
    h = da // HEAD
    gh = min(REC_HEADS, h)
    nc = t // CHUNK

    def body(r_ref, lw_ref, k_ref, v_ref, kk_ref, b_ref, y_ref, s0_ref, state):
        @pl.when(pl.program_id(1) == 0)
        def _():
            state[...] = jnp.zeros_like(state)

        for i in range(gh):
            sl = slice(i * HEAD, (i + 1) * HEAD)
            s0 = state[i]
            s0_ref[0, i] = s0
            y, s1 = _chunk_math(s0, r_ref[:, sl], lw_ref[:, sl], k_ref[:, sl], v_ref[:, sl],
                                kk_ref[:, sl], b_ref[:, sl])
            y_ref[:, sl] = y
            state[i] = s1

    spec = _rec_specs(t, da, gh, False)
    return pl.pallas_call(
        body, name="rwkv_recurrence_fwd", grid=(h // gh, nc),
        in_specs=[spec] * 6,
        out_specs=[spec, pl.BlockSpec((1, gh, HEAD, HEAD), lambda hg, c: (c, hg, 0, 0))],
        out_shape=[jax.ShapeDtypeStruct((t, da), F32), jax.ShapeDtypeStruct((nc, h, HEAD, HEAD), F32)],
        scratch_shapes=[pltpu.VMEM((gh, HEAD, HEAD), F32)],
        compiler_params=_cparams(("parallel", "arbitrary")),
    )(r, lw, k, v, kk, b)


def _rec_bwd(r, lw, k, v, kk, b, s0s, dy):
    t, da = r.shape
    h = da // HEAD
    gh = min(REC_HEADS, h)
    nc = t // CHUNK

    def body(r_ref, lw_ref, k_ref, v_ref, kk_ref, b_ref, dy_ref, s0_ref,
             dr_ref, dlw_ref, dk_ref, dv_ref, dkk_ref, db_ref, dstate):
        @pl.when(pl.program_id(1) == 0)
        def _():
            dstate[...] = jnp.zeros_like(dstate)

        for i in range(gh):
            sl = slice(i * HEAD, (i + 1) * HEAD)
            _, vjp = jax.vjp(_chunk_math, s0_ref[0, i], r_ref[:, sl], lw_ref[:, sl], k_ref[:, sl],
                             v_ref[:, sl], kk_ref[:, sl], b_ref[:, sl])
            ds0, dr, dlw, dk, dv, dkk, db = vjp((dy_ref[:, sl], dstate[i]))
            dstate[i] = ds0
            dr_ref[:, sl] = dr
            dlw_ref[:, sl] = dlw
            dk_ref[:, sl] = dk
            dv_ref[:, sl] = dv
            dkk_ref[:, sl] = dkk
            db_ref[:, sl] = db

    spec = _rec_specs(t, da, gh, True)
    return pl.pallas_call(
        body, name="rwkv_recurrence_bwd", grid=(h // gh, nc),
        in_specs=[spec] * 7 + [pl.BlockSpec((1, gh, HEAD, HEAD), lambda hg, c: (nc - 1 - c, hg, 0, 0))],
        out_specs=[spec] * 6,
        out_shape=[jax.ShapeDtypeStruct((t, da), F32)] * 6,
        scratch_shapes=[pltpu.VMEM((gh, HEAD, HEAD), F32)],
        compiler_params=_cparams(("parallel", "arbitrary")),
    )(r, lw, k, v, kk, b, dy, s0s)


def _attn_specs(t, hq_w, hkv_w):
    nb = t // ATT_BLOCK
    cur = lambda w: pl.BlockSpec((ATT_BLOCK, w), lambda n: (n, 0))
    prev = lambda w: pl.BlockSpec((ATT_BLOCK, w), lambda n: (jnp.maximum(n - 1, 0), 0))
    return nb, cur, prev


def _attn_fwd(q, kb, vb, bias, sinks):
    t, qw = q.shape
    kw = kb.shape[1]
    nb, cur, prev = _attn_specs(t, qw, kw)

    def body(q_ref, kp_ref, kc_ref, vp_ref, vc_ref, bias_ref, sink_ref, o_ref):
        first = pl.program_id(0) == 0
        o = _attn_math(q_ref[...], kp_ref[...], kc_ref[...], vp_ref[...], vc_ref[...],
                       bias_ref[...], sink_ref[...], first, dot=_raw_dot)
        o_ref[...] = o.astype(o_ref.dtype)

    full = lambda a: pl.BlockSpec(a.shape, lambda n, nd=a.ndim: (0,) * nd)
    return pl.pallas_call(
        body, name="swa_attention_fwd", grid=(nb,),
        in_specs=[cur(qw), prev(kw), cur(kw), prev(kw), cur(kw), full(bias), full(sinks)],
        out_specs=cur(qw), out_shape=jax.ShapeDtypeStruct((t, qw), BF16),
        compiler_params=_cparams(("parallel",)),
    )(q, kb, kb, vb, vb, bias, sinks)


def _attn_bwd(q, kb, vb, bias, sinks, do, col_block):
    t, qw = q.shape
    kw = kb.shape[1]
    nb, cur, prev = _attn_specs(t, qw, kw)

    def body(q_ref, kp_ref, kc_ref, vp_ref, vc_ref, bias_ref, sink_ref, do_ref,
             dq_ref, dkp_ref, dkc_ref, dvp_ref, dvc_ref, dbias_ref, dsink_ref):
        n = pl.program_id(0)
        first = n == 0
        fn = functools.partial(_attn_math, first=first)
        _, vjp = jax.vjp(fn, q_ref[...], kp_ref[...], kc_ref[...], vp_ref[...], vc_ref[...],
                         bias_ref[...], sink_ref[...])
        dq, dkp, dkc, dvp, dvc, dbias, dsink = vjp(do_ref[...].astype(F32))
        dq_ref[...] = dq.astype(dq_ref.dtype)
        dkp_ref[...] = dkp
        dkc_ref[...] = dkc
        dvp_ref[...] = dvp
        dvc_ref[...] = dvc

        @pl.when(first)
        def _():
            dbias_ref[...] = jnp.zeros_like(dbias_ref)
            dsink_ref[...] = jnp.zeros_like(dsink_ref)

        dbias_ref[...] += dbias
        dsink_ref[...] += dsink

    full = lambda a: pl.BlockSpec(a.shape, lambda n, nd=a.ndim: (0,) * nd)
    kshape = jax.ShapeDtypeStruct((t, kw), F32)
    return pl.pallas_call(
        body, name="swa_attention_bwd", grid=(nb,),
        in_specs=[cur(qw), prev(kw), cur(kw), prev(kw), cur(kw), full(bias), full(sinks),
                  pl.BlockSpec((ATT_BLOCK, qw), lambda n: (n, col_block))],
        out_specs=[cur(qw), cur(kw), cur(kw), cur(kw), cur(kw), full(bias), full(sinks)],
        out_shape=[jax.ShapeDtypeStruct((t, qw), BF16), kshape, kshape, kshape, kshape,
                   jax.ShapeDtypeStruct(bias.shape, F32), jax.ShapeDtypeStruct(sinks.shape, F32)],
        compiler_params=_cparams(("arbitrary",)),
    )(q, kb, kb, vb, vb, bias, sinks, do)


def _bucket_onehot():
    qi = jnp.arange(ATT_BLOCK)[:, None]
    kj = jnp.arange(2 * ATT_BLOCK)[None, :]
    n = jnp.maximum(qi + ATT_BLOCK - kj, 0)
    buckets, max_exact = 32, 16
    nf = jnp.maximum(n, 1).astype(F32)
    large = max_exact + (jnp.log(nf / max_exact) / math.log(RPB_MAX_DIST / max_exact)
                         * (buckets - max_exact)).astype(jnp.int32)
    bucket = jnp.where(n < max_exact, n, jnp.minimum(large, buckets - 1)).reshape(-1)
    return (bucket[None, :] == jnp.arange(buckets)[:, None]).astype(F32)


def _small_dot(name, a, b, ca, cb):
    m = a.shape[1 - ca]
    n = b.shape[1 - cb]

    def body(a_ref, b_ref, o_ref):
        o_ref[...] = _raw_dot(a_ref[...], b_ref[...], ca, cb, "hi")

    return pl.pallas_call(body, name=name, out_shape=jax.ShapeDtypeStruct((m, n), F32),
                          compiler_params=_cparams())(a, b)


def _mod_fwd(c_all, w_mod):
    d, n = w_mod.shape
    tn = _tile(n, 512)

    def body(c_ref, w_ref, o_ref, cond_ref):
        cv = c_ref[...]
        cond = cv * _sigmoid(cv)
        cond_ref[...] = cond
        o_ref[...] = _raw_dot(cond, w_ref[...], 1, 0, "hi")

    return pl.pallas_call(
        body, name="adaln_mod_fwd", grid=(n // tn,),
        in_specs=[pl.BlockSpec(c_all.shape, lambda j: (0, 0)), pl.BlockSpec((d, tn), lambda j: (0, j))],
        out_specs=[pl.BlockSpec((c_all.shape[0], tn), lambda j: (0, j)),
                   pl.BlockSpec(c_all.shape, lambda j: (0, 0))],
        out_shape=[jax.ShapeDtypeStruct((c_all.shape[0], n), F32), jax.ShapeDtypeStruct(c_all.shape, F32)],
        compiler_params=_cparams(("arbitrary",)),
    )(c_all, w_mod)


def _adam_math(w, g, m, v):
    m = ADAM_B1 * m + (1.0 - ADAM_B1) * g
    v = ADAM_B2 * v + (1.0 - ADAM_B2) * (g * g)
    m_hat = m / (1.0 - ADAM_B1 ** ADAM_STEP)
    v_hat = v / (1.0 - ADAM_B2 ** ADAM_STEP)
    delta = -ADAM_LR * (m_hat / (jnp.sqrt(v_hat) + ADAM_EPS) + ADAM_WD * w)
    return delta, m, v


def _adamw(name, w, m, v, gparts):
    r, c = w.shape
    p = gparts.shape[0]
    tr = _rtile(r, max(8, (1 << 18) // max(c, 1) // 8 * 8))

    def body(w_ref, m_ref, v_ref, g_ref, go_ref, d_ref, mo_ref, vo_ref):
        g = g_ref[0].astype(F32)
        for s in range(1, p):
            g = g + g_ref[s].astype(F32)
        delta, mn, vn = _adam_math(w_ref[...], g, m_ref[...], v_ref[...])
        go_ref[...] = g
        d_ref[...] = delta
        mo_ref[...] = mn
        vo_ref[...] = vn

    spec = pl.BlockSpec((tr, c), lambda i: (i, 0))
    return pl.pallas_call(
        body, name=name, grid=(r // tr,),
        in_specs=[spec, spec, spec, pl.BlockSpec((p, tr, c), lambda i: (0, i, 0))],
        out_specs=[spec] * 4, out_shape=[jax.ShapeDtypeStruct((r, c), F32)] * 4,
        compiler_params=_cparams(("parallel",)),
    )(w, m, v, gparts)


def _adamw_outer(name, w, m, v, cond_t, dmod):
    d, n = w.shape
    tr, tn = _rtile(d, 512), _tile(n, 1024)

    def body(w_ref, m_ref, v_ref, c_ref, dm_ref, go_ref, d_ref, mo_ref, vo_ref):
        g = _raw_dot(c_ref[...], dm_ref[...], 1, 0, "hi")
        delta, mn, vn = _adam_math(w_ref[...], g, m_ref[...], v_ref[...])
        go_ref[...] = g
        d_ref[...] = delta
        mo_ref[...] = mn
        vo_ref[...] = vn

    spec = pl.BlockSpec((tr, tn), lambda i, j: (i, j))
    return pl.pallas_call(
        body, name=name, grid=(d // tr, n // tn),
        in_specs=[spec, spec, spec, pl.BlockSpec((tr, cond_t.shape[1]), lambda i, j: (i, 0)),
                  pl.BlockSpec((dmod.shape[0], tn), lambda i, j: (0, j))],
        out_specs=[spec] * 4, out_shape=[jax.ShapeDtypeStruct((d, n), F32)] * 4,
        compiler_params=_cparams(("parallel", "parallel")),
    )(w, m, v, cond_t, dmod)


def _exchange(name, arrays, scatter):
    n = len(arrays)
    if scatter:
        out_shape = [jax.ShapeDtypeStruct(a.shape, a.dtype) for a in arrays]
    else:
        out_shape = [jax.ShapeDtypeStruct((N_DEV,) + a.shape, a.dtype) for a in arrays]

    def body(*refs):
        ins, outs = refs[:n], refs[n:2 * n]
        send_sems, recv_sems, local_sems = refs[2 * n:]
        x, y, c = lax.axis_index("x"), lax.axis_index("y"), lax.axis_index("c")
        me = 4 * x + 2 * y + c

        def peer(p):
            px, py, pc = x ^ ((p >> 2) & 1), y ^ ((p >> 1) & 1), c ^ (p & 1)
            return (px, py, pc), 4 * px + 2 * py + pc

        def remote(a, p):
            dev, idx = peer(p)
            src = ins[a].at[idx] if scatter else ins[a]
            return pltpu.make_async_remote_copy(
                src_ref=src, dst_ref=outs[a].at[me], send_sem=send_sems.at[a, p - 1],
                recv_sem=recv_sems.at[a, p - 1], device_id=dev, device_id_type=pl.DeviceIdType.MESH)

        def arrival(a, p):
            dev, idx = peer(p)
            src = ins[a].at[idx] if scatter else ins[a]
            return pltpu.make_async_remote_copy(
                src_ref=src, dst_ref=outs[a].at[idx], send_sem=send_sems.at[a, p - 1],
                recv_sem=recv_sems.at[a, p - 1], device_id=dev, device_id_type=pl.DeviceIdType.MESH)

        locals_ = []
        for a in range(n):
            src = ins[a].at[me] if scatter else ins[a]
            cp = pltpu.make_async_copy(src, outs[a].at[me], local_sems.at[a])
            cp.start()
            locals_.append(cp)
        sends = [remote(a, p) for a in range(n) for p in range(1, N_DEV)]
        for cp in sends:
            cp.start()
        for a in range(n):
            for p in range(1, N_DEV):
                arrival(a, p).wait_recv()
        for cp in sends:
            cp.wait_send()
        for cp in locals_:
            cp.wait()

    any_spec = pl.BlockSpec(memory_space=pl.ANY)
    return pl.pallas_call(
        body, name=name, in_specs=[any_spec] * n, out_specs=[any_spec] * n, out_shape=out_shape,
        scratch_shapes=[pltpu.SemaphoreType.DMA((n, N_DEV - 1)), pltpu.SemaphoreType.DMA((n, N_DEV - 1)),
                        pltpu.SemaphoreType.DMA((n,))],
    )(*arrays)


def _cols_to_shards(a):
    r, c = a.shape
    return a.reshape(r, N_DEV, c // N_DEV).transpose(1, 0, 2)


def _shards_to_cols(a):
    d, r, n = a.shape
    return a.transpose(1, 0, 2).reshape(r, d * n)


def kernel(x, c, ln_emb_g, ln_emb_b, rpb_table, w_mod, b_mod, w_in, mu_shift, w0, w_decay_up, a0, w_iclr_up, w_gate_up, k_k, k_a, r_k, lnx_g, lnx_b, attn_sinks, w_out, ln1_g, ln1_b, w_up, w_down, ln2_g, ln2_b, loss_target, m_ln_emb_g, m_ln_emb_b, m_rpb_table, m_w_mod, m_b_mod, m_w_in, m_mu_shift, m_w0, m_w_decay_up, m_a0, m_w_iclr_up, m_w_gate_up, m_k_k, m_k_a, m_r_k, m_lnx_g, m_lnx_b, m_attn_sinks, m_w_out, m_ln1_g, m_ln1_b, m_w_up, m_w_down, m_ln2_g, m_ln2_b, v_ln_emb_g, v_ln_emb_b, v_rpb_table, v_w_mod, v_b_mod, v_w_in, v_mu_shift, v_w0, v_w_decay_up, v_a0, v_w_iclr_up, v_w_gate_up, v_k_k, v_k_a, v_r_k, v_lnx_g, v_lnx_b, v_attn_sinks, v_w_out, v_ln1_g, v_ln1_b, v_w_up, v_w_down, v_ln2_g, v_ln2_b):
    names = ["ln_emb_g", "ln_emb_b", "rpb_table", "w_mod", "b_mod", "w_in", "mu_shift", "w0", "w_decay_up",
             "a0", "w_iclr_up", "w_gate_up", "k_k", "k_a", "r_k", "lnx_g", "lnx_b", "attn_sinks", "w_out",
             "ln1_g", "ln1_b", "w_up", "w_down", "ln2_g", "ln2_b"]
    env = dict(locals())
    weights = {nm: env[nm] for nm in names}
    mom_m = {nm: env["m_" + nm] for nm in names}
    mom_v = {nm: env["v_" + nm] for nm in names}

    t, d = x.shape[1], x.shape[2]
    da = d // 2
    h_a = da // HEAD
    hq = (d - da) // HEAD
    hkv = hq // GQA_RATIO
    l_w, l_a, l_g = w_decay_up.shape[1], w_iclr_up.shape[1], w_gate_up.shape[1]
    o_w, o_a, o_g = 3 * da, 3 * da + l_w, 3 * da + l_w + l_a
    n_rwkv = o_g + l_g
    o_kb, o_vb = n_rwkv + hq * HEAD, n_rwkv + hq * HEAD + hkv * HEAD
    me = 4 * lax.axis_index("x") + 2 * lax.axis_index("y") + lax.axis_index("c")

    x2d, tgt = x[0], loss_target[0]
    row = lambda a: a.reshape(1, -1)
    seg = (jnp.arange(da)[:, None] // HEAD == jnp.arange(h_a)[None, :]).astype(F32)
    seg_t = seg.T

    (c_all,) = _exchange("gather_cond", [c], False)
    c_all = c_all.reshape(N_DEV, d)
    mod_rows, cond_all = _mod_fwd(c_all, w_mod[0])
    gathered = _exchange("gather_weights", [
        mod_rows, w_in[0].astype(BF16), w_out[0].astype(BF16), w_up[0].astype(BF16), w_down[0].astype(BF16),
        w_decay_up[0], w_iclr_up[0], w_gate_up[0]], False)
    mod_all, win_g, wout_g, wup_g, wdown_g, wd_g, wa_g, wg_g = gathered
    mod = lax.dynamic_index_in_dim(mod_all, me, axis=1, keepdims=False).reshape(1, -1) + b_mod
    sh1, sc1, g1, sh2, sc2, g2 = [mod[:, i * d:(i + 1) * d] for i in range(6)]
    w_in_f = _shards_to_cols(win_g)
    w_out_f = wout_g.reshape(d, d)
    w_up_f = _shards_to_cols(wup_g)
    w_down_f = wdown_g.reshape(-1, d)
    wd_f, wa_f, wg_f = _shards_to_cols(wd_g), _shards_to_cols(wa_g), _shards_to_cols(wg_g)

    tr = _rtile(t, 256)
    lng, lnb = row(ln_emb_g), row(ln_emb_b)

    def embed_fn(i, nb, xb, g, b, sc, sh):
        return _embed_math(xb, g, b, sc, sh), ()
    x0, u1 = _rowwise("embed_ln_mod", embed_fn, [x2d], [lng, lnb, sc1, sh1], [(d, F32), (d, BF16)], [], tr)

    (p,) = _matmul("in_proj", u1, w_in_f, "nn", [F32])
    p_rkv, p_w, p_a, p_g = p[:, :o_w], p[:, o_w:o_a], p[:, o_a:o_g], p[:, o_g:n_rwkv]
    q, kb, vb = p[:, n_rwkv:o_kb], p[:, o_kb:o_vb], p[:, o_vb:]
    mu_rkv, mu_w, mu_a, mu_g = (mu_shift[:, :o_w], mu_shift[:, o_w:o_a], mu_shift[:, o_a:o_g],
                                mu_shift[:, o_g:n_rwkv])
    pre_params = [w0, wd_f, a0, wa_f, wg_f, k_k, k_a, seg, seg_t]
    tr_pre = _rtile(t, 128)

    def shifted(i, blocks, halos, mus):
        return [xb + (_shift_prev(xb, hb, i) - xb) * mb for xb, hb, mb in zip(blocks, halos, mus)]

    def split3(a):
        return a[:, :da], a[:, da:2 * da], a[:, 2 * da:]

    def pre_fn(i, nb, b_rkv, b_w, b_a, b_g, h_rkv, h_w, h_a_, h_g, m_rkv, m_w, m_a, m_g, *params):
        s_rkv, s_w, s_a, s_g = shifted(i, [b_rkv, b_w, b_a, b_g], [h_rkv, h_w, h_a_, h_g],
                                       [m_rkv, m_w, m_a, m_g])
        return _rwkv_pre_math(*split3(s_rkv), s_w, s_a, s_g, *params), ()

    pre_rows = [p_rkv, p_w, p_a, p_g]
    pre_halos = [(a, "prev") for a in pre_rows]
    r_, lw_, k2_, v_, kk_, b_, gate_ = _rowwise(
        "rwkv_pre", pre_fn, pre_rows, [mu_rkv, mu_w, mu_a, mu_g] + pre_params,
        [(da, F32)] * 7, [], tr_pre, halos=pre_halos)

    y_rec, s0s = _rec_fwd(r_, lw_, k2_, v_, kk_, b_)

    rk_flat = r_k.reshape(1, da)
    post_params = [lnx_g, lnx_b, rk_flat, seg, seg_t]

    def post_fn(i, nb, yb, rb, kb_, vb_, gb, *params):
        return (_rwkv_post_math(yb, rb, kb_, vb_, gb, *params),), ()
    (ya,) = _rowwise("rwkv_post", post_fn, [y_rec, r_, k2_, v_, gate_], post_params, [(da, BF16)], [], tr_pre)

    onehot = _bucket_onehot()
    bias = _small_dot("rpb_gather", rpb_table, onehot, 0, 0)
    bias = bias.reshape(hq, ATT_BLOCK, 2 * ATT_BLOCK)
    yb = _attn_fwd(q, kb, vb, bias, attn_sinks)

    mix_in = jnp.concatenate([ya, yb], axis=1)
    (mix,) = _matmul("out_proj", mix_in, w_out_f, "nn", [F32])

    def post1_fn(i, nb, xin, yv, gate, g, b, sc, sh):
        return _post_math(xin, yv, gate, g, b, sc, sh), ()
    x1, u2 = _rowwise("ln1_mod", post1_fn, [x0, mix], [g1, ln1_g, ln1_b, sc2, sh2],
                      [(d, F32), (d, BF16)], [], tr)

    def relu2(acc):
        rl = jnp.maximum(acc, 0.0)
        return acc, rl * rl
    hpre, hact = _matmul("mlp_up", u2, w_up_f, "nn", [F32, BF16], epilogue=relu2)
    (hmlp,) = _matmul("mlp_down", hact, w_down_f, "nn", [F32])

    def loss_fn(i, nb, xin, hv, tg, gate, g, b):
        val, vjp = jax.vjp(_loss_math, xin, hv, tg, gate, g, b)
        dxin, dh, _, dgate, dg, db = vjp(jnp.ones((), F32))
        return (dxin, dh), (val, dgate, dg, db)
    dx1, dh, loss_acc, dg2, dln2g, dln2b = _rowwise(
        "ln2_loss", loss_fn, [x1, hmlp, tgt], [g2, ln2_g, ln2_b], [(d, F32), (d, BF16)],
        [(1, 1), (1, d), (1, d), (1, d)], _rtile(t, 128))

    def drelu2(acc, hp):
        return (acc * 2.0 * jnp.maximum(hp, 0.0),)
    (dhpre,) = _matmul("mlp_down_dgrad", dh, w_down_f, "nt", [BF16], epilogue=drelu2, extras=[hpre])
    (gw_down,) = _matmul("mlp_down_wgrad", hact, dh, "tn", [BF16])
    (du2,) = _matmul("mlp_up_dgrad", dhpre, w_up_f, "nt", [F32])
    (gw_up,) = _matmul("mlp_up_wgrad", u2, dhpre, "tn", [BF16])

    def post1_bwd(i, nb, xin, yv, dx1v, du2v, gate, g, b, sc, sh):
        _, vjp = jax.vjp(_post_math, xin, yv, gate, g, b, sc, sh)
        dxin, dy, dgate, dg, db, dsc, dsh = vjp((dx1v, du2v))
        return (dxin, dy), (dgate, dg, db, dsc, dsh)
    dx0, dmix, dg1, dln1g, dln1b, dsc2, dsh2 = _rowwise(
        "ln1_mod_bwd", post1_bwd, [x0, mix, dx1, du2], [g1, ln1_g, ln1_b, sc2, sh2],
        [(d, F32), (d, BF16)], [(1, d)] * 5, _rtile(t, 128))

    (dmix_in,) = _matmul("out_proj_dgrad", dmix, w_out_f, "nt", [F32])
    (gw_out,) = _matmul("out_proj_wgrad", mix_in, dmix, "tn", [BF16])
    dya = dmix_in[:, :da]

    def post_bwd(i, nb, yb_, rb, kb_, vb_, gb, dyab, *params):
        _, vjp = jax.vjp(_rwkv_post_math, yb_, rb, kb_, vb_, gb, *params)
        dy, dr, dk, dv, dg, dlg, dlb, drk, _, _ = vjp(dyab)
        return (dy, dr, dk, dv, dg), (dlg, dlb, drk)
    dy_rec, dr_e, dk_e, dv_e, dgate, dlnxg, dlnxb, drk = _rowwise(
        "rwkv_post_bwd", post_bwd, [y_rec, r_, k2_, v_, gate_, dya], post_params,
        [(da, F32)] * 5, [(1, da)] * 3, tr_pre)

    dr_r, dlw_r, dk_r, dv_r, dkk_r, db_r = _rec_bwd(r_, lw_, k2_, v_, kk_, b_, s0s, dy_rec)

    def pre_bwd(i, nb, b_rkv, b_w, b_a, b_g, dr1, dr2, dlw, dk1, dk2, dv1, dv2, dkk, dbb, dgt,
                h_rkv, h_w, h_a_, h_g, m_rkv, m_w, m_a, m_g, *params):
        blocks = [b_rkv, b_w, b_a, b_g]
        prevs = [_shift_prev(xb, hb, i) for xb, hb in zip(blocks, [h_rkv, h_w, h_a_, h_g])]
        mus = [m_rkv, m_w, m_a, m_g]
        s_rkv, s_w, s_a, s_g = [xb + (pb - xb) * mb for xb, pb, mb in zip(blocks, prevs, mus)]
        _, vjp = jax.vjp(_rwkv_pre_math, *split3(s_rkv), s_w, s_a, s_g, *params)
        grads = vjp((dr1 + dr2, dlw, dk1 + dk2, dv1 + dv2, dkk, dbb, dgt))
        g_rkv = jnp.concatenate(grads[:3], axis=1)
        g_w, g_a, g_g = grads[3:6]
        dmu = [jnp.sum(gs * (pb - xb), axis=0, keepdims=True)
               for gs, pb, xb in zip([g_rkv, g_w, g_a, g_g], prevs, blocks)]
        dw0, dwd, da0, dwa, dwg, dkk_, dka = grads[6:13]
        return (g_rkv, g_w, g_a, g_g), (*dmu, dw0, dwd, da0, dwa, dwg, dkk_, dka)

    pre_out = _rowwise(
        "rwkv_pre_bwd", pre_bwd,
        pre_rows + [dr_r, dr_e, dlw_r, dk_r, dk_e, dv_r, dv_e, dkk_r, db_r, dgate],
        [mu_rkv, mu_w, mu_a, mu_g] + pre_params,
        [(o_w, F32), (l_w, F32), (l_a, F32), (l_g, F32)],
        [(1, o_w), (1, l_w), (1, l_a), (1, l_g), (1, da), (l_w, da), (1, da), (l_a, da), (l_g, da),
         (1, da), (1, da)],
        _rtile(t, 64), halos=pre_halos)
    gs_rkv, gs_w, gs_a, gs_g = pre_out[:4]
    dmu_parts = pre_out[4:8]
    dw0, dwd, da0, dwa, dwg, dk_k, dk_a = pre_out[8:]

    def unshift_fn(i, nb, a1, a2, a3, a4, n1, n2, n3, n4, m1, m2, m3, m4):
        outs = [gs * (1.0 - mb) + _shift_next(gs * mb, hb * mb, i, nb)
                for gs, hb, mb in zip([a1, a2, a3, a4], [n1, n2, n3, n4], [m1, m2, m3, m4])]
        return outs, ()
    gs_list = [gs_rkv, gs_w, gs_a, gs_g]
    dp_rkv, dp_w, dp_a, dp_g = _rowwise(
        "token_shift_bwd", unshift_fn, gs_list, [mu_rkv, mu_w, mu_a, mu_g],
        [(o_w, BF16), (l_w, BF16), (l_a, BF16), (l_g, BF16)], [], tr_pre,
        halos=[(a, "next") for a in gs_list])

    dq, dkp, dkc, dvp, dvc, dbias, dsinks = _attn_bwd(q, kb, vb, bias, attn_sinks, dmix_in, 1)
    zpad = jnp.zeros((ATT_BLOCK, kb.shape[1]), F32)
    dkb = (dkc + jnp.concatenate([dkp[ATT_BLOCK:], zpad], axis=0)).astype(BF16)
    dvb = (dvc + jnp.concatenate([dvp[ATT_BLOCK:], zpad], axis=0)).astype(BF16)
    d_rpb = _small_dot("rpb_scatter", onehot, dbias.reshape(hq, -1), 1, 1)

    dp = jnp.concatenate([dp_rkv, dp_w, dp_a, dp_g, dq, dkb, dvb], axis=1)
    (du1,) = _matmul("in_proj_dgrad", dp, w_in_f, "nt", [F32])
    (gw_in,) = _matmul("in_proj_wgrad", u1, dp, "tn", [BF16])

    def embed_bwd(i, nb, xb, dx0v, du1v, g, b, sc, sh):
        _, vjp = jax.vjp(_embed_math, xb, g, b, sc, sh)
        dxv, dg, db, dsc, dsh = vjp((dx0v, du1v))
        return (dxv,), (dg, db, dsc, dsh)
    grad_x, dlng, dlnb, dsc1, dsh1 = _rowwise(
        "embed_ln_mod_bwd", embed_bwd, [x2d, dx0, du1], [lng, lnb, sc1, sh1], [(d, F32)], [(1, d)] * 4,
        _rtile(t, 128))

    dmod = jnp.concatenate([dsh1, dsc1, dg1, dsh2, dsc2, dg2], axis=1)
    small = {"ln_emb_g": dlng, "ln_emb_b": dlnb, "rpb_table": d_rpb, "b_mod": dmod,
             "mu_shift": jnp.concatenate(dmu_parts, axis=1), "w0": dw0, "a0": da0, "k_k": dk_k, "k_a": dk_a,
             "r_k": drk, "lnx_g": dlnxg, "lnx_b": dlnxb, "attn_sinks": dsinks, "ln1_g": dln1g, "ln1_b": dln1b,
             "ln2_g": dln2g, "ln2_b": dln2b}
    small_names = list(small)
    packed = jnp.concatenate([small[nm].reshape(1, -1) for nm in small_names], axis=1)
    (packed_all,) = _exchange("gather_small_grads", [packed], False)
    packed_all = packed_all.reshape(N_DEV, -1)

    scattered = _exchange("scatter_weight_grads", [
        _cols_to_shards(gw_in), gw_out.reshape(N_DEV, d // N_DEV, d), _cols_to_shards(gw_up),
        gw_down.reshape(N_DEV, -1, d), _cols_to_shards(dwd), _cols_to_shards(dwa), _cols_to_shards(dwg)], True)
    big = dict(zip(["w_in", "w_out", "w_up", "w_down", "w_decay_up", "w_iclr_up", "w_gate_up"], scattered))

    grads, deltas, new_m, new_v = {}, {}, {}, {}

    def put(nm, res):
        shape = weights[nm].shape
        grads[nm], deltas[nm], new_m[nm], new_v[nm] = [a.reshape(shape) for a in res]

    for nm, parts in big.items():
        w2 = weights[nm][0]
        put(nm, _adamw("adamw_" + nm, w2, mom_m[nm][0], mom_v[nm][0], parts))

    n_mod = w_mod.shape[2]
    dmod_cols = lax.dynamic_slice_in_dim(packed_all[:, _offset(small, small_names, "b_mod"):], me * n_mod, n_mod,
                                         axis=1)
    put("w_mod", _adamw_outer("adamw_w_mod", w_mod[0], m_w_mod[0], v_w_mod[0], cond_all.T, dmod_cols))

    off = 0
    for nm in small_names:
        size = small[nm].size
        wshape = weights[nm].shape
        two_d = (1, size) if nm != "rpb_table" else wshape
        parts = packed_all[:, off:off + size].reshape((N_DEV,) + two_d)
        off += size
        put(nm, _adamw("adamw_" + nm, weights[nm].reshape(two_d), mom_m[nm].reshape(two_d),
                       mom_v[nm].reshape(two_d), parts))

    loss = lax.psum(loss_acc[0, 0], MESH_AXES)
    return (loss, grad_x[None], *[grads[nm] for nm in names], *[deltas[nm] for nm in names],
            *[new_m[nm] for nm in names], *[new_v[nm] for nm in names])


def _offset(small, small_names, name):
    off = 0
    for nm in small_names:
        if nm == name:
            return off
        off += small[nm].size
    raise KeyError(name)
```

```python
import functools
import math

import jax
import jax.numpy as jnp
from jax import lax
from jax.experimental import pallas as pl
from jax.experimental.pallas import tpu as pltpu

F32 = jnp.float32
BF16 = jnp.bfloat16
HI = lax.Precision.HIGHEST
MESH_AXES = ("x", "y", "c")
N_DEV = 8

HEAD = 64
GQA_RATIO = 8
ATT_BLOCK = 128
RPB_MAX_DIST = 128
LN_EPS = 1e-5
LNX_EPS = 64e-5
DEPTH = 1
ALPHA = (2.0 * DEPTH) ** 0.25
CHUNK = 64
REC_HEADS = 16

ADAM_LR = 0.001
ADAM_B1 = 0.9
ADAM_B2 = 0.999
ADAM_EPS = 1e-08
ADAM_WD = 0.01
ADAM_STEP = 10

VMEM_LIMIT = 56 * 1024 * 1024


def _cparams(sem=None):
    return pltpu.CompilerParams(dimension_semantics=sem, vmem_limit_bytes=VMEM_LIMIT)


def _tile(dim, cap):
    best = None
    t = 128
    while t <= min(dim, cap):
        if dim % t == 0:
            best = t
        t += 128
    return best or dim


def _rtile(dim, cap):
    best = None
    t = 8
    while t <= min(dim, cap):
        if dim % t == 0:
            best = t
        t += 8
    return best or dim


def _split2(a):
    hi = a.astype(BF16)
    return hi, (a - hi.astype(F32)).astype(BF16)


def _raw_dot(a, b, ca, cb, prec):
    dims = (((ca,), (cb,)), ((), ()))
    mm = lambda p, q: lax.dot_general(p, q, dims, preferred_element_type=F32)
    if prec == "bf16":
        return mm(a.astype(BF16), b.astype(BF16))
    if prec == "x3":
        (ah, al), (bh, bl) = _split2(a), _split2(b)
        return mm(ah, bh) + (mm(ah, bl) + mm(al, bh))
    if prec == "mask":
        ab = a.astype(BF16)
        b1, b2 = _split2(b)
        b3 = (b - b1.astype(F32) - b2.astype(F32)).astype(BF16)
        return mm(ab, b1) + (mm(ab, b2) + mm(ab, b3))
    return lax.dot_general(a, b, dims, precision=HI, preferred_element_type=F32)


@functools.partial(jax.custom_vjp, nondiff_argnums=(2, 3, 4))
def _bf16_dot(a, b, ca, cb, prec):
    return _raw_dot(a, b, ca, cb, prec)


def _bf16_dot_fwd(a, b, ca, cb, prec):
    return _raw_dot(a, b, ca, cb, prec), (a, b)


def _bf16_dot_bwd(ca, cb, prec, res, g):
    a, b = res
    if prec == "mask":
        return jnp.zeros_like(a), _bf16_dot(a, g, 1 - ca, 0, prec)
    if ca == 1:
        da = _bf16_dot(g, b, 1, 1 - cb, prec)
    else:
        da = _bf16_dot(b, g, 1 - cb, 1, prec)
    if cb == 0:
        db = _bf16_dot(a, g, 1 - ca, 0, prec)
    else:
        db = _bf16_dot(g, a, 0, 1 - ca, prec)
    return da, db


_bf16_dot.defvjp(_bf16_dot_fwd, _bf16_dot_bwd)


def _dot(a, b, ca, cb, prec):
    return _raw_dot(a, b, ca, cb, prec) if prec == "hi" else _bf16_dot(a, b, ca, cb, prec)


def _sigmoid(z):
    return 1.0 / (1.0 + jnp.exp(-z))


def _softplus(z):
    return jnp.maximum(z, 0.0) + jnp.log(1.0 + jnp.exp(-jnp.abs(z)))


def _matmul(name, a, b, mode, out_dtypes, epilogue=None, extras=(), caps=(1024, 1024, 512)):
    if mode == "nn":
        (m, k), n = a.shape, b.shape[1]
    elif mode == "nt":
        (m, k), n = a.shape, b.shape[0]
    else:
        (k, m), n = a.shape, b.shape[1]
    tm, tn, tk = _tile(m, caps[0]), _tile(n, caps[1]), _tile(k, caps[2])
    nk = k // tk
    ne, no = len(extras), len(out_dtypes)
    ca, cb = {"nn": (1, 0), "nt": (1, 1), "tn": (0, 0)}[mode]

    def body(a_ref, b_ref, *rest):
        extra_refs, out_refs, acc = rest[:ne], rest[ne:ne + no], rest[-1]
        kk = pl.program_id(2)

        @pl.when(kk == 0)
        def _():
            acc[...] = jnp.zeros_like(acc)

        acc[...] += _raw_dot(a_ref[...], b_ref[...], ca, cb, "bf16")

        @pl.when(kk == nk - 1)
        def _():
            res = epilogue(acc[...], *[e[...] for e in extra_refs]) if epilogue else (acc[...],)
            for o, v in zip(out_refs, res):
                o[...] = v.astype(o.dtype)

    a_spec = (pl.BlockSpec((tk, tm), lambda i, j, kk: (kk, i)) if mode == "tn"
              else pl.BlockSpec((tm, tk), lambda i, j, kk: (i, kk)))
    b_spec = (pl.BlockSpec((tn, tk), lambda i, j, kk: (j, kk)) if mode == "nt"
              else pl.BlockSpec((tk, tn), lambda i, j, kk: (kk, j)))
    mn_spec = pl.BlockSpec((tm, tn), lambda i, j, kk: (i, j))
    outs = pl.pallas_call(
        body, name=name, grid=(m // tm, n // tn, nk),
        in_specs=[a_spec, b_spec] + [mn_spec] * ne,
        out_specs=[mn_spec] * no,
        out_shape=[jax.ShapeDtypeStruct((m, n), dt) for dt in out_dtypes],
        scratch_shapes=[pltpu.VMEM((tm, tn), F32)],
        compiler_params=_cparams(("parallel", "parallel", "arbitrary")),
    )(a, b, *extras)
    return outs


def _rowwise(name, fn, rows, bcasts, out_rows, out_accs, tr, halos=()):
    t = rows[0].shape[0]
    nb = t // tr
    n_in = len(rows) + len(halos) + len(bcasts)
    n_ro = len(out_rows)

    def body(*refs):
        ins = [r[...] for r in refs[:n_in]]
        o_refs = refs[n_in:]
        i = pl.program_id(0)
        routs, aouts = fn(i, nb, *ins)
        for ref, v in zip(o_refs[:n_ro], routs):
            ref[...] = v.astype(ref.dtype)
        for ref, v in zip(o_refs[n_ro:], aouts):
            @pl.when(i == 0)
            def _(ref=ref):
                ref[...] = jnp.zeros_like(ref)
            ref[...] += v.reshape(ref.shape)

    in_specs = [pl.BlockSpec((tr, r.shape[1]), lambda i: (i, 0)) for r in rows]
    for arr, which in halos:
        if which == "prev":
            in_specs.append(pl.BlockSpec((8, arr.shape[1]), lambda i: (jnp.maximum(i * (tr // 8) - 1, 0), 0)))
        else:
            in_specs.append(pl.BlockSpec((8, arr.shape[1]),
                                         lambda i: (jnp.minimum((i + 1) * (tr // 8), t // 8 - 1), 0)))
    for bc in bcasts:
        in_specs.append(pl.BlockSpec(bc.shape, lambda i, nd=bc.ndim: (0,) * nd))
    out_specs = [pl.BlockSpec((tr, c), lambda i: (i, 0)) for c, _ in out_rows]
    out_specs += [pl.BlockSpec(s, lambda i, nd=len(s): (0,) * nd) for s in out_accs]
    out_shape = [jax.ShapeDtypeStruct((t, c), dt) for c, dt in out_rows]
    out_shape += [jax.ShapeDtypeStruct(s, F32) for s in out_accs]
    return pl.pallas_call(
        body, name=name, grid=(nb,), in_specs=in_specs, out_specs=out_specs, out_shape=out_shape,
        compiler_params=_cparams(("arbitrary",)),
    )(*rows, *[h[0] for h in halos], *bcasts)


def _shift_prev(x, halo, i):
    rolled = pltpu.roll(x, 1, 0)
    first = jnp.where(i == 0, 0.0, halo[7:8, :])
    row = lax.broadcasted_iota(jnp.int32, x.shape, 0)
    return jnp.where(row == 0, first, rolled)


def _shift_next(x, halo, i, nb):
    rolled = pltpu.roll(x, x.shape[0] - 1, 0)
    last = jnp.where(i == nb - 1, 0.0, halo[0:1, :])
    row = lax.broadcasted_iota(jnp.int32, x.shape, 0)
    return jnp.where(row == x.shape[0] - 1, last, rolled)


def _ln(x, g, b, eps=LN_EPS):
    mu = jnp.mean(x, axis=-1, keepdims=True)
    xc = x - mu
    var = jnp.mean(xc * xc, axis=-1, keepdims=True)
    return xc * lax.rsqrt(var + eps) * g + b


def _embed_math(x, g, b, sc, sh):
    x0 = _ln(x, g, b)
    return x0, x0 * (1.0 + sc) + sh


def _post_math(xin, y, gate, g, b, sc, sh):
    x1 = _ln(ALPHA * xin + (1.0 + gate) * y, g, b)
    return x1, x1 * (1.0 + sc) + sh


def _loss_math(xin, h, tgt, gate, g, b):
    x2 = _ln(ALPHA * xin + (1.0 + gate) * h, g, b)
    err = x2 - tgt
    return 0.5 * jnp.sum(jnp.mean(err * err, axis=-1))


def _rwkv_pre_math(r, k, v, xw, xa, xg, w0, wd, a0, wa, wg, k_k, k_a, seg, seg_t):
    wpre = -_softplus(-(w0 + _dot(jnp.tanh(xw), wd, 1, 0, "hi"))) - 0.5
    lw = -jnp.exp(wpre)
    a = _sigmoid(a0 + _dot(xa, wa, 1, 0, "hi"))
    g = _dot(_sigmoid(xg), wg, 1, 0, "hi")
    kk = k * k_k
    norm = jnp.sqrt(_dot(kk * kk, seg, 1, 0, "hi"))
    kkn = kk * _dot(1.0 / jnp.maximum(norm, 1e-12), seg_t, 1, 0, "hi")
    k2 = k * (1.0 + (a - 1.0) * k_a)
    return r, lw, k2, v, kkn, kkn * a, g


def _rwkv_post_math(y, r, k2, v, g, lnx_g, lnx_b, r_k, seg, seg_t):
    inv = 1.0 / HEAD
    mu = _dot(_dot(y, seg, 1, 0, "hi"), seg_t, 1, 0, "hi") * inv
    yc = y - mu
    var = _dot(_dot(yc * yc, seg, 1, 0, "hi"), seg_t, 1, 0, "hi") * inv
    yn = yc * lax.rsqrt(var + LNX_EPS) * lnx_g + lnx_b
    bonus = _dot(_dot(r * k2 * r_k, seg, 1, 0, "hi"), seg_t, 1, 0, "hi") * v
    return (yn + bonus) * g


def _chunk_math(s0, r, lw, k, v, kk, b, dot=_dot):
    n = len(r)
    hs = range(n)
    c = r[0].shape[0]
    ti = lax.broadcasted_iota(jnp.int32, (2 * c, 2 * c), 0)
    tj = lax.broadcasted_iota(jnp.int32, (2 * c, 2 * c), 1)
    tt, jj = ti & (c - 1), tj & (c - 1)
    quad = jnp.where(ti < c, (tt > jj).astype(F32), (tt >= jj).astype(F32))
    incl = quad[c:, :c]
    eye = (ti[:c, :c] == tj[:c, :c]).astype(F32)
    cl = [dot(incl, lw[i], 1, 0, "mask") for i in hs]
    ge = [jnp.exp(cl[i]) for i in hs]
    gi = [jnp.exp(-cl[i]) for i in hs]
    ar = [jnp.concatenate([-kk[i] * jnp.exp(cl[i] - lw[i]), r[i] * ge[i]], axis=0) for i in hs]
    kb = [jnp.concatenate([k[i] * gi[i], b[i] * gi[i]], axis=0) for i in hs]
    m = [dot(ar[i], kb[i], 1, 1, "x3") * quad for i in hs]
    ars0 = [dot(ar[i], s0[i], 1, 1, "x3") for i in hs]
    mv = [dot(m[i][:c, :c], v[i], 1, 0, "x3") for i in hs]
    pw = [m[i][:c, c:] for i in hs]
    inv = [eye + pw[i] for i in hs]
    for _ in range(int(math.log2(c)) - 1):
        pw = [dot(pw[i], pw[i], 1, 0, "x3") for i in hs]
        inv = [inv[i] + dot(inv[i], pw[i], 1, 0, "x3") for i in hs]
    u = [dot(inv[i], ars0[i][:c] + mv[i], 1, 0, "x3") for i in hs]
    vu = [jnp.concatenate([v[i], u[i]], axis=0) for i in hs]
    y = [ars0[i][c:] + dot(m[i][c:], vu[i], 1, 0, "x3") for i in hs]
    s1 = [(s0[i] + dot(vu[i], kb[i], 0, 0, "x3")) * ge[i][c - 1:c, :] for i in hs]
    return y, s1


def _attn_math(q, kp, kc, vp, vc, bias, sinks, first, dot=_dot):
    hq = q.shape[1] // HEAD
    hkv = kc.shape[1] // HEAD
    group = hq // hkv
    qi = lax.broadcasted_iota(jnp.int32, (ATT_BLOCK, 2 * ATT_BLOCK), 0)
    kj = lax.broadcasted_iota(jnp.int32, (ATT_BLOCK, 2 * ATT_BLOCK), 1)
    dist = qi + ATT_BLOCK - kj
    valid = (dist >= 0) & (dist < ATT_BLOCK) & (jnp.logical_not(first) | (kj >= ATT_BLOCK))
    outs = []
    for j in range(hkv):
        kband = jnp.concatenate([kp[:, j * HEAD:(j + 1) * HEAD], kc[:, j * HEAD:(j + 1) * HEAD]], axis=0)
        vband = jnp.concatenate([vp[:, j * HEAD:(j + 1) * HEAD], vc[:, j * HEAD:(j + 1) * HEAD]], axis=0)
        for gq in range(group):
            h = j * group + gq
            s = dot(q[:, h * HEAD:(h + 1) * HEAD], kband, 1, 1, "bf16") * (HEAD ** -0.5) + bias[h]
            s = jnp.where(valid, s, -1e30)
            sink = sinks[0:1, h:h + 1]
            m = jnp.maximum(jnp.max(s, axis=-1, keepdims=True), sink)
            e = jnp.exp(s - m)
            p = e / (jnp.sum(e, axis=-1, keepdims=True) + jnp.exp(sink - m))
            outs.append(dot(p, vband, 1, 0, "bf16"))
    return jnp.concatenate(outs, axis=1)


def _rec_specs(t, da, gh, reverse):
    nc = t // CHUNK
    if reverse:
        return pl.BlockSpec((CHUNK, gh * HEAD), lambda hg, c: (nc - 1 - c, hg))
    return pl.BlockSpec((CHUNK, gh * HEAD), lambda hg, c: (c, hg))


def _rec_fwd(r, lw, k, v, kk, b):
    t, da = r.shape
    h = da // HEAD
    gh = min(REC_HEADS, h)
    nc = t // CHUNK

    def body(r_ref, lw_ref, k_ref, v_ref, kk_ref, b_ref, y_ref, s0_ref, state):
        @pl.when(pl.program_id(1) == 0)
        def _():
            state[...] = jnp.zeros_like(state)

        sls = [slice(i * HEAD, (i + 1) * HEAD) for i in range(gh)]
        heads = lambda ref: [ref[:, sl] for sl in sls]
        s0 = [state[i] for i in range(gh)]
        y, s1 = _chunk_math(s0, heads(r_ref), heads(lw_ref), heads(k_ref), heads(v_ref), heads(kk_ref),
                            heads(b_ref), dot=_raw_dot)
        for i, sl in enumerate(sls):
            s0_ref[0, i] = s0[i]
            y_ref[:, sl] = y[i]
            state[i] = s1[i]

    spec = _rec_specs(t, da, gh, False)
    return pl.pallas_call(
        body, name="rwkv_recurrence_fwd", grid=(h // gh, nc),
        in_specs=[spec] * 6,
        out_specs=[spec, pl.BlockSpec((1, gh, HEAD, HEAD), lambda hg, c: (c, hg, 0, 0))],
        out_shape=[jax.ShapeDtypeStruct((t, da), F32), jax.ShapeDtypeStruct((nc, h, HEAD, HEAD), F32)],
        scratch_shapes=[pltpu.VMEM((gh, HEAD, HEAD), F32)],
        compiler_params=_cparams(("parallel", "arbitrary")),
    )(r, lw, k, v, kk, b)


def _rec_bwd(r, lw, k, v, kk, b, s0s, dy):
    t, da = r.shape
    h = da // HEAD
    gh = min(REC_HEADS, h)
    nc = t // CHUNK

    def body(r_ref, lw_ref, k_ref, v_ref, kk_ref, b_ref, dy_ref, s0_ref,
             dr_ref, dlw_ref, dk_ref, dv_ref, dkk_ref, db_ref, dstate):
        @pl.when(pl.program_id(1) == 0)
        def _():
            dstate[...] = jnp.zeros_like(dstate)

        sls = [slice(i * HEAD, (i + 1) * HEAD) for i in range(gh)]
        heads = lambda ref: [ref[:, sl] for sl in sls]
        _, vjp = jax.vjp(_chunk_math, [s0_ref[0, i] for i in range(gh)], heads(r_ref), heads(lw_ref),
                         heads(k_ref), heads(v_ref), heads(kk_ref), heads(b_ref))
        grads = vjp((heads(dy_ref), [dstate[i] for i in range(gh)]))
        for i, sl in enumerate(sls):
            dstate[i] = grads[0][i]
            for ref, val in zip((dr_ref, dlw_ref, dk_ref, dv_ref, dkk_ref, db_ref), grads[1:]):
                ref[:, sl] = val[i]

    spec = _rec_specs(t, da, gh, True)
    return pl.pallas_call(
        body, name="rwkv_recurrence_bwd", grid=(h // gh, nc),
        in_specs=[spec] * 7 + [pl.BlockSpec((1, gh, HEAD, HEAD), lambda hg, c: (nc - 1 - c, hg, 0, 0))],
        out_specs=[spec] * 6,
        out_shape=[jax.ShapeDtypeStruct((t, da), F32)] * 6,
        scratch_shapes=[pltpu.VMEM((gh, HEAD, HEAD), F32)],
        compiler_params=_cparams(("parallel", "arbitrary")),
    )(r, lw, k, v, kk, b, dy, s0s)


def _attn_specs(t, hq_w, hkv_w):
    nb = t // ATT_BLOCK
    cur = lambda w: pl.BlockSpec((ATT_BLOCK, w), lambda n: (n, 0))
    prev = lambda w: pl.BlockSpec((ATT_BLOCK, w), lambda n: (jnp.maximum(n - 1, 0), 0))
    return nb, cur, prev


def _attn_fwd(q, kb, vb, bias, sinks):
    t, qw = q.shape
    kw = kb.shape[1]
    nb, cur, prev = _attn_specs(t, qw, kw)

    def body(q_ref, kp_ref, kc_ref, vp_ref, vc_ref, bias_ref, sink_ref, o_ref):
        first = pl.program_id(0) == 0
        o = _attn_math(q_ref[...], kp_ref[...], kc_ref[...], vp_ref[...], vc_ref[...],
                       bias_ref[...], sink_ref[...], first, dot=_raw_dot)
        o_ref[...] = o.astype(o_ref.dtype)

    full = lambda a: pl.BlockSpec(a.shape, lambda n, nd=a.ndim: (0,) * nd)
    return pl.pallas_call(
        body, name="swa_attention_fwd", grid=(nb,),
        in_specs=[cur(qw), prev(kw), cur(kw), prev(kw), cur(kw), full(bias), full(sinks)],
        out_specs=cur(qw), out_shape=jax.ShapeDtypeStruct((t, qw), BF16),
        compiler_params=_cparams(("parallel",)),
    )(q, kb, kb, vb, vb, bias, sinks)


def _attn_bwd(q, kb, vb, bias, sinks, do, col_block):
    t, qw = q.shape
    kw = kb.shape[1]
    nb, cur, prev = _attn_specs(t, qw, kw)

    def body(q_ref, kp_ref, kc_ref, vp_ref, vc_ref, bias_ref, sink_ref, do_ref,
             dq_ref, dkp_ref, dkc_ref, dvp_ref, dvc_ref, dbias_ref, dsink_ref):
        n = pl.program_id(0)
        first = n == 0
        fn = functools.partial(_attn_math, first=first)
        _, vjp = jax.vjp(fn, q_ref[...], kp_ref[...], kc_ref[...], vp_ref[...], vc_ref[...],
                         bias_ref[...], sink_ref[...])
        dq, dkp, dkc, dvp, dvc, dbias, dsink = vjp(do_ref[...].astype(F32))
        dq_ref[...] = dq.astype(dq_ref.dtype)
        dkp_ref[...] = dkp
        dkc_ref[...] = dkc
        dvp_ref[...] = dvp
        dvc_ref[...] = dvc

        @pl.when(first)
        def _():
            dbias_ref[...] = jnp.zeros_like(dbias_ref)
            dsink_ref[...] = jnp.zeros_like(dsink_ref)

        dbias_ref[...] += dbias
        dsink_ref[...] += dsink

    full = lambda a: pl.BlockSpec(a.shape, lambda n, nd=a.ndim: (0,) * nd)
    kshape = jax.ShapeDtypeStruct((t, kw), F32)
    return pl.pallas_call(
        body, name="swa_attention_bwd", grid=(nb,),
        in_specs=[cur(qw), prev(kw), cur(kw), prev(kw), cur(kw), full(bias), full(sinks),
                  pl.BlockSpec((ATT_BLOCK, qw), lambda n: (n, col_block))],
        out_specs=[cur(qw), cur(kw), cur(kw), cur(kw), cur(kw), full(bias), full(sinks)],
        out_shape=[jax.ShapeDtypeStruct((t, qw), BF16), kshape, kshape, kshape, kshape,
                   jax.ShapeDtypeStruct(bias.shape, F32), jax.ShapeDtypeStruct(sinks.shape, F32)],
        compiler_params=_cparams(("arbitrary",)),
    )(q, kb, kb, vb, vb, bias, sinks, do)


def _bucket_onehot():
    qi = jnp.arange(ATT_BLOCK)[:, None]
    kj = jnp.arange(2 * ATT_BLOCK)[None, :]
    n = jnp.maximum(qi + ATT_BLOCK - kj, 0)
    buckets, max_exact = 32, 16
    nf = jnp.maximum(n, 1).astype(F32)
    large = max_exact + (jnp.log(nf / max_exact) / math.log(RPB_MAX_DIST / max_exact)
                         * (buckets - max_exact)).astype(jnp.int32)
    bucket = jnp.where(n < max_exact, n, jnp.minimum(large, buckets - 1)).reshape(-1)
    return (bucket[None, :] == jnp.arange(buckets)[:, None]).astype(F32)


def _small_dot(name, a, b, ca, cb):
    m = a.shape[1 - ca]
    n = b.shape[1 - cb]

    def body(a_ref, b_ref, o_ref):
        o_ref[...] = _raw_dot(a_ref[...], b_ref[...], ca, cb, "hi")

    return pl.pallas_call(body, name=name, out_shape=jax.ShapeDtypeStruct((m, n), F32),
                          compiler_params=_cparams())(a, b)


def _mod_fwd(c_all, w_mod):
    d, n = w_mod.shape
    tn = _tile(n, 512)

    def body(c_ref, w_ref, o_ref, cond_ref):
        cv = c_ref[...]
        cond = cv * _sigmoid(cv)
        cond_ref[...] = cond
        o_ref[...] = _raw_dot(cond, w_ref[...], 1, 0, "hi")

    return pl.pallas_call(
        body, name="adaln_mod_fwd", grid=(n // tn,),
        in_specs=[pl.BlockSpec(c_all.shape, lambda j: (0, 0)), pl.BlockSpec((d, tn), lambda j: (0, j))],
        out_specs=[pl.BlockSpec((c_all.shape[0], tn), lambda j: (0, j)),
                   pl.BlockSpec(c_all.shape, lambda j: (0, 0))],
        out_shape=[jax.ShapeDtypeStruct((c_all.shape[0], n), F32), jax.ShapeDtypeStruct(c_all.shape, F32)],
        compiler_params=_cparams(("arbitrary",)),
    )(c_all, w_mod)


def _adam_math(w, g, m, v):
    m = ADAM_B1 * m + (1.0 - ADAM_B1) * g
    v = ADAM_B2 * v + (1.0 - ADAM_B2) * (g * g)
    m_hat = m / (1.0 - ADAM_B1 ** ADAM_STEP)
    v_hat = v / (1.0 - ADAM_B2 ** ADAM_STEP)
    delta = -ADAM_LR * (m_hat / (jnp.sqrt(v_hat) + ADAM_EPS) + ADAM_WD * w)
    return delta, m, v


def _adamw(name, w, m, v, gparts):
    r, c = w.shape
    p = gparts.shape[0]
    tr = _rtile(r, max(8, (1 << 18) // max(c, 1) // 8 * 8))

    def body(w_ref, m_ref, v_ref, g_ref, go_ref, d_ref, mo_ref, vo_ref):
        g = g_ref[0].astype(F32)
        for s in range(1, p):
            g = g + g_ref[s].astype(F32)
        delta, mn, vn = _adam_math(w_ref[...], g, m_ref[...], v_ref[...])
        go_ref[...] = g
        d_ref[...] = delta
        mo_ref[...] = mn
        vo_ref[...] = vn

    spec = pl.BlockSpec((tr, c), lambda i: (i, 0))
    return pl.pallas_call(
        body, name=name, grid=(r // tr,),
        in_specs=[spec, spec, spec, pl.BlockSpec((p, tr, c), lambda i: (0, i, 0))],
        out_specs=[spec] * 4, out_shape=[jax.ShapeDtypeStruct((r, c), F32)] * 4,
        compiler_params=_cparams(("parallel",)),
    )(w, m, v, gparts)


def _adamw_outer(name, w, m, v, cond_t, dmod):
    d, n = w.shape
    tr, tn = _rtile(d, 512), _tile(n, 1024)

    def body(w_ref, m_ref, v_ref, c_ref, dm_ref, go_ref, d_ref, mo_ref, vo_ref):
        g = _raw_dot(c_ref[...], dm_ref[...], 1, 0, "hi")
        delta, mn, vn = _adam_math(w_ref[...], g, m_ref[...], v_ref[...])
        go_ref[...] = g
        d_ref[...] = delta
        mo_ref[...] = mn
        vo_ref[...] = vn

    spec = pl.BlockSpec((tr, tn), lambda i, j: (i, j))
    return pl.pallas_call(
        body, name=name, grid=(d // tr, n // tn),
        in_specs=[spec, spec, spec, pl.BlockSpec((tr, cond_t.shape[1]), lambda i, j: (i, 0)),
                  pl.BlockSpec((dmod.shape[0], tn), lambda i, j: (0, j))],
        out_specs=[spec] * 4, out_shape=[jax.ShapeDtypeStruct((d, n), F32)] * 4,
        compiler_params=_cparams(("parallel", "parallel")),
    )(w, m, v, cond_t, dmod)


def _exchange(name, arrays, scatter):
    n = len(arrays)
    if scatter:
        out_shape = [jax.ShapeDtypeStruct(a.shape, a.dtype) for a in arrays]
    else:
        out_shape = [jax.ShapeDtypeStruct((N_DEV,) + a.shape, a.dtype) for a in arrays]

    def body(*refs):
        ins, outs = refs[:n], refs[n:2 * n]
        send_sems, recv_sems, local_sems = refs[2 * n:]
        x, y, c = lax.axis_index("x"), lax.axis_index("y"), lax.axis_index("c")
        me = 4 * x + 2 * y + c

        def peer(p):
            px, py, pc = x ^ ((p >> 2) & 1), y ^ ((p >> 1) & 1), c ^ (p & 1)
            return (px, py, pc), 4 * px + 2 * py + pc

        def remote(a, p):
            dev, idx = peer(p)
            src = ins[a].at[idx] if scatter else ins[a]
            return pltpu.make_async_remote_copy(
                src_ref=src, dst_ref=outs[a].at[me], send_sem=send_sems.at[a, p - 1],
                recv_sem=recv_sems.at[a, p - 1], device_id=dev, device_id_type=pl.DeviceIdType.MESH)

        def arrival(a, p):
            dev, idx = peer(p)
            src = ins[a].at[idx] if scatter else ins[a]
            return pltpu.make_async_remote_copy(
                src_ref=src, dst_ref=outs[a].at[idx], send_sem=send_sems.at[a, p - 1],
                recv_sem=recv_sems.at[a, p - 1], device_id=dev, device_id_type=pl.DeviceIdType.MESH)

        locals_ = []
        for a in range(n):
            src = ins[a].at[me] if scatter else ins[a]
            cp = pltpu.make_async_copy(src, outs[a].at[me], local_sems.at[a])
            cp.start()
            locals_.append(cp)
        sends = [remote(a, p) for a in range(n) for p in range(1, N_DEV)]
        for cp in sends:
            cp.start()
        for a in range(n):
            for p in range(1, N_DEV):
                arrival(a, p).wait_recv()
        for cp in sends:
            cp.wait_send()
        for cp in locals_:
            cp.wait()

    any_spec = pl.BlockSpec(memory_space=pl.ANY)
    return pl.pallas_call(
        body, name=name, in_specs=[any_spec] * n, out_specs=[any_spec] * n, out_shape=out_shape,
        scratch_shapes=[pltpu.SemaphoreType.DMA((n, N_DEV - 1)), pltpu.SemaphoreType.DMA((n, N_DEV - 1)),
                        pltpu.SemaphoreType.DMA((n,))],
    )(*arrays)


def _peer(p):
    x, y, c = lax.axis_index("x"), lax.axis_index("y"), lax.axis_index("c")
    px, py, pc = x ^ ((p >> 2) & 1), y ^ ((p >> 1) & 1), c ^ (p & 1)
    return (px, py, pc), 4 * px + 2 * py + pc


def _split_copy(src_ref, land_ref, send_sems, recv_sems, p, scatter, arriving):
    x, y, c = lax.axis_index("x"), lax.axis_index("y"), lax.axis_index("c")
    me = 4 * x + 2 * y + c
    dev, idx = _peer(p)
    return pltpu.make_async_remote_copy(
        src_ref=src_ref.at[idx] if scatter else src_ref, dst_ref=land_ref.at[idx if arriving else me],
        send_sem=send_sems.at[p - 1], recv_sem=recv_sems.at[p - 1], device_id=dev,
        device_id_type=pl.DeviceIdType.MESH)


_HBM_SPEC = pl.BlockSpec(memory_space=pltpu.HBM)
_SEM_SPEC = pl.BlockSpec(memory_space=pltpu.SEMAPHORE)
_DATAFLOW = pltpu.SideEffectType.DATAFLOW_SIDE_EFFECTING


def _exchange_start(name, srcs, lands, scatter, after):
    n = len(srcs)

    def body(*refs):
        src_refs, land_refs = refs[:n], refs[n:2 * n]
        outs = refs[2 * n + 1:]
        send, recv, token = outs[:n], outs[n:2 * n], outs[-1]
        for a in range(n):
            for p in range(1, N_DEV):
                _split_copy(src_refs[a], land_refs[a], send[a], recv[a], p, scatter, False).start()
        token[...] = jnp.zeros_like(token)

    sems = [pltpu.SemaphoreType.DMA((N_DEV - 1,))] * (2 * n)
    hbm = [pltpu.HBM(a.shape, a.dtype) for a in list(srcs) + list(lands)]
    res = pl.pallas_call(
        body, name=name,
        out_shape=sems + hbm + [jax.ShapeDtypeStruct((8, 128), F32)],
        in_specs=[_HBM_SPEC] * (2 * n) + [pl.BlockSpec(memory_space=pl.ANY)],
        out_specs=[_SEM_SPEC] * (2 * n) + [_HBM_SPEC] * (2 * n) + [pl.BlockSpec(memory_space=pltpu.VMEM)],
        input_output_aliases={i: 2 * n + i for i in range(2 * n)},
        compiler_params=pltpu.CompilerParams(has_side_effects=_DATAFLOW),
    )(*[pltpu.with_memory_space_constraint(a, pltpu.HBM) for a in list(srcs) + list(lands)], after)
    return res[:n], res[n:2 * n], res[2 * n:3 * n], res[3 * n:4 * n], res[-1]


def _exchange_wait(name, send_sem, recv_sem, src, land, scatter, after):
    def body(src_ref, land_ref, send, recv, after_ref, src_out, land_out):
        for p in range(1, N_DEV):
            cp = _split_copy(src_ref, land_ref, send, recv, p, scatter, True)
            cp.wait_send()
            cp.wait_recv()

    return pl.pallas_call(
        body, name=name,
        out_shape=(pltpu.HBM(src.shape, src.dtype), pltpu.HBM(land.shape, land.dtype)),
        in_specs=[_HBM_SPEC, _HBM_SPEC, _SEM_SPEC, _SEM_SPEC, pl.BlockSpec(memory_space=pl.ANY)],
        out_specs=(_HBM_SPEC, _HBM_SPEC), input_output_aliases={0: 0, 1: 1},
        compiler_params=pltpu.CompilerParams(has_side_effects=_DATAFLOW),
    )(src, land, send_sem, recv_sem, after)[1]


def _own_row(block, me):
    land = lax.empty((N_DEV,) + block.shape, block.dtype)
    return lax.dynamic_update_index_in_dim(land, block, me, 0)


def _cols_to_shards(a):
    r, c = a.shape
    return a.reshape(r, N_DEV, c // N_DEV).transpose(1, 0, 2)


def _shards_to_cols(a):
    d, r, n = a.shape
    return a.transpose(1, 0, 2).reshape(r, d * n)


def kernel(x, c, ln_emb_g, ln_emb_b, rpb_table, w_mod, b_mod, w_in, mu_shift, w0, w_decay_up, a0, w_iclr_up, w_gate_up, k_k, k_a, r_k, lnx_g, lnx_b, attn_sinks, w_out, ln1_g, ln1_b, w_up, w_down, ln2_g, ln2_b, loss_target, m_ln_emb_g, m_ln_emb_b, m_rpb_table, m_w_mod, m_b_mod, m_w_in, m_mu_shift, m_w0, m_w_decay_up, m_a0, m_w_iclr_up, m_w_gate_up, m_k_k, m_k_a, m_r_k, m_lnx_g, m_lnx_b, m_attn_sinks, m_w_out, m_ln1_g, m_ln1_b, m_w_up, m_w_down, m_ln2_g, m_ln2_b, v_ln_emb_g, v_ln_emb_b, v_rpb_table, v_w_mod, v_b_mod, v_w_in, v_mu_shift, v_w0, v_w_decay_up, v_a0, v_w_iclr_up, v_w_gate_up, v_k_k, v_k_a, v_r_k, v_lnx_g, v_lnx_b, v_attn_sinks, v_w_out, v_ln1_g, v_ln1_b, v_w_up, v_w_down, v_ln2_g, v_ln2_b):
    names = ["ln_emb_g", "ln_emb_b", "rpb_table", "w_mod", "b_mod", "w_in", "mu_shift", "w0", "w_decay_up",
             "a0", "w_iclr_up", "w_gate_up", "k_k", "k_a", "r_k", "lnx_g", "lnx_b", "attn_sinks", "w_out",
             "ln1_g", "ln1_b", "w_up", "w_down", "ln2_g", "ln2_b"]
    env = dict(locals())
    weights = {nm: env[nm] for nm in names}
    mom_m = {nm: env["m_" + nm] for nm in names}
    mom_v = {nm: env["v_" + nm] for nm in names}

    t, d = x.shape[1], x.shape[2]
    da = d // 2
    h_a = da // HEAD
    hq = (d - da) // HEAD
    hkv = hq // GQA_RATIO
    l_w, l_a, l_g = w_decay_up.shape[1], w_iclr_up.shape[1], w_gate_up.shape[1]
    o_w, o_a, o_g = 3 * da, 3 * da + l_w, 3 * da + l_w + l_a
    n_rwkv = o_g + l_g
    o_kb, o_vb = n_rwkv + hq * HEAD, n_rwkv + hq * HEAD + hkv * HEAD
    me = 4 * lax.axis_index("x") + 2 * lax.axis_index("y") + lax.axis_index("c")

    x2d, tgt = x[0], loss_target[0]
    row = lambda a: a.reshape(1, -1)
    seg = (jnp.arange(da)[:, None] // HEAD == jnp.arange(h_a)[None, :]).astype(F32)
    seg_t = seg.T

    (c_all,) = _exchange("gather_cond", [c], False)
    c_all = c_all.reshape(N_DEV, d)
    mod_rows, cond_all = _mod_fwd(c_all, w_mod[0])
    gathered = _exchange("gather_weights", [
        mod_rows, w_in[0].astype(BF16), w_decay_up[0], w_iclr_up[0], w_gate_up[0]], False)
    mod_all, win_g, wd_g, wa_g, wg_g = gathered
    late = [w_out[0].astype(BF16), w_up[0].astype(BF16), w_down[0].astype(BF16)]
    late_send, late_recv, late_src, late_land, late_token = _exchange_start(
        "gather_late_weights_start", late, [_own_row(a, me) for a in late], False, mod_all)
    mod = lax.dynamic_index_in_dim(mod_all, me, axis=1, keepdims=False).reshape(1, -1) + b_mod
    mod = mod + late_token[0, 0]
    sh1, sc1, g1, sh2, sc2, g2 = [mod[:, i * d:(i + 1) * d] for i in range(6)]
    w_in_f = _shards_to_cols(win_g)
    wd_f, wa_f, wg_f = _shards_to_cols(wd_g), _shards_to_cols(wa_g), _shards_to_cols(wg_g)

    def late_weight(i, nm, after):
        return _exchange_wait("gather_" + nm + "_wait", late_send[i], late_recv[i], late_src[i], late_land[i],
                              False, after)

    tr = _rtile(t, 256)
    lng, lnb = row(ln_emb_g), row(ln_emb_b)

    def embed_fn(i, nb, xb, g, b, sc, sh):
        return _embed_math(xb, g, b, sc, sh), ()
    x0, u1 = _rowwise("embed_ln_mod", embed_fn, [x2d], [lng, lnb, sc1, sh1], [(d, F32), (d, BF16)], [], tr)

    (p,) = _matmul("in_proj", u1, w_in_f, "nn", [F32])
    p_rkv, p_w, p_a, p_g = p[:, :o_w], p[:, o_w:o_a], p[:, o_a:o_g], p[:, o_g:n_rwkv]
    q, kb, vb = p[:, n_rwkv:o_kb], p[:, o_kb:o_vb], p[:, o_vb:]
    mu_rkv, mu_w, mu_a, mu_g = (mu_shift[:, :o_w], mu_shift[:, o_w:o_a], mu_shift[:, o_a:o_g],
                                mu_shift[:, o_g:n_rwkv])
    pre_params = [w0, wd_f, a0, wa_f, wg_f, k_k, k_a, seg, seg_t]
    tr_pre = _rtile(t, 128)

    def shifted(i, blocks, halos, mus):
        return [xb + (_shift_prev(xb, hb, i) - xb) * mb for xb, hb, mb in zip(blocks, halos, mus)]

    def split3(a):
        return a[:, :da], a[:, da:2 * da], a[:, 2 * da:]

    def pre_fn(i, nb, b_rkv, b_w, b_a, b_g, h_rkv, h_w, h_a_, h_g, m_rkv, m_w, m_a, m_g, *params):
        s_rkv, s_w, s_a, s_g = shifted(i, [b_rkv, b_w, b_a, b_g], [h_rkv, h_w, h_a_, h_g],
                                       [m_rkv, m_w, m_a, m_g])
        return _rwkv_pre_math(*split3(s_rkv), s_w, s_a, s_g, *params), ()

    pre_rows = [p_rkv, p_w, p_a, p_g]
    pre_halos = [(a, "prev") for a in pre_rows]
    r_, lw_, k2_, v_, kk_, b_, gate_ = _rowwise(
        "rwkv_pre", pre_fn, pre_rows, [mu_rkv, mu_w, mu_a, mu_g] + pre_params,
        [(da, F32)] * 7, [], tr_pre, halos=pre_halos)

    y_rec, s0s = _rec_fwd(r_, lw_, k2_, v_, kk_, b_)

    rk_flat = r_k.reshape(1, da)
    post_params = [lnx_g, lnx_b, rk_flat, seg, seg_t]

    def post_fn(i, nb, yb, rb, kb_, vb_, gb, *params):
        return (_rwkv_post_math(yb, rb, kb_, vb_, gb, *params),), ()
    (ya,) = _rowwise("rwkv_post", post_fn, [y_rec, r_, k2_, v_, gate_], post_params, [(da, BF16)], [], tr_pre)

    onehot = _bucket_onehot()
    bias = _small_dot("rpb_gather", rpb_table, onehot, 0, 0)
    bias = bias.reshape(hq, ATT_BLOCK, 2 * ATT_BLOCK)
    yb = _attn_fwd(q, kb, vb, bias, attn_sinks)

    mix_in = jnp.concatenate([ya, yb], axis=1)
    w_out_f = late_weight(0, "w_out", mix_in).reshape(d, d)
    (mix,) = _matmul("out_proj", mix_in, w_out_f, "nn", [F32])

    def post1_fn(i, nb, xin, yv, gate, g, b, sc, sh):
        return _post_math(xin, yv, gate, g, b, sc, sh), ()
    x1, u2 = _rowwise("ln1_mod", post1_fn, [x0, mix], [g1, ln1_g, ln1_b, sc2, sh2],
                      [(d, F32), (d, BF16)], [], tr)

    def relu2(acc):
        rl = jnp.maximum(acc, 0.0)
        return acc, rl * rl
    w_up_f = _shards_to_cols(late_weight(1, "w_up", u2))
    hpre, hact = _matmul("mlp_up", u2, w_up_f, "nn", [F32, BF16], epilogue=relu2)
    w_down_f = late_weight(2, "w_down", hact).reshape(-1, d)
    (hmlp,) = _matmul("mlp_down", hact, w_down_f, "nn", [F32])

    def loss_fn(i, nb, xin, hv, tg, gate, g, b):
        val, vjp = jax.vjp(_loss_math, xin, hv, tg, gate, g, b)
        dxin, dh, _, dgate, dg, db = vjp(jnp.ones((), F32))
        return (dxin, dh), (val, dgate, dg, db)
    dx1, dh, loss_acc, dg2, dln2g, dln2b = _rowwise(
        "ln2_loss", loss_fn, [x1, hmlp, tgt], [g2, ln2_g, ln2_b], [(d, F32), (d, BF16)],
        [(1, 1), (1, d), (1, d), (1, d)], _rtile(t, 128))

    def drelu2(acc, hp):
        return (acc * 2.0 * jnp.maximum(hp, 0.0),)
    (dhpre,) = _matmul("mlp_down_dgrad", dh, w_down_f, "nt", [BF16], epilogue=drelu2, extras=[hpre])
    (gw_down,) = _matmul("mlp_down_wgrad", hact, dh, "tn", [BF16])
    (du2,) = _matmul("mlp_up_dgrad", dhpre, w_up_f, "nt", [F32])
    (gw_up,) = _matmul("mlp_up_wgrad", u2, dhpre, "tn", [BF16])

    def own_slice(src):
        return _own_row(lax.dynamic_index_in_dim(src, me, 0, keepdims=False), me)
    mlp_src = [gw_down.reshape(N_DEV, -1, d), _cols_to_shards(gw_up)]
    mlp_send, mlp_recv, mlp_src, mlp_land, mlp_token = _exchange_start(
        "scatter_mlp_grads_start", mlp_src, [own_slice(a) for a in mlp_src], True, gw_up)

    def post1_bwd(i, nb, xin, yv, dx1v, du2v, gate, g, b, sc, sh):
        _, vjp = jax.vjp(_post_math, xin, yv, gate, g, b, sc, sh)
        dxin, dy, dgate, dg, db, dsc, dsh = vjp((dx1v, du2v))
        return (dxin, dy), (dgate, dg, db, dsc, dsh)
    dx0, dmix, dg1, dln1g, dln1b, dsc2, dsh2 = _rowwise(
        "ln1_mod_bwd", post1_bwd, [x0, mix, dx1, du2], [g1 + mlp_token[0, 0], ln1_g, ln1_b, sc2, sh2],
        [(d, F32), (d, BF16)], [(1, d)] * 5, _rtile(t, 128))

    (dmix_in,) = _matmul("out_proj_dgrad", dmix, w_out_f, "nt", [F32])
    (gw_out,) = _matmul("out_proj_wgrad", mix_in, dmix, "tn", [BF16])
    dya = dmix_in[:, :da]
    out_src = [gw_out.reshape(N_DEV, d // N_DEV, d)]
    out_send, out_recv, out_src, out_land, out_token = _exchange_start(
        "scatter_out_grad_start", out_src, [own_slice(a) for a in out_src], True, gw_out)
    post_params_bwd = [lnx_g, lnx_b, rk_flat + out_token[0, 0], seg, seg_t]

    def post_bwd(i, nb, yb_, rb, kb_, vb_, gb, dyab, *params):
        _, vjp = jax.vjp(_rwkv_post_math, yb_, rb, kb_, vb_, gb, *params)
        dy, dr, dk, dv, dg, dlg, dlb, drk, _, _ = vjp(dyab)
        return (dy, dr, dk, dv, dg), (dlg, dlb, drk)
    dy_rec, dr_e, dk_e, dv_e, dgate, dlnxg, dlnxb, drk = _rowwise(
        "rwkv_post_bwd", post_bwd, [y_rec, r_, k2_, v_, gate_, dya], post_params_bwd,
        [(da, F32)] * 5, [(1, da)] * 3, tr_pre)

    dr_r, dlw_r, dk_r, dv_r, dkk_r, db_r = _rec_bwd(r_, lw_, k2_, v_, kk_, b_, s0s, dy_rec)

    def pre_bwd(i, nb, b_rkv, b_w, b_a, b_g, dr1, dr2, dlw, dk1, dk2, dv1, dv2, dkk, dbb, dgt,
                h_rkv, h_w, h_a_, h_g, m_rkv, m_w, m_a, m_g, *params):
        blocks = [b_rkv, b_w, b_a, b_g]
        prevs = [_shift_prev(xb, hb, i) for xb, hb in zip(blocks, [h_rkv, h_w, h_a_, h_g])]
        mus = [m_rkv, m_w, m_a, m_g]
        s_rkv, s_w, s_a, s_g = [xb + (pb - xb) * mb for xb, pb, mb in zip(blocks, prevs, mus)]
        _, vjp = jax.vjp(_rwkv_pre_math, *split3(s_rkv), s_w, s_a, s_g, *params)
        grads = vjp((dr1 + dr2, dlw, dk1 + dk2, dv1 + dv2, dkk, dbb, dgt))
        g_rkv = jnp.concatenate(grads[:3], axis=1)
        g_w, g_a, g_g = grads[3:6]
        dmu = [jnp.sum(gs * (pb - xb), axis=0, keepdims=True)
               for gs, pb, xb in zip([g_rkv, g_w, g_a, g_g], prevs, blocks)]
        dw0, dwd, da0, dwa, dwg, dkk_, dka = grads[6:13]
        return (g_rkv, g_w, g_a, g_g), (*dmu, dw0, dwd, da0, dwa, dwg, dkk_, dka)

    pre_out = _rowwise(
        "rwkv_pre_bwd", pre_bwd,
        pre_rows + [dr_r, dr_e, dlw_r, dk_r, dk_e, dv_r, dv_e, dkk_r, db_r, dgate],
        [mu_rkv, mu_w, mu_a, mu_g] + pre_params,
        [(o_w, F32), (l_w, F32), (l_a, F32), (l_g, F32)],
        [(1, o_w), (1, l_w), (1, l_a), (1, l_g), (1, da), (l_w, da), (1, da), (l_a, da), (l_g, da),
         (1, da), (1, da)],
        _rtile(t, 64), halos=pre_halos)
    gs_rkv, gs_w, gs_a, gs_g = pre_out[:4]
    dmu_parts = pre_out[4:8]
    dw0, dwd, da0, dwa, dwg, dk_k, dk_a = pre_out[8:]

    def unshift_fn(i, nb, a1, a2, a3, a4, n1, n2, n3, n4, m1, m2, m3, m4):
        outs = [gs * (1.0 - mb) + _shift_next(gs * mb, hb * mb, i, nb)
                for gs, hb, mb in zip([a1, a2, a3, a4], [n1, n2, n3, n4], [m1, m2, m3, m4])]
        return outs, ()
    gs_list = [gs_rkv, gs_w, gs_a, gs_g]
    dp_rkv, dp_w, dp_a, dp_g = _rowwise(
        "token_shift_bwd", unshift_fn, gs_list, [mu_rkv, mu_w, mu_a, mu_g],
        [(o_w, BF16), (l_w, BF16), (l_a, BF16), (l_g, BF16)], [], tr_pre,
        halos=[(a, "next") for a in gs_list])

    dq, dkp, dkc, dvp, dvc, dbias, dsinks = _attn_bwd(q, kb, vb, bias, attn_sinks, dmix_in, 1)
    zpad = jnp.zeros((ATT_BLOCK, kb.shape[1]), F32)
    dkb = (dkc + jnp.concatenate([dkp[ATT_BLOCK:], zpad], axis=0)).astype(BF16)
    dvb = (dvc + jnp.concatenate([dvp[ATT_BLOCK:], zpad], axis=0)).astype(BF16)
    d_rpb = _small_dot("rpb_scatter", onehot, dbias.reshape(hq, -1), 1, 1)

    dp = jnp.concatenate([dp_rkv, dp_w, dp_a, dp_g, dq, dkb, dvb], axis=1)
    (du1,) = _matmul("in_proj_dgrad", dp, w_in_f, "nt", [F32])
    (gw_in,) = _matmul("in_proj_wgrad", u1, dp, "tn", [BF16])

    def embed_bwd(i, nb, xb, dx0v, du1v, g, b, sc, sh):
        _, vjp = jax.vjp(_embed_math, xb, g, b, sc, sh)
        dxv, dg, db, dsc, dsh = vjp((dx0v, du1v))
        return (dxv,), (dg, db, dsc, dsh)
    grad_x, dlng, dlnb, dsc1, dsh1 = _rowwise(
        "embed_ln_mod_bwd", embed_bwd, [x2d, dx0, du1], [lng, lnb, sc1, sh1], [(d, F32)], [(1, d)] * 4,
        _rtile(t, 128))

    dmod = jnp.concatenate([dsh1, dsc1, dg1, dsh2, dsc2, dg2], axis=1)
    small = {"ln_emb_g": dlng, "ln_emb_b": dlnb, "rpb_table": d_rpb, "b_mod": dmod,
             "mu_shift": jnp.concatenate(dmu_parts, axis=1), "w0": dw0, "a0": da0, "k_k": dk_k, "k_a": dk_a,
             "r_k": drk, "lnx_g": dlnxg, "lnx_b": dlnxb, "attn_sinks": dsinks, "ln1_g": dln1g, "ln1_b": dln1b,
             "ln2_g": dln2g, "ln2_b": dln2b}
    small_names = list(small)
    packed = jnp.concatenate([small[nm].reshape(1, -1) for nm in small_names], axis=1)
    (packed_all,) = _exchange("gather_small_grads", [packed], False)
    packed_all = packed_all.reshape(N_DEV, -1)

    scattered = _exchange("scatter_weight_grads", [
        _cols_to_shards(gw_in), _cols_to_shards(dwd), _cols_to_shards(dwa), _cols_to_shards(dwg)], True)
    big = dict(zip(["w_in", "w_decay_up", "w_iclr_up", "w_gate_up"], scattered))
    big["w_down"] = _exchange_wait("scatter_w_down_wait", mlp_send[0], mlp_recv[0], mlp_src[0], mlp_land[0],
                                   True, scattered[0])
    big["w_up"] = _exchange_wait("scatter_w_up_wait", mlp_send[1], mlp_recv[1], mlp_src[1], mlp_land[1],
                                 True, scattered[0])
    big["w_out"] = _exchange_wait("scatter_w_out_wait", out_send[0], out_recv[0], out_src[0], out_land[0],
                                  True, scattered[0])

    grads, deltas, new_m, new_v = {}, {}, {}, {}

    def put(nm, res):
        shape = weights[nm].shape
        grads[nm], deltas[nm], new_m[nm], new_v[nm] = [a.reshape(shape) for a in res]

    for nm, parts in big.items():
        w2 = weights[nm][0]
        put(nm, _adamw("adamw_" + nm, w2, mom_m[nm][0], mom_v[nm][0], parts))

    n_mod = w_mod.shape[2]
    dmod_cols = lax.dynamic_slice_in_dim(packed_all[:, _offset(small, small_names, "b_mod"):], me * n_mod, n_mod,
                                         axis=1)
    put("w_mod", _adamw_outer("adamw_w_mod", w_mod[0], m_w_mod[0], v_w_mod[0], cond_all.T, dmod_cols))

    off = 0
    for nm in small_names:
        size = small[nm].size
        wshape = weights[nm].shape
        two_d = (1, size) if nm != "rpb_table" else wshape
        parts = packed_all[:, off:off + size].reshape((N_DEV,) + two_d)
        off += size
        put(nm, _adamw("adamw_" + nm, weights[nm].reshape(two_d), mom_m[nm].reshape(two_d),
                       mom_v[nm].reshape(two_d), parts))

    loss = lax.psum(loss_acc[0, 0], MESH_AXES)
    return (loss, grad_x[None], *[grads[nm] for nm in names], *[deltas[nm] for nm in names],
            *[new_m[nm] for nm in names], *[new_v[nm] for nm in names])


def _offset(small, small_names, name):
    off = 0
    for nm in small_names:
        if nm == name:
            return off
        off += small[nm].size
    raise KeyError(name)
```

```python
import functools
import math

import jax
import jax.numpy as jnp
from jax import lax
from jax.experimental import pallas as pl
from jax.experimental.pallas import tpu as pltpu

F32 = jnp.float32
BF16 = jnp.bfloat16
HI = lax.Precision.HIGHEST
MESH_AXES = ("x", "y", "c")
N_DEV = 8

HEAD = 64
GQA_RATIO = 8
ATT_BLOCK = 128
RPB_MAX_DIST = 128
LN_EPS = 1e-5
LNX_EPS = 64e-5
DEPTH = 1
ALPHA = (2.0 * DEPTH) ** 0.25
CHUNK = 64
REC_HEADS = 16

ADAM_LR = 0.001
ADAM_B1 = 0.9
ADAM_B2 = 0.999
ADAM_EPS = 1e-08
ADAM_WD = 0.01
ADAM_STEP = 10

VMEM_LIMIT = 56 * 1024 * 1024


def _cparams(sem=None):
    return pltpu.CompilerParams(dimension_semantics=sem, vmem_limit_bytes=VMEM_LIMIT)


def _tile(dim, cap):
    best = None
    t = 128
    while t <= min(dim, cap):
        if dim % t == 0:
            best = t
        t += 128
    return best or dim


def _rtile(dim, cap):
    best = None
    t = 8
    while t <= min(dim, cap):
        if dim % t == 0:
            best = t
        t += 8
    return best or dim


def _split2(a):
    hi = a.astype(BF16)
    return hi, (a - hi.astype(F32)).astype(BF16)


def _raw_dot(a, b, ca, cb, prec):
    dims = (((ca,), (cb,)), ((), ()))
    mm = lambda p, q: lax.dot_general(p, q, dims, preferred_element_type=F32)
    if prec == "bf16":
        return mm(a.astype(BF16), b.astype(BF16))
    if prec == "x3":
        (ah, al), (bh, bl) = _split2(a), _split2(b)
        return mm(ah, bh) + (mm(ah, bl) + mm(al, bh))
    if prec == "mask":
        ab = a.astype(BF16)
        b1, b2 = _split2(b)
        b3 = (b - b1.astype(F32) - b2.astype(F32)).astype(BF16)
        return mm(ab, b1) + (mm(ab, b2) + mm(ab, b3))
    return lax.dot_general(a, b, dims, precision=HI, preferred_element_type=F32)


@functools.partial(jax.custom_vjp, nondiff_argnums=(2, 3, 4))
def _bf16_dot(a, b, ca, cb, prec):
    return _raw_dot(a, b, ca, cb, prec)


def _bf16_dot_fwd(a, b, ca, cb, prec):
    return _raw_dot(a, b, ca, cb, prec), (a, b)


def _bf16_dot_bwd(ca, cb, prec, res, g):
    a, b = res
    if prec == "mask":
        return jnp.zeros_like(a), _bf16_dot(a, g, 1 - ca, 0, prec)
    if ca == 1:
        da = _bf16_dot(g, b, 1, 1 - cb, prec)
    else:
        da = _bf16_dot(b, g, 1 - cb, 1, prec)
    if cb == 0:
        db = _bf16_dot(a, g, 1 - ca, 0, prec)
    else:
        db = _bf16_dot(g, a, 0, 1 - ca, prec)
    return da, db


_bf16_dot.defvjp(_bf16_dot_fwd, _bf16_dot_bwd)


def _dot(a, b, ca, cb, prec):
    return _raw_dot(a, b, ca, cb, prec) if prec == "hi" else _bf16_dot(a, b, ca, cb, prec)


def _sigmoid(z):
    return 1.0 / (1.0 + jnp.exp(-z))


def _softplus(z):
    return jnp.maximum(z, 0.0) + jnp.log(1.0 + jnp.exp(-jnp.abs(z)))


def _matmul(name, a, b, mode, out_dtypes, epilogue=None, extras=(), caps=(1024, 1024, 2048), after=None):
    if mode == "nn":
        (m, k), n = a.shape, b.shape[1]
    elif mode == "nt":
        (m, k), n = a.shape, b.shape[0]
    else:
        (k, m), n = a.shape, b.shape[1]
    tm, tn, tk = _tile(m, caps[0]), _tile(n, caps[1]), _tile(k, caps[2])
    nk = k // tk
    ne, no = len(extras), len(out_dtypes)
    ca, cb = {"nn": (1, 0), "nt": (1, 1), "tn": (0, 0)}[mode]

    n_after = 0 if after is None else 1

    def body(a_ref, b_ref, *rest):
        rest = rest[n_after:]
        extra_refs, out_refs, acc = rest[:ne], rest[ne:ne + no], rest[-1]
        kk = pl.program_id(2)
        part = _raw_dot(a_ref[...], b_ref[...], ca, cb, "bf16")

        def finish(total):
            res = epilogue(total, *[e[...] for e in extra_refs]) if epilogue else (total,)
            for o, v in zip(out_refs, res):
                o[...] = v.astype(o.dtype)

        if nk == 1:
            finish(part)
            return

        @pl.when(kk == 0)
        def _():
            acc[...] = part

        @pl.when((kk > 0) & (kk < nk - 1))
        def _():
            acc[...] += part

        @pl.when(kk == nk - 1)
        def _():
            finish(acc[...] + part)

    a_spec = (pl.BlockSpec((tk, tm), lambda i, j, kk: (kk, i)) if mode == "tn"
              else pl.BlockSpec((tm, tk), lambda i, j, kk: (i, kk)))
    b_spec = (pl.BlockSpec((tn, tk), lambda i, j, kk: (j, kk)) if mode == "nt"
              else pl.BlockSpec((tk, tn), lambda i, j, kk: (kk, j)))
    mn_spec = pl.BlockSpec((tm, tn), lambda i, j, kk: (i, j))
    after_specs = [pl.BlockSpec(memory_space=pl.ANY)] * n_after
    outs = pl.pallas_call(
        body, name=name, grid=(m // tm, n // tn, nk),
        in_specs=[a_spec, b_spec] + after_specs + [mn_spec] * ne,
        out_specs=[mn_spec] * no,
        out_shape=[jax.ShapeDtypeStruct((m, n), dt) for dt in out_dtypes],
        scratch_shapes=[pltpu.VMEM((tm, tn), F32)],
        compiler_params=_cparams(("parallel", "parallel", "arbitrary")),
    )(a, b, *([after] * n_after), *extras)
    return outs


def _rowwise(name, fn, rows, bcasts, out_rows, out_accs, tr, halos=()):
    t = rows[0].shape[0]
    nb = t // tr
    n_in = len(rows) + len(halos) + len(bcasts)
    n_ro = len(out_rows)

    def body(*refs):
        ins = [r[...] for r in refs[:n_in]]
        o_refs = refs[n_in:]
        i = pl.program_id(0)
        routs, aouts = fn(i, nb, *ins)
        for ref, v in zip(o_refs[:n_ro], routs):
            ref[...] = v.astype(ref.dtype)
        for ref, v in zip(o_refs[n_ro:], aouts):
            @pl.when(i == 0)
            def _(ref=ref):
                ref[...] = jnp.zeros_like(ref)
            ref[...] += v.reshape(ref.shape)

    in_specs = [pl.BlockSpec((tr, r.shape[1]), lambda i: (i, 0)) for r in rows]
    for arr, which in halos:
        if which == "prev":
            in_specs.append(pl.BlockSpec((8, arr.shape[1]), lambda i: (jnp.maximum(i * (tr // 8) - 1, 0), 0)))
        else:
            in_specs.append(pl.BlockSpec((8, arr.shape[1]),
                                         lambda i: (jnp.minimum((i + 1) * (tr // 8), t // 8 - 1), 0)))
    for bc in bcasts:
        in_specs.append(pl.BlockSpec(bc.shape, lambda i, nd=bc.ndim: (0,) * nd))
    out_specs = [pl.BlockSpec((tr, c), lambda i: (i, 0)) for c, _ in out_rows]
    out_specs += [pl.BlockSpec(s, lambda i, nd=len(s): (0,) * nd) for s in out_accs]
    out_shape = [jax.ShapeDtypeStruct((t, c), dt) for c, dt in out_rows]
    out_shape += [jax.ShapeDtypeStruct(s, F32) for s in out_accs]
    return pl.pallas_call(
        body, name=name, grid=(nb,), in_specs=in_specs, out_specs=out_specs, out_shape=out_shape,
        compiler_params=_cparams(("arbitrary",)),
    )(*rows, *[h[0] for h in halos], *bcasts)


def _shift_prev(x, halo, i):
    rolled = pltpu.roll(x, 1, 0)
    first = jnp.where(i == 0, 0.0, halo[7:8, :])
    row = lax.broadcasted_iota(jnp.int32, x.shape, 0)
    return jnp.where(row == 0, first, rolled)


def _shift_next(x, halo, i, nb):
    rolled = pltpu.roll(x, x.shape[0] - 1, 0)
    last = jnp.where(i == nb - 1, 0.0, halo[0:1, :])
    row = lax.broadcasted_iota(jnp.int32, x.shape, 0)
    return jnp.where(row == x.shape[0] - 1, last, rolled)


def _ln(x, g, b, eps=LN_EPS):
    mu = jnp.mean(x, axis=-1, keepdims=True)
    xc = x - mu
    var = jnp.mean(xc * xc, axis=-1, keepdims=True)
    return xc * lax.rsqrt(var + eps) * g + b


def _embed_math(x, g, b, sc, sh):
    x0 = _ln(x, g, b)
    return x0, x0 * (1.0 + sc) + sh


def _post_math(xin, y, gate, g, b, sc, sh):
    x1 = _ln(ALPHA * xin + (1.0 + gate) * y, g, b)
    return x1, x1 * (1.0 + sc) + sh


def _loss_math(xin, h, tgt, gate, g, b):
    x2 = _ln(ALPHA * xin + (1.0 + gate) * h, g, b)
    err = x2 - tgt
    return 0.5 * jnp.sum(jnp.mean(err * err, axis=-1))


def _rwkv_pre_math(r, k, v, xw, xa, xg, w0, wd, a0, wa, wg, k_k, k_a, seg, seg_t):
    wpre = -_softplus(-(w0 + _dot(jnp.tanh(xw), wd, 1, 0, "hi"))) - 0.5
    lw = -jnp.exp(wpre)
    a = _sigmoid(a0 + _dot(xa, wa, 1, 0, "hi"))
    g = _dot(_sigmoid(xg), wg, 1, 0, "hi")
    kk = k * k_k
    norm = jnp.sqrt(_dot(kk * kk, seg, 1, 0, "hi"))
    kkn = kk * _dot(1.0 / jnp.maximum(norm, 1e-12), seg_t, 1, 0, "hi")
    k2 = k * (1.0 + (a - 1.0) * k_a)
    return r, lw, k2, v, kkn, kkn * a, g


def _rwkv_post_math(y, r, k2, v, g, lnx_g, lnx_b, r_k, seg, seg_t):
    inv = 1.0 / HEAD
    mu = _dot(_dot(y, seg, 1, 0, "hi"), seg_t, 1, 0, "hi") * inv
    yc = y - mu
    var = _dot(_dot(yc * yc, seg, 1, 0, "hi"), seg_t, 1, 0, "hi") * inv
    yn = yc * lax.rsqrt(var + LNX_EPS) * lnx_g + lnx_b
    bonus = _dot(_dot(r * k2 * r_k, seg, 1, 0, "hi"), seg_t, 1, 0, "hi") * v
    return (yn + bonus) * g


def _chunk_math(s0, r, lw, k, v, kk, b, dot=_dot):
    n = len(r)
    hs = range(n)
    c = r[0].shape[0]
    ti = lax.broadcasted_iota(jnp.int32, (2 * c, 2 * c), 0)
    tj = lax.broadcasted_iota(jnp.int32, (2 * c, 2 * c), 1)
    tt, jj = ti & (c - 1), tj & (c - 1)
    quad = jnp.where(ti < c, (tt > jj).astype(F32), (tt >= jj).astype(F32))
    incl = quad[c:, :c]
    eye = (ti[:c, :c] == tj[:c, :c]).astype(F32)
    cl = [dot(incl, lw[i], 1, 0, "mask") for i in hs]
    ge = [jnp.exp(cl[i]) for i in hs]
    gi = [jnp.exp(-cl[i]) for i in hs]
    ar = [jnp.concatenate([-kk[i] * jnp.exp(cl[i] - lw[i]), r[i] * ge[i]], axis=0) for i in hs]
    kb = [jnp.concatenate([k[i] * gi[i], b[i] * gi[i]], axis=0) for i in hs]
    m = [dot(ar[i], kb[i], 1, 1, "x3") * quad for i in hs]
    ars0 = [dot(ar[i], s0[i], 1, 1, "x3") for i in hs]
    mv = [dot(m[i][:c, :c], v[i], 1, 0, "x3") for i in hs]
    pw = [m[i][:c, c:] for i in hs]
    inv = [eye + pw[i] for i in hs]
    for _ in range(int(math.log2(c)) - 1):
        pw = [dot(pw[i], pw[i], 1, 0, "x3") for i in hs]
        inv = [inv[i] + dot(inv[i], pw[i], 1, 0, "x3") for i in hs]
    u = [dot(inv[i], ars0[i][:c] + mv[i], 1, 0, "x3") for i in hs]
    vu = [jnp.concatenate([v[i], u[i]], axis=0) for i in hs]
    y = [ars0[i][c:] + dot(m[i][c:], vu[i], 1, 0, "x3") for i in hs]
    s1 = [(s0[i] + dot(vu[i], kb[i], 0, 0, "x3")) * ge[i][c - 1:c, :] for i in hs]
    return y, s1


def _attn_math(q, kp, kc, vp, vc, bias, sinks, first, dot=_dot):
    hq = q.shape[1] // HEAD
    hkv = kc.shape[1] // HEAD
    group = hq // hkv
    rows = group * ATT_BLOCK
    qi = lax.broadcasted_iota(jnp.int32, (rows, 2 * ATT_BLOCK), 0) & (ATT_BLOCK - 1)
    kj = lax.broadcasted_iota(jnp.int32, (rows, 2 * ATT_BLOCK), 1)
    dist = qi + ATT_BLOCK - kj
    valid = (dist >= 0) & (dist < ATT_BLOCK) & (jnp.logical_not(first) | (kj >= ATT_BLOCK))
    outs = []
    for j in range(hkv):
        heads = range(j * group, (j + 1) * group)
        kband = jnp.concatenate([kp[:, j * HEAD:(j + 1) * HEAD], kc[:, j * HEAD:(j + 1) * HEAD]], axis=0)
        vband = jnp.concatenate([vp[:, j * HEAD:(j + 1) * HEAD], vc[:, j * HEAD:(j + 1) * HEAD]], axis=0)
        qg = jnp.concatenate([q[:, h * HEAD:(h + 1) * HEAD] for h in heads], axis=0)
        bias_g = bias[j * group:(j + 1) * group].reshape(rows, 2 * ATT_BLOCK)
        sink = jnp.concatenate([jnp.broadcast_to(sinks[0:1, h:h + 1], (ATT_BLOCK, 1)) for h in heads], axis=0)
        s = dot(qg, kband, 1, 1, "bf16") * (HEAD ** -0.5) + bias_g
        s = jnp.where(valid, s, -1e30)
        m = jnp.maximum(jnp.max(s, axis=-1, keepdims=True), sink)
        e = jnp.exp(s - m)
        p = e / (jnp.sum(e, axis=-1, keepdims=True) + jnp.exp(sink - m))
        o = dot(p, vband, 1, 0, "bf16")
        outs += [o[g * ATT_BLOCK:(g + 1) * ATT_BLOCK] for g in range(group)]
    return jnp.concatenate(outs, axis=1)


def _rec_specs(t, da, gh, reverse):
    nc = t // CHUNK
    if reverse:
        return pl.BlockSpec((CHUNK, gh * HEAD), lambda hg, c: (nc - 1 - c, hg))
    return pl.BlockSpec((CHUNK, gh * HEAD), lambda hg, c: (c, hg))


def _rec_fwd(r, lw, k, v, kk, b):
    t, da = r.shape
    h = da // HEAD
    gh = min(REC_HEADS, h)
    nc = t // CHUNK

    def body(r_ref, lw_ref, k_ref, v_ref, kk_ref, b_ref, y_ref, s0_ref, state):
        @pl.when(pl.program_id(1) == 0)
        def _():
            state[...] = jnp.zeros_like(state)

        sls = [slice(i * HEAD, (i + 1) * HEAD) for i in range(gh)]
        heads = lambda ref: [ref[:, sl] for sl in sls]
        s0 = [state[i] for i in range(gh)]
        y, s1 = _chunk_math(s0, heads(r_ref), heads(lw_ref), heads(k_ref), heads(v_ref), heads(kk_ref),
                            heads(b_ref), dot=_raw_dot)
        for i, sl in enumerate(sls):
            s0_ref[0, i] = s0[i]
            y_ref[:, sl] = y[i]
            state[i] = s1[i]

    spec = _rec_specs(t, da, gh, False)
    return pl.pallas_call(
        body, name="rwkv_recurrence_fwd", grid=(h // gh, nc),
        in_specs=[spec] * 6,
        out_specs=[spec, pl.BlockSpec((1, gh, HEAD, HEAD), lambda hg, c: (c, hg, 0, 0))],
        out_shape=[jax.ShapeDtypeStruct((t, da), F32), jax.ShapeDtypeStruct((nc, h, HEAD, HEAD), F32)],
        scratch_shapes=[pltpu.VMEM((gh, HEAD, HEAD), F32)],
        compiler_params=_cparams(("parallel", "arbitrary")),
    )(r, lw, k, v, kk, b)


def _rec_bwd(r, lw, k, v, kk, b, s0s, dy):
    t, da = r.shape
    h = da // HEAD
    gh = min(REC_HEADS, h)
    nc = t // CHUNK

    def body(r_ref, lw_ref, k_ref, v_ref, kk_ref, b_ref, dy_ref, s0_ref,
             dr_ref, dlw_ref, dk_ref, dv_ref, dkk_ref, db_ref, dstate):
        @pl.when(pl.program_id(1) == 0)
        def _():
            dstate[...] = jnp.zeros_like(dstate)

        sls = [slice(i * HEAD, (i + 1) * HEAD) for i in range(gh)]
        heads = lambda ref: [ref[:, sl] for sl in sls]
        _, vjp = jax.vjp(_chunk_math, [s0_ref[0, i] for i in range(gh)], heads(r_ref), heads(lw_ref),
                         heads(k_ref), heads(v_ref), heads(kk_ref), heads(b_ref))
        grads = vjp((heads(dy_ref), [dstate[i] for i in range(gh)]))
        for i, sl in enumerate(sls):
            dstate[i] = grads[0][i]
            for ref, val in zip((dr_ref, dlw_ref, dk_ref, dv_ref, dkk_ref, db_ref), grads[1:]):
                ref[:, sl] = val[i]

    spec = _rec_specs(t, da, gh, True)
    return pl.pallas_call(
        body, name="rwkv_recurrence_bwd", grid=(h // gh, nc),
        in_specs=[spec] * 7 + [pl.BlockSpec((1, gh, HEAD, HEAD), lambda hg, c: (nc - 1 - c, hg, 0, 0))],
        out_specs=[spec] * 6,
        out_shape=[jax.ShapeDtypeStruct((t, da), F32)] * 6,
        scratch_shapes=[pltpu.VMEM((gh, HEAD, HEAD), F32)],
        compiler_params=_cparams(("parallel", "arbitrary")),
    )(r, lw, k, v, kk, b, dy, s0s)


def _attn_specs(t, hq_w, hkv_w):
    nb = t // ATT_BLOCK
    cur = lambda w: pl.BlockSpec((ATT_BLOCK, w), lambda n: (n, 0))
    prev = lambda w: pl.BlockSpec((ATT_BLOCK, w), lambda n: (jnp.maximum(n - 1, 0), 0))
    return nb, cur, prev


def _attn_fwd(q, kb, vb, bias, sinks):
    t, qw = q.shape
    kw = kb.shape[1]
    nb, cur, prev = _attn_specs(t, qw, kw)

    def body(q_ref, kp_ref, kc_ref, vp_ref, vc_ref, bias_ref, sink_ref, o_ref):
        first = pl.program_id(0) == 0
        o = _attn_math(q_ref[...], kp_ref[...], kc_ref[...], vp_ref[...], vc_ref[...],
                       bias_ref[...], sink_ref[...], first, dot=_raw_dot)
        o_ref[...] = o.astype(o_ref.dtype)

    full = lambda a: pl.BlockSpec(a.shape, lambda n, nd=a.ndim: (0,) * nd)
    return pl.pallas_call(
        body, name="swa_attention_fwd", grid=(nb,),
        in_specs=[cur(qw), prev(kw), cur(kw), prev(kw), cur(kw), full(bias), full(sinks)],
        out_specs=cur(qw), out_shape=jax.ShapeDtypeStruct((t, qw), BF16),
        compiler_params=_cparams(("parallel",)),
    )(q, kb, kb, vb, vb, bias, sinks)


def _attn_bwd(q, kb, vb, bias, sinks, do, col_block):
    t, qw = q.shape
    kw = kb.shape[1]
    nb, cur, prev = _attn_specs(t, qw, kw)

    def body(q_ref, kp_ref, kc_ref, vp_ref, vc_ref, bias_ref, sink_ref, do_ref,
             dq_ref, dkp_ref, dkc_ref, dvp_ref, dvc_ref, dbias_ref, dsink_ref):
        n = pl.program_id(0)
        first = n == 0
        fn = functools.partial(_attn_math, first=first)
        _, vjp = jax.vjp(fn, q_ref[...], kp_ref[...], kc_ref[...], vp_ref[...], vc_ref[...],
                         bias_ref[...], sink_ref[...])
        dq, dkp, dkc, dvp, dvc, dbias, dsink = vjp(do_ref[...].astype(F32))
        dq_ref[...] = dq.astype(dq_ref.dtype)
        dkp_ref[...] = dkp
        dkc_ref[...] = dkc
        dvp_ref[...] = dvp
        dvc_ref[...] = dvc

        @pl.when(first)
        def _():
            dbias_ref[...] = jnp.zeros_like(dbias_ref)
            dsink_ref[...] = jnp.zeros_like(dsink_ref)

        dbias_ref[...] += dbias
        dsink_ref[...] += dsink

    full = lambda a: pl.BlockSpec(a.shape, lambda n, nd=a.ndim: (0,) * nd)
    kshape = jax.ShapeDtypeStruct((t, kw), F32)
    return pl.pallas_call(
        body, name="swa_attention_bwd", grid=(nb,),
        in_specs=[cur(qw), prev(kw), cur(kw), prev(kw), cur(kw), full(bias), full(sinks),
                  pl.BlockSpec((ATT_BLOCK, qw), lambda n: (n, col_block))],
        out_specs=[cur(qw), cur(kw), cur(kw), cur(kw), cur(kw), full(bias), full(sinks)],
        out_shape=[jax.ShapeDtypeStruct((t, qw), BF16), kshape, kshape, kshape, kshape,
                   jax.ShapeDtypeStruct(bias.shape, F32), jax.ShapeDtypeStruct(sinks.shape, F32)],
        compiler_params=_cparams(("arbitrary",)),
    )(q, kb, kb, vb, vb, bias, sinks, do)


def _bucket_onehot():
    qi = jnp.arange(ATT_BLOCK)[:, None]
    kj = jnp.arange(2 * ATT_BLOCK)[None, :]
    n = jnp.maximum(qi + ATT_BLOCK - kj, 0)
    buckets, max_exact = 32, 16
    nf = jnp.maximum(n, 1).astype(F32)
    large = max_exact + (jnp.log(nf / max_exact) / math.log(RPB_MAX_DIST / max_exact)
                         * (buckets - max_exact)).astype(jnp.int32)
    bucket = jnp.where(n < max_exact, n, jnp.minimum(large, buckets - 1)).reshape(-1)
    return (bucket[None, :] == jnp.arange(buckets)[:, None]).astype(F32)


def _small_dot(name, a, b, ca, cb):
    m = a.shape[1 - ca]
    n = b.shape[1 - cb]

    def body(a_ref, b_ref, o_ref):
        o_ref[...] = _raw_dot(a_ref[...], b_ref[...], ca, cb, "hi")

    return pl.pallas_call(body, name=name, out_shape=jax.ShapeDtypeStruct((m, n), F32),
                          compiler_params=_cparams())(a, b)


def _mod_fwd(c_all, w_mod):
    d, n = w_mod.shape
    tn = _tile(n, 512)

    def body(c_ref, w_ref, o_ref, cond_ref):
        cv = c_ref[...]
        cond = cv * _sigmoid(cv)
        cond_ref[...] = cond
        o_ref[...] = _raw_dot(cond, w_ref[...], 1, 0, "hi")

    return pl.pallas_call(
        body, name="adaln_mod_fwd", grid=(n // tn,),
        in_specs=[pl.BlockSpec(c_all.shape, lambda j: (0, 0)), pl.BlockSpec((d, tn), lambda j: (0, j))],
        out_specs=[pl.BlockSpec((c_all.shape[0], tn), lambda j: (0, j)),
                   pl.BlockSpec(c_all.shape, lambda j: (0, 0))],
        out_shape=[jax.ShapeDtypeStruct((c_all.shape[0], n), F32), jax.ShapeDtypeStruct(c_all.shape, F32)],
        compiler_params=_cparams(("arbitrary",)),
    )(c_all, w_mod)


def _adam_math(w, g, m, v):
    m = ADAM_B1 * m + (1.0 - ADAM_B1) * g
    v = ADAM_B2 * v + (1.0 - ADAM_B2) * (g * g)
    m_hat = m / (1.0 - ADAM_B1 ** ADAM_STEP)
    v_hat = v / (1.0 - ADAM_B2 ** ADAM_STEP)
    delta = -ADAM_LR * (m_hat / (jnp.sqrt(v_hat) + ADAM_EPS) + ADAM_WD * w)
    return delta, m, v


def _adamw(name, w, m, v, gparts):
    r, c = w.shape
    p = gparts.shape[0]
    tr = _rtile(r, max(8, (1 << 18) // max(c, 1) // 8 * 8))

    def body(w_ref, m_ref, v_ref, g_ref, go_ref, d_ref, mo_ref, vo_ref):
        g = g_ref[0].astype(F32)
        for s in range(1, p):
            g = g + g_ref[s].astype(F32)
        delta, mn, vn = _adam_math(w_ref[...], g, m_ref[...], v_ref[...])
        go_ref[...] = g
        d_ref[...] = delta
        mo_ref[...] = mn
        vo_ref[...] = vn

    spec = pl.BlockSpec((tr, c), lambda i: (i, 0))
    return pl.pallas_call(
        body, name=name, grid=(r // tr,),
        in_specs=[spec, spec, spec, pl.BlockSpec((p, tr, c), lambda i: (0, i, 0))],
        out_specs=[spec] * 4, out_shape=[jax.ShapeDtypeStruct((r, c), F32)] * 4,
        compiler_params=_cparams(("parallel",)),
    )(w, m, v, gparts)


def _adamw_outer(name, w, m, v, cond_t, dmod):
    d, n = w.shape
    tr, tn = _rtile(d, 512), _tile(n, 1024)

    def body(w_ref, m_ref, v_ref, c_ref, dm_ref, go_ref, d_ref, mo_ref, vo_ref):
        g = _raw_dot(c_ref[...], dm_ref[...], 1, 0, "hi")
        delta, mn, vn = _adam_math(w_ref[...], g, m_ref[...], v_ref[...])
        go_ref[...] = g
        d_ref[...] = delta
        mo_ref[...] = mn
        vo_ref[...] = vn

    spec = pl.BlockSpec((tr, tn), lambda i, j: (i, j))
    return pl.pallas_call(
        body, name=name, grid=(d // tr, n // tn),
        in_specs=[spec, spec, spec, pl.BlockSpec((tr, cond_t.shape[1]), lambda i, j: (i, 0)),
                  pl.BlockSpec((dmod.shape[0], tn), lambda i, j: (0, j))],
        out_specs=[spec] * 4, out_shape=[jax.ShapeDtypeStruct((d, n), F32)] * 4,
        compiler_params=_cparams(("parallel", "parallel")),
    )(w, m, v, cond_t, dmod)


def _exchange(name, arrays, scatter):
    n = len(arrays)
    if scatter:
        out_shape = [jax.ShapeDtypeStruct(a.shape, a.dtype) for a in arrays]
    else:
        out_shape = [jax.ShapeDtypeStruct((N_DEV,) + a.shape, a.dtype) for a in arrays]

    def body(*refs):
        ins, outs = refs[:n], refs[n:2 * n]
        send_sems, recv_sems, local_sems = refs[2 * n:]
        x, y, c = lax.axis_index("x"), lax.axis_index("y"), lax.axis_index("c")
        me = 4 * x + 2 * y + c

        def peer(p):
            px, py, pc = x ^ ((p >> 2) & 1), y ^ ((p >> 1) & 1), c ^ (p & 1)
            return (px, py, pc), 4 * px + 2 * py + pc

        def remote(a, p):
            dev, idx = peer(p)
            src = ins[a].at[idx] if scatter else ins[a]
            return pltpu.make_async_remote_copy(
                src_ref=src, dst_ref=outs[a].at[me], send_sem=send_sems.at[a, p - 1],
                recv_sem=recv_sems.at[a, p - 1], device_id=dev, device_id_type=pl.DeviceIdType.MESH)

        def arrival(a, p):
            dev, idx = peer(p)
            src = ins[a].at[idx] if scatter else ins[a]
            return pltpu.make_async_remote_copy(
                src_ref=src, dst_ref=outs[a].at[idx], send_sem=send_sems.at[a, p - 1],
                recv_sem=recv_sems.at[a, p - 1], device_id=dev, device_id_type=pl.DeviceIdType.MESH)

        locals_ = []
        for a in range(n):
            src = ins[a].at[me] if scatter else ins[a]
            cp = pltpu.make_async_copy(src, outs[a].at[me], local_sems.at[a])
            cp.start()
            locals_.append(cp)
        sends = [remote(a, p) for a in range(n) for p in range(1, N_DEV)]
        for cp in sends:
            cp.start()
        for a in range(n):
            for p in range(1, N_DEV):
                arrival(a, p).wait_recv()
        for cp in sends:
            cp.wait_send()
        for cp in locals_:
            cp.wait()

    any_spec = pl.BlockSpec(memory_space=pl.ANY)
    return pl.pallas_call(
        body, name=name, in_specs=[any_spec] * n, out_specs=[any_spec] * n, out_shape=out_shape,
        scratch_shapes=[pltpu.SemaphoreType.DMA((n, N_DEV - 1)), pltpu.SemaphoreType.DMA((n, N_DEV - 1)),
                        pltpu.SemaphoreType.DMA((n,))],
    )(*arrays)


def _peer(p):
    x, y, c = lax.axis_index("x"), lax.axis_index("y"), lax.axis_index("c")
    px, py, pc = x ^ ((p >> 2) & 1), y ^ ((p >> 1) & 1), c ^ (p & 1)
    return (px, py, pc), 4 * px + 2 * py + pc


def _split_copy(src_ref, land_ref, send_sems, recv_sems, p, scatter, arriving, cols=None):
    x, y, c = lax.axis_index("x"), lax.axis_index("y"), lax.axis_index("c")
    me = 4 * x + 2 * y + c
    dev, idx = _peer(p)
    block = lambda ref, d: ref.at[:, pl.ds(pl.multiple_of(d * cols, cols), cols)] if cols else ref.at[d]
    if scatter:
        src, dst = block(src_ref, idx), land_ref.at[idx if arriving else me]
    else:
        src, dst = src_ref, block(land_ref, idx if arriving else me)
    return pltpu.make_async_remote_copy(
        src_ref=src, dst_ref=dst, send_sem=send_sems.at[p - 1], recv_sem=recv_sems.at[p - 1], device_id=dev,
        device_id_type=pl.DeviceIdType.MESH)


_HBM_SPEC = pl.BlockSpec(memory_space=pltpu.HBM)
_SEM_SPEC = pl.BlockSpec(memory_space=pltpu.SEMAPHORE)
_DATAFLOW = pltpu.SideEffectType.DATAFLOW_SIDE_EFFECTING


def _exchange_start(name, srcs, lands, scatter, after, cols=None):
    n = len(srcs)
    cols = cols or [None] * n

    def body(*refs):
        src_refs, land_refs = refs[:n], refs[n:2 * n]
        outs = refs[2 * n + 1:]
        send, recv, token = outs[:n], outs[n:2 * n], outs[-1]
        for a in range(n):
            for p in range(1, N_DEV):
                _split_copy(src_refs[a], land_refs[a], send[a], recv[a], p, scatter, False, cols[a]).start()
        token[...] = jnp.zeros_like(token)

    sems = [pltpu.SemaphoreType.DMA((N_DEV - 1,))] * (2 * n)
    hbm = [pltpu.HBM(a.shape, a.dtype) for a in list(srcs) + list(lands)]
    res = pl.pallas_call(
        body, name=name,
        out_shape=sems + hbm + [jax.ShapeDtypeStruct((8, 128), F32)],
        in_specs=[_HBM_SPEC] * (2 * n) + [pl.BlockSpec(memory_space=pl.ANY)],
        out_specs=[_SEM_SPEC] * (2 * n) + [_HBM_SPEC] * (2 * n) + [pl.BlockSpec(memory_space=pltpu.VMEM)],
        input_output_aliases={i: 2 * n + i for i in range(2 * n)},
        compiler_params=pltpu.CompilerParams(has_side_effects=_DATAFLOW),
    )(*[pltpu.with_memory_space_constraint(a, pltpu.HBM) for a in list(srcs) + list(lands)], after)
    return res[:n], res[n:2 * n], res[2 * n:3 * n], res[3 * n:4 * n], res[-1]


def _exchange_wait(name, send_sem, recv_sem, src, land, scatter, after, cols=None):
    def body(src_ref, land_ref, send, recv, after_ref, src_out, land_out):
        for p in range(1, N_DEV):
            cp = _split_copy(src_ref, land_ref, send, recv, p, scatter, True, cols)
            cp.wait_send()
            cp.wait_recv()

    return pl.pallas_call(
        body, name=name,
        out_shape=(pltpu.HBM(src.shape, src.dtype), pltpu.HBM(land.shape, land.dtype)),
        in_specs=[_HBM_SPEC, _HBM_SPEC, _SEM_SPEC, _SEM_SPEC, pl.BlockSpec(memory_space=pl.ANY)],
        out_specs=(_HBM_SPEC, _HBM_SPEC), input_output_aliases={0: 0, 1: 1},
        compiler_params=pltpu.CompilerParams(has_side_effects=_DATAFLOW),
    )(src, land, send_sem, recv_sem, after)[1]


def _own_row(block, me):
    land = lax.empty((N_DEV,) + block.shape, block.dtype)
    return lax.dynamic_update_index_in_dim(land, block, me, 0)


def _cols_to_shards(a):
    r, c = a.shape
    return a.reshape(r, N_DEV, c // N_DEV).transpose(1, 0, 2)


def _shards_to_cols(a):
    d, r, n = a.shape
    return a.transpose(1, 0, 2).reshape(r, d * n)


def kernel(x, c, ln_emb_g, ln_emb_b, rpb_table, w_mod, b_mod, w_in, mu_shift, w0, w_decay_up, a0, w_iclr_up, w_gate_up, k_k, k_a, r_k, lnx_g, lnx_b, attn_sinks, w_out, ln1_g, ln1_b, w_up, w_down, ln2_g, ln2_b, loss_target, m_ln_emb_g, m_ln_emb_b, m_rpb_table, m_w_mod, m_b_mod, m_w_in, m_mu_shift, m_w0, m_w_decay_up, m_a0, m_w_iclr_up, m_w_gate_up, m_k_k, m_k_a, m_r_k, m_lnx_g, m_lnx_b, m_attn_sinks, m_w_out, m_ln1_g, m_ln1_b, m_w_up, m_w_down, m_ln2_g, m_ln2_b, v_ln_emb_g, v_ln_emb_b, v_rpb_table, v_w_mod, v_b_mod, v_w_in, v_mu_shift, v_w0, v_w_decay_up, v_a0, v_w_iclr_up, v_w_gate_up, v_k_k, v_k_a, v_r_k, v_lnx_g, v_lnx_b, v_attn_sinks, v_w_out, v_ln1_g, v_ln1_b, v_w_up, v_w_down, v_ln2_g, v_ln2_b):
    names = ["ln_emb_g", "ln_emb_b", "rpb_table", "w_mod", "b_mod", "w_in", "mu_shift", "w0", "w_decay_up",
             "a0", "w_iclr_up", "w_gate_up", "k_k", "k_a", "r_k", "lnx_g", "lnx_b", "attn_sinks", "w_out",
             "ln1_g", "ln1_b", "w_up", "w_down", "ln2_g", "ln2_b"]
    env = dict(locals())
    weights = {nm: env[nm] for nm in names}
    mom_m = {nm: env["m_" + nm] for nm in names}
    mom_v = {nm: env["v_" + nm] for nm in names}

    t, d = x.shape[1], x.shape[2]
    da = d // 2
    h_a = da // HEAD
    hq = (d - da) // HEAD
    hkv = hq // GQA_RATIO
    l_w, l_a, l_g = w_decay_up.shape[1], w_iclr_up.shape[1], w_gate_up.shape[1]
    o_w, o_a, o_g = 3 * da, 3 * da + l_w, 3 * da + l_w + l_a
    n_rwkv = o_g + l_g
    o_kb, o_vb = n_rwkv + hq * HEAD, n_rwkv + hq * HEAD + hkv * HEAD
    me = 4 * lax.axis_index("x") + 2 * lax.axis_index("y") + lax.axis_index("c")

    x2d, tgt = x[0], loss_target[0]
    row = lambda a: a.reshape(1, -1)
    seg = (jnp.arange(da)[:, None] // HEAD == jnp.arange(h_a)[None, :]).astype(F32)
    seg_t = seg.T

    (c_all,) = _exchange("gather_cond", [c], False)
    c_all = c_all.reshape(N_DEV, d)
    mod_rows, cond_all = _mod_fwd(c_all, w_mod[0])
    gathered = _exchange("gather_weights", [
        mod_rows, w_in[0].astype(BF16), w_decay_up[0], w_iclr_up[0], w_gate_up[0]], False)
    mod_all, win_g, wd_g, wa_g, wg_g = gathered
    late = [w_out[0].astype(BF16), w_up[0].astype(BF16), w_down[0].astype(BF16)]
    n_up = w_up.shape[2]
    late_cols = [None, n_up, None]
    up_land = lax.dynamic_update_slice_in_dim(lax.empty((d, N_DEV * n_up), BF16), late[1], me * n_up, axis=1)
    late_send, late_recv, late_src, late_land, late_token = _exchange_start(
        "gather_late_weights_start", late, [_own_row(late[0], me), up_land, _own_row(late[2], me)], False,
        mod_all, late_cols)
    mod = lax.dynamic_index_in_dim(mod_all, me, axis=1, keepdims=False).reshape(1, -1) + b_mod
    mod = mod + late_token[0, 0]
    sh1, sc1, g1, sh2, sc2, g2 = [mod[:, i * d:(i + 1) * d] for i in range(6)]
    w_in_f = _shards_to_cols(win_g)
    wd_f, wa_f, wg_f = _shards_to_cols(wd_g), _shards_to_cols(wa_g), _shards_to_cols(wg_g)

    def late_weight(i, nm, after):
        return _exchange_wait("gather_" + nm + "_wait", late_send[i], late_recv[i], late_src[i], late_land[i],
                              False, after, late_cols[i])

    tr = _rtile(t, 256)
    lng, lnb = row(ln_emb_g), row(ln_emb_b)

    def embed_fn(i, nb, xb, g, b, sc, sh):
        return _embed_math(xb, g, b, sc, sh), ()
    x0, u1 = _rowwise("embed_ln_mod", embed_fn, [x2d], [lng, lnb, sc1, sh1], [(d, F32), (d, BF16)], [], tr)

    (p,) = _matmul("in_proj", u1, w_in_f, "nn", [F32])
    p_rkv, p_w, p_a, p_g = p[:, :o_w], p[:, o_w:o_a], p[:, o_a:o_g], p[:, o_g:n_rwkv]
    q, kb, vb = p[:, n_rwkv:o_kb], p[:, o_kb:o_vb], p[:, o_vb:]
    mu_rkv, mu_w, mu_a, mu_g = (mu_shift[:, :o_w], mu_shift[:, o_w:o_a], mu_shift[:, o_a:o_g],
                                mu_shift[:, o_g:n_rwkv])
    pre_params = [w0, wd_f, a0, wa_f, wg_f, k_k, k_a, seg, seg_t]
    tr_pre = _rtile(t, 128)

    def shifted(i, blocks, halos, mus):
        return [xb + (_shift_prev(xb, hb, i) - xb) * mb for xb, hb, mb in zip(blocks, halos, mus)]

    def split3(a):
        return a[:, :da], a[:, da:2 * da], a[:, 2 * da:]

    def pre_fn(i, nb, b_rkv, b_w, b_a, b_g, h_rkv, h_w, h_a_, h_g, m_rkv, m_w, m_a, m_g, *params):
        s_rkv, s_w, s_a, s_g = shifted(i, [b_rkv, b_w, b_a, b_g], [h_rkv, h_w, h_a_, h_g],
                                       [m_rkv, m_w, m_a, m_g])
        return _rwkv_pre_math(*split3(s_rkv), s_w, s_a, s_g, *params), ()

    pre_rows = [p_rkv, p_w, p_a, p_g]
    pre_halos = [(a, "prev") for a in pre_rows]
    r_, lw_, k2_, v_, kk_, b_, gate_ = _rowwise(
        "rwkv_pre", pre_fn, pre_rows, [mu_rkv, mu_w, mu_a, mu_g] + pre_params,
        [(da, F32)] * 7, [], tr_pre, halos=pre_halos)

    y_rec, s0s = _rec_fwd(r_, lw_, k2_, v_, kk_, b_)

    rk_flat = r_k.reshape(1, da)
    post_params = [lnx_g, lnx_b, rk_flat, seg, seg_t]

    def post_fn(i, nb, yb, rb, kb_, vb_, gb, *params):
        return (_rwkv_post_math(yb, rb, kb_, vb_, gb, *params),), ()
    (ya,) = _rowwise("rwkv_post", post_fn, [y_rec, r_, k2_, v_, gate_], post_params, [(da, BF16)], [], tr_pre)

    onehot = _bucket_onehot()
    bias = _small_dot("rpb_gather", rpb_table, onehot, 0, 0)
    bias = bias.reshape(hq, ATT_BLOCK, 2 * ATT_BLOCK)
    yb = _attn_fwd(q, kb, vb, bias, attn_sinks)

    mix_in = jnp.concatenate([ya, yb], axis=1)
    w_out_f = late_weight(0, "w_out", mix_in).reshape(d, d)
    (mix,) = _matmul("out_proj", mix_in, w_out_f, "nn", [F32])

    def post1_fn(i, nb, xin, yv, gate, g, b, sc, sh):
        return _post_math(xin, yv, gate, g, b, sc, sh), ()
    x1, u2 = _rowwise("ln1_mod", post1_fn, [x0, mix], [g1, ln1_g, ln1_b, sc2, sh2],
                      [(d, F32), (d, BF16)], [], tr)

    def relu2(acc):
        rl = jnp.maximum(acc, 0.0)
        return acc, rl * rl
    w_up_f = late_weight(1, "w_up", u2)
    hpre, hact = _matmul("mlp_up", u2, w_up_f, "nn", [F32, BF16], epilogue=relu2)
    w_down_f = late_weight(2, "w_down", hact).reshape(-1, d)
    (hmlp,) = _matmul("mlp_down", hact, w_down_f, "nn", [F32])

    def loss_fn(i, nb, xin, hv, tg, gate, g, b):
        val, vjp = jax.vjp(_loss_math, xin, hv, tg, gate, g, b)
        dxin, dh, _, dgate, dg, db = vjp(jnp.ones((), F32))
        return (dxin, dh), (val, dgate, dg, db)
    dx1, dh, loss_acc, dg2, dln2g, dln2b = _rowwise(
        "ln2_loss", loss_fn, [x1, hmlp, tgt], [g2, ln2_g, ln2_b], [(d, F32), (d, BF16)],
        [(1, 1), (1, d), (1, d), (1, d)], _rtile(t, 128))

    def drelu2(acc, hp):
        return (acc * 2.0 * jnp.maximum(hp, 0.0),)
    (dhpre,) = _matmul("mlp_down_dgrad", dh, w_down_f, "nt", [BF16], epilogue=drelu2, extras=[hpre])
    (gw_down,) = _matmul("mlp_down_wgrad", hact, dh, "tn", [BF16])
    (du2,) = _matmul("mlp_up_dgrad", dhpre, w_up_f, "nt", [F32])
    (gw_up,) = _matmul("mlp_up_wgrad", u2, dhpre, "tn", [BF16])

    def own_slice(src):
        return _own_row(lax.dynamic_index_in_dim(src, me, 0, keepdims=False), me)
    mlp_src = [gw_down.reshape(N_DEV, -1, d), gw_up]
    mlp_cols = [None, n_up]
    up_own = _own_row(lax.dynamic_slice_in_dim(gw_up, me * n_up, n_up, axis=1), me)
    mlp_send, mlp_recv, mlp_src, mlp_land, mlp_token = _exchange_start(
        "scatter_mlp_grads_start", mlp_src, [own_slice(mlp_src[0]), up_own], True, gw_up, mlp_cols)

    def post1_bwd(i, nb, xin, yv, dx1v, du2v, gate, g, b, sc, sh):
        _, vjp = jax.vjp(_post_math, xin, yv, gate, g, b, sc, sh)
        dxin, dy, dgate, dg, db, dsc, dsh = vjp((dx1v, du2v))
        return (dxin, dy), (dgate, dg, db, dsc, dsh)
    dx0, dmix, dg1, dln1g, dln1b, dsc2, dsh2 = _rowwise(
        "ln1_mod_bwd", post1_bwd, [x0, mix, dx1, du2], [g1 + mlp_token[0, 0], ln1_g, ln1_b, sc2, sh2],
        [(d, F32), (d, BF16)], [(1, d)] * 5, _rtile(t, 128))

    (dmix_in,) = _matmul("out_proj_dgrad", dmix, w_out_f, "nt", [F32])
    (gw_out,) = _matmul("out_proj_wgrad", mix_in, dmix, "tn", [BF16])
    dya = dmix_in[:, :da]
    out_src = [gw_out.reshape(N_DEV, d // N_DEV, d)]
    out_send, out_recv, out_src, out_land, out_token = _exchange_start(
        "scatter_out_grad_start", out_src, [own_slice(a) for a in out_src], True, gw_out)
    post_params_bwd = [lnx_g, lnx_b, rk_flat + out_token[0, 0], seg, seg_t]

    def post_bwd(i, nb, yb_, rb, kb_, vb_, gb, dyab, *params):
        _, vjp = jax.vjp(_rwkv_post_math, yb_, rb, kb_, vb_, gb, *params)
        dy, dr, dk, dv, dg, dlg, dlb, drk, _, _ = vjp(dyab)
        return (dy, dr, dk, dv, dg), (dlg, dlb, drk)
    dy_rec, dr_e, dk_e, dv_e, dgate, dlnxg, dlnxb, drk = _rowwise(
        "rwkv_post_bwd", post_bwd, [y_rec, r_, k2_, v_, gate_, dya], post_params_bwd,
        [(da, F32)] * 5, [(1, da)] * 3, tr_pre)

    dr_r, dlw_r, dk_r, dv_r, dkk_r, db_r = _rec_bwd(r_, lw_, k2_, v_, kk_, b_, s0s, dy_rec)

    def pre_bwd(i, nb, b_rkv, b_w, b_a, b_g, dr1, dr2, dlw, dk1, dk2, dv1, dv2, dkk, dbb, dgt,
                h_rkv, h_w, h_a_, h_g, m_rkv, m_w, m_a, m_g, *params):
        blocks = [b_rkv, b_w, b_a, b_g]
        prevs = [_shift_prev(xb, hb, i) for xb, hb in zip(blocks, [h_rkv, h_w, h_a_, h_g])]
        mus = [m_rkv, m_w, m_a, m_g]
        s_rkv, s_w, s_a, s_g = [xb + (pb - xb) * mb for xb, pb, mb in zip(blocks, prevs, mus)]
        _, vjp = jax.vjp(_rwkv_pre_math, *split3(s_rkv), s_w, s_a, s_g, *params)
        grads = vjp((dr1 + dr2, dlw, dk1 + dk2, dv1 + dv2, dkk, dbb, dgt))
        g_rkv = jnp.concatenate(grads[:3], axis=1)
        g_w, g_a, g_g = grads[3:6]
        dmu = [jnp.sum(gs * (pb - xb), axis=0, keepdims=True)
               for gs, pb, xb in zip([g_rkv, g_w, g_a, g_g], prevs, blocks)]
        dw0, dwd, da0, dwa, dwg, dkk_, dka = grads[6:13]
        return (g_rkv, g_w, g_a, g_g), (*dmu, dw0, dwd, da0, dwa, dwg, dkk_, dka)

    pre_out = _rowwise(
        "rwkv_pre_bwd", pre_bwd,
        pre_rows + [dr_r, dr_e, dlw_r, dk_r, dk_e, dv_r, dv_e, dkk_r, db_r, dgate],
        [mu_rkv, mu_w, mu_a, mu_g] + pre_params,
        [(o_w, F32), (l_w, F32), (l_a, F32), (l_g, F32)],
        [(1, o_w), (1, l_w), (1, l_a), (1, l_g), (1, da), (l_w, da), (1, da), (l_a, da), (l_g, da),
         (1, da), (1, da)],
        _rtile(t, 64), halos=pre_halos)
    gs_rkv, gs_w, gs_a, gs_g = pre_out[:4]
    dmu_parts = pre_out[4:8]
    dw0, dwd, da0, dwa, dwg, dk_k, dk_a = pre_out[8:]

    def unshift_fn(i, nb, a1, a2, a3, a4, n1, n2, n3, n4, m1, m2, m3, m4):
        outs = [gs * (1.0 - mb) + _shift_next(gs * mb, hb * mb, i, nb)
                for gs, hb, mb in zip([a1, a2, a3, a4], [n1, n2, n3, n4], [m1, m2, m3, m4])]
        return outs, ()
    gs_list = [gs_rkv, gs_w, gs_a, gs_g]
    dp_rkv, dp_w, dp_a, dp_g = _rowwise(
        "token_shift_bwd", unshift_fn, gs_list, [mu_rkv, mu_w, mu_a, mu_g],
        [(o_w, BF16), (l_w, BF16), (l_a, BF16), (l_g, BF16)], [], tr_pre,
        halos=[(a, "next") for a in gs_list])

    dq, dkp, dkc, dvp, dvc, dbias, dsinks = _attn_bwd(q, kb, vb, bias, attn_sinks, dmix_in, 1)
    zpad = jnp.zeros((ATT_BLOCK, kb.shape[1]), F32)
    dkb = (dkc + jnp.concatenate([dkp[ATT_BLOCK:], zpad], axis=0)).astype(BF16)
    dvb = (dvc + jnp.concatenate([dvp[ATT_BLOCK:], zpad], axis=0)).astype(BF16)
    d_rpb = _small_dot("rpb_scatter", onehot, dbias.reshape(hq, -1), 1, 1)

    dp = jnp.concatenate([dp_rkv, dp_w, dp_a, dp_g, dq, dkb, dvb], axis=1)
    (gw_in,) = _matmul("in_proj_wgrad", u1, dp, "tn", [BF16])
    in_names = ["w_in", "w_decay_up", "w_iclr_up", "w_gate_up"]
    in_src = [_cols_to_shards(gw_in), _cols_to_shards(dwd), _cols_to_shards(dwa), _cols_to_shards(dwg)]
    in_send, in_recv, in_src, in_land, in_token = _exchange_start(
        "scatter_in_grads_start", in_src, [own_slice(a) for a in in_src], True, gw_in)
    (du1,) = _matmul("in_proj_dgrad", dp, w_in_f, "nt", [F32], after=in_token)

    def embed_bwd(i, nb, xb, dx0v, du1v, g, b, sc, sh):
        _, vjp = jax.vjp(_embed_math, xb, g, b, sc, sh)
        dxv, dg, db, dsc, dsh = vjp((dx0v, du1v))
        return (dxv,), (dg, db, dsc, dsh)
    grad_x, dlng, dlnb, dsc1, dsh1 = _rowwise(
        "embed_ln_mod_bwd", embed_bwd, [x2d, dx0, du1], [lng, lnb, sc1, sh1], [(d, F32)], [(1, d)] * 4,
        _rtile(t, 128))

    dmod = jnp.concatenate([dsh1, dsc1, dg1, dsh2, dsc2, dg2], axis=1)
    small = {"ln_emb_g": dlng, "ln_emb_b": dlnb, "rpb_table": d_rpb, "b_mod": dmod,
             "mu_shift": jnp.concatenate(dmu_parts, axis=1), "w0": dw0, "a0": da0, "k_k": dk_k, "k_a": dk_a,
             "r_k": drk, "lnx_g": dlnxg, "lnx_b": dlnxb, "attn_sinks": dsinks, "ln1_g": dln1g, "ln1_b": dln1b,
             "ln2_g": dln2g, "ln2_b": dln2b}
    small_names = list(small)
    packed = jnp.concatenate([small[nm].reshape(1, -1) for nm in small_names], axis=1)
    (packed_all,) = _exchange("gather_small_grads", [packed], False)
    packed_all = packed_all.reshape(N_DEV, -1)

    grads, deltas, new_m, new_v = {}, {}, {}, {}

    def put(nm, res):
        shape = weights[nm].shape
        grads[nm], deltas[nm], new_m[nm], new_v[nm] = [a.reshape(shape) for a in res]

    def big_update(nm, parts):
        put(nm, _adamw("adamw_" + nm, weights[nm][0], mom_m[nm][0], mom_v[nm][0], parts))

    n_mod = w_mod.shape[2]
    dmod_cols = lax.dynamic_slice_in_dim(packed_all[:, _offset(small, small_names, "b_mod"):], me * n_mod, n_mod,
                                         axis=1)
    put("w_mod", _adamw_outer("adamw_w_mod", w_mod[0], m_w_mod[0], v_w_mod[0], cond_all.T, dmod_cols))

    off = 0
    for nm in small_names:
        size = small[nm].size
        wshape = weights[nm].shape
        two_d = (1, size) if nm != "rpb_table" else wshape
        parts = packed_all[:, off:off + size].reshape((N_DEV,) + two_d)
        off += size
        put(nm, _adamw("adamw_" + nm, weights[nm].reshape(two_d), mom_m[nm].reshape(two_d),
                       mom_v[nm].reshape(two_d), parts))

    behind = deltas["w_mod"]
    big_update("w_down", _exchange_wait("scatter_w_down_wait", mlp_send[0], mlp_recv[0], mlp_src[0], mlp_land[0],
                                        True, behind, mlp_cols[0]))
    big_update("w_up", _exchange_wait("scatter_w_up_wait", mlp_send[1], mlp_recv[1], mlp_src[1], mlp_land[1],
                                      True, behind, mlp_cols[1]))
    big_update("w_out", _exchange_wait("scatter_w_out_wait", out_send[0], out_recv[0], out_src[0], out_land[0],
                                       True, behind))
    behind = deltas["w_up"]
    for i, nm in enumerate(in_names):
        big_update(nm, _exchange_wait("scatter_" + nm + "_wait", in_send[i], in_recv[i], in_src[i], in_land[i],
                                      True, behind))

    loss = lax.psum(loss_acc[0, 0], MESH_AXES)
    return (loss, grad_x[None], *[grads[nm] for nm in names], *[deltas[nm] for nm in names],
            *[new_m[nm] for nm in names], *[new_v[nm] for nm in names])


def _offset(small, small_names, name):
    off = 0
    for nm in small_names:
        if nm == name:
            return off
        off += small[nm].size
    raise KeyError(name)
```

```python
import functools
import math

import jax
import jax.numpy as jnp
from jax import lax
from jax.experimental import pallas as pl
from jax.experimental.pallas import tpu as pltpu

F32 = jnp.float32
BF16 = jnp.bfloat16
HI = lax.Precision.HIGHEST
MESH_AXES = ("x", "y", "c")
N_DEV = 8

HEAD = 64
GQA_RATIO = 8
ATT_BLOCK = 128
RPB_MAX_DIST = 128
LN_EPS = 1e-5
LNX_EPS = 64e-5
DEPTH = 1
ALPHA = (2.0 * DEPTH) ** 0.25
CHUNK = 64
REC_HEADS = 16

ADAM_LR = 0.001
ADAM_B1 = 0.9
ADAM_B2 = 0.999
ADAM_EPS = 1e-08
ADAM_WD = 0.01
ADAM_STEP = 10

VMEM_LIMIT = 56 * 1024 * 1024


def _cparams(sem=None):
    return pltpu.CompilerParams(dimension_semantics=sem, vmem_limit_bytes=VMEM_LIMIT)


def _tile(dim, cap):
    best = None
    t = 128
    while t <= min(dim, cap):
        if dim % t == 0:
            best = t
        t += 128
    return best or dim


def _rtile(dim, cap):
    best = None
    t = 8
    while t <= min(dim, cap):
        if dim % t == 0:
            best = t
        t += 8
    return best or dim


def _split2(a):
    hi = a.astype(BF16)
    return hi, (a - hi.astype(F32)).astype(BF16)


def _raw_dot(a, b, ca, cb, prec):
    dims = (((ca,), (cb,)), ((), ()))
    mm = lambda p, q: lax.dot_general(p, q, dims, preferred_element_type=F32)
    if prec == "bf16":
        return mm(a.astype(BF16), b.astype(BF16))
    if prec == "x3":
        (ah, al), (bh, bl) = _split2(a), _split2(b)
        return mm(ah, bh) + (mm(ah, bl) + mm(al, bh))
    if prec == "mask":
        ab = a.astype(BF16)
        b1, b2 = _split2(b)
        b3 = (b - b1.astype(F32) - b2.astype(F32)).astype(BF16)
        return mm(ab, b1) + (mm(ab, b2) + mm(ab, b3))
    if prec == "mb2":
        (ah, al), bb = _split2(a), b.astype(BF16)
        return mm(ah, bb) + mm(al, bb)
    return lax.dot_general(a, b, dims, precision=HI, preferred_element_type=F32)


@functools.partial(jax.custom_vjp, nondiff_argnums=(2, 3, 4))
def _bf16_dot(a, b, ca, cb, prec):
    return _raw_dot(a, b, ca, cb, prec)


def _bf16_dot_fwd(a, b, ca, cb, prec):
    return _raw_dot(a, b, ca, cb, prec), (a, b)


def _bf16_dot_bwd(ca, cb, prec, res, g):
    a, b = res
    if prec == "mask":
        return jnp.zeros_like(a), _bf16_dot(a, g, 1 - ca, 0, prec)
    if prec == "mb2":
        return _bf16_dot(g, b, 1, 1, prec), jnp.zeros_like(b)
    if ca == 1:
        da = _bf16_dot(g, b, 1, 1 - cb, prec)
    else:
        da = _bf16_dot(b, g, 1 - cb, 1, prec)
    if cb == 0:
        db = _bf16_dot(a, g, 1 - ca, 0, prec)
    else:
        db = _bf16_dot(g, a, 0, 1 - ca, prec)
    return da, db


_bf16_dot.defvjp(_bf16_dot_fwd, _bf16_dot_bwd)


def _dot(a, b, ca, cb, prec):
    return _raw_dot(a, b, ca, cb, prec) if prec == "hi" else _bf16_dot(a, b, ca, cb, prec)


def _sigmoid(z):
    return 1.0 / (1.0 + jnp.exp(-z))


def _softplus(z):
    return jnp.maximum(z, 0.0) + jnp.log(1.0 + jnp.exp(-jnp.abs(z)))


def _matmul(name, a, b, mode, out_dtypes, epilogue=None, extras=(), caps=(1024, 1024, 2048), after=None):
    if mode == "nn":
        (m, k), n = a.shape, b.shape[1]
    elif mode == "nt":
        (m, k), n = a.shape, b.shape[0]
    else:
        (k, m), n = a.shape, b.shape[1]
    tm, tn, tk = _tile(m, caps[0]), _tile(n, caps[1]), _tile(k, caps[2])
    nk = k // tk
    ne, no = len(extras), len(out_dtypes)
    ca, cb = {"nn": (1, 0), "nt": (1, 1), "tn": (0, 0)}[mode]

    n_after = 0 if after is None else 1

    def body(a_ref, b_ref, *rest):
        rest = rest[n_after:]
        extra_refs, out_refs, acc = rest[:ne], rest[ne:ne + no], rest[-1]
        kk = pl.program_id(2)
        part = _raw_dot(a_ref[...], b_ref[...], ca, cb, "bf16")

        def finish(total):
            res = epilogue(total, *[e[...] for e in extra_refs]) if epilogue else (total,)
            for o, v in zip(out_refs, res):
                o[...] = v.astype(o.dtype)

        if nk == 1:
            finish(part)
            return

        @pl.when(kk == 0)
        def _():
            acc[...] = part

        @pl.when((kk > 0) & (kk < nk - 1))
        def _():
            acc[...] += part

        @pl.when(kk == nk - 1)
        def _():
            finish(acc[...] + part)

    a_spec = (pl.BlockSpec((tk, tm), lambda i, j, kk: (kk, i)) if mode == "tn"
              else pl.BlockSpec((tm, tk), lambda i, j, kk: (i, kk)))
    b_spec = (pl.BlockSpec((tn, tk), lambda i, j, kk: (j, kk)) if mode == "nt"
              else pl.BlockSpec((tk, tn), lambda i, j, kk: (kk, j)))
    mn_spec = pl.BlockSpec((tm, tn), lambda i, j, kk: (i, j))
    after_specs = [pl.BlockSpec(memory_space=pl.ANY)] * n_after
    outs = pl.pallas_call(
        body, name=name, grid=(m // tm, n // tn, nk),
        in_specs=[a_spec, b_spec] + after_specs + [mn_spec] * ne,
        out_specs=[mn_spec] * no,
        out_shape=[jax.ShapeDtypeStruct((m, n), dt) for dt in out_dtypes],
        scratch_shapes=[pltpu.VMEM((tm, tn), F32)],
        compiler_params=_cparams(("parallel", "parallel", "arbitrary")),
    )(a, b, *([after] * n_after), *extras)
    return outs


def _rowwise(name, fn, rows, bcasts, out_rows, out_accs, tr, halos=()):
    t = rows[0].shape[0]
    nb = t // tr
    n_in = len(rows) + len(halos) + len(bcasts)
    n_ro = len(out_rows)

    def body(*refs):
        ins = [r[...] for r in refs[:n_in]]
        o_refs = refs[n_in:]
        i = pl.program_id(0)
        routs, aouts = fn(i, nb, *ins)
        for ref, v in zip(o_refs[:n_ro], routs):
            ref[...] = v.astype(ref.dtype)
        for ref, v in zip(o_refs[n_ro:], aouts):
            @pl.when(i == 0)
            def _(ref=ref):
                ref[...] = jnp.zeros_like(ref)
            ref[...] += v.reshape(ref.shape)

    in_specs = [pl.BlockSpec((tr, r.shape[1]), lambda i: (i, 0)) for r in rows]
    for arr, which in halos:
        if which == "prev":
            in_specs.append(pl.BlockSpec((8, arr.shape[1]), lambda i: (jnp.maximum(i * (tr // 8) - 1, 0), 0)))
        else:
            in_specs.append(pl.BlockSpec((8, arr.shape[1]),
                                         lambda i: (jnp.minimum((i + 1) * (tr // 8), t // 8 - 1), 0)))
    for bc in bcasts:
        in_specs.append(pl.BlockSpec(bc.shape, lambda i, nd=bc.ndim: (0,) * nd))
    out_specs = [pl.BlockSpec((tr, c), lambda i: (i, 0)) for c, _ in out_rows]
    out_specs += [pl.BlockSpec(s, lambda i, nd=len(s): (0,) * nd) for s in out_accs]
    out_shape = [jax.ShapeDtypeStruct((t, c), dt) for c, dt in out_rows]
    out_shape += [jax.ShapeDtypeStruct(s, F32) for s in out_accs]
    return pl.pallas_call(
        body, name=name, grid=(nb,), in_specs=in_specs, out_specs=out_specs, out_shape=out_shape,
        compiler_params=_cparams(("arbitrary",)),
    )(*rows, *[h[0] for h in halos], *bcasts)


def _shift_prev(x, halo, i):
    rolled = pltpu.roll(x, 1, 0)
    first = jnp.where(i == 0, 0.0, halo[7:8, :])
    row = lax.broadcasted_iota(jnp.int32, x.shape, 0)
    return jnp.where(row == 0, first, rolled)


def _shift_next(x, halo, i, nb):
    rolled = pltpu.roll(x, x.shape[0] - 1, 0)
    last = jnp.where(i == nb - 1, 0.0, halo[0:1, :])
    row = lax.broadcasted_iota(jnp.int32, x.shape, 0)
    return jnp.where(row == x.shape[0] - 1, last, rolled)


def _ln(x, g, b, eps=LN_EPS):
    mu = jnp.mean(x, axis=-1, keepdims=True)
    xc = x - mu
    var = jnp.mean(xc * xc, axis=-1, keepdims=True)
    return xc * lax.rsqrt(var + eps) * g + b


def _embed_math(x, g, b, sc, sh):
    x0 = _ln(x, g, b)
    return x0, x0 * (1.0 + sc) + sh


def _post_math(xin, y, gate, g, b, sc, sh):
    x1 = _ln(ALPHA * xin + (1.0 + gate) * y, g, b)
    return x1, x1 * (1.0 + sc) + sh


def _loss_math(xin, h, tgt, gate, g, b):
    x2 = _ln(ALPHA * xin + (1.0 + gate) * h, g, b)
    err = x2 - tgt
    return 0.5 * jnp.sum(jnp.mean(err * err, axis=-1))


def _rwkv_pre_math(r, k, v, xw, xa, xg, w0, wd, a0, wa, wg, k_k, k_a, seg, seg_t, dot=_dot):
    wpre = -_softplus(-(w0 + dot(jnp.tanh(xw), wd, 1, 0, "x3"))) - 0.5
    lw = -jnp.exp(wpre)
    a = _sigmoid(a0 + dot(xa, wa, 1, 0, "x3"))
    g = dot(_sigmoid(xg), wg, 1, 0, "x3")
    kk = k * k_k
    norm = jnp.sqrt(dot(kk * kk, seg, 1, 0, "mb2"))
    kkn = kk * dot(1.0 / jnp.maximum(norm, 1e-12), seg_t, 1, 0, "mb2")
    k2 = k * (1.0 + (a - 1.0) * k_a)
    return r, lw, k2, v, kkn, kkn * a, g


def _rwkv_post_math(y, r, k2, v, g, lnx_g, lnx_b, r_k, seg, seg_t, dot=_dot):
    inv = 1.0 / HEAD
    spread = lambda z: dot(dot(z, seg, 1, 0, "mb2"), seg_t, 1, 0, "mb2")
    mu = spread(y) * inv
    yc = y - mu
    var = spread(yc * yc) * inv
    yn = yc * lax.rsqrt(var + LNX_EPS) * lnx_g + lnx_b
    bonus = spread(r * k2 * r_k) * v
    return (yn + bonus) * g


def _chunk_math(s0, r, lw, k, v, kk, b, dot=_dot):
    n = len(r)
    hs = range(n)
    c = r[0].shape[0]
    ti = lax.broadcasted_iota(jnp.int32, (2 * c, 2 * c), 0)
    tj = lax.broadcasted_iota(jnp.int32, (2 * c, 2 * c), 1)
    tt, jj = ti & (c - 1), tj & (c - 1)
    quad = jnp.where(ti < c, (tt > jj).astype(F32), (tt >= jj).astype(F32))
    incl = quad[c:, :c]
    eye = (ti[:c, :c] == tj[:c, :c]).astype(F32)
    cl = [dot(incl, lw[i], 1, 0, "mask") for i in hs]
    ge = [jnp.exp(cl[i]) for i in hs]
    gi = [jnp.exp(-cl[i]) for i in hs]
    ar = [jnp.concatenate([-kk[i] * jnp.exp(cl[i] - lw[i]), r[i] * ge[i]], axis=0) for i in hs]
    kb = [jnp.concatenate([k[i] * gi[i], b[i] * gi[i]], axis=0) for i in hs]
    m = [dot(ar[i], kb[i], 1, 1, "x3") * quad for i in hs]
    ars0 = [dot(ar[i], s0[i], 1, 1, "bf16") for i in hs]
    mv = [dot(m[i][:c, :c], v[i], 1, 0, "bf16") for i in hs]
    pw = [m[i][:c, c:] for i in hs]
    inv = [eye + pw[i] for i in hs]
    for _ in range(int(math.log2(c)) - 1):
        pw = [dot(pw[i], pw[i], 1, 0, "bf16") for i in hs]
        inv = [inv[i] + dot(inv[i], pw[i], 1, 0, "bf16") for i in hs]
    u = [dot(inv[i], ars0[i][:c] + mv[i], 1, 0, "bf16") for i in hs]
    vu = [jnp.concatenate([v[i], u[i]], axis=0) for i in hs]
    y = [ars0[i][c:] + dot(m[i][c:], vu[i], 1, 0, "bf16") for i in hs]
    s1 = [(s0[i] + dot(vu[i], kb[i], 0, 0, "x3")) * ge[i][c - 1:c, :] for i in hs]
    return y, s1


def _attn_math(q, kp, kc, vp, vc, bias, sinks, first, dot=_dot):
    hq = q.shape[1] // HEAD
    hkv = kc.shape[1] // HEAD
    group = hq // hkv
    rows = group * ATT_BLOCK
    qi = lax.broadcasted_iota(jnp.int32, (rows, 2 * ATT_BLOCK), 0) & (ATT_BLOCK - 1)
    kj = lax.broadcasted_iota(jnp.int32, (rows, 2 * ATT_BLOCK), 1)
    dist = qi + ATT_BLOCK - kj
    valid = (dist >= 0) & (dist < ATT_BLOCK) & (jnp.logical_not(first) | (kj >= ATT_BLOCK))
    outs = []
    for j in range(hkv):
        heads = range(j * group, (j + 1) * group)
        kband = jnp.concatenate([kp[:, j * HEAD:(j + 1) * HEAD], kc[:, j * HEAD:(j + 1) * HEAD]], axis=0)
        vband = jnp.concatenate([vp[:, j * HEAD:(j + 1) * HEAD], vc[:, j * HEAD:(j + 1) * HEAD]], axis=0)
        qg = jnp.concatenate([q[:, h * HEAD:(h + 1) * HEAD] for h in heads], axis=0)
        bias_g = bias[j * group:(j + 1) * group].reshape(rows, 2 * ATT_BLOCK)
        sink = jnp.concatenate([jnp.broadcast_to(sinks[0:1, h:h + 1], (ATT_BLOCK, 1)) for h in heads], axis=0)
        s = dot(qg, kband, 1, 1, "bf16") * (HEAD ** -0.5) + bias_g
        s = jnp.where(valid, s, -1e30)
        m = jnp.maximum(jnp.max(s, axis=-1, keepdims=True), sink)
        e = jnp.exp(s - m)
        p = e / (jnp.sum(e, axis=-1, keepdims=True) + jnp.exp(sink - m))
        o = dot(p, vband, 1, 0, "bf16")
        outs += [o[g * ATT_BLOCK:(g + 1) * ATT_BLOCK] for g in range(group)]
    return jnp.concatenate(outs, axis=1)


def _rec_specs(t, da, gh, reverse):
    nc = t // CHUNK
    if reverse:
        return pl.BlockSpec((CHUNK, gh * HEAD), lambda hg, c: (nc - 1 - c, hg))
    return pl.BlockSpec((CHUNK, gh * HEAD), lambda hg, c: (c, hg))


def _rec_fwd(r, lw, k, v, kk, b):
    t, da = r.shape
    h = da // HEAD
    gh = min(REC_HEADS, h)
    nc = t // CHUNK

    def body(r_ref, lw_ref, k_ref, v_ref, kk_ref, b_ref, y_ref, s0_ref, state):
        @pl.when(pl.program_id(1) == 0)
        def _():
            state[...] = jnp.zeros_like(state)

        sls = [slice(i * HEAD, (i + 1) * HEAD) for i in range(gh)]
        heads = lambda ref: [ref[:, sl] for sl in sls]
        s0 = [state[i] for i in range(gh)]
        y, s1 = _chunk_math(s0, heads(r_ref), heads(lw_ref), heads(k_ref), heads(v_ref), heads(kk_ref),
                            heads(b_ref), dot=_raw_dot)
        for i, sl in enumerate(sls):
            s0_ref[0, i] = s0[i]
            y_ref[:, sl] = y[i]
            state[i] = s1[i]

    spec = _rec_specs(t, da, gh, False)
    return pl.pallas_call(
        body, name="rwkv_recurrence_fwd", grid=(h // gh, nc),
        in_specs=[spec] * 6,
        out_specs=[spec, pl.BlockSpec((1, gh, HEAD, HEAD), lambda hg, c: (c, hg, 0, 0))],
        out_shape=[jax.ShapeDtypeStruct((t, da), F32), jax.ShapeDtypeStruct((nc, h, HEAD, HEAD), F32)],
        scratch_shapes=[pltpu.VMEM((gh, HEAD, HEAD), F32)],
        compiler_params=_cparams(("parallel", "arbitrary")),
    )(r, lw, k, v, kk, b)


def _rec_bwd(r, lw, k, v, kk, b, s0s, dy):
    t, da = r.shape
    h = da // HEAD
    gh = min(REC_HEADS, h)
    nc = t // CHUNK

    def body(r_ref, lw_ref, k_ref, v_ref, kk_ref, b_ref, dy_ref, s0_ref,
             dr_ref, dlw_ref, dk_ref, dv_ref, dkk_ref, db_ref, dstate):
        @pl.when(pl.program_id(1) == 0)
        def _():
            dstate[...] = jnp.zeros_like(dstate)

        sls = [slice(i * HEAD, (i + 1) * HEAD) for i in range(gh)]
        heads = lambda ref: [ref[:, sl] for sl in sls]
        _, vjp = jax.vjp(_chunk_math, [s0_ref[0, i] for i in range(gh)], heads(r_ref), heads(lw_ref),
                         heads(k_ref), heads(v_ref), heads(kk_ref), heads(b_ref))
        grads = vjp((heads(dy_ref), [dstate[i] for i in range(gh)]))
        for i, sl in enumerate(sls):
            dstate[i] = grads[0][i]
            for ref, val in zip((dr_ref, dlw_ref, dk_ref, dv_ref, dkk_ref, db_ref), grads[1:]):
                ref[:, sl] = val[i]

    spec = _rec_specs(t, da, gh, True)
    return pl.pallas_call(
        body, name="rwkv_recurrence_bwd", grid=(h // gh, nc),
        in_specs=[spec] * 7 + [pl.BlockSpec((1, gh, HEAD, HEAD), lambda hg, c: (nc - 1 - c, hg, 0, 0))],
        out_specs=[spec] * 6,
        out_shape=[jax.ShapeDtypeStruct((t, da), F32)] * 6,
        scratch_shapes=[pltpu.VMEM((gh, HEAD, HEAD), F32)],
        compiler_params=_cparams(("parallel", "arbitrary")),
    )(r, lw, k, v, kk, b, dy, s0s)


def _attn_specs(t, hq_w, hkv_w):
    nb = t // ATT_BLOCK
    cur = lambda w: pl.BlockSpec((ATT_BLOCK, w), lambda n: (n, 0))
    prev = lambda w: pl.BlockSpec((ATT_BLOCK, w), lambda n: (jnp.maximum(n - 1, 0), 0))
    return nb, cur, prev


def _attn_fwd(q, kb, vb, bias, sinks):
    t, qw = q.shape
    kw = kb.shape[1]
    nb, cur, prev = _attn_specs(t, qw, kw)

    def body(q_ref, kp_ref, kc_ref, vp_ref, vc_ref, bias_ref, sink_ref, o_ref):
        first = pl.program_id(0) == 0
        o = _attn_math(q_ref[...], kp_ref[...], kc_ref[...], vp_ref[...], vc_ref[...],
                       bias_ref[...], sink_ref[...], first, dot=_raw_dot)
        o_ref[...] = o.astype(o_ref.dtype)

    full = lambda a: pl.BlockSpec(a.shape, lambda n, nd=a.ndim: (0,) * nd)
    return pl.pallas_call(
        body, name="swa_attention_fwd", grid=(nb,),
        in_specs=[cur(qw), prev(kw), cur(kw), prev(kw), cur(kw), full(bias), full(sinks)],
        out_specs=cur(qw), out_shape=jax.ShapeDtypeStruct((t, qw), BF16),
        compiler_params=_cparams(("parallel",)),
    )(q, kb, kb, vb, vb, bias, sinks)


def _attn_bwd(q, kb, vb, bias, sinks, do, col_block):
    t, qw = q.shape
    kw = kb.shape[1]
    nb, cur, prev = _attn_specs(t, qw, kw)

    def body(q_ref, kp_ref, kc_ref, vp_ref, vc_ref, bias_ref, sink_ref, do_ref,
             dq_ref, dkp_ref, dkc_ref, dvp_ref, dvc_ref, dbias_ref, dsink_ref):
        n = pl.program_id(0)
        first = n == 0
        fn = functools.partial(_attn_math, first=first)
        _, vjp = jax.vjp(fn, q_ref[...], kp_ref[...], kc_ref[...], vp_ref[...], vc_ref[...],
                         bias_ref[...], sink_ref[...])
        dq, dkp, dkc, dvp, dvc, dbias, dsink = vjp(do_ref[...].astype(F32))
        dq_ref[...] = dq.astype(dq_ref.dtype)
        dkp_ref[...] = dkp
        dkc_ref[...] = dkc
        dvp_ref[...] = dvp
        dvc_ref[...] = dvc

        @pl.when(first)
        def _():
            dbias_ref[...] = jnp.zeros_like(dbias_ref)
            dsink_ref[...] = jnp.zeros_like(dsink_ref)

        dbias_ref[...] += dbias
        dsink_ref[...] += dsink

    full = lambda a: pl.BlockSpec(a.shape, lambda n, nd=a.ndim: (0,) * nd)
    kshape = jax.ShapeDtypeStruct((t, kw), F32)
    return pl.pallas_call(
        body, name="swa_attention_bwd", grid=(nb,),
        in_specs=[cur(qw), prev(kw), cur(kw), prev(kw), cur(kw), full(bias), full(sinks),
                  pl.BlockSpec((ATT_BLOCK, qw), lambda n: (n, col_block))],
        out_specs=[cur(qw), cur(kw), cur(kw), cur(kw), cur(kw), full(bias), full(sinks)],
        out_shape=[jax.ShapeDtypeStruct((t, qw), BF16), kshape, kshape, kshape, kshape,
                   jax.ShapeDtypeStruct(bias.shape, F32), jax.ShapeDtypeStruct(sinks.shape, F32)],
        compiler_params=_cparams(("arbitrary",)),
    )(q, kb, kb, vb, vb, bias, sinks, do)


def _bucket_onehot():
    qi = jnp.arange(ATT_BLOCK)[:, None]
    kj = jnp.arange(2 * ATT_BLOCK)[None, :]
    n = jnp.maximum(qi + ATT_BLOCK - kj, 0)
    buckets, max_exact = 32, 16
    nf = jnp.maximum(n, 1).astype(F32)
    large = max_exact + (jnp.log(nf / max_exact) / math.log(RPB_MAX_DIST / max_exact)
                         * (buckets - max_exact)).astype(jnp.int32)
    bucket = jnp.where(n < max_exact, n, jnp.minimum(large, buckets - 1)).reshape(-1)
    return (bucket[None, :] == jnp.arange(buckets)[:, None]).astype(F32)


def _small_dot(name, a, b, ca, cb):
    m = a.shape[1 - ca]
    n = b.shape[1 - cb]

    def body(a_ref, b_ref, o_ref):
        o_ref[...] = _raw_dot(a_ref[...], b_ref[...], ca, cb, "hi")

    return pl.pallas_call(body, name=name, out_shape=jax.ShapeDtypeStruct((m, n), F32),
                          compiler_params=_cparams())(a, b)


def _mod_fwd(c_all, w_mod):
    d, n = w_mod.shape
    tn = _tile(n, 512)

    def body(c_ref, w_ref, o_ref, cond_ref):
        cv = c_ref[...]
        cond = cv * _sigmoid(cv)
        cond_ref[...] = cond
        o_ref[...] = _raw_dot(cond, w_ref[...], 1, 0, "hi")

    return pl.pallas_call(
        body, name="adaln_mod_fwd", grid=(n // tn,),
        in_specs=[pl.BlockSpec(c_all.shape, lambda j: (0, 0)), pl.BlockSpec((d, tn), lambda j: (0, j))],
        out_specs=[pl.BlockSpec((c_all.shape[0], tn), lambda j: (0, j)),
                   pl.BlockSpec(c_all.shape, lambda j: (0, 0))],
        out_shape=[jax.ShapeDtypeStruct((c_all.shape[0], n), F32), jax.ShapeDtypeStruct(c_all.shape, F32)],
        compiler_params=_cparams(("arbitrary",)),
    )(c_all, w_mod)


def _adam_math(w, g, m, v):
    m = ADAM_B1 * m + (1.0 - ADAM_B1) * g
    v = ADAM_B2 * v + (1.0 - ADAM_B2) * (g * g)
    m_hat = m / (1.0 - ADAM_B1 ** ADAM_STEP)
    v_hat = v / (1.0 - ADAM_B2 ** ADAM_STEP)
    delta = -ADAM_LR * (m_hat / (jnp.sqrt(v_hat) + ADAM_EPS) + ADAM_WD * w)
    return delta, m, v


def _adamw(name, w, m, v, gparts):
    r, c = w.shape
    p = gparts.shape[0]
    tr = _rtile(r, max(8, (1 << 18) // max(c, 1) // 8 * 8))

    def body(w_ref, m_ref, v_ref, g_ref, go_ref, d_ref, mo_ref, vo_ref):
        g = g_ref[0].astype(F32)
        for s in range(1, p):
            g = g + g_ref[s].astype(F32)
        delta, mn, vn = _adam_math(w_ref[...], g, m_ref[...], v_ref[...])
        go_ref[...] = g
        d_ref[...] = delta
        mo_ref[...] = mn
        vo_ref[...] = vn

    spec = pl.BlockSpec((tr, c), lambda i: (i, 0))
    return pl.pallas_call(
        body, name=name, grid=(r // tr,),
        in_specs=[spec, spec, spec, pl.BlockSpec((p, tr, c), lambda i: (0, i, 0))],
        out_specs=[spec] * 4, out_shape=[jax.ShapeDtypeStruct((r, c), F32)] * 4,
        compiler_params=_cparams(("parallel",)),
    )(w, m, v, gparts)


def _adamw_outer(name, w, m, v, cond_t, dmod):
    d, n = w.shape
    tr, tn = _rtile(d, 512), _tile(n, 1024)

    def body(w_ref, m_ref, v_ref, c_ref, dm_ref, go_ref, d_ref, mo_ref, vo_ref):
        g = _raw_dot(c_ref[...], dm_ref[...], 1, 0, "hi")
        delta, mn, vn = _adam_math(w_ref[...], g, m_ref[...], v_ref[...])
        go_ref[...] = g
        d_ref[...] = delta
        mo_ref[...] = mn
        vo_ref[...] = vn

    spec = pl.BlockSpec((tr, tn), lambda i, j: (i, j))
    return pl.pallas_call(
        body, name=name, grid=(d // tr, n // tn),
        in_specs=[spec, spec, spec, pl.BlockSpec((tr, cond_t.shape[1]), lambda i, j: (i, 0)),
                  pl.BlockSpec((dmod.shape[0], tn), lambda i, j: (0, j))],
        out_specs=[spec] * 4, out_shape=[jax.ShapeDtypeStruct((d, n), F32)] * 4,
        compiler_params=_cparams(("parallel", "parallel")),
    )(w, m, v, cond_t, dmod)


def _all_gather(name, arrays):
    n = len(arrays)

    def body(*refs):
        ins, outs = refs[:n], refs[n:2 * n]
        send_sems, recv_sems, local_sems = refs[2 * n:]
        x, y, c = lax.axis_index("x"), lax.axis_index("y"), lax.axis_index("c")
        me, sibling = (x, y, c), (x, y, 1 - c)
        chips = [(1 - x, y), (x, 1 - y), (1 - x, 1 - y)]

        def copy(a, k, block, to, src=None):
            rows = outs[a].at[4 * block[0] + 2 * block[1] + block[2]]
            return pltpu.make_async_remote_copy(
                src_ref=rows if src is None else src, dst_ref=rows, send_sem=send_sems.at[a, k],
                recv_sem=recv_sems.at[a, k], device_id=to, device_id_type=pl.DeviceIdType.MESH)

        mine = [pltpu.make_async_copy(ins[a], outs[a].at[4 * x + 2 * y + c], local_sems.at[a]) for a in range(n)]
        for cp in mine:
            cp.start()
        first = []
        for a in range(n):
            first.append(copy(a, 0, me, sibling, src=ins[a]))
            first += [copy(a, 1 + j, me, (*chip, c), src=ins[a]) for j, chip in enumerate(chips)]
        for cp in first:
            cp.start()
        passed = []
        for a in range(n):
            for j, chip in enumerate(chips):
                copy(a, 1 + j, (*chip, c), me).wait_recv()
                passed.append(copy(a, 4 + j, (*chip, c), sibling))
                passed[-1].start()
        for a in range(n):
            copy(a, 0, sibling, me).wait_recv()
            for j, chip in enumerate(chips):
                copy(a, 4 + j, (*chip, 1 - c), me).wait_recv()
        for cp in first + passed:
            cp.wait_send()
        for cp in mine:
            cp.wait()

    any_spec = pl.BlockSpec(memory_space=pl.ANY)
    return pl.pallas_call(
        body, name=name, in_specs=[any_spec] * n, out_specs=[any_spec] * n,
        out_shape=[jax.ShapeDtypeStruct((N_DEV,) + a.shape, a.dtype) for a in arrays],
        scratch_shapes=[pltpu.SemaphoreType.DMA((n, N_DEV - 1)), pltpu.SemaphoreType.DMA((n, N_DEV - 1)),
                        pltpu.SemaphoreType.DMA((n,))],
    )(*arrays)


def _peer(p):
    x, y, c = lax.axis_index("x"), lax.axis_index("y"), lax.axis_index("c")
    px, py, pc = x ^ ((p >> 2) & 1), y ^ ((p >> 1) & 1), c ^ (p & 1)
    return (px, py, pc), 4 * px + 2 * py + pc


def _split_copy(src_ref, land_ref, send_sems, recv_sems, p, scatter, arriving, cols=None):
    x, y, c = lax.axis_index("x"), lax.axis_index("y"), lax.axis_index("c")
    me = 4 * x + 2 * y + c
    dev, idx = _peer(p)
    block = lambda ref, d: ref.at[:, pl.ds(pl.multiple_of(d * cols, cols), cols)] if cols else ref.at[d]
    if scatter:
        src, dst = block(src_ref, idx), land_ref.at[idx if arriving else me]
    else:
        src, dst = src_ref, block(land_ref, idx if arriving else me)
    return pltpu.make_async_remote_copy(
        src_ref=src, dst_ref=dst, send_sem=send_sems.at[p - 1], recv_sem=recv_sems.at[p - 1], device_id=dev,
        device_id_type=pl.DeviceIdType.MESH)


_HBM_SPEC = pl.BlockSpec(memory_space=pltpu.HBM)
_SEM_SPEC = pl.BlockSpec(memory_space=pltpu.SEMAPHORE)
_DATAFLOW = pltpu.SideEffectType.DATAFLOW_SIDE_EFFECTING


def _exchange_start(name, srcs, lands, scatter, after, cols=None):
    n = len(srcs)
    cols = cols or [None] * n

    def body(*refs):
        src_refs, land_refs = refs[:n], refs[n:2 * n]
        outs = refs[2 * n + 1:]
        send, recv, token = outs[:n], outs[n:2 * n], outs[-1]
        for a in range(n):
            for p in range(1, N_DEV):
                _split_copy(src_refs[a], land_refs[a], send[a], recv[a], p, scatter, False, cols[a]).start()
        token[...] = jnp.zeros_like(token)

    sems = [pltpu.SemaphoreType.DMA((N_DEV - 1,))] * (2 * n)
    hbm = [pltpu.HBM(a.shape, a.dtype) for a in list(srcs) + list(lands)]
    res = pl.pallas_call(
        body, name=name,
        out_shape=sems + hbm + [jax.ShapeDtypeStruct((8, 128), F32)],
        in_specs=[_HBM_SPEC] * (2 * n) + [pl.BlockSpec(memory_space=pl.ANY)],
        out_specs=[_SEM_SPEC] * (2 * n) + [_HBM_SPEC] * (2 * n) + [pl.BlockSpec(memory_space=pltpu.VMEM)],
        input_output_aliases={i: 2 * n + i for i in range(2 * n)},
        compiler_params=pltpu.CompilerParams(has_side_effects=_DATAFLOW),
    )(*[pltpu.with_memory_space_constraint(a, pltpu.HBM) for a in list(srcs) + list(lands)], after)
    return res[:n], res[n:2 * n], res[2 * n:3 * n], res[3 * n:4 * n], res[-1]


def _exchange_wait(name, send_sem, recv_sem, src, land, scatter, after, cols=None):
    def body(src_ref, land_ref, send, recv, after_ref, src_out, land_out, own_sem):
        me = 4 * lax.axis_index("x") + 2 * lax.axis_index("y") + lax.axis_index("c")
        block = lambda ref: ref.at[:, pl.ds(pl.multiple_of(me * cols, cols), cols)] if cols else ref.at[me]
        own = (pltpu.make_async_copy(block(src_ref), land_ref.at[me], own_sem) if scatter
               else pltpu.make_async_copy(src_ref, block(land_ref), own_sem))
        own.start()
        for p in range(1, N_DEV):
            cp = _split_copy(src_ref, land_ref, send, recv, p, scatter, True, cols)
            cp.wait_send()
            cp.wait_recv()
        own.wait()

    return pl.pallas_call(
        body, name=name,
        out_shape=(pltpu.HBM(src.shape, src.dtype), pltpu.HBM(land.shape, land.dtype)),
        in_specs=[_HBM_SPEC, _HBM_SPEC, _SEM_SPEC, _SEM_SPEC, pl.BlockSpec(memory_space=pl.ANY)],
        out_specs=(_HBM_SPEC, _HBM_SPEC), input_output_aliases={0: 0, 1: 1},
        scratch_shapes=[pltpu.SemaphoreType.DMA],
        compiler_params=pltpu.CompilerParams(has_side_effects=_DATAFLOW),
    )(src, land, send_sem, recv_sem, after)[1]


def _cols_to_shards(a):
    r, c = a.shape
    return a.reshape(r, N_DEV, c // N_DEV).transpose(1, 0, 2)


def _shards_to_cols(a):
    d, r, n = a.shape
    return a.transpose(1, 0, 2).reshape(r, d * n)


def kernel(x, c, ln_emb_g, ln_emb_b, rpb_table, w_mod, b_mod, w_in, mu_shift, w0, w_decay_up, a0, w_iclr_up, w_gate_up, k_k, k_a, r_k, lnx_g, lnx_b, attn_sinks, w_out, ln1_g, ln1_b, w_up, w_down, ln2_g, ln2_b, loss_target, m_ln_emb_g, m_ln_emb_b, m_rpb_table, m_w_mod, m_b_mod, m_w_in, m_mu_shift, m_w0, m_w_decay_up, m_a0, m_w_iclr_up, m_w_gate_up, m_k_k, m_k_a, m_r_k, m_lnx_g, m_lnx_b, m_attn_sinks, m_w_out, m_ln1_g, m_ln1_b, m_w_up, m_w_down, m_ln2_g, m_ln2_b, v_ln_emb_g, v_ln_emb_b, v_rpb_table, v_w_mod, v_b_mod, v_w_in, v_mu_shift, v_w0, v_w_decay_up, v_a0, v_w_iclr_up, v_w_gate_up, v_k_k, v_k_a, v_r_k, v_lnx_g, v_lnx_b, v_attn_sinks, v_w_out, v_ln1_g, v_ln1_b, v_w_up, v_w_down, v_ln2_g, v_ln2_b):
    names = ["ln_emb_g", "ln_emb_b", "rpb_table", "w_mod", "b_mod", "w_in", "mu_shift", "w0", "w_decay_up",
             "a0", "w_iclr_up", "w_gate_up", "k_k", "k_a", "r_k", "lnx_g", "lnx_b", "attn_sinks", "w_out",
             "ln1_g", "ln1_b", "w_up", "w_down", "ln2_g", "ln2_b"]
    env = dict(locals())
    weights = {nm: env[nm] for nm in names}
    mom_m = {nm: env["m_" + nm] for nm in names}
    mom_v = {nm: env["v_" + nm] for nm in names}

    t, d = x.shape[1], x.shape[2]
    da = d // 2
    h_a = da // HEAD
    hq = (d - da) // HEAD
    hkv = hq // GQA_RATIO
    l_w, l_a, l_g = w_decay_up.shape[1], w_iclr_up.shape[1], w_gate_up.shape[1]
    o_w, o_a, o_g = 3 * da, 3 * da + l_w, 3 * da + l_w + l_a
    n_rwkv = o_g + l_g
    o_kb, o_vb = n_rwkv + hq * HEAD, n_rwkv + hq * HEAD + hkv * HEAD
    me = 4 * lax.axis_index("x") + 2 * lax.axis_index("y") + lax.axis_index("c")

    x2d, tgt = x[0], loss_target[0]
    row = lambda a: a.reshape(1, -1)
    seg = (jnp.arange(da)[:, None] // HEAD == jnp.arange(h_a)[None, :]).astype(F32)
    seg_t = seg.T

    (c_all,) = _all_gather("gather_cond", [c])
    c_all = c_all.reshape(N_DEV, d)
    mod_rows, cond_all = _mod_fwd(c_all, w_mod[0])
    gathered = _all_gather("gather_weights", [
        mod_rows, w_in[0].astype(BF16), w_decay_up[0], w_iclr_up[0], w_gate_up[0]])
    mod_all, win_g, wd_g, wa_g, wg_g = gathered
    late = [w_out[0].astype(BF16), w_up[0].astype(BF16), w_down[0].astype(BF16)]
    n_up = w_up.shape[2]
    late_cols = [None, n_up, None]
    late_lands = [lax.empty((N_DEV,) + late[0].shape, BF16), lax.empty((d, N_DEV * n_up), BF16),
                  lax.empty((N_DEV,) + late[2].shape, BF16)]
    late_send, late_recv, late_src, late_land, late_token = _exchange_start(
        "gather_late_weights_start", late, late_lands, False, mod_all, late_cols)
    mod = lax.dynamic_index_in_dim(mod_all, me, axis=1, keepdims=False).reshape(1, -1) + b_mod
    mod = mod + late_token[0, 0]
    sh1, sc1, g1, sh2, sc2, g2 = [mod[:, i * d:(i + 1) * d] for i in range(6)]
    w_in_f = _shards_to_cols(win_g)
    wd_f, wa_f, wg_f = _shards_to_cols(wd_g), _shards_to_cols(wa_g), _shards_to_cols(wg_g)

    def late_weight(i, nm, after):
        return _exchange_wait("gather_" + nm + "_wait", late_send[i], late_recv[i], late_src[i], late_land[i],
                              False, after, late_cols[i])

    tr = _rtile(t, 256)
    lng, lnb = row(ln_emb_g), row(ln_emb_b)

    def embed_fn(i, nb, xb, g, b, sc, sh):
        return _embed_math(xb, g, b, sc, sh), ()
    x0, u1 = _rowwise("embed_ln_mod", embed_fn, [x2d], [lng, lnb, sc1, sh1], [(d, F32), (d, BF16)], [], tr)

    (p,) = _matmul("in_proj", u1, w_in_f, "nn", [F32])
    p_rkv, p_w, p_a, p_g = p[:, :o_w], p[:, o_w:o_a], p[:, o_a:o_g], p[:, o_g:n_rwkv]
    q, kb, vb = p[:, n_rwkv:o_kb], p[:, o_kb:o_vb], p[:, o_vb:]
    mu_rkv, mu_w, mu_a, mu_g = (mu_shift[:, :o_w], mu_shift[:, o_w:o_a], mu_shift[:, o_a:o_g],
                                mu_shift[:, o_g:n_rwkv])
    pre_params = [w0, wd_f, a0, wa_f, wg_f, k_k, k_a, seg, seg_t]
    tr_pre = _rtile(t, 128)

    def shifted(i, blocks, halos, mus):
        return [xb + (_shift_prev(xb, hb, i) - xb) * mb for xb, hb, mb in zip(blocks, halos, mus)]

    def split3(a):
        return a[:, :da], a[:, da:2 * da], a[:, 2 * da:]

    def pre_fn(i, nb, b_rkv, b_w, b_a, b_g, h_rkv, h_w, h_a_, h_g, m_rkv, m_w, m_a, m_g, *params):
        s_rkv, s_w, s_a, s_g = shifted(i, [b_rkv, b_w, b_a, b_g], [h_rkv, h_w, h_a_, h_g],
                                       [m_rkv, m_w, m_a, m_g])
        return _rwkv_pre_math(*split3(s_rkv), s_w, s_a, s_g, *params, dot=_raw_dot), ()

    pre_rows = [p_rkv, p_w, p_a, p_g]
    pre_halos = [(a, "prev") for a in pre_rows]
    r_, lw_, k2_, v_, kk_, b_, gate_ = _rowwise(
        "rwkv_pre", pre_fn, pre_rows, [mu_rkv, mu_w, mu_a, mu_g] + pre_params,
        [(da, F32)] * 7, [], tr_pre, halos=pre_halos)

    y_rec, s0s = _rec_fwd(r_, lw_, k2_, v_, kk_, b_)

    rk_flat = r_k.reshape(1, da)
    post_params = [lnx_g, lnx_b, rk_flat, seg, seg_t]

    def post_fn(i, nb, yb, rb, kb_, vb_, gb, *params):
        return (_rwkv_post_math(yb, rb, kb_, vb_, gb, *params, dot=_raw_dot),), ()
    (ya,) = _rowwise("rwkv_post", post_fn, [y_rec, r_, k2_, v_, gate_], post_params, [(da, BF16)], [], tr_pre)

    onehot = _bucket_onehot()
    bias = _small_dot("rpb_gather", rpb_table, onehot, 0, 0)
    bias = bias.reshape(hq, ATT_BLOCK, 2 * ATT_BLOCK)
    yb = _attn_fwd(q, kb, vb, bias, attn_sinks)

    mix_in = jnp.concatenate([ya, yb], axis=1)
    w_out_f = late_weight(0, "w_out", mix_in).reshape(d, d)
    (mix,) = _matmul("out_proj", mix_in, w_out_f, "nn", [F32])

    def post1_fn(i, nb, xin, yv, gate, g, b, sc, sh):
        return _post_math(xin, yv, gate, g, b, sc, sh), ()
    x1, u2 = _rowwise("ln1_mod", post1_fn, [x0, mix], [g1, ln1_g, ln1_b, sc2, sh2],
                      [(d, F32), (d, BF16)], [], tr)

    def relu2(acc):
        rl = jnp.maximum(acc, 0.0)
        return acc, rl * rl
    w_up_f = late_weight(1, "w_up", u2)
    hpre, hact = _matmul("mlp_up", u2, w_up_f, "nn", [F32, BF16], epilogue=relu2)
    w_down_f = late_weight(2, "w_down", hact).reshape(-1, d)
    (hmlp,) = _matmul("mlp_down", hact, w_down_f, "nn", [F32])

    def loss_fn(i, nb, xin, hv, tg, gate, g, b):
        val, vjp = jax.vjp(_loss_math, xin, hv, tg, gate, g, b)
        dxin, dh, _, dgate, dg, db = vjp(jnp.ones((), F32))
        return (dxin, dh), (val, dgate, dg, db)
    dx1, dh, loss_acc, dg2, dln2g, dln2b = _rowwise(
        "ln2_loss", loss_fn, [x1, hmlp, tgt], [g2, ln2_g, ln2_b], [(d, F32), (d, BF16)],
        [(1, 1), (1, d), (1, d), (1, d)], _rtile(t, 128))

    def drelu2(acc, hp):
        return (acc * 2.0 * jnp.maximum(hp, 0.0),)
    (dhpre,) = _matmul("mlp_down_dgrad", dh, w_down_f, "nt", [BF16], epilogue=drelu2, extras=[hpre])
    (gw_down,) = _matmul("mlp_down_wgrad", hact, dh, "tn", [BF16])
    (du2,) = _matmul("mlp_up_dgrad", dhpre, w_up_f, "nt", [F32])
    (gw_up,) = _matmul("mlp_up_wgrad", u2, dhpre, "tn", [BF16])

    def landing(src):
        return lax.empty(src.shape, src.dtype)
    mlp_src = [gw_down.reshape(N_DEV, -1, d), gw_up]
    mlp_cols = [None, n_up]
    mlp_send, mlp_recv, mlp_src, mlp_land, mlp_token = _exchange_start(
        "scatter_mlp_grads_start", mlp_src, [landing(mlp_src[0]), lax.empty((N_DEV, d, n_up), BF16)], True,
        gw_up, mlp_cols)

    def post1_bwd(i, nb, xin, yv, dx1v, du2v, gate, g, b, sc, sh):
        _, vjp = jax.vjp(_post_math, xin, yv, gate, g, b, sc, sh)
        dxin, dy, dgate, dg, db, dsc, dsh = vjp((dx1v, du2v))
        return (dxin, dy), (dgate, dg, db, dsc, dsh)
    dx0, dmix, dg1, dln1g, dln1b, dsc2, dsh2 = _rowwise(
        "ln1_mod_bwd", post1_bwd, [x0, mix, dx1, du2], [g1 + mlp_token[0, 0], ln1_g, ln1_b, sc2, sh2],
        [(d, F32), (d, BF16)], [(1, d)] * 5, _rtile(t, 128))

    (dmix_in,) = _matmul("out_proj_dgrad", dmix, w_out_f, "nt", [F32])
    (gw_out,) = _matmul("out_proj_wgrad", mix_in, dmix, "tn", [BF16])
    dya = dmix_in[:, :da]
    out_src = [gw_out.reshape(N_DEV, d // N_DEV, d)]
    out_send, out_recv, out_src, out_land, out_token = _exchange_start(
        "scatter_out_grad_start", out_src, [landing(a) for a in out_src], True, gw_out)
    post_params_bwd = [lnx_g, lnx_b, rk_flat + out_token[0, 0], seg, seg_t]

    def post_bwd(i, nb, yb_, rb, kb_, vb_, gb, dyab, *params):
        _, vjp = jax.vjp(_rwkv_post_math, yb_, rb, kb_, vb_, gb, *params)
        dy, dr, dk, dv, dg, dlg, dlb, drk, _, _ = vjp(dyab)
        return (dy, dr, dk, dv, dg), (dlg, dlb, drk)
    dy_rec, dr_e, dk_e, dv_e, dgate, dlnxg, dlnxb, drk = _rowwise(
        "rwkv_post_bwd", post_bwd, [y_rec, r_, k2_, v_, gate_, dya], post_params_bwd,
        [(da, F32)] * 5, [(1, da)] * 3, tr_pre)

    dr_r, dlw_r, dk_r, dv_r, dkk_r, db_r = _rec_bwd(r_, lw_, k2_, v_, kk_, b_, s0s, dy_rec)

    def pre_bwd(i, nb, b_rkv, b_w, b_a, b_g, dr1, dr2, dlw, dk1, dk2, dv1, dv2, dkk, dbb, dgt,
                h_rkv, h_w, h_a_, h_g, m_rkv, m_w, m_a, m_g, *params):
        blocks = [b_rkv, b_w, b_a, b_g]
        prevs = [_shift_prev(xb, hb, i) for xb, hb in zip(blocks, [h_rkv, h_w, h_a_, h_g])]
        mus = [m_rkv, m_w, m_a, m_g]
        s_rkv, s_w, s_a, s_g = [xb + (pb - xb) * mb for xb, pb, mb in zip(blocks, prevs, mus)]
        _, vjp = jax.vjp(_rwkv_pre_math, *split3(s_rkv), s_w, s_a, s_g, *params)
        grads = vjp((dr1 + dr2, dlw, dk1 + dk2, dv1 + dv2, dkk, dbb, dgt))
        g_rkv = jnp.concatenate(grads[:3], axis=1)
        g_w, g_a, g_g = grads[3:6]
        dmu = [jnp.sum(gs * (pb - xb), axis=0, keepdims=True)
               for gs, pb, xb in zip([g_rkv, g_w, g_a, g_g], prevs, blocks)]
        dw0, dwd, da0, dwa, dwg, dkk_, dka = grads[6:13]
        return (g_rkv, g_w, g_a, g_g), (*dmu, dw0, dwd, da0, dwa, dwg, dkk_, dka)

    pre_out = _rowwise(
        "rwkv_pre_bwd", pre_bwd,
        pre_rows + [dr_r, dr_e, dlw_r, dk_r, dk_e, dv_r, dv_e, dkk_r, db_r, dgate],
        [mu_rkv, mu_w, mu_a, mu_g] + pre_params,
        [(o_w, F32), (l_w, F32), (l_a, F32), (l_g, F32)],
        [(1, o_w), (1, l_w), (1, l_a), (1, l_g), (1, da), (l_w, da), (1, da), (l_a, da), (l_g, da),
         (1, da), (1, da)],
        _rtile(t, 64), halos=pre_halos)
    gs_rkv, gs_w, gs_a, gs_g = pre_out[:4]
    dmu_parts = pre_out[4:8]
    dw0, dwd, da0, dwa, dwg, dk_k, dk_a = pre_out[8:]

    def unshift_fn(i, nb, a1, a2, a3, a4, n1, n2, n3, n4, m1, m2, m3, m4):
        outs = [gs * (1.0 - mb) + _shift_next(gs * mb, hb * mb, i, nb)
                for gs, hb, mb in zip([a1, a2, a3, a4], [n1, n2, n3, n4], [m1, m2, m3, m4])]
        return outs, ()
    gs_list = [gs_rkv, gs_w, gs_a, gs_g]
    dp_rkv, dp_w, dp_a, dp_g = _rowwise(
        "token_shift_bwd", unshift_fn, gs_list, [mu_rkv, mu_w, mu_a, mu_g],
        [(o_w, BF16), (l_w, BF16), (l_a, BF16), (l_g, BF16)], [], tr_pre,
        halos=[(a, "next") for a in gs_list])

    dq, dkp, dkc, dvp, dvc, dbias, dsinks = _attn_bwd(q, kb, vb, bias, attn_sinks, dmix_in, 1)
    zpad = jnp.zeros((ATT_BLOCK, kb.shape[1]), F32)
    dkb = (dkc + jnp.concatenate([dkp[ATT_BLOCK:], zpad], axis=0)).astype(BF16)
    dvb = (dvc + jnp.concatenate([dvp[ATT_BLOCK:], zpad], axis=0)).astype(BF16)
    d_rpb = _small_dot("rpb_scatter", onehot, dbias.reshape(hq, -1), 1, 1)

    dp = jnp.concatenate([dp_rkv, dp_w, dp_a, dp_g, dq, dkb, dvb], axis=1)
    (gw_in,) = _matmul("in_proj_wgrad", u1, dp, "tn", [BF16])
    in_names = ["w_in", "w_decay_up", "w_iclr_up", "w_gate_up"]
    in_src = [_cols_to_shards(gw_in), _cols_to_shards(dwd), _cols_to_shards(dwa), _cols_to_shards(dwg)]
    in_send, in_recv, in_src, in_land, in_token = _exchange_start(
        "scatter_in_grads_start", in_src, [landing(a) for a in in_src], True, gw_in)
    (du1,) = _matmul("in_proj_dgrad", dp, w_in_f, "nt", [F32], after=in_token)

    def embed_bwd(i, nb, xb, dx0v, du1v, g, b, sc, sh):
        _, vjp = jax.vjp(_embed_math, xb, g, b, sc, sh)
        dxv, dg, db, dsc, dsh = vjp((dx0v, du1v))
        return (dxv,), (dg, db, dsc, dsh)
    grad_x, dlng, dlnb, dsc1, dsh1 = _rowwise(
        "embed_ln_mod_bwd", embed_bwd, [x2d, dx0, du1], [lng, lnb, sc1, sh1], [(d, F32)], [(1, d)] * 4,
        _rtile(t, 128))

    dmod = jnp.concatenate([dsh1, dsc1, dg1, dsh2, dsc2, dg2], axis=1)
    small = {"ln_emb_g": dlng, "ln_emb_b": dlnb, "rpb_table": d_rpb, "b_mod": dmod,
             "mu_shift": jnp.concatenate(dmu_parts, axis=1), "w0": dw0, "a0": da0, "k_k": dk_k, "k_a": dk_a,
             "r_k": drk, "lnx_g": dlnxg, "lnx_b": dlnxb, "attn_sinks": dsinks, "ln1_g": dln1g, "ln1_b": dln1b,
             "ln2_g": dln2g, "ln2_b": dln2b}
    small_names = list(small)
    packed = jnp.concatenate([small[nm].reshape(1, -1) for nm in small_names], axis=1)
    (packed_all,) = _all_gather("gather_small_grads", [packed])
    packed_all = packed_all.reshape(N_DEV, -1)

    grads, deltas, new_m, new_v = {}, {}, {}, {}

    def put(nm, res):
        shape = weights[nm].shape
        grads[nm], deltas[nm], new_m[nm], new_v[nm] = [a.reshape(shape) for a in res]

    def big_update(nm, parts):
        put(nm, _adamw("adamw_" + nm, weights[nm][0], mom_m[nm][0], mom_v[nm][0], parts))

    n_mod = w_mod.shape[2]
    dmod_cols = lax.dynamic_slice_in_dim(packed_all[:, _offset(small, small_names, "b_mod"):], me * n_mod, n_mod,
                                         axis=1)
    put("w_mod", _adamw_outer("adamw_w_mod", w_mod[0], m_w_mod[0], v_w_mod[0], cond_all.T, dmod_cols))

    off = 0
    for nm in small_names:
        size = small[nm].size
        wshape = weights[nm].shape
        two_d = (1, size) if nm != "rpb_table" else wshape
        parts = packed_all[:, off:off + size].reshape((N_DEV,) + two_d)
        off += size
        put(nm, _adamw("adamw_" + nm, weights[nm].reshape(two_d), mom_m[nm].reshape(two_d),
                       mom_v[nm].reshape(two_d), parts))

    behind = deltas["w_mod"]
    big_update("w_down", _exchange_wait("scatter_w_down_wait", mlp_send[0], mlp_recv[0], mlp_src[0], mlp_land[0],
                                        True, behind, mlp_cols[0]))
    big_update("w_up", _exchange_wait("scatter_w_up_wait", mlp_send[1], mlp_recv[1], mlp_src[1], mlp_land[1],
                                      True, behind, mlp_cols[1]))
    big_update("w_out", _exchange_wait("scatter_w_out_wait", out_send[0], out_recv[0], out_src[0], out_land[0],
                                       True, behind))
    behind = deltas["w_up"]
    for i, nm in enumerate(in_names):
        big_update(nm, _exchange_wait("scatter_" + nm + "_wait", in_send[i], in_recv[i], in_src[i], in_land[i],
                                      True, behind))

    loss = lax.psum(loss_acc[0, 0], MESH_AXES)
    return (loss, grad_x[None], *[grads[nm] for nm in names], *[deltas[nm] for nm in names],
            *[new_m[nm] for nm in names], *[new_v[nm] for nm in names])


def _offset(small, small_names, name):
    off = 0
    for nm in small_names:
        if nm == name:
            return off
        off += small[nm].size
    raise KeyError(name)
```

```python
import functools
import math

import jax
import jax.numpy as jnp
from jax import lax
from jax.experimental import pallas as pl
from jax.experimental.pallas import tpu as pltpu

F32 = jnp.float32
BF16 = jnp.bfloat16
HI = lax.Precision.HIGHEST
MESH_AXES = ("x", "y", "c")
N_DEV = 8

HEAD = 64
GQA_RATIO = 8
ATT_BLOCK = 128
RPB_MAX_DIST = 128
LN_EPS = 1e-5
LNX_EPS = 64e-5
DEPTH = 1
ALPHA = (2.0 * DEPTH) ** 0.25
CHUNK = 64
REC_HEADS = 16

ADAM_LR = 0.001
ADAM_B1 = 0.9
ADAM_B2 = 0.999
ADAM_EPS = 1e-08
ADAM_WD = 0.01
ADAM_STEP = 10

VMEM_LIMIT = 56 * 1024 * 1024


def _cparams(sem=None):
    return pltpu.CompilerParams(dimension_semantics=sem, vmem_limit_bytes=VMEM_LIMIT)


def _tile(dim, cap):
    best = None
    t = 128
    while t <= min(dim, cap):
        if dim % t == 0:
            best = t
        t += 128
    return best or dim


def _rtile(dim, cap):
    best = None
    t = 8
    while t <= min(dim, cap):
        if dim % t == 0:
            best = t
        t += 8
    return best or dim


def _split2(a):
    hi = a.astype(BF16)
    return hi, (a - hi.astype(F32)).astype(BF16)


def _raw_dot(a, b, ca, cb, prec):
    dims = (((ca,), (cb,)), ((), ()))
    mm = lambda p, q: lax.dot_general(p, q, dims, preferred_element_type=F32)
    if prec == "bf16":
        return mm(a.astype(BF16), b.astype(BF16))
    if prec == "x3":
        (ah, al), (bh, bl) = _split2(a), _split2(b)
        return mm(ah, bh) + (mm(ah, bl) + mm(al, bh))
    if prec == "mask":
        ab = a.astype(BF16)
        b1, b2 = _split2(b)
        b3 = (b - b1.astype(F32) - b2.astype(F32)).astype(BF16)
        return mm(ab, b1) + (mm(ab, b2) + mm(ab, b3))
    if prec == "mb2":
        (ah, al), bb = _split2(a), b.astype(BF16)
        return mm(ah, bb) + mm(al, bb)
    return lax.dot_general(a, b, dims, precision=HI, preferred_element_type=F32)


@functools.partial(jax.custom_vjp, nondiff_argnums=(2, 3, 4))
def _bf16_dot(a, b, ca, cb, prec):
    return _raw_dot(a, b, ca, cb, prec)


def _bf16_dot_fwd(a, b, ca, cb, prec):
    return _raw_dot(a, b, ca, cb, prec), (a, b)


def _bf16_dot_bwd(ca, cb, prec, res, g):
    a, b = res
    if prec == "mask":
        return jnp.zeros_like(a), _bf16_dot(a, g, 1 - ca, 0, prec)
    if prec == "mb2":
        return _bf16_dot(g, b, 1, 1, prec), jnp.zeros_like(b)
    if ca == 1:
        da = _bf16_dot(g, b, 1, 1 - cb, prec)
    else:
        da = _bf16_dot(b, g, 1 - cb, 1, prec)
    if cb == 0:
        db = _bf16_dot(a, g, 1 - ca, 0, prec)
    else:
        db = _bf16_dot(g, a, 0, 1 - ca, prec)
    return da, db


_bf16_dot.defvjp(_bf16_dot_fwd, _bf16_dot_bwd)


def _dot(a, b, ca, cb, prec):
    return _raw_dot(a, b, ca, cb, prec) if prec == "hi" else _bf16_dot(a, b, ca, cb, prec)


def _sigmoid(z):
    return 1.0 / (1.0 + jnp.exp(-z))


def _softplus(z):
    return jnp.maximum(z, 0.0) + jnp.log(1.0 + jnp.exp(-jnp.abs(z)))


def _matmul(name, a, b, mode, out_dtypes, epilogue=None, extras=(), caps=(1024, 1024, 2048), after=None):
    if mode == "nn":
        (m, k), n = a.shape, b.shape[1]
    elif mode == "nt":
        (m, k), n = a.shape, b.shape[0]
    else:
        (k, m), n = a.shape, b.shape[1]
    tm, tn, tk = _tile(m, caps[0]), _tile(n, caps[1]), _tile(k, caps[2])
    nk = k // tk
    ne, no = len(extras), len(out_dtypes)
    ca, cb = {"nn": (1, 0), "nt": (1, 1), "tn": (0, 0)}[mode]

    n_after = 0 if after is None else 1

    def body(a_ref, b_ref, *rest):
        rest = rest[n_after:]
        extra_refs, out_refs, acc = rest[:ne], rest[ne:ne + no], rest[-1]
        kk = pl.program_id(2)
        part = _raw_dot(a_ref[...], b_ref[...], ca, cb, "bf16")

        def finish(total):
            res = epilogue(total, *[e[...] for e in extra_refs]) if epilogue else (total,)
            for o, v in zip(out_refs, res):
                o[...] = v.astype(o.dtype)

        if nk == 1:
            finish(part)
            return

        @pl.when(kk == 0)
        def _():
            acc[...] = part

        @pl.when((kk > 0) & (kk < nk - 1))
        def _():
            acc[...] += part

        @pl.when(kk == nk - 1)
        def _():
            finish(acc[...] + part)

    a_spec = (pl.BlockSpec((tk, tm), lambda i, j, kk: (kk, i)) if mode == "tn"
              else pl.BlockSpec((tm, tk), lambda i, j, kk: (i, kk)))
    b_spec = (pl.BlockSpec((tn, tk), lambda i, j, kk: (j, kk)) if mode == "nt"
              else pl.BlockSpec((tk, tn), lambda i, j, kk: (kk, j)))
    mn_spec = pl.BlockSpec((tm, tn), lambda i, j, kk: (i, j))
    after_specs = [pl.BlockSpec(memory_space=pl.ANY)] * n_after
    outs = pl.pallas_call(
        body, name=name, grid=(m // tm, n // tn, nk),
        in_specs=[a_spec, b_spec] + after_specs + [mn_spec] * ne,
        out_specs=[mn_spec] * no,
        out_shape=[jax.ShapeDtypeStruct((m, n), dt) for dt in out_dtypes],
        scratch_shapes=[pltpu.VMEM((tm, tn), F32)],
        compiler_params=_cparams(("parallel", "parallel", "arbitrary")),
    )(a, b, *([after] * n_after), *extras)
    return outs


def _rowwise(name, fn, rows, bcasts, out_rows, out_accs, tr, halos=()):
    t = rows[0].shape[0]
    nb = t // tr
    n_in = len(rows) + len(halos) + len(bcasts)
    n_ro = len(out_rows)

    def body(*refs):
        ins = [r[...] for r in refs[:n_in]]
        o_refs = refs[n_in:]
        i = pl.program_id(0)
        routs, aouts = fn(i, nb, *ins)
        for ref, v in zip(o_refs[:n_ro], routs):
            ref[...] = v.astype(ref.dtype)
        for ref, v in zip(o_refs[n_ro:], aouts):
            @pl.when(i == 0)
            def _(ref=ref):
                ref[...] = jnp.zeros_like(ref)
            ref[...] += v.reshape(ref.shape)

    in_specs = [pl.BlockSpec((tr, r.shape[1]), lambda i: (i, 0)) for r in rows]
    for arr, which in halos:
        if which == "prev":
            in_specs.append(pl.BlockSpec((8, arr.shape[1]), lambda i: (jnp.maximum(i * (tr // 8) - 1, 0), 0)))
        else:
            in_specs.append(pl.BlockSpec((8, arr.shape[1]),
                                         lambda i: (jnp.minimum((i + 1) * (tr // 8), t // 8 - 1), 0)))
    for bc in bcasts:
        in_specs.append(pl.BlockSpec(bc.shape, lambda i, nd=bc.ndim: (0,) * nd))
    out_specs = [pl.BlockSpec((tr, c), lambda i: (i, 0)) for c, _ in out_rows]
    out_specs += [pl.BlockSpec(s, lambda i, nd=len(s): (0,) * nd) for s in out_accs]
    out_shape = [jax.ShapeDtypeStruct((t, c), dt) for c, dt in out_rows]
    out_shape += [jax.ShapeDtypeStruct(s, F32) for s in out_accs]
    return pl.pallas_call(
        body, name=name, grid=(nb,), in_specs=in_specs, out_specs=out_specs, out_shape=out_shape,
        compiler_params=_cparams(("arbitrary",)),
    )(*rows, *[h[0] for h in halos], *bcasts)


def _shift_prev(x, halo, i):
    rolled = pltpu.roll(x, 1, 0)
    first = jnp.where(i == 0, 0.0, halo[7:8, :])
    row = lax.broadcasted_iota(jnp.int32, x.shape, 0)
    return jnp.where(row == 0, first, rolled)


def _shift_next(x, halo, i, nb):
    rolled = pltpu.roll(x, x.shape[0] - 1, 0)
    last = jnp.where(i == nb - 1, 0.0, halo[0:1, :])
    row = lax.broadcasted_iota(jnp.int32, x.shape, 0)
    return jnp.where(row == x.shape[0] - 1, last, rolled)


def _ln(x, g, b, eps=LN_EPS):
    mu = jnp.mean(x, axis=-1, keepdims=True)
    xc = x - mu
    var = jnp.mean(xc * xc, axis=-1, keepdims=True)
    return xc * lax.rsqrt(var + eps) * g + b


def _embed_math(x, g, b, sc, sh):
    x0 = _ln(x, g, b)
    return x0, x0 * (1.0 + sc) + sh


def _post_math(xin, y, gate, g, b, sc, sh):
    x1 = _ln(ALPHA * xin + (1.0 + gate) * y, g, b)
    return x1, x1 * (1.0 + sc) + sh


def _loss_math(xin, h, tgt, gate, g, b):
    x2 = _ln(ALPHA * xin + (1.0 + gate) * h, g, b)
    err = x2 - tgt
    return 0.5 * jnp.sum(jnp.mean(err * err, axis=-1))


def _rwkv_pre_math(r, k, v, xw, xa, xg, w0, wd, a0, wa, wg, k_k, k_a, seg, seg_t, dot=_dot):
    wpre = -_softplus(-(w0 + dot(jnp.tanh(xw), wd, 1, 0, "x3"))) - 0.5
    lw = -jnp.exp(wpre)
    a = _sigmoid(a0 + dot(xa, wa, 1, 0, "x3"))
    g = dot(_sigmoid(xg), wg, 1, 0, "x3")
    kk = k * k_k
    norm = jnp.sqrt(dot(kk * kk, seg, 1, 0, "mb2"))
    kkn = kk * dot(1.0 / jnp.maximum(norm, 1e-12), seg_t, 1, 0, "mb2")
    k2 = k * (1.0 + (a - 1.0) * k_a)
    return r, lw, k2, v, kkn, kkn * a, g


def _rwkv_post_math(y, r, k2, v, g, lnx_g, lnx_b, r_k, seg, seg_t, dot=_dot):
    inv = 1.0 / HEAD
    spread = lambda z: dot(dot(z, seg, 1, 0, "mb2"), seg_t, 1, 0, "mb2")
    mu = spread(y) * inv
    yc = y - mu
    var = spread(yc * yc) * inv
    yn = yc * lax.rsqrt(var + LNX_EPS) * lnx_g + lnx_b
    bonus = spread(r * k2 * r_k) * v
    return (yn + bonus) * g


def _chunk_math(s0, r, lw, k, v, kk, b, dot=_dot):
    n = len(r)
    hs = range(n)
    c = r[0].shape[0]
    ti = lax.broadcasted_iota(jnp.int32, (2 * c, 2 * c), 0)
    tj = lax.broadcasted_iota(jnp.int32, (2 * c, 2 * c), 1)
    tt, jj = ti & (c - 1), tj & (c - 1)
    quad = jnp.where(ti < c, (tt > jj).astype(F32), (tt >= jj).astype(F32))
    incl = quad[c:, :c]
    eye = (ti[:c, :c] == tj[:c, :c]).astype(F32)
    cl = [dot(incl, lw[i], 1, 0, "mask") for i in hs]
    ge = [jnp.exp(cl[i]) for i in hs]
    gi = [jnp.exp(-cl[i]) for i in hs]
    ar = [jnp.concatenate([-kk[i] * jnp.exp(cl[i] - lw[i]), r[i] * ge[i]], axis=0) for i in hs]
    kb = [jnp.concatenate([k[i] * gi[i], b[i] * gi[i]], axis=0) for i in hs]
    m = [dot(ar[i], kb[i], 1, 1, "x3") * quad for i in hs]
    ars0 = [dot(ar[i], s0[i], 1, 1, "bf16") for i in hs]
    mv = [dot(m[i][:c, :c], v[i], 1, 0, "bf16") for i in hs]
    pw = [m[i][:c, c:] for i in hs]
    inv = [eye + pw[i] for i in hs]
    for _ in range(int(math.log2(c)) - 1):
        pw = [dot(pw[i], pw[i], 1, 0, "bf16") for i in hs]
        inv = [inv[i] + dot(inv[i], pw[i], 1, 0, "bf16") for i in hs]
    u = [dot(inv[i], ars0[i][:c] + mv[i], 1, 0, "bf16") for i in hs]
    vu = [jnp.concatenate([v[i], u[i]], axis=0) for i in hs]
    y = [ars0[i][c:] + dot(m[i][c:], vu[i], 1, 0, "bf16") for i in hs]
    s1 = [(s0[i] + dot(vu[i], kb[i], 0, 0, "x3")) * ge[i][c - 1:c, :] for i in hs]
    return y, s1


def _attn_math(q, kp, kc, vp, vc, bias, sinks, first, dot=_dot):
    hq = q.shape[1] // HEAD
    hkv = kc.shape[1] // HEAD
    group = hq // hkv
    cols = group * ATT_BLOCK
    kj = lax.broadcasted_iota(jnp.int32, (2 * ATT_BLOCK, cols), 0)
    qi = lax.broadcasted_iota(jnp.int32, (2 * ATT_BLOCK, cols), 1) & (ATT_BLOCK - 1)
    dist = qi + ATT_BLOCK - kj
    valid = (dist >= 0) & (dist < ATT_BLOCK) & (jnp.logical_not(first) | (kj >= ATT_BLOCK))
    eye = (lax.broadcasted_iota(jnp.int32, (ATT_BLOCK, ATT_BLOCK), 0)
           == lax.broadcasted_iota(jnp.int32, (ATT_BLOCK, ATT_BLOCK), 1)).astype(F32)
    outs = []
    for j in range(hkv):
        heads = range(j * group, (j + 1) * group)
        kband = jnp.concatenate([kp[:, j * HEAD:(j + 1) * HEAD], kc[:, j * HEAD:(j + 1) * HEAD]], axis=0)
        vband = jnp.concatenate([vp[:, j * HEAD:(j + 1) * HEAD], vc[:, j * HEAD:(j + 1) * HEAD]], axis=0)
        qg = jnp.concatenate([q[:, h * HEAD:(h + 1) * HEAD] for h in heads], axis=0)
        bias_g = jnp.concatenate([bias[h] for h in heads], axis=1)
        sink = jnp.concatenate([jnp.broadcast_to(sinks[0:1, h:h + 1], (1, ATT_BLOCK)) for h in heads], axis=1)
        s = dot(kband, qg, 1, 1, "bf16") * (HEAD ** -0.5) + bias_g
        s = jnp.where(valid, s, -1e30)
        m = jnp.maximum(jnp.max(s, axis=0, keepdims=True), sink)
        e = jnp.exp(s - m)
        p = e / (jnp.sum(e, axis=0, keepdims=True) + jnp.exp(sink - m))
        o_t = dot(vband, p, 0, 0, "bf16")
        outs += [dot(eye, o_t[:, g * ATT_BLOCK:(g + 1) * ATT_BLOCK], 1, 1, "bf16") for g in range(group)]
    return jnp.concatenate(outs, axis=1)


def _rec_specs(t, da, gh, reverse):
    nc = t // CHUNK
    if reverse:
        return pl.BlockSpec((CHUNK, gh * HEAD), lambda hg, c: (nc - 1 - c, hg))
    return pl.BlockSpec((CHUNK, gh * HEAD), lambda hg, c: (c, hg))


def _rec_fwd(r, lw, k, v, kk, b):
    t, da = r.shape
    h = da // HEAD
    gh = min(REC_HEADS, h)
    nc = t // CHUNK

    def body(r_ref, lw_ref, k_ref, v_ref, kk_ref, b_ref, y_ref, s0_ref, state):
        @pl.when(pl.program_id(1) == 0)
        def _():
            state[...] = jnp.zeros_like(state)

        sls = [slice(i * HEAD, (i + 1) * HEAD) for i in range(gh)]
        heads = lambda ref: [ref[:, sl] for sl in sls]
        s0 = [state[i] for i in range(gh)]
        y, s1 = _chunk_math(s0, heads(r_ref), heads(lw_ref), heads(k_ref), heads(v_ref), heads(kk_ref),
                            heads(b_ref), dot=_raw_dot)
        for i, sl in enumerate(sls):
            s0_ref[0, i] = s0[i]
            y_ref[:, sl] = y[i]
            state[i] = s1[i]

    spec = _rec_specs(t, da, gh, False)
    return pl.pallas_call(
        body, name="rwkv_recurrence_fwd", grid=(h // gh, nc),
        in_specs=[spec] * 6,
        out_specs=[spec, pl.BlockSpec((1, gh, HEAD, HEAD), lambda hg, c: (c, hg, 0, 0))],
        out_shape=[jax.ShapeDtypeStruct((t, da), F32), jax.ShapeDtypeStruct((nc, h, HEAD, HEAD), F32)],
        scratch_shapes=[pltpu.VMEM((gh, HEAD, HEAD), F32)],
        compiler_params=_cparams(("parallel", "arbitrary")),
    )(r, lw, k, v, kk, b)


def _rec_bwd(r, lw, k, v, kk, b, s0s, dy):
    t, da = r.shape
    h = da // HEAD
    gh = min(REC_HEADS, h)
    nc = t // CHUNK

    def body(r_ref, lw_ref, k_ref, v_ref, kk_ref, b_ref, dy_ref, s0_ref,
             dr_ref, dlw_ref, dk_ref, dv_ref, dkk_ref, db_ref, dstate):
        @pl.when(pl.program_id(1) == 0)
        def _():
            dstate[...] = jnp.zeros_like(dstate)

        sls = [slice(i * HEAD, (i + 1) * HEAD) for i in range(gh)]
        heads = lambda ref: [ref[:, sl] for sl in sls]
        _, vjp = jax.vjp(_chunk_math, [s0_ref[0, i] for i in range(gh)], heads(r_ref), heads(lw_ref),
                         heads(k_ref), heads(v_ref), heads(kk_ref), heads(b_ref))
        grads = vjp((heads(dy_ref), [dstate[i] for i in range(gh)]))
        for i, sl in enumerate(sls):
            dstate[i] = grads[0][i]
            for ref, val in zip((dr_ref, dlw_ref, dk_ref, dv_ref, dkk_ref, db_ref), grads[1:]):
                ref[:, sl] = val[i]

    spec = _rec_specs(t, da, gh, True)
    return pl.pallas_call(
        body, name="rwkv_recurrence_bwd", grid=(h // gh, nc),
        in_specs=[spec] * 7 + [pl.BlockSpec((1, gh, HEAD, HEAD), lambda hg, c: (nc - 1 - c, hg, 0, 0))],
        out_specs=[spec] * 6,
        out_shape=[jax.ShapeDtypeStruct((t, da), F32)] * 6,
        scratch_shapes=[pltpu.VMEM((gh, HEAD, HEAD), F32)],
        compiler_params=_cparams(("parallel", "arbitrary")),
    )(r, lw, k, v, kk, b, dy, s0s)


def _attn_specs(t, hq_w, hkv_w):
    nb = t // ATT_BLOCK
    cur = lambda w: pl.BlockSpec((ATT_BLOCK, w), lambda n: (n, 0))
    prev = lambda w: pl.BlockSpec((ATT_BLOCK, w), lambda n: (jnp.maximum(n - 1, 0), 0))
    return nb, cur, prev


def _attn_fwd(q, kb, vb, bias, sinks):
    t, qw = q.shape
    kw = kb.shape[1]
    nb, cur, prev = _attn_specs(t, qw, kw)

    def body(q_ref, kp_ref, kc_ref, vp_ref, vc_ref, bias_ref, sink_ref, o_ref):
        first = pl.program_id(0) == 0
        o = _attn_math(q_ref[...], kp_ref[...], kc_ref[...], vp_ref[...], vc_ref[...],
                       bias_ref[...], sink_ref[...], first, dot=_raw_dot)
        o_ref[...] = o.astype(o_ref.dtype)

    full = lambda a: pl.BlockSpec(a.shape, lambda n, nd=a.ndim: (0,) * nd)
    return pl.pallas_call(
        body, name="swa_attention_fwd", grid=(nb,),
        in_specs=[cur(qw), prev(kw), cur(kw), prev(kw), cur(kw), full(bias), full(sinks)],
        out_specs=cur(qw), out_shape=jax.ShapeDtypeStruct((t, qw), BF16),
        compiler_params=_cparams(("parallel",)),
    )(q, kb, kb, vb, vb, bias, sinks)


def _attn_bwd(q, kb, vb, bias, sinks, do, col_block):
    t, qw = q.shape
    kw = kb.shape[1]
    nb, cur, prev = _attn_specs(t, qw, kw)

    def body(q_ref, kp_ref, kc_ref, vp_ref, vc_ref, bias_ref, sink_ref, do_ref,
             dq_ref, dkp_ref, dkc_ref, dvp_ref, dvc_ref, dbias_ref, dsink_ref):
        n = pl.program_id(0)
        first = n == 0
        fn = functools.partial(_attn_math, first=first)
        _, vjp = jax.vjp(fn, q_ref[...], kp_ref[...], kc_ref[...], vp_ref[...], vc_ref[...],
                         bias_ref[...], sink_ref[...])
        dq, dkp, dkc, dvp, dvc, dbias, dsink = vjp(do_ref[...].astype(F32))
        dq_ref[...] = dq.astype(dq_ref.dtype)
        dkp_ref[...] = dkp
        dkc_ref[...] = dkc
        dvp_ref[...] = dvp
        dvc_ref[...] = dvc

        @pl.when(first)
        def _():
            dbias_ref[...] = jnp.zeros_like(dbias_ref)
            dsink_ref[...] = jnp.zeros_like(dsink_ref)

        dbias_ref[...] += dbias
        dsink_ref[...] += dsink

    full = lambda a: pl.BlockSpec(a.shape, lambda n, nd=a.ndim: (0,) * nd)
    kshape = jax.ShapeDtypeStruct((t, kw), F32)
    return pl.pallas_call(
        body, name="swa_attention_bwd", grid=(nb,),
        in_specs=[cur(qw), prev(kw), cur(kw), prev(kw), cur(kw), full(bias), full(sinks),
                  pl.BlockSpec((ATT_BLOCK, qw), lambda n: (n, col_block))],
        out_specs=[cur(qw), cur(kw), cur(kw), cur(kw), cur(kw), full(bias), full(sinks)],
        out_shape=[jax.ShapeDtypeStruct((t, qw), BF16), kshape, kshape, kshape, kshape,
                   jax.ShapeDtypeStruct(bias.shape, F32), jax.ShapeDtypeStruct(sinks.shape, F32)],
        compiler_params=_cparams(("arbitrary",)),
    )(q, kb, kb, vb, vb, bias, sinks, do)


def _bucket_onehot():
    qi = jnp.arange(ATT_BLOCK)[None, :]
    kj = jnp.arange(2 * ATT_BLOCK)[:, None]
    n = jnp.maximum(qi + ATT_BLOCK - kj, 0)
    buckets, max_exact = 32, 16
    nf = jnp.maximum(n, 1).astype(F32)
    large = max_exact + (jnp.log(nf / max_exact) / math.log(RPB_MAX_DIST / max_exact)
                         * (buckets - max_exact)).astype(jnp.int32)
    bucket = jnp.where(n < max_exact, n, jnp.minimum(large, buckets - 1)).reshape(-1)
    return (bucket[None, :] == jnp.arange(buckets)[:, None]).astype(F32)


def _small_dot(name, a, b, ca, cb):
    m = a.shape[1 - ca]
    n = b.shape[1 - cb]

    def body(a_ref, b_ref, o_ref):
        o_ref[...] = _raw_dot(a_ref[...], b_ref[...], ca, cb, "hi")

    return pl.pallas_call(body, name=name, out_shape=jax.ShapeDtypeStruct((m, n), F32),
                          compiler_params=_cparams())(a, b)


def _mod_fwd(c_all, w_mod):
    d, n = w_mod.shape
    tn = _tile(n, 512)

    def body(c_ref, w_ref, o_ref, cond_ref):
        cv = c_ref[...]
        cond = cv * _sigmoid(cv)
        cond_ref[...] = cond
        o_ref[...] = _raw_dot(cond, w_ref[...], 1, 0, "hi")

    return pl.pallas_call(
        body, name="adaln_mod_fwd", grid=(n // tn,),
        in_specs=[pl.BlockSpec(c_all.shape, lambda j: (0, 0)), pl.BlockSpec((d, tn), lambda j: (0, j))],
        out_specs=[pl.BlockSpec((c_all.shape[0], tn), lambda j: (0, j)),
                   pl.BlockSpec(c_all.shape, lambda j: (0, 0))],
        out_shape=[jax.ShapeDtypeStruct((c_all.shape[0], n), F32), jax.ShapeDtypeStruct(c_all.shape, F32)],
        compiler_params=_cparams(("arbitrary",)),
    )(c_all, w_mod)


def _adam_math(w, g, m, v):
    m = ADAM_B1 * m + (1.0 - ADAM_B1) * g
    v = ADAM_B2 * v + (1.0 - ADAM_B2) * (g * g)
    m_hat = m / (1.0 - ADAM_B1 ** ADAM_STEP)
    v_hat = v / (1.0 - ADAM_B2 ** ADAM_STEP)
    delta = -ADAM_LR * (m_hat / (jnp.sqrt(v_hat) + ADAM_EPS) + ADAM_WD * w)
    return delta, m, v


def _adamw(name, w, m, v, gparts):
    r, c = w.shape
    p = gparts.shape[0]
    tr = _rtile(r, max(8, (1 << 18) // max(c, 1) // 8 * 8))

    def body(w_ref, m_ref, v_ref, g_ref, go_ref, d_ref, mo_ref, vo_ref):
        g = g_ref[0].astype(F32)
        for s in range(1, p):
            g = g + g_ref[s].astype(F32)
        delta, mn, vn = _adam_math(w_ref[...], g, m_ref[...], v_ref[...])
        go_ref[...] = g
        d_ref[...] = delta
        mo_ref[...] = mn
        vo_ref[...] = vn

    spec = pl.BlockSpec((tr, c), lambda i: (i, 0))
    return pl.pallas_call(
        body, name=name, grid=(r // tr,),
        in_specs=[spec, spec, spec, pl.BlockSpec((p, tr, c), lambda i: (0, i, 0))],
        out_specs=[spec] * 4, out_shape=[jax.ShapeDtypeStruct((r, c), F32)] * 4,
        compiler_params=_cparams(("parallel",)),
    )(w, m, v, gparts)


def _adamw_outer(name, w, m, v, cond_t, dmod):
    d, n = w.shape
    tr, tn = _rtile(d, 512), _tile(n, 1024)

    def body(w_ref, m_ref, v_ref, c_ref, dm_ref, go_ref, d_ref, mo_ref, vo_ref):
        g = _raw_dot(c_ref[...], dm_ref[...], 1, 0, "hi")
        delta, mn, vn = _adam_math(w_ref[...], g, m_ref[...], v_ref[...])
        go_ref[...] = g
        d_ref[...] = delta
        mo_ref[...] = mn
        vo_ref[...] = vn

    spec = pl.BlockSpec((tr, tn), lambda i, j: (i, j))
    return pl.pallas_call(
        body, name=name, grid=(d // tr, n // tn),
        in_specs=[spec, spec, spec, pl.BlockSpec((tr, cond_t.shape[1]), lambda i, j: (i, 0)),
                  pl.BlockSpec((dmod.shape[0], tn), lambda i, j: (0, j))],
        out_specs=[spec] * 4, out_shape=[jax.ShapeDtypeStruct((d, n), F32)] * 4,
        compiler_params=_cparams(("parallel", "parallel")),
    )(w, m, v, cond_t, dmod)


def _all_gather(name, arrays):
    n = len(arrays)

    def body(*refs):
        ins, outs = refs[:n], refs[n:2 * n]
        send_sems, recv_sems, local_sems = refs[2 * n:]
        x, y, c = lax.axis_index("x"), lax.axis_index("y"), lax.axis_index("c")
        me, sibling = (x, y, c), (x, y, 1 - c)
        chips = [(1 - x, y), (x, 1 - y), (1 - x, 1 - y)]

        def copy(a, k, block, to, src=None):
            rows = outs[a].at[4 * block[0] + 2 * block[1] + block[2]]
            return pltpu.make_async_remote_copy(
                src_ref=rows if src is None else src, dst_ref=rows, send_sem=send_sems.at[a, k],
                recv_sem=recv_sems.at[a, k], device_id=to, device_id_type=pl.DeviceIdType.MESH)

        mine = [pltpu.make_async_copy(ins[a], outs[a].at[4 * x + 2 * y + c], local_sems.at[a]) for a in range(n)]
        for cp in mine:
            cp.start()
        first = []
        for a in range(n):
            first.append(copy(a, 0, me, sibling, src=ins[a]))
            first += [copy(a, 1 + j, me, (*chip, c), src=ins[a]) for j, chip in enumerate(chips)]
        for cp in first:
            cp.start()
        passed = []
        for a in range(n):
            for j, chip in enumerate(chips):
                copy(a, 1 + j, (*chip, c), me).wait_recv()
                passed.append(copy(a, 4 + j, (*chip, c), sibling))
                passed[-1].start()
        for a in range(n):
            copy(a, 0, sibling, me).wait_recv()
            for j, chip in enumerate(chips):
                copy(a, 4 + j, (*chip, 1 - c), me).wait_recv()
        for cp in first + passed:
            cp.wait_send()
        for cp in mine:
            cp.wait()

    any_spec = pl.BlockSpec(memory_space=pl.ANY)
    return pl.pallas_call(
        body, name=name, in_specs=[any_spec] * n, out_specs=[any_spec] * n,
        out_shape=[jax.ShapeDtypeStruct((N_DEV,) + a.shape, a.dtype) for a in arrays],
        scratch_shapes=[pltpu.SemaphoreType.DMA((n, N_DEV - 1)), pltpu.SemaphoreType.DMA((n, N_DEV - 1)),
                        pltpu.SemaphoreType.DMA((n,))],
    )(*arrays)


def _peer(p):
    x, y, c = lax.axis_index("x"), lax.axis_index("y"), lax.axis_index("c")
    px, py, pc = x ^ ((p >> 2) & 1), y ^ ((p >> 1) & 1), c ^ (p & 1)
    return (px, py, pc), 4 * px + 2 * py + pc


def _split_copy(src_ref, land_ref, send_sems, recv_sems, p, scatter, arriving, cols=None):
    x, y, c = lax.axis_index("x"), lax.axis_index("y"), lax.axis_index("c")
    me = 4 * x + 2 * y + c
    dev, idx = _peer(p)
    block = lambda ref, d: ref.at[:, pl.ds(pl.multiple_of(d * cols, cols), cols)] if cols else ref.at[d]
    if scatter:
        src, dst = block(src_ref, idx), land_ref.at[idx if arriving else me]
    else:
        src, dst = src_ref, block(land_ref, idx if arriving else me)
    return pltpu.make_async_remote_copy(
        src_ref=src, dst_ref=dst, send_sem=send_sems.at[p - 1], recv_sem=recv_sems.at[p - 1], device_id=dev,
        device_id_type=pl.DeviceIdType.MESH)


_HBM_SPEC = pl.BlockSpec(memory_space=pltpu.HBM)
_SEM_SPEC = pl.BlockSpec(memory_space=pltpu.SEMAPHORE)
_DATAFLOW = pltpu.SideEffectType.DATAFLOW_SIDE_EFFECTING


def _own_copy(src_ref, land_ref, send_sems, scatter, cols):
    me = 4 * lax.axis_index("x") + 2 * lax.axis_index("y") + lax.axis_index("c")
    block = lambda ref: ref.at[:, pl.ds(pl.multiple_of(me * cols, cols), cols)] if cols else ref.at[me]
    src, dst = (block(src_ref), land_ref.at[me]) if scatter else (src_ref, block(land_ref))
    return pltpu.make_async_copy(src, dst, send_sems.at[N_DEV - 1])


def _exchange_start(name, srcs, lands, scatter, after, cols=None):
    n = len(srcs)
    cols = cols or [None] * n

    def body(*refs):
        src_refs, land_refs = refs[:n], refs[n:2 * n]
        outs = refs[2 * n + 1:]
        send, recv, token = outs[:n], outs[n:2 * n], outs[-1]
        for a in range(n):
            for p in range(1, N_DEV):
                _split_copy(src_refs[a], land_refs[a], send[a], recv[a], p, scatter, False, cols[a]).start()
            _own_copy(src_refs[a], land_refs[a], send[a], scatter, cols[a]).start()
        token[...] = jnp.zeros_like(token)

    sems = [pltpu.SemaphoreType.DMA((N_DEV,))] * n + [pltpu.SemaphoreType.DMA((N_DEV - 1,))] * n
    hbm = [pltpu.HBM(a.shape, a.dtype) for a in list(srcs) + list(lands)]
    res = pl.pallas_call(
        body, name=name,
        out_shape=sems + hbm + [jax.ShapeDtypeStruct((8, 128), F32)],
        in_specs=[_HBM_SPEC] * (2 * n) + [pl.BlockSpec(memory_space=pl.ANY)],
        out_specs=[_SEM_SPEC] * (2 * n) + [_HBM_SPEC] * (2 * n) + [pl.BlockSpec(memory_space=pltpu.VMEM)],
        input_output_aliases={i: 2 * n + i for i in range(2 * n)},
        compiler_params=pltpu.CompilerParams(has_side_effects=_DATAFLOW),
    )(*[pltpu.with_memory_space_constraint(a, pltpu.HBM) for a in list(srcs) + list(lands)], after)
    return res[:n], res[n:2 * n], res[2 * n:3 * n], res[3 * n:4 * n], res[-1]


def _exchange_wait(name, send_sem, recv_sem, src, land, scatter, after, cols=None):
    def body(src_ref, land_ref, send, recv, after_ref, src_out, land_out):
        for p in range(1, N_DEV):
            cp = _split_copy(src_ref, land_ref, send, recv, p, scatter, True, cols)
            cp.wait_send()
            cp.wait_recv()
        _own_copy(src_ref, land_ref, send, scatter, cols).wait()

    return pl.pallas_call(
        body, name=name,
        out_shape=(pltpu.HBM(src.shape, src.dtype), pltpu.HBM(land.shape, land.dtype)),
        in_specs=[_HBM_SPEC, _HBM_SPEC, _SEM_SPEC, _SEM_SPEC, pl.BlockSpec(memory_space=pl.ANY)],
        out_specs=(_HBM_SPEC, _HBM_SPEC), input_output_aliases={0: 0, 1: 1},
        compiler_params=pltpu.CompilerParams(has_side_effects=_DATAFLOW),
    )(src, land, send_sem, recv_sem, after)[1]


def _cols_to_shards(a):
    r, c = a.shape
    return a.reshape(r, N_DEV, c // N_DEV).transpose(1, 0, 2)


def _shards_to_cols(a):
    d, r, n = a.shape
    return a.transpose(1, 0, 2).reshape(r, d * n)


def kernel(x, c, ln_emb_g, ln_emb_b, rpb_table, w_mod, b_mod, w_in, mu_shift, w0, w_decay_up, a0, w_iclr_up, w_gate_up, k_k, k_a, r_k, lnx_g, lnx_b, attn_sinks, w_out, ln1_g, ln1_b, w_up, w_down, ln2_g, ln2_b, loss_target, m_ln_emb_g, m_ln_emb_b, m_rpb_table, m_w_mod, m_b_mod, m_w_in, m_mu_shift, m_w0, m_w_decay_up, m_a0, m_w_iclr_up, m_w_gate_up, m_k_k, m_k_a, m_r_k, m_lnx_g, m_lnx_b, m_attn_sinks, m_w_out, m_ln1_g, m_ln1_b, m_w_up, m_w_down, m_ln2_g, m_ln2_b, v_ln_emb_g, v_ln_emb_b, v_rpb_table, v_w_mod, v_b_mod, v_w_in, v_mu_shift, v_w0, v_w_decay_up, v_a0, v_w_iclr_up, v_w_gate_up, v_k_k, v_k_a, v_r_k, v_lnx_g, v_lnx_b, v_attn_sinks, v_w_out, v_ln1_g, v_ln1_b, v_w_up, v_w_down, v_ln2_g, v_ln2_b):
    names = ["ln_emb_g", "ln_emb_b", "rpb_table", "w_mod", "b_mod", "w_in", "mu_shift", "w0", "w_decay_up",
             "a0", "w_iclr_up", "w_gate_up", "k_k", "k_a", "r_k", "lnx_g", "lnx_b", "attn_sinks", "w_out",
             "ln1_g", "ln1_b", "w_up", "w_down", "ln2_g", "ln2_b"]
    env = dict(locals())
    weights = {nm: env[nm] for nm in names}
    mom_m = {nm: env["m_" + nm] for nm in names}
    mom_v = {nm: env["v_" + nm] for nm in names}

    t, d = x.shape[1], x.shape[2]
    da = d // 2
    h_a = da // HEAD
    hq = (d - da) // HEAD
    hkv = hq // GQA_RATIO
    l_w, l_a, l_g = w_decay_up.shape[1], w_iclr_up.shape[1], w_gate_up.shape[1]
    o_w, o_a, o_g = 3 * da, 3 * da + l_w, 3 * da + l_w + l_a
    n_rwkv = o_g + l_g
    o_kb, o_vb = n_rwkv + hq * HEAD, n_rwkv + hq * HEAD + hkv * HEAD
    me = 4 * lax.axis_index("x") + 2 * lax.axis_index("y") + lax.axis_index("c")

    x2d, tgt = x[0], loss_target[0]
    row = lambda a: a.reshape(1, -1)
    seg = (jnp.arange(da)[:, None] // HEAD == jnp.arange(h_a)[None, :]).astype(F32)
    seg_t = seg.T

    (c_all,) = _all_gather("gather_cond", [c])
    c_all = c_all.reshape(N_DEV, d)
    mod_rows, cond_all = _mod_fwd(c_all, w_mod[0])
    gathered = _all_gather("gather_weights", [
        mod_rows, w_in[0].astype(BF16), w_decay_up[0], w_iclr_up[0], w_gate_up[0]])
    mod_all, win_g, wd_g, wa_g, wg_g = gathered
    late = [w_out[0].astype(BF16), w_up[0].astype(BF16), w_down[0].astype(BF16)]
    n_up = w_up.shape[2]
    late_cols = [None, n_up, None]
    late_lands = [lax.empty((N_DEV,) + late[0].shape, BF16), lax.empty((d, N_DEV * n_up), BF16),
                  lax.empty((N_DEV,) + late[2].shape, BF16)]
    late_send, late_recv, late_src, late_land, late_token = _exchange_start(
        "gather_late_weights_start", late, late_lands, False, mod_all, late_cols)
    mod = lax.dynamic_index_in_dim(mod_all, me, axis=1, keepdims=False).reshape(1, -1) + b_mod
    mod = mod + late_token[0, 0]
    sh1, sc1, g1, sh2, sc2, g2 = [mod[:, i * d:(i + 1) * d] for i in range(6)]
    w_in_f = _shards_to_cols(win_g)
    wd_f, wa_f, wg_f = _shards_to_cols(wd_g), _shards_to_cols(wa_g), _shards_to_cols(wg_g)

    def late_weight(i, nm, after):
        return _exchange_wait("gather_" + nm + "_wait", late_send[i], late_recv[i], late_src[i], late_land[i],
                              False, after, late_cols[i])

    tr = _rtile(t, 256)
    lng, lnb = row(ln_emb_g), row(ln_emb_b)

    def embed_fn(i, nb, xb, g, b, sc, sh):
        return _embed_math(xb, g, b, sc, sh), ()
    x0, u1 = _rowwise("embed_ln_mod", embed_fn, [x2d], [lng, lnb, sc1, sh1], [(d, F32), (d, BF16)], [], tr)

    (p_rkv,) = _matmul("in_proj_rkv", u1, w_in_f[:, :o_w], "nn", [F32])
    (p_lora,) = _matmul("in_proj_lora", u1, w_in_f[:, o_w:n_rwkv], "nn", [F32])
    (q,) = _matmul("in_proj_q", u1, w_in_f[:, n_rwkv:o_kb], "nn", [F32])
    (p_kv,) = _matmul("in_proj_kv", u1, w_in_f[:, o_kb:], "nn", [F32])
    p_w, p_a, p_g = p_lora[:, :l_w], p_lora[:, l_w:l_w + l_a], p_lora[:, l_w + l_a:]
    kb, vb = p_kv[:, :hkv * HEAD], p_kv[:, hkv * HEAD:]
    mu_rkv, mu_w, mu_a, mu_g = (mu_shift[:, :o_w], mu_shift[:, o_w:o_a], mu_shift[:, o_a:o_g],
                                mu_shift[:, o_g:n_rwkv])
    pre_params = [w0, wd_f, a0, wa_f, wg_f, k_k, k_a, seg, seg_t]
    tr_pre = _rtile(t, 128)

    def shifted(i, blocks, halos, mus):
        return [xb + (_shift_prev(xb, hb, i) - xb) * mb for xb, hb, mb in zip(blocks, halos, mus)]

    def split3(a):
        return a[:, :da], a[:, da:2 * da], a[:, 2 * da:]

    def pre_fn(i, nb, b_rkv, b_w, b_a, b_g, h_rkv, h_w, h_a_, h_g, m_rkv, m_w, m_a, m_g, *params):
        s_rkv, s_w, s_a, s_g = shifted(i, [b_rkv, b_w, b_a, b_g], [h_rkv, h_w, h_a_, h_g],
                                       [m_rkv, m_w, m_a, m_g])
        return _rwkv_pre_math(*split3(s_rkv), s_w, s_a, s_g, *params, dot=_raw_dot), ()

    pre_rows = [p_rkv, p_w, p_a, p_g]
    pre_halos = [(a, "prev") for a in pre_rows]
    r_, lw_, k2_, v_, kk_, b_, gate_ = _rowwise(
        "rwkv_pre", pre_fn, pre_rows, [mu_rkv, mu_w, mu_a, mu_g] + pre_params,
        [(da, F32)] * 7, [], tr_pre, halos=pre_halos)

    y_rec, s0s = _rec_fwd(r_, lw_, k2_, v_, kk_, b_)

    rk_flat = r_k.reshape(1, da)
    post_params = [lnx_g, lnx_b, rk_flat, seg, seg_t]

    def post_fn(i, nb, yb, rb, kb_, vb_, gb, *params):
        return (_rwkv_post_math(yb, rb, kb_, vb_, gb, *params, dot=_raw_dot),), ()
    (ya,) = _rowwise("rwkv_post", post_fn, [y_rec, r_, k2_, v_, gate_], post_params, [(da, BF16)], [], tr_pre)

    onehot = _bucket_onehot()
    bias = _small_dot("rpb_gather", rpb_table, onehot, 0, 0)
    bias = bias.reshape(hq, 2 * ATT_BLOCK, ATT_BLOCK)
    yb = _attn_fwd(q, kb, vb, bias, attn_sinks)

    mix_in = jnp.concatenate([ya, yb], axis=1)
    w_out_f = late_weight(0, "w_out", mix_in).reshape(d, d)
    (mix,) = _matmul("out_proj", mix_in, w_out_f, "nn", [F32])

    def post1_fn(i, nb, xin, yv, gate, g, b, sc, sh):
        return _post_math(xin, yv, gate, g, b, sc, sh), ()
    x1, u2 = _rowwise("ln1_mod", post1_fn, [x0, mix], [g1, ln1_g, ln1_b, sc2, sh2],
                      [(d, F32), (d, BF16)], [], tr)

    def relu2(acc):
        rl = jnp.maximum(acc, 0.0)
        return acc, rl * rl
    w_up_f = late_weight(1, "w_up", u2)
    hpre, hact = _matmul("mlp_up", u2, w_up_f, "nn", [F32, BF16], epilogue=relu2)
    w_down_f = late_weight(2, "w_down", hact).reshape(-1, d)
    (hmlp,) = _matmul("mlp_down", hact, w_down_f, "nn", [F32])

    def loss_fn(i, nb, xin, hv, tg, gate, g, b):
        val, vjp = jax.vjp(_loss_math, xin, hv, tg, gate, g, b)
        dxin, dh, _, dgate, dg, db = vjp(jnp.ones((), F32))
        return (dxin, dh), (val, dgate, dg, db)
    dx1, dh, loss_acc, dg2, dln2g, dln2b = _rowwise(
        "ln2_loss", loss_fn, [x1, hmlp, tgt], [g2, ln2_g, ln2_b], [(d, F32), (d, BF16)],
        [(1, 1), (1, d), (1, d), (1, d)], _rtile(t, 128))

    def drelu2(acc, hp):
        return (acc * 2.0 * jnp.maximum(hp, 0.0),)
    (dhpre,) = _matmul("mlp_down_dgrad", dh, w_down_f, "nt", [BF16], epilogue=drelu2, extras=[hpre])
    (gw_down,) = _matmul("mlp_down_wgrad", hact, dh, "tn", [BF16])
    (du2,) = _matmul("mlp_up_dgrad", dhpre, w_up_f, "nt", [F32])
    (gw_up,) = _matmul("mlp_up_wgrad", u2, dhpre, "tn", [BF16])

    def landing(src):
        return lax.empty(src.shape, src.dtype)
    mlp_src = [gw_down.reshape(N_DEV, -1, d), gw_up]
    mlp_cols = [None, n_up]
    mlp_send, mlp_recv, mlp_src, mlp_land, mlp_token = _exchange_start(
        "scatter_mlp_grads_start", mlp_src, [landing(mlp_src[0]), lax.empty((N_DEV, d, n_up), BF16)], True,
        gw_up, mlp_cols)

    def post1_bwd(i, nb, xin, yv, dx1v, du2v, gate, g, b, sc, sh):
        _, vjp = jax.vjp(_post_math, xin, yv, gate, g, b, sc, sh)
        dxin, dy, dgate, dg, db, dsc, dsh = vjp((dx1v, du2v))
        return (dxin, dy), (dgate, dg, db, dsc, dsh)
    dx0, dmix, dg1, dln1g, dln1b, dsc2, dsh2 = _rowwise(
        "ln1_mod_bwd", post1_bwd, [x0, mix, dx1, du2], [g1 + mlp_token[0, 0], ln1_g, ln1_b, sc2, sh2],
        [(d, F32), (d, BF16)], [(1, d)] * 5, _rtile(t, 128))

    (dmix_in,) = _matmul("out_proj_dgrad", dmix, w_out_f, "nt", [F32])
    (gw_out,) = _matmul("out_proj_wgrad", mix_in, dmix, "tn", [BF16])
    dya = dmix_in[:, :da]
    out_src = [gw_out.reshape(N_DEV, d // N_DEV, d)]
    out_send, out_recv, out_src, out_land, out_token = _exchange_start(
        "scatter_out_grad_start", out_src, [landing(a) for a in out_src], True, gw_out)
    post_params_bwd = [lnx_g, lnx_b, rk_flat + out_token[0, 0], seg, seg_t]

    def post_bwd(i, nb, yb_, rb, kb_, vb_, gb, dyab, *params):
        _, vjp = jax.vjp(_rwkv_post_math, yb_, rb, kb_, vb_, gb, *params)
        dy, dr, dk, dv, dg, dlg, dlb, drk, _, _ = vjp(dyab)
        return (dy, dr, dk, dv, dg), (dlg, dlb, drk)
    dy_rec, dr_e, dk_e, dv_e, dgate, dlnxg, dlnxb, drk = _rowwise(
        "rwkv_post_bwd", post_bwd, [y_rec, r_, k2_, v_, gate_, dya], post_params_bwd,
        [(da, F32)] * 5, [(1, da)] * 3, tr_pre)

    dr_r, dlw_r, dk_r, dv_r, dkk_r, db_r = _rec_bwd(r_, lw_, k2_, v_, kk_, b_, s0s, dy_rec)

    def pre_bwd(i, nb, b_rkv, b_w, b_a, b_g, dr1, dr2, dlw, dk1, dk2, dv1, dv2, dkk, dbb, dgt,
                h_rkv, h_w, h_a_, h_g, m_rkv, m_w, m_a, m_g, *params):
        blocks = [b_rkv, b_w, b_a, b_g]
        prevs = [_shift_prev(xb, hb, i) for xb, hb in zip(blocks, [h_rkv, h_w, h_a_, h_g])]
        mus = [m_rkv, m_w, m_a, m_g]
        s_rkv, s_w, s_a, s_g = [xb + (pb - xb) * mb for xb, pb, mb in zip(blocks, prevs, mus)]
        _, vjp = jax.vjp(_rwkv_pre_math, *split3(s_rkv), s_w, s_a, s_g, *params)
        grads = vjp((dr1 + dr2, dlw, dk1 + dk2, dv1 + dv2, dkk, dbb, dgt))
        g_rkv = jnp.concatenate(grads[:3], axis=1)
        g_w, g_a, g_g = grads[3:6]
        dmu = [jnp.sum(gs * (pb - xb), axis=0, keepdims=True)
               for gs, pb, xb in zip([g_rkv, g_w, g_a, g_g], prevs, blocks)]
        dw0, dwd, da0, dwa, dwg, dkk_, dka = grads[6:13]
        return (g_rkv, g_w, g_a, g_g), (*dmu, dw0, dwd, da0, dwa, dwg, dkk_, dka)

    pre_out = _rowwise(
        "rwkv_pre_bwd", pre_bwd,
        pre_rows + [dr_r, dr_e, dlw_r, dk_r, dk_e, dv_r, dv_e, dkk_r, db_r, dgate],
        [mu_rkv, mu_w, mu_a, mu_g] + pre_params,
        [(o_w, F32), (l_w, F32), (l_a, F32), (l_g, F32)],
        [(1, o_w), (1, l_w), (1, l_a), (1, l_g), (1, da), (l_w, da), (1, da), (l_a, da), (l_g, da),
         (1, da), (1, da)],
        _rtile(t, 64), halos=pre_halos)
    gs_rkv, gs_w, gs_a, gs_g = pre_out[:4]
    dmu_parts = pre_out[4:8]
    dw0, dwd, da0, dwa, dwg, dk_k, dk_a = pre_out[8:]

    def unshift_fn(i, nb, a1, a2, a3, a4, n1, n2, n3, n4, m1, m2, m3, m4):
        outs = [gs * (1.0 - mb) + _shift_next(gs * mb, hb * mb, i, nb)
                for gs, hb, mb in zip([a1, a2, a3, a4], [n1, n2, n3, n4], [m1, m2, m3, m4])]
        return outs, ()
    gs_list = [gs_rkv, gs_w, gs_a, gs_g]
    dp_rkv, dp_w, dp_a, dp_g = _rowwise(
        "token_shift_bwd", unshift_fn, gs_list, [mu_rkv, mu_w, mu_a, mu_g],
        [(o_w, BF16), (l_w, BF16), (l_a, BF16), (l_g, BF16)], [], tr_pre,
        halos=[(a, "next") for a in gs_list])

    dq, dkp, dkc, dvp, dvc, dbias, dsinks = _attn_bwd(q, kb, vb, bias, attn_sinks, dmix_in, 1)
    zpad = jnp.zeros((ATT_BLOCK, kb.shape[1]), F32)
    dkb = (dkc + jnp.concatenate([dkp[ATT_BLOCK:], zpad], axis=0)).astype(BF16)
    dvb = (dvc + jnp.concatenate([dvp[ATT_BLOCK:], zpad], axis=0)).astype(BF16)
    d_rpb = _small_dot("rpb_scatter", onehot, dbias.reshape(hq, -1), 1, 1)

    dp = jnp.concatenate([dp_rkv, dp_w, dp_a, dp_g, dq, dkb, dvb], axis=1)
    (gw_in,) = _matmul("in_proj_wgrad", u1, dp, "tn", [BF16])
    in_names = ["w_in", "w_decay_up", "w_iclr_up", "w_gate_up"]
    in_src = [_cols_to_shards(gw_in), _cols_to_shards(dwd), _cols_to_shards(dwa), _cols_to_shards(dwg)]
    in_send, in_recv, in_src, in_land, in_token = _exchange_start(
        "scatter_in_grads_start", in_src, [landing(a) for a in in_src], True, gw_in)
    (du1,) = _matmul("in_proj_dgrad", dp, w_in_f, "nt", [F32], after=in_token)

    def embed_bwd(i, nb, xb, dx0v, du1v, g, b, sc, sh):
        _, vjp = jax.vjp(_embed_math, xb, g, b, sc, sh)
        dxv, dg, db, dsc, dsh = vjp((dx0v, du1v))
        return (dxv,), (dg, db, dsc, dsh)
    grad_x, dlng, dlnb, dsc1, dsh1 = _rowwise(
        "embed_ln_mod_bwd", embed_bwd, [x2d, dx0, du1], [lng, lnb, sc1, sh1], [(d, F32)], [(1, d)] * 4,
        _rtile(t, 128))

    dmod = jnp.concatenate([dsh1, dsc1, dg1, dsh2, dsc2, dg2], axis=1)
    small = {"ln_emb_g": dlng, "ln_emb_b": dlnb, "rpb_table": d_rpb, "b_mod": dmod,
             "mu_shift": jnp.concatenate(dmu_parts, axis=1), "w0": dw0, "a0": da0, "k_k": dk_k, "k_a": dk_a,
             "r_k": drk, "lnx_g": dlnxg, "lnx_b": dlnxb, "attn_sinks": dsinks, "ln1_g": dln1g, "ln1_b": dln1b,
             "ln2_g": dln2g, "ln2_b": dln2b}
    small_names = list(small)
    packed = jnp.concatenate([small[nm].reshape(1, -1) for nm in small_names], axis=1)
    (packed_all,) = _all_gather("gather_small_grads", [packed])
    packed_all = packed_all.reshape(N_DEV, -1)

    grads, deltas, new_m, new_v = {}, {}, {}, {}

    def put(nm, res):
        shape = weights[nm].shape
        grads[nm], deltas[nm], new_m[nm], new_v[nm] = [a.reshape(shape) for a in res]

    def big_update(nm, parts):
        put(nm, _adamw("adamw_" + nm, weights[nm][0], mom_m[nm][0], mom_v[nm][0], parts))

    n_mod = w_mod.shape[2]
    dmod_cols = lax.dynamic_slice_in_dim(packed_all[:, _offset(small, small_names, "b_mod"):], me * n_mod, n_mod,
                                         axis=1)
    put("w_mod", _adamw_outer("adamw_w_mod", w_mod[0], m_w_mod[0], v_w_mod[0], cond_all.T, dmod_cols))

    off = 0
    for nm in small_names:
        size = small[nm].size
        wshape = weights[nm].shape
        two_d = (1, size) if nm != "rpb_table" else wshape
        parts = packed_all[:, off:off + size].reshape((N_DEV,) + two_d)
        off += size
        put(nm, _adamw("adamw_" + nm, weights[nm].reshape(two_d), mom_m[nm].reshape(two_d),
                       mom_v[nm].reshape(two_d), parts))

    behind = deltas["w_mod"]
    big_update("w_down", _exchange_wait("scatter_w_down_wait", mlp_send[0], mlp_recv[0], mlp_src[0], mlp_land[0],
                                        True, behind, mlp_cols[0]))
    big_update("w_up", _exchange_wait("scatter_w_up_wait", mlp_send[1], mlp_recv[1], mlp_src[1], mlp_land[1],
                                      True, behind, mlp_cols[1]))
    big_update("w_out", _exchange_wait("scatter_w_out_wait", out_send[0], out_recv[0], out_src[0], out_land[0],
                                       True, behind))
    behind = deltas["w_up"]
    for i, nm in enumerate(in_names):
        big_update(nm, _exchange_wait("scatter_" + nm + "_wait", in_send[i], in_recv[i], in_src[i], in_land[i],
                                      True, behind))

    loss = lax.psum(loss_acc[0, 0], MESH_AXES)
    return (loss, grad_x[None], *[grads[nm] for nm in names], *[deltas[nm] for nm in names],
            *[new_m[nm] for nm in names], *[new_v[nm] for nm in names])


def _offset(small, small_names, name):
    off = 0
    for nm in small_names:
        if nm == name:
            return off
        off += small[nm].size
    raise KeyError(name)
```

```python
import functools
import math

import jax
import jax.numpy as jnp
from jax import lax
from jax.experimental import pallas as pl
from jax.experimental.pallas import tpu as pltpu

F32 = jnp.float32
BF16 = jnp.bfloat16
HI = lax.Precision.HIGHEST
MESH_AXES = ("x", "y", "c")
N_DEV = 8

HEAD = 64
GQA_RATIO = 8
ATT_BLOCK = 128
RPB_MAX_DIST = 128
LN_EPS = 1e-5
LNX_EPS = 64e-5
DEPTH = 1
ALPHA = (2.0 * DEPTH) ** 0.25
CHUNK = 64
REC_HEADS = 16

ADAM_LR = 0.001
ADAM_B1 = 0.9
ADAM_B2 = 0.999
ADAM_EPS = 1e-08
ADAM_WD = 0.01
ADAM_STEP = 10

VMEM_LIMIT = 56 * 1024 * 1024


def _cparams(sem=None):
    return pltpu.CompilerParams(dimension_semantics=sem, vmem_limit_bytes=VMEM_LIMIT)


def _tile(dim, cap):
    best = None
    t = 128
    while t <= min(dim, cap):
        if dim % t == 0:
            best = t
        t += 128
    return best or dim


def _rtile(dim, cap):
    best = None
    t = 8
    while t <= min(dim, cap):
        if dim % t == 0:
            best = t
        t += 8
    return best or dim


def _split2(a):
    hi = a.astype(BF16)
    return hi, (a - hi.astype(F32)).astype(BF16)


def _raw_dot(a, b, ca, cb, prec):
    dims = (((ca,), (cb,)), ((), ()))
    mm = lambda p, q: lax.dot_general(p, q, dims, preferred_element_type=F32)
    if prec == "bf16":
        return mm(a.astype(BF16), b.astype(BF16))
    if prec == "x3":
        (ah, al), (bh, bl) = _split2(a), _split2(b)
        return mm(ah, bh) + (mm(ah, bl) + mm(al, bh))
    if prec == "mask":
        ab = a.astype(BF16)
        b1, b2 = _split2(b)
        b3 = (b - b1.astype(F32) - b2.astype(F32)).astype(BF16)
        return mm(ab, b1) + (mm(ab, b2) + mm(ab, b3))
    if prec == "mb2":
        (ah, al), bb = _split2(a), b.astype(BF16)
        return mm(ah, bb) + mm(al, bb)
    return lax.dot_general(a, b, dims, precision=HI, preferred_element_type=F32)


@functools.partial(jax.custom_vjp, nondiff_argnums=(2, 3, 4))
def _bf16_dot(a, b, ca, cb, prec):
    return _raw_dot(a, b, ca, cb, prec)


def _bf16_dot_fwd(a, b, ca, cb, prec):
    return _raw_dot(a, b, ca, cb, prec), (a, b)


def _bf16_dot_bwd(ca, cb, prec, res, g):
    a, b = res
    if prec == "mask":
        return jnp.zeros_like(a), _bf16_dot(a, g, 1 - ca, 0, prec)
    if prec == "mb2":
        return _bf16_dot(g, b, 1, 1, prec), jnp.zeros_like(b)
    if ca == 1:
        da = _bf16_dot(g, b, 1, 1 - cb, prec)
    else:
        da = _bf16_dot(b, g, 1 - cb, 1, prec)
    if cb == 0:
        db = _bf16_dot(a, g, 1 - ca, 0, prec)
    else:
        db = _bf16_dot(g, a, 0, 1 - ca, prec)
    return da, db


_bf16_dot.defvjp(_bf16_dot_fwd, _bf16_dot_bwd)


def _dot(a, b, ca, cb, prec):
    return _raw_dot(a, b, ca, cb, prec) if prec == "hi" else _bf16_dot(a, b, ca, cb, prec)


def _sigmoid(z):
    return 1.0 / (1.0 + jnp.exp(-z))


def _softplus(z):
    return jnp.maximum(z, 0.0) + jnp.log(1.0 + jnp.exp(-jnp.abs(z)))


def _matmul(name, a, b, mode, out_dtypes, epilogue=None, extras=(), caps=(1024, 1024, 2048), after=None):
    if mode == "nn":
        (m, k), n = a.shape, b.shape[1]
    elif mode == "nt":
        (m, k), n = a.shape, b.shape[0]
    else:
        (k, m), n = a.shape, b.shape[1]
    tm, tn, tk = _tile(m, caps[0]), _tile(n, caps[1]), _tile(k, caps[2])
    nk = k // tk
    ne, no = len(extras), len(out_dtypes)
    ca, cb = {"nn": (1, 0), "nt": (1, 1), "tn": (0, 0)}[mode]

    n_after = 0 if after is None else 1

    def body(a_ref, b_ref, *rest):
        rest = rest[n_after:]
        extra_refs, out_refs, acc = rest[:ne], rest[ne:ne + no], rest[-1]
        kk = pl.program_id(2)
        part = _raw_dot(a_ref[...], b_ref[...], ca, cb, "bf16")

        def finish(total):
            res = epilogue(total, *[e[...] for e in extra_refs]) if epilogue else (total,)
            for o, v in zip(out_refs, res):
                o[...] = v.astype(o.dtype)

        if nk == 1:
            finish(part)
            return

        @pl.when(kk == 0)
        def _():
            acc[...] = part

        @pl.when((kk > 0) & (kk < nk - 1))
        def _():
            acc[...] += part

        @pl.when(kk == nk - 1)
        def _():
            finish(acc[...] + part)

    a_spec = (pl.BlockSpec((tk, tm), lambda i, j, kk: (kk, i)) if mode == "tn"
              else pl.BlockSpec((tm, tk), lambda i, j, kk: (i, kk)))
    b_spec = (pl.BlockSpec((tn, tk), lambda i, j, kk: (j, kk)) if mode == "nt"
              else pl.BlockSpec((tk, tn), lambda i, j, kk: (kk, j)))
    mn_spec = pl.BlockSpec((tm, tn), lambda i, j, kk: (i, j))
    after_specs = [pl.BlockSpec(memory_space=pl.ANY)] * n_after
    outs = pl.pallas_call(
        body, name=name, grid=(m // tm, n // tn, nk),
        in_specs=[a_spec, b_spec] + after_specs + [mn_spec] * ne,
        out_specs=[mn_spec] * no,
        out_shape=[jax.ShapeDtypeStruct((m, n), dt) for dt in out_dtypes],
        scratch_shapes=[pltpu.VMEM((tm, tn), F32)],
        compiler_params=_cparams(("parallel", "parallel", "arbitrary")),
    )(a, b, *([after] * n_after), *extras)
    return outs


def _rowwise(name, fn, rows, bcasts, out_rows, out_accs, tr, halos=()):
    t = rows[0].shape[0]
    nb = t // tr
    n_in = len(rows) + len(halos) + len(bcasts)
    n_ro = len(out_rows)

    def body(*refs):
        ins = [r[...] for r in refs[:n_in]]
        o_refs = refs[n_in:]
        i = pl.program_id(0)
        routs, aouts = fn(i, nb, *ins)
        for ref, v in zip(o_refs[:n_ro], routs):
            ref[...] = v.astype(ref.dtype)
        for ref, v in zip(o_refs[n_ro:], aouts):
            @pl.when(i == 0)
            def _(ref=ref):
                ref[...] = jnp.zeros_like(ref)
            ref[...] += v.reshape(ref.shape)

    in_specs = [pl.BlockSpec((tr, r.shape[1]), lambda i: (i, 0)) for r in rows]
    for arr, which in halos:
        if which == "prev":
            in_specs.append(pl.BlockSpec((8, arr.shape[1]), lambda i: (jnp.maximum(i * (tr // 8) - 1, 0), 0)))
        else:
            in_specs.append(pl.BlockSpec((8, arr.shape[1]),
                                         lambda i: (jnp.minimum((i + 1) * (tr // 8), t // 8 - 1), 0)))
    for bc in bcasts:
        in_specs.append(pl.BlockSpec(bc.shape, lambda i, nd=bc.ndim: (0,) * nd))
    out_specs = [pl.BlockSpec((tr, c), lambda i: (i, 0)) for c, _ in out_rows]
    out_specs += [pl.BlockSpec(s, lambda i, nd=len(s): (0,) * nd) for s in out_accs]
    out_shape = [jax.ShapeDtypeStruct((t, c), dt) for c, dt in out_rows]
    out_shape += [jax.ShapeDtypeStruct(s, F32) for s in out_accs]
    return pl.pallas_call(
        body, name=name, grid=(nb,), in_specs=in_specs, out_specs=out_specs, out_shape=out_shape,
        compiler_params=_cparams(("arbitrary",)),
    )(*rows, *[h[0] for h in halos], *bcasts)


def _shift_prev(x, halo, i):
    rolled = pltpu.roll(x, 1, 0)
    first = jnp.where(i == 0, 0.0, halo[7:8, :])
    row = lax.broadcasted_iota(jnp.int32, x.shape, 0)
    return jnp.where(row == 0, first, rolled)


def _shift_next(x, halo, i, nb):
    rolled = pltpu.roll(x, x.shape[0] - 1, 0)
    last = jnp.where(i == nb - 1, 0.0, halo[0:1, :])
    row = lax.broadcasted_iota(jnp.int32, x.shape, 0)
    return jnp.where(row == x.shape[0] - 1, last, rolled)


def _ln(x, g, b, eps=LN_EPS):
    mu = jnp.mean(x, axis=-1, keepdims=True)
    xc = x - mu
    var = jnp.mean(xc * xc, axis=-1, keepdims=True)
    return xc * lax.rsqrt(var + eps) * g + b


def _embed_math(x, g, b, sc, sh):
    x0 = _ln(x, g, b)
    return x0, x0 * (1.0 + sc) + sh


def _post_math(xin, y, gate, g, b, sc, sh):
    x1 = _ln(ALPHA * xin + (1.0 + gate) * y, g, b)
    return x1, x1 * (1.0 + sc) + sh


def _loss_math(xin, h, tgt, gate, g, b):
    x2 = _ln(ALPHA * xin + (1.0 + gate) * h, g, b)
    err = x2 - tgt
    return 0.5 * jnp.sum(jnp.mean(err * err, axis=-1))


def _rwkv_pre_math(r, k, v, xw, xa, xg, w0, wd, a0, wa, wg, k_k, k_a, seg, seg_t, dot=_dot):
    wpre = -_softplus(-(w0 + dot(jnp.tanh(xw), wd, 1, 0, "x3"))) - 0.5
    lw = -jnp.exp(wpre)
    a = _sigmoid(a0 + dot(xa, wa, 1, 0, "x3"))
    g = dot(_sigmoid(xg), wg, 1, 0, "x3")
    kk = k * k_k
    norm = jnp.sqrt(dot(kk * kk, seg, 1, 0, "mb2"))
    kkn = kk * dot(1.0 / jnp.maximum(norm, 1e-12), seg_t, 1, 0, "mb2")
    k2 = k * (1.0 + (a - 1.0) * k_a)
    return r, lw, k2, v, kkn, kkn * a, g


def _rwkv_post_math(y, r, k2, v, g, lnx_g, lnx_b, r_k, seg, seg_t, dot=_dot):
    inv = 1.0 / HEAD
    spread = lambda z: dot(dot(z, seg, 1, 0, "mb2"), seg_t, 1, 0, "mb2")
    mu = spread(y) * inv
    yc = y - mu
    var = spread(yc * yc) * inv
    yn = yc * lax.rsqrt(var + LNX_EPS) * lnx_g + lnx_b
    bonus = spread(r * k2 * r_k) * v
    return (yn + bonus) * g


def _chunk_math(s0, r, lw, k, v, kk, b, dot=_dot):
    n = len(r)
    hs = range(n)
    c = r[0].shape[0]
    ti = lax.broadcasted_iota(jnp.int32, (2 * c, 2 * c), 0)
    tj = lax.broadcasted_iota(jnp.int32, (2 * c, 2 * c), 1)
    tt, jj = ti & (c - 1), tj & (c - 1)
    quad = jnp.where(ti < c, (tt > jj).astype(F32), (tt >= jj).astype(F32))
    incl = quad[c:, :c]
    eye = (ti[:c, :c] == tj[:c, :c]).astype(F32)
    cl = [dot(incl, lw[i], 1, 0, "mask") for i in hs]
    ge = [jnp.exp(cl[i]) for i in hs]
    gi = [jnp.exp(-cl[i]) for i in hs]
    ar = [jnp.concatenate([-kk[i] * jnp.exp(cl[i] - lw[i]), r[i] * ge[i]], axis=0) for i in hs]
    kb = [jnp.concatenate([k[i] * gi[i], b[i] * gi[i]], axis=0) for i in hs]
    m = [dot(ar[i], kb[i], 1, 1, "x3") * quad for i in hs]
    ars0 = [dot(ar[i], s0[i], 1, 1, "bf16") for i in hs]
    mv = [dot(m[i][:c, :c], v[i], 1, 0, "bf16") for i in hs]
    pw = [m[i][:c, c:] for i in hs]
    inv = [eye + pw[i] for i in hs]
    for _ in range(int(math.log2(c)) - 1):
        pw = [dot(pw[i], pw[i], 1, 0, "bf16") for i in hs]
        inv = [inv[i] + dot(inv[i], pw[i], 1, 0, "bf16") for i in hs]
    u = [dot(inv[i], ars0[i][:c] + mv[i], 1, 0, "bf16") for i in hs]
    vu = [jnp.concatenate([v[i], u[i]], axis=0) for i in hs]
    y = [ars0[i][c:] + dot(m[i][c:], vu[i], 1, 0, "bf16") for i in hs]
    s1 = [(s0[i] + dot(vu[i], kb[i], 0, 0, "x3")) * ge[i][c - 1:c, :] for i in hs]
    return y, s1


def _attn_math(q, kp, kc, vp, vc, bias, sinks, first, dot=_dot):
    hq = q.shape[1] // HEAD
    hkv = kc.shape[1] // HEAD
    group = hq // hkv
    cols = group * ATT_BLOCK
    kj = lax.broadcasted_iota(jnp.int32, (2 * ATT_BLOCK, cols), 0)
    qi = lax.broadcasted_iota(jnp.int32, (2 * ATT_BLOCK, cols), 1) & (ATT_BLOCK - 1)
    dist = qi + ATT_BLOCK - kj
    valid = (dist >= 0) & (dist < ATT_BLOCK) & (jnp.logical_not(first) | (kj >= ATT_BLOCK))
    eye = (lax.broadcasted_iota(jnp.int32, (ATT_BLOCK, ATT_BLOCK), 0)
           == lax.broadcasted_iota(jnp.int32, (ATT_BLOCK, ATT_BLOCK), 1)).astype(F32)
    outs = []
    for j in range(hkv):
        heads = range(j * group, (j + 1) * group)
        kband = jnp.concatenate([kp[:, j * HEAD:(j + 1) * HEAD], kc[:, j * HEAD:(j + 1) * HEAD]], axis=0)
        vband = jnp.concatenate([vp[:, j * HEAD:(j + 1) * HEAD], vc[:, j * HEAD:(j + 1) * HEAD]], axis=0)
        qg = jnp.concatenate([q[:, h * HEAD:(h + 1) * HEAD] for h in heads], axis=0)
        bias_g = jnp.concatenate([bias[h] for h in heads], axis=1)
        sink = jnp.concatenate([jnp.broadcast_to(sinks[0:1, h:h + 1], (1, ATT_BLOCK)) for h in heads], axis=1)
        s = dot(kband, qg, 1, 1, "bf16") * (HEAD ** -0.5) + bias_g
        s = jnp.where(valid, s, -1e30)
        m = jnp.maximum(jnp.max(s, axis=0, keepdims=True), sink)
        e = jnp.exp(s - m)
        p = e / (jnp.sum(e, axis=0, keepdims=True) + jnp.exp(sink - m))
        o_t = dot(vband, p, 0, 0, "bf16")
        outs += [dot(eye, o_t[:, g * ATT_BLOCK:(g + 1) * ATT_BLOCK], 1, 1, "bf16") for g in range(group)]
    return jnp.concatenate(outs, axis=1)


def _rec_specs(t, da, gh, reverse):
    nc = t // CHUNK
    if reverse:
        return pl.BlockSpec((CHUNK, gh * HEAD), lambda hg, c: (nc - 1 - c, hg))
    return pl.BlockSpec((CHUNK, gh * HEAD), lambda hg, c: (c, hg))


def _rec_fwd(r, lw, k, v, kk, b):
    t, da = r.shape
    h = da // HEAD
    gh = min(REC_HEADS, h)
    nc = t // CHUNK

    def body(r_ref, lw_ref, k_ref, v_ref, kk_ref, b_ref, y_ref, s0_ref, state):
        @pl.when(pl.program_id(1) == 0)
        def _():
            state[...] = jnp.zeros_like(state)

        sls = [slice(i * HEAD, (i + 1) * HEAD) for i in range(gh)]
        heads = lambda ref: [ref[:, sl] for sl in sls]
        s0 = [state[i] for i in range(gh)]
        y, s1 = _chunk_math(s0, heads(r_ref), heads(lw_ref), heads(k_ref), heads(v_ref), heads(kk_ref),
                            heads(b_ref), dot=_raw_dot)
        for i, sl in enumerate(sls):
            s0_ref[0, i] = s0[i]
            y_ref[:, sl] = y[i]
            state[i] = s1[i]

    spec = _rec_specs(t, da, gh, False)
    return pl.pallas_call(
        body, name="rwkv_recurrence_fwd", grid=(h // gh, nc),
        in_specs=[spec] * 6,
        out_specs=[spec, pl.BlockSpec((1, gh, HEAD, HEAD), lambda hg, c: (c, hg, 0, 0))],
        out_shape=[jax.ShapeDtypeStruct((t, da), F32), jax.ShapeDtypeStruct((nc, h, HEAD, HEAD), F32)],
        scratch_shapes=[pltpu.VMEM((gh, HEAD, HEAD), F32)],
        compiler_params=_cparams(("parallel", "arbitrary")),
    )(r, lw, k, v, kk, b)


def _rec_bwd(r, lw, k, v, kk, b, s0s, dy):
    t, da = r.shape
    h = da // HEAD
    gh = min(REC_HEADS, h)
    nc = t // CHUNK

    def body(r_ref, lw_ref, k_ref, v_ref, kk_ref, b_ref, dy_ref, s0_ref,
             dr_ref, dlw_ref, dk_ref, dv_ref, dkk_ref, db_ref, dstate):
        @pl.when(pl.program_id(1) == 0)
        def _():
            dstate[...] = jnp.zeros_like(dstate)

        sls = [slice(i * HEAD, (i + 1) * HEAD) for i in range(gh)]
        heads = lambda ref: [ref[:, sl] for sl in sls]
        _, vjp = jax.vjp(_chunk_math, [s0_ref[0, i] for i in range(gh)], heads(r_ref), heads(lw_ref),
                         heads(k_ref), heads(v_ref), heads(kk_ref), heads(b_ref))
        grads = vjp((heads(dy_ref), [dstate[i] for i in range(gh)]))
        for i, sl in enumerate(sls):
            dstate[i] = grads[0][i]
            for ref, val in zip((dr_ref, dlw_ref, dk_ref, dv_ref, dkk_ref, db_ref), grads[1:]):
                ref[:, sl] = val[i]

    spec = _rec_specs(t, da, gh, True)
    return pl.pallas_call(
        body, name="rwkv_recurrence_bwd", grid=(h // gh, nc),
        in_specs=[spec] * 7 + [pl.BlockSpec((1, gh, HEAD, HEAD), lambda hg, c: (nc - 1 - c, hg, 0, 0))],
        out_specs=[spec] * 6,
        out_shape=[jax.ShapeDtypeStruct((t, da), F32)] * 6,
        scratch_shapes=[pltpu.VMEM((gh, HEAD, HEAD), F32)],
        compiler_params=_cparams(("parallel", "arbitrary")),
    )(r, lw, k, v, kk, b, dy, s0s)


def _attn_specs(t, hq_w, hkv_w):
    nb = t // ATT_BLOCK
    cur = lambda w: pl.BlockSpec((ATT_BLOCK, w), lambda n: (n, 0))
    prev = lambda w: pl.BlockSpec((ATT_BLOCK, w), lambda n: (jnp.maximum(n - 1, 0), 0))
    return nb, cur, prev


def _attn_fwd(q, kb, vb, bias, sinks):
    t, qw = q.shape
    kw = kb.shape[1]
    nb, cur, prev = _attn_specs(t, qw, kw)

    def body(q_ref, kp_ref, kc_ref, vp_ref, vc_ref, bias_ref, sink_ref, o_ref):
        first = pl.program_id(0) == 0
        o = _attn_math(q_ref[...], kp_ref[...], kc_ref[...], vp_ref[...], vc_ref[...],
                       bias_ref[...], sink_ref[...], first, dot=_raw_dot)
        o_ref[...] = o.astype(o_ref.dtype)

    full = lambda a: pl.BlockSpec(a.shape, lambda n, nd=a.ndim: (0,) * nd)
    return pl.pallas_call(
        body, name="swa_attention_fwd", grid=(nb,),
        in_specs=[cur(qw), prev(kw), cur(kw), prev(kw), cur(kw), full(bias), full(sinks)],
        out_specs=cur(qw), out_shape=jax.ShapeDtypeStruct((t, qw), BF16),
        compiler_params=_cparams(("parallel",)),
    )(q, kb, kb, vb, vb, bias, sinks)


def _attn_bwd(q, kb, vb, bias, sinks, do, col_block):
    t, qw = q.shape
    kw = kb.shape[1]
    nb, cur, prev = _attn_specs(t, qw, kw)

    def body(q_ref, kp_ref, kc_ref, vp_ref, vc_ref, bias_ref, sink_ref, do_ref,
             dq_ref, dkp_ref, dkc_ref, dvp_ref, dvc_ref, dbias_ref, dsink_ref):
        n = pl.program_id(0)
        first = n == 0
        fn = functools.partial(_attn_math, first=first)
        _, vjp = jax.vjp(fn, q_ref[...], kp_ref[...], kc_ref[...], vp_ref[...], vc_ref[...],
                         bias_ref[...], sink_ref[...])
        dq, dkp, dkc, dvp, dvc, dbias, dsink = vjp(do_ref[...].astype(F32))
        dq_ref[...] = dq.astype(dq_ref.dtype)
        dkp_ref[...] = dkp
        dkc_ref[...] = dkc
        dvp_ref[...] = dvp
        dvc_ref[...] = dvc

        @pl.when(first)
        def _():
            dbias_ref[...] = jnp.zeros_like(dbias_ref)
            dsink_ref[...] = jnp.zeros_like(dsink_ref)

        dbias_ref[...] += dbias
        dsink_ref[...] += dsink

    full = lambda a: pl.BlockSpec(a.shape, lambda n, nd=a.ndim: (0,) * nd)
    kshape = jax.ShapeDtypeStruct((t, kw), F32)
    return pl.pallas_call(
        body, name="swa_attention_bwd", grid=(nb,),
        in_specs=[cur(qw), prev(kw), cur(kw), prev(kw), cur(kw), full(bias), full(sinks),
                  pl.BlockSpec((ATT_BLOCK, qw), lambda n: (n, col_block))],
        out_specs=[cur(qw), cur(kw), cur(kw), cur(kw), cur(kw), full(bias), full(sinks)],
        out_shape=[jax.ShapeDtypeStruct((t, qw), BF16), kshape, kshape, kshape, kshape,
                   jax.ShapeDtypeStruct(bias.shape, F32), jax.ShapeDtypeStruct(sinks.shape, F32)],
        compiler_params=_cparams(("arbitrary",)),
    )(q, kb, kb, vb, vb, bias, sinks, do)


def _bucket_onehot():
    qi = jnp.arange(ATT_BLOCK)[None, :]
    kj = jnp.arange(2 * ATT_BLOCK)[:, None]
    n = jnp.maximum(qi + ATT_BLOCK - kj, 0)
    buckets, max_exact = 32, 16
    nf = jnp.maximum(n, 1).astype(F32)
    large = max_exact + (jnp.log(nf / max_exact) / math.log(RPB_MAX_DIST / max_exact)
                         * (buckets - max_exact)).astype(jnp.int32)
    bucket = jnp.where(n < max_exact, n, jnp.minimum(large, buckets - 1)).reshape(-1)
    return (bucket[None, :] == jnp.arange(buckets)[:, None]).astype(F32)


def _small_dot(name, a, b, ca, cb):
    m = a.shape[1 - ca]
    n = b.shape[1 - cb]

    def body(a_ref, b_ref, o_ref):
        o_ref[...] = _raw_dot(a_ref[...], b_ref[...], ca, cb, "hi")

    return pl.pallas_call(body, name=name, out_shape=jax.ShapeDtypeStruct((m, n), F32),
                          compiler_params=_cparams())(a, b)


def _mod_fwd(c_all, w_mod):
    d, n = w_mod.shape
    tn = _tile(n, 512)

    def body(c_ref, w_ref, o_ref, cond_ref):
        cv = c_ref[...]
        cond = cv * _sigmoid(cv)
        cond_ref[...] = cond
        o_ref[...] = _raw_dot(cond, w_ref[...], 1, 0, "hi")

    return pl.pallas_call(
        body, name="adaln_mod_fwd", grid=(n // tn,),
        in_specs=[pl.BlockSpec(c_all.shape, lambda j: (0, 0)), pl.BlockSpec((d, tn), lambda j: (0, j))],
        out_specs=[pl.BlockSpec((c_all.shape[0], tn), lambda j: (0, j)),
                   pl.BlockSpec(c_all.shape, lambda j: (0, 0))],
        out_shape=[jax.ShapeDtypeStruct((c_all.shape[0], n), F32), jax.ShapeDtypeStruct(c_all.shape, F32)],
        compiler_params=_cparams(("arbitrary",)),
    )(c_all, w_mod)


def _adam_math(w, g, m, v):
    m = ADAM_B1 * m + (1.0 - ADAM_B1) * g
    v = ADAM_B2 * v + (1.0 - ADAM_B2) * (g * g)
    m_hat = m / (1.0 - ADAM_B1 ** ADAM_STEP)
    v_hat = v / (1.0 - ADAM_B2 ** ADAM_STEP)
    delta = -ADAM_LR * (m_hat / (jnp.sqrt(v_hat) + ADAM_EPS) + ADAM_WD * w)
    return delta, m, v


def _adamw(name, w, m, v, gparts):
    r, c = w.shape
    p = gparts.shape[0]
    tr = _rtile(r, max(8, (1 << 18) // max(c, 1) // 8 * 8))

    def body(w_ref, m_ref, v_ref, g_ref, go_ref, d_ref, mo_ref, vo_ref):
        g = g_ref[0].astype(F32)
        for s in range(1, p):
            g = g + g_ref[s].astype(F32)
        delta, mn, vn = _adam_math(w_ref[...], g, m_ref[...], v_ref[...])
        go_ref[...] = g
        d_ref[...] = delta
        mo_ref[...] = mn
        vo_ref[...] = vn

    spec = pl.BlockSpec((tr, c), lambda i: (i, 0))
    return pl.pallas_call(
        body, name=name, grid=(r // tr,),
        in_specs=[spec, spec, spec, pl.BlockSpec((p, tr, c), lambda i: (0, i, 0))],
        out_specs=[spec] * 4, out_shape=[jax.ShapeDtypeStruct((r, c), F32)] * 4,
        compiler_params=_cparams(("parallel",)),
    )(w, m, v, gparts)


def _adamw_outer(name, w, m, v, cond_t, dmod):
    d, n = w.shape
    tr, tn = _rtile(d, 512), _tile(n, 1024)

    def body(w_ref, m_ref, v_ref, c_ref, dm_ref, go_ref, d_ref, mo_ref, vo_ref):
        g = _raw_dot(c_ref[...], dm_ref[...], 1, 0, "hi")
        delta, mn, vn = _adam_math(w_ref[...], g, m_ref[...], v_ref[...])
        go_ref[...] = g
        d_ref[...] = delta
        mo_ref[...] = mn
        vo_ref[...] = vn

    spec = pl.BlockSpec((tr, tn), lambda i, j: (i, j))
    return pl.pallas_call(
        body, name=name, grid=(d // tr, n // tn),
        in_specs=[spec, spec, spec, pl.BlockSpec((tr, cond_t.shape[1]), lambda i, j: (i, 0)),
                  pl.BlockSpec((dmod.shape[0], tn), lambda i, j: (0, j))],
        out_specs=[spec] * 4, out_shape=[jax.ShapeDtypeStruct((d, n), F32)] * 4,
        compiler_params=_cparams(("parallel", "parallel")),
    )(w, m, v, cond_t, dmod)


def _all_gather(name, arrays):
    n = len(arrays)

    def body(*refs):
        ins, outs = refs[:n], refs[n:2 * n]
        send_sems, recv_sems, local_sems = refs[2 * n:]
        x, y, c = lax.axis_index("x"), lax.axis_index("y"), lax.axis_index("c")
        me, sibling = (x, y, c), (x, y, 1 - c)
        chips = [(1 - x, y), (x, 1 - y), (1 - x, 1 - y)]

        def copy(a, k, block, to, src=None):
            rows = outs[a].at[4 * block[0] + 2 * block[1] + block[2]]
            return pltpu.make_async_remote_copy(
                src_ref=rows if src is None else src, dst_ref=rows, send_sem=send_sems.at[a, k],
                recv_sem=recv_sems.at[a, k], device_id=to, device_id_type=pl.DeviceIdType.MESH)

        mine = [pltpu.make_async_copy(ins[a], outs[a].at[4 * x + 2 * y + c], local_sems.at[a]) for a in range(n)]
        for cp in mine:
            cp.start()
        first = []
        for a in range(n):
            first.append(copy(a, 0, me, sibling, src=ins[a]))
            first += [copy(a, 1 + j, me, (*chip, c), src=ins[a]) for j, chip in enumerate(chips)]
        for cp in first:
            cp.start()
        passed = []
        for a in range(n):
            for j, chip in enumerate(chips):
                copy(a, 1 + j, (*chip, c), me).wait_recv()
                passed.append(copy(a, 4 + j, (*chip, c), sibling))
                passed[-1].start()
        for a in range(n):
            copy(a, 0, sibling, me).wait_recv()
            for j, chip in enumerate(chips):
                copy(a, 4 + j, (*chip, 1 - c), me).wait_recv()
        for cp in first + passed:
            cp.wait_send()
        for cp in mine:
            cp.wait()

    any_spec = pl.BlockSpec(memory_space=pl.ANY)
    return pl.pallas_call(
        body, name=name, in_specs=[any_spec] * n, out_specs=[any_spec] * n,
        out_shape=[jax.ShapeDtypeStruct((N_DEV,) + a.shape, a.dtype) for a in arrays],
        scratch_shapes=[pltpu.SemaphoreType.DMA((n, N_DEV - 1)), pltpu.SemaphoreType.DMA((n, N_DEV - 1)),
                        pltpu.SemaphoreType.DMA((n,))],
    )(*arrays)


def _peer(p):
    x, y, c = lax.axis_index("x"), lax.axis_index("y"), lax.axis_index("c")
    px, py, pc = x ^ ((p >> 2) & 1), y ^ ((p >> 1) & 1), c ^ (p & 1)
    return (px, py, pc), 4 * px + 2 * py + pc


def _block(ref, d, cols, rows=None):
    if cols:
        r = slice(None) if rows is None else pl.ds(pl.multiple_of(rows[0], 8), rows[1])
        return ref.at[r, pl.ds(pl.multiple_of(d * cols, cols), cols)]
    return ref.at[d] if rows is None else ref.at[d, pl.ds(pl.multiple_of(rows[0], 8), rows[1])]


def _split_copy(src_ref, land_ref, send_sems, recv_sems, p, scatter, arriving, cols=None, halves=False):
    x, y, c = lax.axis_index("x"), lax.axis_index("y"), lax.axis_index("c")
    me = 4 * x + 2 * y + c
    dev, idx = _peer(p)
    if scatter:
        src, dst = _block(src_ref, idx, cols), land_ref.at[idx if arriving else me]
    elif halves and p >= 2:
        half = src_ref.shape[0] // 2
        rows = ((c if arriving else dev[2]) * half, half)
        src = src_ref.at[pl.ds(pl.multiple_of(rows[0], 8), half)]
        dst = _block(land_ref, idx if arriving else me, cols, rows)
    else:
        src, dst = src_ref, _block(land_ref, idx if arriving else me, cols)
    return pltpu.make_async_remote_copy(
        src_ref=src, dst_ref=dst, send_sem=send_sems.at[p - 1], recv_sem=recv_sems.at[p - 1], device_id=dev,
        device_id_type=pl.DeviceIdType.MESH)


_HBM_SPEC = pl.BlockSpec(memory_space=pltpu.HBM)
_SEM_SPEC = pl.BlockSpec(memory_space=pltpu.SEMAPHORE)
_DATAFLOW = pltpu.SideEffectType.DATAFLOW_SIDE_EFFECTING


def _own_copy(src_ref, land_ref, send_sems, scatter, cols):
    me = 4 * lax.axis_index("x") + 2 * lax.axis_index("y") + lax.axis_index("c")
    src, dst = (_block(src_ref, me, cols), land_ref.at[me]) if scatter else (src_ref, _block(land_ref, me, cols))
    return pltpu.make_async_copy(src, dst, send_sems.at[N_DEV - 1])


def _exchange_start(name, srcs, lands, scatter, after, cols=None, halves=False):
    n = len(srcs)
    cols = cols or [None] * n

    def body(*refs):
        src_refs, land_refs = refs[:n], refs[n:2 * n]
        outs = refs[2 * n + 1:]
        send, recv, token = outs[:n], outs[n:2 * n], outs[-1]
        for a in range(n):
            for p in range(1, N_DEV):
                _split_copy(src_refs[a], land_refs[a], send[a], recv[a], p, scatter, False, cols[a],
                            halves).start()
            _own_copy(src_refs[a], land_refs[a], send[a], scatter, cols[a]).start()
        token[...] = jnp.zeros_like(token)

    sems = [pltpu.SemaphoreType.DMA((N_DEV,))] * n + [pltpu.SemaphoreType.DMA((N_DEV - 1,))] * n
    hbm = [pltpu.HBM(a.shape, a.dtype) for a in list(srcs) + list(lands)]
    res = pl.pallas_call(
        body, name=name,
        out_shape=sems + hbm + [jax.ShapeDtypeStruct((8, 128), F32)],
        in_specs=[_HBM_SPEC] * (2 * n) + [pl.BlockSpec(memory_space=pl.ANY)],
        out_specs=[_SEM_SPEC] * (2 * n) + [_HBM_SPEC] * (2 * n) + [pl.BlockSpec(memory_space=pltpu.VMEM)],
        input_output_aliases={i: 2 * n + i for i in range(2 * n)},
        compiler_params=pltpu.CompilerParams(has_side_effects=_DATAFLOW),
    )(*[pltpu.with_memory_space_constraint(a, pltpu.HBM) for a in list(srcs) + list(lands)], after)
    return res[:n], res[n:2 * n], res[2 * n:3 * n], res[3 * n:4 * n], res[-1]


def _exchange_wait(name, send_sem, recv_sem, src, land, scatter, after, cols=None, halves=False):
    def body(src_ref, land_ref, send, recv, after_ref, src_out, land_out):
        for p in range(1, N_DEV):
            cp = _split_copy(src_ref, land_ref, send, recv, p, scatter, True, cols, halves)
            cp.wait_send()
            cp.wait_recv()
        _own_copy(src_ref, land_ref, send, scatter, cols).wait()

    return pl.pallas_call(
        body, name=name,
        out_shape=(pltpu.HBM(src.shape, src.dtype), pltpu.HBM(land.shape, land.dtype)),
        in_specs=[_HBM_SPEC, _HBM_SPEC, _SEM_SPEC, _SEM_SPEC, pl.BlockSpec(memory_space=pl.ANY)],
        out_specs=(_HBM_SPEC, _HBM_SPEC), input_output_aliases={0: 0, 1: 1},
        compiler_params=pltpu.CompilerParams(has_side_effects=_DATAFLOW),
    )(src, land, send_sem, recv_sem, after)[1]


def _sibling_fill(name, land, cols):
    rows = land.shape[0] if cols else land.shape[1]
    half = rows // 2

    def body(in_ref, out_ref, send_sems, recv_sems):
        x, y, c = lax.axis_index("x"), lax.axis_index("y"), lax.axis_index("c")

        def copy(p, core):
            _, idx = _peer(p)
            return pltpu.make_async_remote_copy(
                src_ref=_block(in_ref, idx, cols, (core * half, half)),
                dst_ref=_block(out_ref, idx, cols, (core * half, half)),
                send_sem=send_sems.at[idx], recv_sem=recv_sems.at[idx], device_id=(x, y, 1 - c),
                device_id_type=pl.DeviceIdType.MESH)

        sends = [copy(p, c) for p in range(2, N_DEV)]
        for cp in sends:
            cp.start()
        for p in range(2, N_DEV):
            copy(p, 1 - c).wait_recv()
        for cp in sends:
            cp.wait_send()

    any_spec = pl.BlockSpec(memory_space=pl.ANY)
    return pl.pallas_call(
        body, name=name, in_specs=[any_spec], out_specs=any_spec,
        out_shape=jax.ShapeDtypeStruct(land.shape, land.dtype), input_output_aliases={0: 0},
        scratch_shapes=[pltpu.SemaphoreType.DMA((N_DEV,)), pltpu.SemaphoreType.DMA((N_DEV,))],
    )(land)


def _cols_to_shards(a):
    r, c = a.shape
    return a.reshape(r, N_DEV, c // N_DEV).transpose(1, 0, 2)


def _shards_to_cols(a):
    d, r, n = a.shape
    return a.transpose(1, 0, 2).reshape(r, d * n)


def kernel(x, c, ln_emb_g, ln_emb_b, rpb_table, w_mod, b_mod, w_in, mu_shift, w0, w_decay_up, a0, w_iclr_up, w_gate_up, k_k, k_a, r_k, lnx_g, lnx_b, attn_sinks, w_out, ln1_g, ln1_b, w_up, w_down, ln2_g, ln2_b, loss_target, m_ln_emb_g, m_ln_emb_b, m_rpb_table, m_w_mod, m_b_mod, m_w_in, m_mu_shift, m_w0, m_w_decay_up, m_a0, m_w_iclr_up, m_w_gate_up, m_k_k, m_k_a, m_r_k, m_lnx_g, m_lnx_b, m_attn_sinks, m_w_out, m_ln1_g, m_ln1_b, m_w_up, m_w_down, m_ln2_g, m_ln2_b, v_ln_emb_g, v_ln_emb_b, v_rpb_table, v_w_mod, v_b_mod, v_w_in, v_mu_shift, v_w0, v_w_decay_up, v_a0, v_w_iclr_up, v_w_gate_up, v_k_k, v_k_a, v_r_k, v_lnx_g, v_lnx_b, v_attn_sinks, v_w_out, v_ln1_g, v_ln1_b, v_w_up, v_w_down, v_ln2_g, v_ln2_b):
    names = ["ln_emb_g", "ln_emb_b", "rpb_table", "w_mod", "b_mod", "w_in", "mu_shift", "w0", "w_decay_up",
             "a0", "w_iclr_up", "w_gate_up", "k_k", "k_a", "r_k", "lnx_g", "lnx_b", "attn_sinks", "w_out",
             "ln1_g", "ln1_b", "w_up", "w_down", "ln2_g", "ln2_b"]
    env = dict(locals())
    weights = {nm: env[nm] for nm in names}
    mom_m = {nm: env["m_" + nm] for nm in names}
    mom_v = {nm: env["v_" + nm] for nm in names}

    t, d = x.shape[1], x.shape[2]
    da = d // 2
    h_a = da // HEAD
    hq = (d - da) // HEAD
    hkv = hq // GQA_RATIO
    l_w, l_a, l_g = w_decay_up.shape[1], w_iclr_up.shape[1], w_gate_up.shape[1]
    o_w, o_a, o_g = 3 * da, 3 * da + l_w, 3 * da + l_w + l_a
    n_rwkv = o_g + l_g
    o_kb, o_vb = n_rwkv + hq * HEAD, n_rwkv + hq * HEAD + hkv * HEAD
    me = 4 * lax.axis_index("x") + 2 * lax.axis_index("y") + lax.axis_index("c")

    x2d, tgt = x[0], loss_target[0]
    row = lambda a: a.reshape(1, -1)
    seg = (jnp.arange(da)[:, None] // HEAD == jnp.arange(h_a)[None, :]).astype(F32)
    seg_t = seg.T

    (c_all,) = _all_gather("gather_cond", [c])
    c_all = c_all.reshape(N_DEV, d)
    mod_rows, cond_all = _mod_fwd(c_all, w_mod[0])
    gathered = _all_gather("gather_weights", [
        mod_rows, w_in[0].astype(BF16), w_decay_up[0], w_iclr_up[0], w_gate_up[0]])
    mod_all, win_g, wd_g, wa_g, wg_g = gathered
    late = [w_out[0].astype(BF16), w_up[0].astype(BF16), w_down[0].astype(BF16)]
    n_up = w_up.shape[2]
    late_cols = [None, n_up, None]
    late_lands = [lax.empty((N_DEV,) + late[0].shape, BF16), lax.empty((d, N_DEV * n_up), BF16),
                  lax.empty((N_DEV,) + late[2].shape, BF16)]
    late_send, late_recv, late_src, late_land, late_token = _exchange_start(
        "gather_late_weights_start", late, late_lands, False, mod_all, late_cols, halves=True)
    mod = lax.dynamic_index_in_dim(mod_all, me, axis=1, keepdims=False).reshape(1, -1) + b_mod
    mod = mod + late_token[0, 0]
    sh1, sc1, g1, sh2, sc2, g2 = [mod[:, i * d:(i + 1) * d] for i in range(6)]
    w_in_f = _shards_to_cols(win_g)
    wd_f, wa_f, wg_f = _shards_to_cols(wd_g), _shards_to_cols(wa_g), _shards_to_cols(wg_g)

    def late_weight(i, nm, after):
        land = _exchange_wait("gather_" + nm + "_wait", late_send[i], late_recv[i], late_src[i], late_land[i],
                              False, after, late_cols[i], halves=True)
        return _sibling_fill("gather_" + nm + "_fill", land, late_cols[i])

    tr = _rtile(t, 256)
    lng, lnb = row(ln_emb_g), row(ln_emb_b)

    def embed_fn(i, nb, xb, g, b, sc, sh):
        return _embed_math(xb, g, b, sc, sh), ()
    x0, u1 = _rowwise("embed_ln_mod", embed_fn, [x2d], [lng, lnb, sc1, sh1], [(d, F32), (d, BF16)], [], tr)

    (p_rkv,) = _matmul("in_proj_rkv", u1, w_in_f[:, :o_w], "nn", [F32])
    (p_lora,) = _matmul("in_proj_lora", u1, w_in_f[:, o_w:n_rwkv], "nn", [F32])
    (q,) = _matmul("in_proj_q", u1, w_in_f[:, n_rwkv:o_kb], "nn", [F32])
    (p_kv,) = _matmul("in_proj_kv", u1, w_in_f[:, o_kb:], "nn", [F32])
    p_w, p_a, p_g = p_lora[:, :l_w], p_lora[:, l_w:l_w + l_a], p_lora[:, l_w + l_a:]
    kb, vb = p_kv[:, :hkv * HEAD], p_kv[:, hkv * HEAD:]
    mu_rkv, mu_w, mu_a, mu_g = (mu_shift[:, :o_w], mu_shift[:, o_w:o_a], mu_shift[:, o_a:o_g],
                                mu_shift[:, o_g:n_rwkv])
    pre_params = [w0, wd_f, a0, wa_f, wg_f, k_k, k_a, seg, seg_t]
    tr_pre = _rtile(t, 128)

    def shifted(i, blocks, halos, mus):
        return [xb + (_shift_prev(xb, hb, i) - xb) * mb for xb, hb, mb in zip(blocks, halos, mus)]

    def split3(a):
        return a[:, :da], a[:, da:2 * da], a[:, 2 * da:]

    def pre_fn(i, nb, b_rkv, b_w, b_a, b_g, h_rkv, h_w, h_a_, h_g, m_rkv, m_w, m_a, m_g, *params):
        s_rkv, s_w, s_a, s_g = shifted(i, [b_rkv, b_w, b_a, b_g], [h_rkv, h_w, h_a_, h_g],
                                       [m_rkv, m_w, m_a, m_g])
        return _rwkv_pre_math(*split3(s_rkv), s_w, s_a, s_g, *params, dot=_raw_dot), ()

    pre_rows = [p_rkv, p_w, p_a, p_g]
    pre_halos = [(a, "prev") for a in pre_rows]
    r_, lw_, k2_, v_, kk_, b_, gate_ = _rowwise(
        "rwkv_pre", pre_fn, pre_rows, [mu_rkv, mu_w, mu_a, mu_g] + pre_params,
        [(da, F32)] * 7, [], tr_pre, halos=pre_halos)

    y_rec, s0s = _rec_fwd(r_, lw_, k2_, v_, kk_, b_)

    rk_flat = r_k.reshape(1, da)
    post_params = [lnx_g, lnx_b, rk_flat, seg, seg_t]

    def post_fn(i, nb, yb, rb, kb_, vb_, gb, *params):
        return (_rwkv_post_math(yb, rb, kb_, vb_, gb, *params, dot=_raw_dot),), ()
    (ya,) = _rowwise("rwkv_post", post_fn, [y_rec, r_, k2_, v_, gate_], post_params, [(da, BF16)], [], tr_pre)

    onehot = _bucket_onehot()
    bias = _small_dot("rpb_gather", rpb_table, onehot, 0, 0)
    bias = bias.reshape(hq, 2 * ATT_BLOCK, ATT_BLOCK)
    yb = _attn_fwd(q, kb, vb, bias, attn_sinks)

    mix_in = jnp.concatenate([ya, yb], axis=1)
    w_out_f = late_weight(0, "w_out", mix_in).reshape(d, d)
    (mix,) = _matmul("out_proj", mix_in, w_out_f, "nn", [F32])

    def post1_fn(i, nb, xin, yv, gate, g, b, sc, sh):
        return _post_math(xin, yv, gate, g, b, sc, sh), ()
    x1, u2 = _rowwise("ln1_mod", post1_fn, [x0, mix], [g1, ln1_g, ln1_b, sc2, sh2],
                      [(d, F32), (d, BF16)], [], tr)

    def relu2(acc):
        rl = jnp.maximum(acc, 0.0)
        return acc, rl * rl
    w_up_f = late_weight(1, "w_up", u2)
    hpre, hact = _matmul("mlp_up", u2, w_up_f, "nn", [F32, BF16], epilogue=relu2)
    w_down_f = late_weight(2, "w_down", hact).reshape(-1, d)
    (hmlp,) = _matmul("mlp_down", hact, w_down_f, "nn", [F32])

    def loss_fn(i, nb, xin, hv, tg, gate, g, b):
        val, vjp = jax.vjp(_loss_math, xin, hv, tg, gate, g, b)
        dxin, dh, _, dgate, dg, db = vjp(jnp.ones((), F32))
        return (dxin, dh), (val, dgate, dg, db)
    dx1, dh, loss_acc, dg2, dln2g, dln2b = _rowwise(
        "ln2_loss", loss_fn, [x1, hmlp, tgt], [g2, ln2_g, ln2_b], [(d, F32), (d, BF16)],
        [(1, 1), (1, d), (1, d), (1, d)], _rtile(t, 128))

    def drelu2(acc, hp):
        return (acc * 2.0 * jnp.maximum(hp, 0.0),)
    (dhpre,) = _matmul("mlp_down_dgrad", dh, w_down_f, "nt", [BF16], epilogue=drelu2, extras=[hpre])
    (gw_down,) = _matmul("mlp_down_wgrad", hact, dh, "tn", [BF16])
    (du2,) = _matmul("mlp_up_dgrad", dhpre, w_up_f, "nt", [F32])
    (gw_up,) = _matmul("mlp_up_wgrad", u2, dhpre, "tn", [BF16])

    def landing(src):
        return lax.empty(src.shape, src.dtype)
    mlp_src = [gw_down.reshape(N_DEV, -1, d), gw_up]
    mlp_cols = [None, n_up]
    mlp_send, mlp_recv, mlp_src, mlp_land, mlp_token = _exchange_start(
        "scatter_mlp_grads_start", mlp_src, [landing(mlp_src[0]), lax.empty((N_DEV, d, n_up), BF16)], True,
        gw_up, mlp_cols)

    def post1_bwd(i, nb, xin, yv, dx1v, du2v, gate, g, b, sc, sh):
        _, vjp = jax.vjp(_post_math, xin, yv, gate, g, b, sc, sh)
        dxin, dy, dgate, dg, db, dsc, dsh = vjp((dx1v, du2v))
        return (dxin, dy), (dgate, dg, db, dsc, dsh)
    dx0, dmix, dg1, dln1g, dln1b, dsc2, dsh2 = _rowwise(
        "ln1_mod_bwd", post1_bwd, [x0, mix, dx1, du2], [g1 + mlp_token[0, 0], ln1_g, ln1_b, sc2, sh2],
        [(d, F32), (d, BF16)], [(1, d)] * 5, _rtile(t, 128))

    (dmix_in,) = _matmul("out_proj_dgrad", dmix, w_out_f, "nt", [F32])
    (gw_out,) = _matmul("out_proj_wgrad", mix_in, dmix, "tn", [BF16])
    dya = dmix_in[:, :da]
    out_src = [gw_out.reshape(N_DEV, d // N_DEV, d)]
    out_send, out_recv, out_src, out_land, out_token = _exchange_start(
        "scatter_out_grad_start", out_src, [landing(a) for a in out_src], True, gw_out)
    post_params_bwd = [lnx_g, lnx_b, rk_flat + out_token[0, 0], seg, seg_t]

    def post_bwd(i, nb, yb_, rb, kb_, vb_, gb, dyab, *params):
        _, vjp = jax.vjp(_rwkv_post_math, yb_, rb, kb_, vb_, gb, *params)
        dy, dr, dk, dv, dg, dlg, dlb, drk, _, _ = vjp(dyab)
        return (dy, dr, dk, dv, dg), (dlg, dlb, drk)
    dy_rec, dr_e, dk_e, dv_e, dgate, dlnxg, dlnxb, drk = _rowwise(
        "rwkv_post_bwd", post_bwd, [y_rec, r_, k2_, v_, gate_, dya], post_params_bwd,
        [(da, F32)] * 5, [(1, da)] * 3, tr_pre)

    dr_r, dlw_r, dk_r, dv_r, dkk_r, db_r = _rec_bwd(r_, lw_, k2_, v_, kk_, b_, s0s, dy_rec)

    def pre_bwd(i, nb, b_rkv, b_w, b_a, b_g, dr1, dr2, dlw, dk1, dk2, dv1, dv2, dkk, dbb, dgt,
                h_rkv, h_w, h_a_, h_g, m_rkv, m_w, m_a, m_g, *params):
        blocks = [b_rkv, b_w, b_a, b_g]
        prevs = [_shift_prev(xb, hb, i) for xb, hb in zip(blocks, [h_rkv, h_w, h_a_, h_g])]
        mus = [m_rkv, m_w, m_a, m_g]
        s_rkv, s_w, s_a, s_g = [xb + (pb - xb) * mb for xb, pb, mb in zip(blocks, prevs, mus)]
        _, vjp = jax.vjp(_rwkv_pre_math, *split3(s_rkv), s_w, s_a, s_g, *params)
        grads = vjp((dr1 + dr2, dlw, dk1 + dk2, dv1 + dv2, dkk, dbb, dgt))
        g_rkv = jnp.concatenate(grads[:3], axis=1)
        g_w, g_a, g_g = grads[3:6]
        dmu = [jnp.sum(gs * (pb - xb), axis=0, keepdims=True)
               for gs, pb, xb in zip([g_rkv, g_w, g_a, g_g], prevs, blocks)]
        dw0, dwd, da0, dwa, dwg, dkk_, dka = grads[6:13]
        return (g_rkv, g_w, g_a, g_g), (*dmu, dw0, dwd, da0, dwa, dwg, dkk_, dka)

    pre_out = _rowwise(
        "rwkv_pre_bwd", pre_bwd,
        pre_rows + [dr_r, dr_e, dlw_r, dk_r, dk_e, dv_r, dv_e, dkk_r, db_r, dgate],
        [mu_rkv, mu_w, mu_a, mu_g] + pre_params,
        [(o_w, F32), (l_w, F32), (l_a, F32), (l_g, F32)],
        [(1, o_w), (1, l_w), (1, l_a), (1, l_g), (1, da), (l_w, da), (1, da), (l_a, da), (l_g, da),
         (1, da), (1, da)],
        _rtile(t, 64), halos=pre_halos)
    gs_rkv, gs_w, gs_a, gs_g = pre_out[:4]
    dmu_parts = pre_out[4:8]
    dw0, dwd, da0, dwa, dwg, dk_k, dk_a = pre_out[8:]

    def unshift_fn(i, nb, a1, a2, a3, a4, n1, n2, n3, n4, m1, m2, m3, m4):
        outs = [gs * (1.0 - mb) + _shift_next(gs * mb, hb * mb, i, nb)
                for gs, hb, mb in zip([a1, a2, a3, a4], [n1, n2, n3, n4], [m1, m2, m3, m4])]
        return outs, ()
    gs_list = [gs_rkv, gs_w, gs_a, gs_g]
    dp_rkv, dp_w, dp_a, dp_g = _rowwise(
        "token_shift_bwd", unshift_fn, gs_list, [mu_rkv, mu_w, mu_a, mu_g],
        [(o_w, BF16), (l_w, BF16), (l_a, BF16), (l_g, BF16)], [], tr_pre,
        halos=[(a, "next") for a in gs_list])

    dq, dkp, dkc, dvp, dvc, dbias, dsinks = _attn_bwd(q, kb, vb, bias, attn_sinks, dmix_in, 1)
    zpad = jnp.zeros((ATT_BLOCK, kb.shape[1]), F32)
    dkb = (dkc + jnp.concatenate([dkp[ATT_BLOCK:], zpad], axis=0)).astype(BF16)
    dvb = (dvc + jnp.concatenate([dvp[ATT_BLOCK:], zpad], axis=0)).astype(BF16)
    d_rpb = _small_dot("rpb_scatter", onehot, dbias.reshape(hq, -1), 1, 1)

    dp = jnp.concatenate([dp_rkv, dp_w, dp_a, dp_g, dq, dkb, dvb], axis=1)
    (gw_in,) = _matmul("in_proj_wgrad", u1, dp, "tn", [BF16])
    in_names = ["w_in", "w_decay_up", "w_iclr_up", "w_gate_up"]
    in_src = [_cols_to_shards(gw_in), _cols_to_shards(dwd), _cols_to_shards(dwa), _cols_to_shards(dwg)]
    in_send, in_recv, in_src, in_land, in_token = _exchange_start(
        "scatter_in_grads_start", in_src, [landing(a) for a in in_src], True, gw_in)
    (du1,) = _matmul("in_proj_dgrad", dp, w_in_f, "nt", [F32], after=in_token)

    def embed_bwd(i, nb, xb, dx0v, du1v, g, b, sc, sh):
        _, vjp = jax.vjp(_embed_math, xb, g, b, sc, sh)
        dxv, dg, db, dsc, dsh = vjp((dx0v, du1v))
        return (dxv,), (dg, db, dsc, dsh)
    grad_x, dlng, dlnb, dsc1, dsh1 = _rowwise(
        "embed_ln_mod_bwd", embed_bwd, [x2d, dx0, du1], [lng, lnb, sc1, sh1], [(d, F32)], [(1, d)] * 4,
        _rtile(t, 128))

    dmod = jnp.concatenate([dsh1, dsc1, dg1, dsh2, dsc2, dg2], axis=1)
    small = {"ln_emb_g": dlng, "ln_emb_b": dlnb, "rpb_table": d_rpb, "b_mod": dmod,
             "mu_shift": jnp.concatenate(dmu_parts, axis=1), "w0": dw0, "a0": da0, "k_k": dk_k, "k_a": dk_a,
             "r_k": drk, "lnx_g": dlnxg, "lnx_b": dlnxb, "attn_sinks": dsinks, "ln1_g": dln1g, "ln1_b": dln1b,
             "ln2_g": dln2g, "ln2_b": dln2b}
    small_names = list(small)
    packed = jnp.concatenate([small[nm].reshape(1, -1) for nm in small_names], axis=1)
    sm_send, sm_recv, sm_src, sm_land, sm_token = _exchange_start(
        "gather_small_grads_start", [packed], [lax.empty((N_DEV,) + packed.shape, F32)], False, grad_x)

    grads, deltas, new_m, new_v = {}, {}, {}, {}

    def put(nm, res):
        shape = weights[nm].shape
        grads[nm], deltas[nm], new_m[nm], new_v[nm] = [a.reshape(shape) for a in res]

    def big_update(nm, parts):
        put(nm, _adamw("adamw_" + nm, weights[nm][0], mom_m[nm][0], mom_v[nm][0], parts))

    behind = sm_token
    big_update("w_down", _exchange_wait("scatter_w_down_wait", mlp_send[0], mlp_recv[0], mlp_src[0], mlp_land[0],
                                        True, behind, mlp_cols[0]))
    big_update("w_up", _exchange_wait("scatter_w_up_wait", mlp_send[1], mlp_recv[1], mlp_src[1], mlp_land[1],
                                      True, behind, mlp_cols[1]))
    big_update("w_out", _exchange_wait("scatter_w_out_wait", out_send[0], out_recv[0], out_src[0], out_land[0],
                                       True, behind))

    packed_all = _exchange_wait("gather_small_grads_wait", sm_send[0], sm_recv[0], sm_src[0], sm_land[0], False,
                                deltas["w_out"]).reshape(N_DEV, -1)
    n_mod = w_mod.shape[2]
    dmod_cols = lax.dynamic_slice_in_dim(packed_all[:, _offset(small, small_names, "b_mod"):], me * n_mod, n_mod,
                                         axis=1)
    put("w_mod", _adamw_outer("adamw_w_mod", w_mod[0], m_w_mod[0], v_w_mod[0], cond_all.T, dmod_cols))

    off = 0
    for nm in small_names:
        size = small[nm].size
        wshape = weights[nm].shape
        two_d = (1, size) if nm != "rpb_table" else wshape
        parts = packed_all[:, off:off + size].reshape((N_DEV,) + two_d)
        off += size
        put(nm, _adamw("adamw_" + nm, weights[nm].reshape(two_d), mom_m[nm].reshape(two_d),
                       mom_v[nm].reshape(two_d), parts))

    behind = deltas[small_names[-1]]
    for i, nm in enumerate(in_names):
        big_update(nm, _exchange_wait("scatter_" + nm + "_wait", in_send[i], in_recv[i], in_src[i], in_land[i],
                                      True, behind))

    loss = lax.psum(loss_acc[0, 0], MESH_AXES)
    return (loss, grad_x[None], *[grads[nm] for nm in names], *[deltas[nm] for nm in names],
            *[new_m[nm] for nm in names], *[new_v[nm] for nm in names])


def _offset(small, small_names, name):
    off = 0
    for nm in small_names:
        if nm == name:
            return off
        off += small[nm].size
    raise KeyError(name)
```

```python
import functools
import math

import jax
import jax.numpy as jnp
from jax import lax
from jax.experimental import pallas as pl
from jax.experimental.pallas import tpu as pltpu

F32 = jnp.float32
BF16 = jnp.bfloat16
HI = lax.Precision.HIGHEST
MESH_AXES = ("x", "y", "c")
N_DEV = 8

HEAD = 64
GQA_RATIO = 8
ATT_BLOCK = 128
RPB_MAX_DIST = 128
LN_EPS = 1e-5
LNX_EPS = 64e-5
DEPTH = 1
ALPHA = (2.0 * DEPTH) ** 0.25
CHUNK = 64
REC_HEADS = 16

ADAM_LR = 0.001
ADAM_B1 = 0.9
ADAM_B2 = 0.999
ADAM_EPS = 1e-08
ADAM_WD = 0.01
ADAM_STEP = 10

VMEM_LIMIT = 56 * 1024 * 1024


def _cparams(sem=None):
    return pltpu.CompilerParams(dimension_semantics=sem, vmem_limit_bytes=VMEM_LIMIT)


def _tile(dim, cap):
    best = None
    t = 128
    while t <= min(dim, cap):
        if dim % t == 0:
            best = t
        t += 128
    return best or dim


def _rtile(dim, cap):
    best = None
    t = 8
    while t <= min(dim, cap):
        if dim % t == 0:
            best = t
        t += 8
    return best or dim


def _split2(a):
    hi = a.astype(BF16)
    return hi, (a - hi.astype(F32)).astype(BF16)


def _raw_dot(a, b, ca, cb, prec):
    dims = (((ca,), (cb,)), ((), ()))
    mm = lambda p, q: lax.dot_general(p, q, dims, preferred_element_type=F32)
    if prec == "bf16":
        return mm(a.astype(BF16), b.astype(BF16))
    if prec == "x3":
        (ah, al), (bh, bl) = _split2(a), _split2(b)
        return mm(ah, bh) + (mm(ah, bl) + mm(al, bh))
    if prec == "mask":
        ab = a.astype(BF16)
        b1, b2 = _split2(b)
        b3 = (b - b1.astype(F32) - b2.astype(F32)).astype(BF16)
        return mm(ab, b1) + (mm(ab, b2) + mm(ab, b3))
    if prec == "mb2":
        (ah, al), bb = _split2(a), b.astype(BF16)
        return mm(ah, bb) + mm(al, bb)
    return lax.dot_general(a, b, dims, precision=HI, preferred_element_type=F32)


@functools.partial(jax.custom_vjp, nondiff_argnums=(2, 3, 4))
def _bf16_dot(a, b, ca, cb, prec):
    return _raw_dot(a, b, ca, cb, prec)


def _bf16_dot_fwd(a, b, ca, cb, prec):
    return _raw_dot(a, b, ca, cb, prec), (a, b)


def _bf16_dot_bwd(ca, cb, prec, res, g):
    a, b = res
    if prec == "mask":
        return jnp.zeros_like(a), _bf16_dot(a, g, 1 - ca, 0, prec)
    if prec == "mb2":
        return _bf16_dot(g, b, 1, 1, prec), jnp.zeros_like(b)
    if ca == 1:
        da = _bf16_dot(g, b, 1, 1 - cb, prec)
    else:
        da = _bf16_dot(b, g, 1 - cb, 1, prec)
    if cb == 0:
        db = _bf16_dot(a, g, 1 - ca, 0, prec)
    else:
        db = _bf16_dot(g, a, 0, 1 - ca, prec)
    return da, db


_bf16_dot.defvjp(_bf16_dot_fwd, _bf16_dot_bwd)


def _dot(a, b, ca, cb, prec):
    return _raw_dot(a, b, ca, cb, prec) if prec == "hi" else _bf16_dot(a, b, ca, cb, prec)


def _sigmoid(z):
    return 1.0 / (1.0 + jnp.exp(-z))


def _softplus(z):
    return jnp.maximum(z, 0.0) + jnp.log(1.0 + jnp.exp(-jnp.abs(z)))


MATMUL_VMEM_BUDGET = 46 * 1024 * 1024


def _matmul_tiles(m, n, k, in_bytes, out_dtypes, n_extras):
    tm, tn = _tile(m, 1024), _tile(n, 1024)
    out_bytes = sum(jnp.dtype(dt).itemsize for dt in out_dtypes)
    for cap in (4096, 2048, 1024, 512, 256, 128):
        tk = _tile(k, cap)
        acc = 4 * tm * tn if tk < k else 0
        need = 2 * in_bytes * (tm + tn) * tk + 2 * tm * tn * (out_bytes + 4 * n_extras) + acc + 4 * tm * tn
        if need <= MATMUL_VMEM_BUDGET:
            break
    return tm, tn, tk


def _matmul(name, a, b, mode, out_dtypes, epilogue=None, extras=(), after=None):
    if mode == "nn":
        (m, k), n = a.shape, b.shape[1]
    elif mode == "nt":
        (m, k), n = a.shape, b.shape[0]
    else:
        (k, m), n = a.shape, b.shape[1]
    tm, tn, tk = _matmul_tiles(m, n, k, a.dtype.itemsize, out_dtypes, len(extras))
    nk = k // tk
    ne, no = len(extras), len(out_dtypes)
    ca, cb = {"nn": (1, 0), "nt": (1, 1), "tn": (0, 0)}[mode]

    n_after = 0 if after is None else 1

    def body(a_ref, b_ref, *rest):
        rest = rest[n_after:]
        extra_refs, out_refs = rest[:ne], rest[ne:ne + no]
        acc = rest[-1] if nk > 1 else None
        kk = pl.program_id(2)
        part = _raw_dot(a_ref[...], b_ref[...], ca, cb, "bf16")

        def finish(total):
            res = epilogue(total, *[e[...] for e in extra_refs]) if epilogue else (total,)
            for o, v in zip(out_refs, res):
                o[...] = v.astype(o.dtype)

        if nk == 1:
            finish(part)
            return

        @pl.when(kk == 0)
        def _():
            acc[...] = part

        @pl.when((kk > 0) & (kk < nk - 1))
        def _():
            acc[...] += part

        @pl.when(kk == nk - 1)
        def _():
            finish(acc[...] + part)

    a_spec = (pl.BlockSpec((tk, tm), lambda i, j, kk: (kk, i)) if mode == "tn"
              else pl.BlockSpec((tm, tk), lambda i, j, kk: (i, kk)))
    b_spec = (pl.BlockSpec((tn, tk), lambda i, j, kk: (j, kk)) if mode == "nt"
              else pl.BlockSpec((tk, tn), lambda i, j, kk: (kk, j)))
    mn_spec = pl.BlockSpec((tm, tn), lambda i, j, kk: (i, j))
    after_specs = [pl.BlockSpec(memory_space=pl.ANY)] * n_after
    outs = pl.pallas_call(
        body, name=name, grid=(m // tm, n // tn, nk),
        in_specs=[a_spec, b_spec] + after_specs + [mn_spec] * ne,
        out_specs=[mn_spec] * no,
        out_shape=[jax.ShapeDtypeStruct((m, n), dt) for dt in out_dtypes],
        scratch_shapes=[pltpu.VMEM((tm, tn), F32)] if nk > 1 else [],
        compiler_params=_cparams(("parallel", "parallel", "arbitrary")),
    )(a, b, *([after] * n_after), *extras)
    return outs


def _rowwise(name, fn, rows, bcasts, out_rows, out_accs, tr, halos=()):
    t = rows[0].shape[0]
    nb = t // tr
    n_in = len(rows) + len(halos) + len(bcasts)
    n_ro = len(out_rows)

    def body(*refs):
        ins = [r[...] for r in refs[:n_in]]
        o_refs = refs[n_in:]
        i = pl.program_id(0)
        routs, aouts = fn(i, nb, *ins)
        for ref, v in zip(o_refs[:n_ro], routs):
            ref[...] = v.astype(ref.dtype)
        for ref, v in zip(o_refs[n_ro:], aouts):
            @pl.when(i == 0)
            def _(ref=ref):
                ref[...] = jnp.zeros_like(ref)
            ref[...] += v.reshape(ref.shape)

    in_specs = [pl.BlockSpec((tr, r.shape[1]), lambda i: (i, 0)) for r in rows]
    for arr, which in halos:
        if which == "prev":
            in_specs.append(pl.BlockSpec((8, arr.shape[1]), lambda i: (jnp.maximum(i * (tr // 8) - 1, 0), 0)))
        else:
            in_specs.append(pl.BlockSpec((8, arr.shape[1]),
                                         lambda i: (jnp.minimum((i + 1) * (tr // 8), t // 8 - 1), 0)))
    for bc in bcasts:
        in_specs.append(pl.BlockSpec(bc.shape, lambda i, nd=bc.ndim: (0,) * nd))
    out_specs = [pl.BlockSpec((tr, c), lambda i: (i, 0)) for c, _ in out_rows]
    out_specs += [pl.BlockSpec(s, lambda i, nd=len(s): (0,) * nd) for s in out_accs]
    out_shape = [jax.ShapeDtypeStruct((t, c), dt) for c, dt in out_rows]
    out_shape += [jax.ShapeDtypeStruct(s, F32) for s in out_accs]
    return pl.pallas_call(
        body, name=name, grid=(nb,), in_specs=in_specs, out_specs=out_specs, out_shape=out_shape,
        compiler_params=_cparams(("arbitrary",)),
    )(*rows, *[h[0] for h in halos], *bcasts)


def _shift_prev(x, halo, i):
    rolled = pltpu.roll(x, 1, 0)
    first = jnp.where(i == 0, 0.0, halo[7:8, :])
    row = lax.broadcasted_iota(jnp.int32, x.shape, 0)
    return jnp.where(row == 0, first, rolled)


def _shift_next(x, halo, i, nb):
    rolled = pltpu.roll(x, x.shape[0] - 1, 0)
    last = jnp.where(i == nb - 1, 0.0, halo[0:1, :])
    row = lax.broadcasted_iota(jnp.int32, x.shape, 0)
    return jnp.where(row == x.shape[0] - 1, last, rolled)


def _ln(x, g, b, eps=LN_EPS):
    mu = jnp.mean(x, axis=-1, keepdims=True)
    xc = x - mu
    var = jnp.mean(xc * xc, axis=-1, keepdims=True)
    return xc * lax.rsqrt(var + eps) * g + b


def _embed_math(x, g, b, sc, sh):
    x0 = _ln(x, g, b)
    return x0, x0 * (1.0 + sc) + sh


def _post_math(xin, y, gate, g, b, sc, sh):
    x1 = _ln(ALPHA * xin + (1.0 + gate) * y, g, b)
    return x1, x1 * (1.0 + sc) + sh


def _loss_math(xin, h, tgt, gate, g, b):
    x2 = _ln(ALPHA * xin + (1.0 + gate) * h, g, b)
    err = x2 - tgt
    return 0.5 * jnp.sum(jnp.mean(err * err, axis=-1))


def _rwkv_pre_math(r, k, v, xw, xa, xg, w0, wd, a0, wa, wg, k_k, k_a, seg, seg_t, dot=_dot):
    wpre = -_softplus(-(w0 + dot(jnp.tanh(xw), wd, 1, 0, "x3"))) - 0.5
    lw = -jnp.exp(wpre)
    a = _sigmoid(a0 + dot(xa, wa, 1, 0, "x3"))
    g = dot(_sigmoid(xg), wg, 1, 0, "x3")
    kk = k * k_k
    norm = jnp.sqrt(dot(kk * kk, seg, 1, 0, "mb2"))
    kkn = kk * dot(1.0 / jnp.maximum(norm, 1e-12), seg_t, 1, 0, "mb2")
    k2 = k * (1.0 + (a - 1.0) * k_a)
    return r, lw, k2, v, kkn, kkn * a, g


def _rwkv_post_math(y, r, k2, v, g, lnx_g, lnx_b, r_k, seg, seg_t, dot=_dot):
    inv = 1.0 / HEAD
    spread = lambda z: dot(dot(z, seg, 1, 0, "mb2"), seg_t, 1, 0, "mb2")
    mu = spread(y) * inv
    yc = y - mu
    var = spread(yc * yc) * inv
    yn = yc * lax.rsqrt(var + LNX_EPS) * lnx_g + lnx_b
    bonus = spread(r * k2 * r_k) * v
    return (yn + bonus) * g


def _chunk_math(s0, r, lw, k, v, kk, b, dot=_dot):
    n = len(r)
    hs = range(n)
    c = r[0].shape[0]
    ti = lax.broadcasted_iota(jnp.int32, (2 * c, 2 * c), 0)
    tj = lax.broadcasted_iota(jnp.int32, (2 * c, 2 * c), 1)
    tt, jj = ti & (c - 1), tj & (c - 1)
    quad = jnp.where(ti < c, (tt > jj).astype(F32), (tt >= jj).astype(F32))
    incl = quad[c:, :c]
    eye = (ti[:c, :c] == tj[:c, :c]).astype(F32)
    cl = [dot(incl, lw[i], 1, 0, "mask") for i in hs]
    ge = [jnp.exp(cl[i]) for i in hs]
    gi = [jnp.exp(-cl[i]) for i in hs]
    ar = [jnp.concatenate([-kk[i] * jnp.exp(cl[i] - lw[i]), r[i] * ge[i]], axis=0) for i in hs]
    kb = [jnp.concatenate([k[i] * gi[i], b[i] * gi[i]], axis=0) for i in hs]
    m = [dot(ar[i], kb[i], 1, 1, "x3") * quad for i in hs]
    ars0 = [dot(ar[i], s0[i], 1, 1, "bf16") for i in hs]
    mv = [dot(m[i][:c, :c], v[i], 1, 0, "bf16") for i in hs]
    pw = [m[i][:c, c:] for i in hs]
    inv = [eye + pw[i] for i in hs]
    for _ in range(int(math.log2(c)) - 1):
        pw = [dot(pw[i], pw[i], 1, 0, "bf16") for i in hs]
        inv = [inv[i] + dot(inv[i], pw[i], 1, 0, "bf16") for i in hs]
    u = [dot(inv[i], ars0[i][:c] + mv[i], 1, 0, "bf16") for i in hs]
    vu = [jnp.concatenate([v[i], u[i]], axis=0) for i in hs]
    y = [ars0[i][c:] + dot(m[i][c:], vu[i], 1, 0, "bf16") for i in hs]
    s1 = [(s0[i] + dot(vu[i], kb[i], 0, 0, "x3")) * ge[i][c - 1:c, :] for i in hs]
    return y, s1


def _attn_math(q, kp, kc, vp, vc, bias, sinks, first, dot=_dot):
    hq = q.shape[1] // HEAD
    hkv = kc.shape[1] // HEAD
    group = hq // hkv
    cols = group * ATT_BLOCK
    kj = lax.broadcasted_iota(jnp.int32, (2 * ATT_BLOCK, cols), 0)
    qi = lax.broadcasted_iota(jnp.int32, (2 * ATT_BLOCK, cols), 1) & (ATT_BLOCK - 1)
    dist = qi + ATT_BLOCK - kj
    valid = (dist >= 0) & (dist < ATT_BLOCK) & (jnp.logical_not(first) | (kj >= ATT_BLOCK))
    eye = (lax.broadcasted_iota(jnp.int32, (ATT_BLOCK, ATT_BLOCK), 0)
           == lax.broadcasted_iota(jnp.int32, (ATT_BLOCK, ATT_BLOCK), 1)).astype(F32)
    outs = []
    for j in range(hkv):
        heads = range(j * group, (j + 1) * group)
        kband = jnp.concatenate([kp[:, j * HEAD:(j + 1) * HEAD], kc[:, j * HEAD:(j + 1) * HEAD]], axis=0)
        vband = jnp.concatenate([vp[:, j * HEAD:(j + 1) * HEAD], vc[:, j * HEAD:(j + 1) * HEAD]], axis=0)
        qg = jnp.concatenate([q[:, h * HEAD:(h + 1) * HEAD] for h in heads], axis=0)
        bias_g = jnp.concatenate([bias[h] for h in heads], axis=1)
        sink = jnp.concatenate([jnp.broadcast_to(sinks[0:1, h:h + 1], (1, ATT_BLOCK)) for h in heads], axis=1)
        s = dot(kband, qg, 1, 1, "bf16") * (HEAD ** -0.5) + bias_g
        s = jnp.where(valid, s, -1e30)
        m = jnp.maximum(jnp.max(s, axis=0, keepdims=True), sink)
        e = jnp.exp(s - m)
        p = e / (jnp.sum(e, axis=0, keepdims=True) + jnp.exp(sink - m))
        o_t = dot(vband, p, 0, 0, "bf16")
        outs += [dot(eye, o_t[:, g * ATT_BLOCK:(g + 1) * ATT_BLOCK], 1, 1, "bf16") for g in range(group)]
    return jnp.concatenate(outs, axis=1)


def _rec_specs(t, da, gh, reverse):
    nc = t // CHUNK
    if reverse:
        return pl.BlockSpec((CHUNK, gh * HEAD), lambda hg, c: (nc - 1 - c, hg))
    return pl.BlockSpec((CHUNK, gh * HEAD), lambda hg, c: (c, hg))


def _rec_fwd(r, lw, k, v, kk, b):
    t, da = r.shape
    h = da // HEAD
    gh = min(REC_HEADS, h)
    nc = t // CHUNK

    def body(r_ref, lw_ref, k_ref, v_ref, kk_ref, b_ref, y_ref, s0_ref, state):
        @pl.when(pl.program_id(1) == 0)
        def _():
            state[...] = jnp.zeros_like(state)

        sls = [slice(i * HEAD, (i + 1) * HEAD) for i in range(gh)]
        heads = lambda ref: [ref[:, sl] for sl in sls]
        s0 = [state[i] for i in range(gh)]
        y, s1 = _chunk_math(s0, heads(r_ref), heads(lw_ref), heads(k_ref), heads(v_ref), heads(kk_ref),
                            heads(b_ref), dot=_raw_dot)
        for i, sl in enumerate(sls):
            s0_ref[0, i] = s0[i]
            y_ref[:, sl] = y[i]
            state[i] = s1[i]

    spec = _rec_specs(t, da, gh, False)
    return pl.pallas_call(
        body, name="rwkv_recurrence_fwd", grid=(h // gh, nc),
        in_specs=[spec] * 6,
        out_specs=[spec, pl.BlockSpec((1, gh, HEAD, HEAD), lambda hg, c: (c, hg, 0, 0))],
        out_shape=[jax.ShapeDtypeStruct((t, da), F32), jax.ShapeDtypeStruct((nc, h, HEAD, HEAD), F32)],
        scratch_shapes=[pltpu.VMEM((gh, HEAD, HEAD), F32)],
        compiler_params=_cparams(("parallel", "arbitrary")),
    )(r, lw, k, v, kk, b)


def _rec_bwd(r, lw, k, v, kk, b, s0s, dy):
    t, da = r.shape
    h = da // HEAD
    gh = min(REC_HEADS, h)
    nc = t // CHUNK

    def body(r_ref, lw_ref, k_ref, v_ref, kk_ref, b_ref, dy_ref, s0_ref,
             dr_ref, dlw_ref, dk_ref, dv_ref, dkk_ref, db_ref, dstate):
        @pl.when(pl.program_id(1) == 0)
        def _():
            dstate[...] = jnp.zeros_like(dstate)

        sls = [slice(i * HEAD, (i + 1) * HEAD) for i in range(gh)]
        heads = lambda ref: [ref[:, sl] for sl in sls]
        _, vjp = jax.vjp(_chunk_math, [s0_ref[0, i] for i in range(gh)], heads(r_ref), heads(lw_ref),
                         heads(k_ref), heads(v_ref), heads(kk_ref), heads(b_ref))
        grads = vjp((heads(dy_ref), [dstate[i] for i in range(gh)]))
        for i, sl in enumerate(sls):
            dstate[i] = grads[0][i]
            for ref, val in zip((dr_ref, dlw_ref, dk_ref, dv_ref, dkk_ref, db_ref), grads[1:]):
                ref[:, sl] = val[i]

    spec = _rec_specs(t, da, gh, True)
    return pl.pallas_call(
        body, name="rwkv_recurrence_bwd", grid=(h // gh, nc),
        in_specs=[spec] * 7 + [pl.BlockSpec((1, gh, HEAD, HEAD), lambda hg, c: (nc - 1 - c, hg, 0, 0))],
        out_specs=[spec] * 6,
        out_shape=[jax.ShapeDtypeStruct((t, da), F32)] * 6,
        scratch_shapes=[pltpu.VMEM((gh, HEAD, HEAD), F32)],
        compiler_params=_cparams(("parallel", "arbitrary")),
    )(r, lw, k, v, kk, b, dy, s0s)


def _attn_specs(t, hq_w, hkv_w):
    nb = t // ATT_BLOCK
    cur = lambda w: pl.BlockSpec((ATT_BLOCK, w), lambda n: (n, 0))
    prev = lambda w: pl.BlockSpec((ATT_BLOCK, w), lambda n: (jnp.maximum(n - 1, 0), 0))
    return nb, cur, prev


def _attn_fwd(q, kb, vb, bias, sinks):
    t, qw = q.shape
    kw = kb.shape[1]
    nb, cur, prev = _attn_specs(t, qw, kw)

    def body(q_ref, kp_ref, kc_ref, vp_ref, vc_ref, bias_ref, sink_ref, o_ref):
        first = pl.program_id(0) == 0
        o = _attn_math(q_ref[...], kp_ref[...], kc_ref[...], vp_ref[...], vc_ref[...],
                       bias_ref[...], sink_ref[...], first, dot=_raw_dot)
        o_ref[...] = o.astype(o_ref.dtype)

    full = lambda a: pl.BlockSpec(a.shape, lambda n, nd=a.ndim: (0,) * nd)
    return pl.pallas_call(
        body, name="swa_attention_fwd", grid=(nb,),
        in_specs=[cur(qw), prev(kw), cur(kw), prev(kw), cur(kw), full(bias), full(sinks)],
        out_specs=cur(qw), out_shape=jax.ShapeDtypeStruct((t, qw), BF16),
        compiler_params=_cparams(("parallel",)),
    )(q, kb, kb, vb, vb, bias, sinks)


def _attn_bwd(q, kb, vb, bias, sinks, do, col_block):
    t, qw = q.shape
    kw = kb.shape[1]
    nb, cur, prev = _attn_specs(t, qw, kw)

    def body(q_ref, kp_ref, kc_ref, vp_ref, vc_ref, bias_ref, sink_ref, do_ref,
             dq_ref, dkp_ref, dkc_ref, dvp_ref, dvc_ref, dbias_ref, dsink_ref):
        n = pl.program_id(0)
        first = n == 0
        fn = functools.partial(_attn_math, first=first)
        _, vjp = jax.vjp(fn, q_ref[...], kp_ref[...], kc_ref[...], vp_ref[...], vc_ref[...],
                         bias_ref[...], sink_ref[...])
        dq, dkp, dkc, dvp, dvc, dbias, dsink = vjp(do_ref[...].astype(F32))
        dq_ref[...] = dq.astype(dq_ref.dtype)
        dkp_ref[...] = dkp
        dkc_ref[...] = dkc
        dvp_ref[...] = dvp
        dvc_ref[...] = dvc

        @pl.when(first)
        def _():
            dbias_ref[...] = jnp.zeros_like(dbias_ref)
            dsink_ref[...] = jnp.zeros_like(dsink_ref)

        dbias_ref[...] += dbias
        dsink_ref[...] += dsink

    full = lambda a: pl.BlockSpec(a.shape, lambda n, nd=a.ndim: (0,) * nd)
    kshape = jax.ShapeDtypeStruct((t, kw), F32)
    return pl.pallas_call(
        body, name="swa_attention_bwd", grid=(nb,),
        in_specs=[cur(qw), prev(kw), cur(kw), prev(kw), cur(kw), full(bias), full(sinks),
                  pl.BlockSpec((ATT_BLOCK, qw), lambda n: (n, col_block))],
        out_specs=[cur(qw), cur(kw), cur(kw), cur(kw), cur(kw), full(bias), full(sinks)],
        out_shape=[jax.ShapeDtypeStruct((t, qw), BF16), kshape, kshape, kshape, kshape,
                   jax.ShapeDtypeStruct(bias.shape, F32), jax.ShapeDtypeStruct(sinks.shape, F32)],
        compiler_params=_cparams(("arbitrary",)),
    )(q, kb, kb, vb, vb, bias, sinks, do)


def _bucket_onehot():
    qi = jnp.arange(ATT_BLOCK)[None, :]
    kj = jnp.arange(2 * ATT_BLOCK)[:, None]
    n = jnp.maximum(qi + ATT_BLOCK - kj, 0)
    buckets, max_exact = 32, 16
    nf = jnp.maximum(n, 1).astype(F32)
    large = max_exact + (jnp.log(nf / max_exact) / math.log(RPB_MAX_DIST / max_exact)
                         * (buckets - max_exact)).astype(jnp.int32)
    bucket = jnp.where(n < max_exact, n, jnp.minimum(large, buckets - 1)).reshape(-1)
    return (bucket[None, :] == jnp.arange(buckets)[:, None]).astype(F32)


def _small_dot(name, a, b, ca, cb):
    m = a.shape[1 - ca]
    n = b.shape[1 - cb]

    def body(a_ref, b_ref, o_ref):
        o_ref[...] = _raw_dot(a_ref[...], b_ref[...], ca, cb, "hi")

    return pl.pallas_call(body, name=name, out_shape=jax.ShapeDtypeStruct((m, n), F32),
                          compiler_params=_cparams())(a, b)


def _mod_fwd(c_all, w_mod):
    d, n = w_mod.shape
    tn = _tile(n, 512)

    def body(c_ref, w_ref, o_ref, cond_ref):
        cv = c_ref[...]
        cond = cv * _sigmoid(cv)
        cond_ref[...] = cond
        o_ref[...] = _raw_dot(cond, w_ref[...], 1, 0, "hi")

    return pl.pallas_call(
        body, name="adaln_mod_fwd", grid=(n // tn,),
        in_specs=[pl.BlockSpec(c_all.shape, lambda j: (0, 0)), pl.BlockSpec((d, tn), lambda j: (0, j))],
        out_specs=[pl.BlockSpec((c_all.shape[0], tn), lambda j: (0, j)),
                   pl.BlockSpec(c_all.shape, lambda j: (0, 0))],
        out_shape=[jax.ShapeDtypeStruct((c_all.shape[0], n), F32), jax.ShapeDtypeStruct(c_all.shape, F32)],
        compiler_params=_cparams(("arbitrary",)),
    )(c_all, w_mod)


def _adam_math(w, g, m, v):
    m = ADAM_B1 * m + (1.0 - ADAM_B1) * g
    v = ADAM_B2 * v + (1.0 - ADAM_B2) * (g * g)
    m_hat = m / (1.0 - ADAM_B1 ** ADAM_STEP)
    v_hat = v / (1.0 - ADAM_B2 ** ADAM_STEP)
    delta = -ADAM_LR * (m_hat / (jnp.sqrt(v_hat) + ADAM_EPS) + ADAM_WD * w)
    return delta, m, v


def _adamw(name, w, m, v, gparts):
    r, c = w.shape
    p = gparts.shape[0]
    tr = _rtile(r, max(8, (1 << 18) // max(c, 1) // 8 * 8))

    def body(w_ref, m_ref, v_ref, g_ref, go_ref, d_ref, mo_ref, vo_ref):
        g = g_ref[0].astype(F32)
        for s in range(1, p):
            g = g + g_ref[s].astype(F32)
        delta, mn, vn = _adam_math(w_ref[...], g, m_ref[...], v_ref[...])
        go_ref[...] = g
        d_ref[...] = delta
        mo_ref[...] = mn
        vo_ref[...] = vn

    spec = pl.BlockSpec((tr, c), lambda i: (i, 0))
    return pl.pallas_call(
        body, name=name, grid=(r // tr,),
        in_specs=[spec, spec, spec, pl.BlockSpec((p, tr, c), lambda i: (0, i, 0))],
        out_specs=[spec] * 4, out_shape=[jax.ShapeDtypeStruct((r, c), F32)] * 4,
        compiler_params=_cparams(("parallel",)),
    )(w, m, v, gparts)


def _adamw_outer(name, w, m, v, cond_t, dmod):
    d, n = w.shape
    tr, tn = _rtile(d, 512), _tile(n, 1024)

    def body(w_ref, m_ref, v_ref, c_ref, dm_ref, go_ref, d_ref, mo_ref, vo_ref):
        g = _raw_dot(c_ref[...], dm_ref[...], 1, 0, "hi")
        delta, mn, vn = _adam_math(w_ref[...], g, m_ref[...], v_ref[...])
        go_ref[...] = g
        d_ref[...] = delta
        mo_ref[...] = mn
        vo_ref[...] = vn

    spec = pl.BlockSpec((tr, tn), lambda i, j: (i, j))
    return pl.pallas_call(
        body, name=name, grid=(d // tr, n // tn),
        in_specs=[spec, spec, spec, pl.BlockSpec((tr, cond_t.shape[1]), lambda i, j: (i, 0)),
                  pl.BlockSpec((dmod.shape[0], tn), lambda i, j: (0, j))],
        out_specs=[spec] * 4, out_shape=[jax.ShapeDtypeStruct((d, n), F32)] * 4,
        compiler_params=_cparams(("parallel", "parallel")),
    )(w, m, v, cond_t, dmod)


def _all_gather(name, arrays):
    n = len(arrays)

    def body(*refs):
        ins, outs = refs[:n], refs[n:2 * n]
        send_sems, recv_sems, local_sems = refs[2 * n:]
        x, y, c = lax.axis_index("x"), lax.axis_index("y"), lax.axis_index("c")
        me, sibling = (x, y, c), (x, y, 1 - c)
        chips = [(1 - x, y), (x, 1 - y), (1 - x, 1 - y)]

        def copy(a, k, block, to, src=None):
            rows = outs[a].at[4 * block[0] + 2 * block[1] + block[2]]
            return pltpu.make_async_remote_copy(
                src_ref=rows if src is None else src, dst_ref=rows, send_sem=send_sems.at[a, k],
                recv_sem=recv_sems.at[a, k], device_id=to, device_id_type=pl.DeviceIdType.MESH)

        mine = [pltpu.make_async_copy(ins[a], outs[a].at[4 * x + 2 * y + c], local_sems.at[a]) for a in range(n)]
        for cp in mine:
            cp.start()
        first = []
        for a in range(n):
            first.append(copy(a, 0, me, sibling, src=ins[a]))
            first += [copy(a, 1 + j, me, (*chip, c), src=ins[a]) for j, chip in enumerate(chips)]
        for cp in first:
            cp.start()
        passed = []
        for a in range(n):
            for j, chip in enumerate(chips):
                copy(a, 1 + j, (*chip, c), me).wait_recv()
                passed.append(copy(a, 4 + j, (*chip, c), sibling))
                passed[-1].start()
        for a in range(n):
            copy(a, 0, sibling, me).wait_recv()
            for j, chip in enumerate(chips):
                copy(a, 4 + j, (*chip, 1 - c), me).wait_recv()
        for cp in first + passed:
            cp.wait_send()
        for cp in mine:
            cp.wait()

    any_spec = pl.BlockSpec(memory_space=pl.ANY)
    return pl.pallas_call(
        body, name=name, in_specs=[any_spec] * n, out_specs=[any_spec] * n,
        out_shape=[jax.ShapeDtypeStruct((N_DEV,) + a.shape, a.dtype) for a in arrays],
        scratch_shapes=[pltpu.SemaphoreType.DMA((n, N_DEV - 1)), pltpu.SemaphoreType.DMA((n, N_DEV - 1)),
                        pltpu.SemaphoreType.DMA((n,))],
    )(*arrays)


def _peer(p):
    x, y, c = lax.axis_index("x"), lax.axis_index("y"), lax.axis_index("c")
    px, py, pc = x ^ ((p >> 2) & 1), y ^ ((p >> 1) & 1), c ^ (p & 1)
    return (px, py, pc), 4 * px + 2 * py + pc


def _block(ref, d, cols, rows=None):
    if cols:
        r = slice(None) if rows is None else pl.ds(pl.multiple_of(rows[0], 8), rows[1])
        return ref.at[r, pl.ds(pl.multiple_of(d * cols, cols), cols)]
    return ref.at[d] if rows is None else ref.at[d, pl.ds(pl.multiple_of(rows[0], 8), rows[1])]


def _split_copy(src_ref, land_ref, send_sems, recv_sems, p, scatter, arriving, cols=None, halves=False):
    x, y, c = lax.axis_index("x"), lax.axis_index("y"), lax.axis_index("c")
    me = 4 * x + 2 * y + c
    dev, idx = _peer(p)
    if scatter == "chips":
        src, dst = src_ref.at[idx >> 1], land_ref.at[(idx if arriving else me) >> 1]
    elif scatter:
        src, dst = _block(src_ref, idx, cols), land_ref.at[idx if arriving else me]
    elif halves and p >= 2:
        half = src_ref.shape[0] // 2
        rows = ((c if arriving else dev[2]) * half, half)
        src = src_ref.at[pl.ds(pl.multiple_of(rows[0], 8), half)]
        dst = _block(land_ref, idx if arriving else me, cols, rows)
    else:
        src, dst = src_ref, _block(land_ref, idx if arriving else me, cols)
    return pltpu.make_async_remote_copy(
        src_ref=src, dst_ref=dst, send_sem=send_sems.at[p - 1], recv_sem=recv_sems.at[p - 1], device_id=dev,
        device_id_type=pl.DeviceIdType.MESH)


_HBM_SPEC = pl.BlockSpec(memory_space=pltpu.HBM)
_SEM_SPEC = pl.BlockSpec(memory_space=pltpu.SEMAPHORE)
_DATAFLOW = pltpu.SideEffectType.DATAFLOW_SIDE_EFFECTING


def _own_copy(src_ref, land_ref, send_sems, scatter, cols):
    me = 4 * lax.axis_index("x") + 2 * lax.axis_index("y") + lax.axis_index("c")
    if scatter == "chips":
        src, dst = src_ref.at[me >> 1], land_ref.at[me >> 1]
    elif scatter:
        src, dst = _block(src_ref, me, cols), land_ref.at[me]
    else:
        src, dst = src_ref, _block(land_ref, me, cols)
    return pltpu.make_async_copy(src, dst, send_sems.at[N_DEV - 1])


def _peers(scatter):
    return (2, 4, 6) if scatter == "chips" else tuple(range(1, N_DEV))


def _exchange_start(name, srcs, lands, scatter, after, cols=None, halves=False):
    n = len(srcs)
    cols = cols or [None] * n

    def body(*refs):
        src_refs, land_refs = refs[:n], refs[n:2 * n]
        outs = refs[2 * n + 1:]
        send, recv, token = outs[:n], outs[n:2 * n], outs[-1]
        for a in range(n):
            for p in _peers(scatter):
                _split_copy(src_refs[a], land_refs[a], send[a], recv[a], p, scatter, False, cols[a],
                            halves).start()
            _own_copy(src_refs[a], land_refs[a], send[a], scatter, cols[a]).start()
        token[...] = jnp.zeros_like(token)

    sems = [pltpu.SemaphoreType.DMA((N_DEV,))] * n + [pltpu.SemaphoreType.DMA((N_DEV - 1,))] * n
    hbm = [pltpu.HBM(a.shape, a.dtype) for a in list(srcs) + list(lands)]
    res = pl.pallas_call(
        body, name=name,
        out_shape=sems + hbm + [jax.ShapeDtypeStruct((8, 128), F32)],
        in_specs=[_HBM_SPEC] * (2 * n) + [pl.BlockSpec(memory_space=pl.ANY)],
        out_specs=[_SEM_SPEC] * (2 * n) + [_HBM_SPEC] * (2 * n) + [pl.BlockSpec(memory_space=pltpu.VMEM)],
        input_output_aliases={i: 2 * n + i for i in range(2 * n)},
        compiler_params=pltpu.CompilerParams(has_side_effects=_DATAFLOW),
    )(*[pltpu.with_memory_space_constraint(a, pltpu.HBM) for a in list(srcs) + list(lands)], after)
    return res[:n], res[n:2 * n], res[2 * n:3 * n], res[3 * n:4 * n], res[-1]


def _exchange_wait(name, send_sem, recv_sem, src, land, scatter, after, cols=None, halves=False):
    def body(src_ref, land_ref, send, recv, after_ref, src_out, land_out):
        for p in _peers(scatter):
            cp = _split_copy(src_ref, land_ref, send, recv, p, scatter, True, cols, halves)
            cp.wait_send()
            cp.wait_recv()
        _own_copy(src_ref, land_ref, send, scatter, cols).wait()

    return pl.pallas_call(
        body, name=name,
        out_shape=(pltpu.HBM(src.shape, src.dtype), pltpu.HBM(land.shape, land.dtype)),
        in_specs=[_HBM_SPEC, _HBM_SPEC, _SEM_SPEC, _SEM_SPEC, pl.BlockSpec(memory_space=pl.ANY)],
        out_specs=(_HBM_SPEC, _HBM_SPEC), input_output_aliases={0: 0, 1: 1},
        compiler_params=pltpu.CompilerParams(has_side_effects=_DATAFLOW),
    )(src, land, send_sem, recv_sem, after)[1]


def _sibling_fill(name, land, cols):
    rows = land.shape[0] if cols else land.shape[1]
    half = rows // 2

    def body(in_ref, out_ref, send_sems, recv_sems):
        x, y, c = lax.axis_index("x"), lax.axis_index("y"), lax.axis_index("c")

        def copy(p, core):
            _, idx = _peer(p)
            return pltpu.make_async_remote_copy(
                src_ref=_block(in_ref, idx, cols, (core * half, half)),
                dst_ref=_block(out_ref, idx, cols, (core * half, half)),
                send_sem=send_sems.at[idx], recv_sem=recv_sems.at[idx], device_id=(x, y, 1 - c),
                device_id_type=pl.DeviceIdType.MESH)

        sends = [copy(p, c) for p in range(2, N_DEV)]
        for cp in sends:
            cp.start()
        for p in range(2, N_DEV):
            copy(p, 1 - c).wait_recv()
        for cp in sends:
            cp.wait_send()

    any_spec = pl.BlockSpec(memory_space=pl.ANY)
    return pl.pallas_call(
        body, name=name, in_specs=[any_spec], out_specs=any_spec,
        out_shape=jax.ShapeDtypeStruct(land.shape, land.dtype), input_output_aliases={0: 0},
        scratch_shapes=[pltpu.SemaphoreType.DMA((N_DEV,)), pltpu.SemaphoreType.DMA((N_DEV,))],
    )(land)


def _pair_exchange(name, arrays):
    n = len(arrays)

    def body(*refs):
        ins, outs = refs[:n], refs[n:2 * n]
        send_sems, recv_sems = refs[2 * n:]
        x, y, c = lax.axis_index("x"), lax.axis_index("y"), lax.axis_index("c")
        copies = [pltpu.make_async_remote_copy(
            src_ref=ins[a].at[2 * k + (1 - c)], dst_ref=outs[a].at[k], send_sem=send_sems.at[a, k],
            recv_sem=recv_sems.at[a, k], device_id=(x, y, 1 - c), device_id_type=pl.DeviceIdType.MESH)
            for a in range(n) for k in range(N_DEV // 2)]
        for cp in copies:
            cp.start()
        for cp in copies:
            cp.wait()

    any_spec = pl.BlockSpec(memory_space=pl.ANY)
    return pl.pallas_call(
        body, name=name, in_specs=[any_spec] * n, out_specs=[any_spec] * n,
        out_shape=[jax.ShapeDtypeStruct((N_DEV // 2,) + a.shape[1:], a.dtype) for a in arrays],
        scratch_shapes=[pltpu.SemaphoreType.DMA((n, N_DEV // 2)), pltpu.SemaphoreType.DMA((n, N_DEV // 2))],
    )(*arrays)


def _pair_sum(name, mine, theirs):
    _, r, c_ = mine.shape
    tr = _rtile(r, max(8, (1 << 19) // c_ // 8 * 8))

    def body(mine_ref, theirs_ref, o_ref):
        core = lax.axis_index("c")
        o_ref[0] = (mine_ref[0, core].astype(F32) + theirs_ref[0].astype(F32)).astype(o_ref.dtype)

    return pl.pallas_call(
        body, name=name, grid=(N_DEV // 2, r // tr),
        in_specs=[pl.BlockSpec((1, 2, tr, c_), lambda k, i: (k, 0, i, 0)),
                  pl.BlockSpec((1, tr, c_), lambda k, i: (k, i, 0))],
        out_specs=pl.BlockSpec((1, tr, c_), lambda k, i: (k, i, 0)),
        out_shape=jax.ShapeDtypeStruct((N_DEV // 2, r, c_), mine.dtype),
        compiler_params=_cparams(("parallel", "parallel")),
    )(mine.reshape(N_DEV // 2, 2, r, c_), theirs)


def _cols_to_shards(a):
    r, c = a.shape
    return a.reshape(r, N_DEV, c // N_DEV).transpose(1, 0, 2)


def _shards_to_cols(a):
    d, r, n = a.shape
    return a.transpose(1, 0, 2).reshape(r, d * n)


def kernel(x, c, ln_emb_g, ln_emb_b, rpb_table, w_mod, b_mod, w_in, mu_shift, w0, w_decay_up, a0, w_iclr_up, w_gate_up, k_k, k_a, r_k, lnx_g, lnx_b, attn_sinks, w_out, ln1_g, ln1_b, w_up, w_down, ln2_g, ln2_b, loss_target, m_ln_emb_g, m_ln_emb_b, m_rpb_table, m_w_mod, m_b_mod, m_w_in, m_mu_shift, m_w0, m_w_decay_up, m_a0, m_w_iclr_up, m_w_gate_up, m_k_k, m_k_a, m_r_k, m_lnx_g, m_lnx_b, m_attn_sinks, m_w_out, m_ln1_g, m_ln1_b, m_w_up, m_w_down, m_ln2_g, m_ln2_b, v_ln_emb_g, v_ln_emb_b, v_rpb_table, v_w_mod, v_b_mod, v_w_in, v_mu_shift, v_w0, v_w_decay_up, v_a0, v_w_iclr_up, v_w_gate_up, v_k_k, v_k_a, v_r_k, v_lnx_g, v_lnx_b, v_attn_sinks, v_w_out, v_ln1_g, v_ln1_b, v_w_up, v_w_down, v_ln2_g, v_ln2_b):
    names = ["ln_emb_g", "ln_emb_b", "rpb_table", "w_mod", "b_mod", "w_in", "mu_shift", "w0", "w_decay_up",
             "a0", "w_iclr_up", "w_gate_up", "k_k", "k_a", "r_k", "lnx_g", "lnx_b", "attn_sinks", "w_out",
             "ln1_g", "ln1_b", "w_up", "w_down", "ln2_g", "ln2_b"]
    env = dict(locals())
    weights = {nm: env[nm] for nm in names}
    mom_m = {nm: env["m_" + nm] for nm in names}
    mom_v = {nm: env["v_" + nm] for nm in names}

    t, d = x.shape[1], x.shape[2]
    da = d // 2
    h_a = da // HEAD
    hq = (d - da) // HEAD
    hkv = hq // GQA_RATIO
    l_w, l_a, l_g = w_decay_up.shape[1], w_iclr_up.shape[1], w_gate_up.shape[1]
    o_w, o_a, o_g = 3 * da, 3 * da + l_w, 3 * da + l_w + l_a
    n_rwkv = o_g + l_g
    o_kb, o_vb = n_rwkv + hq * HEAD, n_rwkv + hq * HEAD + hkv * HEAD
    me = 4 * lax.axis_index("x") + 2 * lax.axis_index("y") + lax.axis_index("c")

    x2d, tgt = x[0], loss_target[0]
    row = lambda a: a.reshape(1, -1)
    seg = (jnp.arange(da)[:, None] // HEAD == jnp.arange(h_a)[None, :]).astype(F32)
    seg_t = seg.T

    (c_all,) = _all_gather("gather_cond", [c])
    c_all = c_all.reshape(N_DEV, d)
    mod_rows, cond_all = _mod_fwd(c_all, w_mod[0])
    gathered = _all_gather("gather_weights", [
        mod_rows, w_in[0].astype(BF16), w_decay_up[0], w_iclr_up[0], w_gate_up[0]])
    mod_all, win_g, wd_g, wa_g, wg_g = gathered
    late = [w_out[0].astype(BF16), w_up[0].astype(BF16), w_down[0].astype(BF16)]
    n_up = w_up.shape[2]
    late_cols = [None, n_up, None]
    late_lands = [lax.empty((N_DEV,) + late[0].shape, BF16), lax.empty((d, N_DEV * n_up), BF16),
                  lax.empty((N_DEV,) + late[2].shape, BF16)]
    late_send, late_recv, late_src, late_land, late_token = _exchange_start(
        "gather_late_weights_start", late, late_lands, False, mod_all, late_cols, halves=True)
    mod = lax.dynamic_index_in_dim(mod_all, me, axis=1, keepdims=False).reshape(1, -1) + b_mod
    mod = mod + late_token[0, 0]
    sh1, sc1, g1, sh2, sc2, g2 = [mod[:, i * d:(i + 1) * d] for i in range(6)]
    w_in_f = _shards_to_cols(win_g)
    wd_f, wa_f, wg_f = _shards_to_cols(wd_g), _shards_to_cols(wa_g), _shards_to_cols(wg_g)

    def late_weight(i, nm, after):
        land = _exchange_wait("gather_" + nm + "_wait", late_send[i], late_recv[i], late_src[i], late_land[i],
                              False, after, late_cols[i], halves=True)
        return _sibling_fill("gather_" + nm + "_fill", land, late_cols[i])

    tr = _rtile(t, 256)
    lng, lnb = row(ln_emb_g), row(ln_emb_b)

    def embed_fn(i, nb, xb, g, b, sc, sh):
        return _embed_math(xb, g, b, sc, sh), ()
    x0, u1 = _rowwise("embed_ln_mod", embed_fn, [x2d], [lng, lnb, sc1, sh1], [(d, F32), (d, BF16)], [], tr)

    (p_rkv,) = _matmul("in_proj_rkv", u1, w_in_f[:, :o_w], "nn", [F32])
    (p_lora,) = _matmul("in_proj_lora", u1, w_in_f[:, o_w:n_rwkv], "nn", [F32])
    (q,) = _matmul("in_proj_q", u1, w_in_f[:, n_rwkv:o_kb], "nn", [F32])
    (p_kv,) = _matmul("in_proj_kv", u1, w_in_f[:, o_kb:], "nn", [F32])
    p_w, p_a, p_g = p_lora[:, :l_w], p_lora[:, l_w:l_w + l_a], p_lora[:, l_w + l_a:]
    kb, vb = p_kv[:, :hkv * HEAD], p_kv[:, hkv * HEAD:]
    mu_rkv, mu_w, mu_a, mu_g = (mu_shift[:, :o_w], mu_shift[:, o_w:o_a], mu_shift[:, o_a:o_g],
                                mu_shift[:, o_g:n_rwkv])
    pre_params = [w0, wd_f, a0, wa_f, wg_f, k_k, k_a, seg, seg_t]
    tr_pre = _rtile(t, 128)

    def shifted(i, blocks, halos, mus):
        return [xb + (_shift_prev(xb, hb, i) - xb) * mb for xb, hb, mb in zip(blocks, halos, mus)]

    def split3(a):
        return a[:, :da], a[:, da:2 * da], a[:, 2 * da:]

    def pre_fn(i, nb, b_rkv, b_w, b_a, b_g, h_rkv, h_w, h_a_, h_g, m_rkv, m_w, m_a, m_g, *params):
        s_rkv, s_w, s_a, s_g = shifted(i, [b_rkv, b_w, b_a, b_g], [h_rkv, h_w, h_a_, h_g],
                                       [m_rkv, m_w, m_a, m_g])
        return _rwkv_pre_math(*split3(s_rkv), s_w, s_a, s_g, *params, dot=_raw_dot), ()

    pre_rows = [p_rkv, p_w, p_a, p_g]
    pre_halos = [(a, "prev") for a in pre_rows]
    r_, lw_, k2_, v_, kk_, b_, gate_ = _rowwise(
        "rwkv_pre", pre_fn, pre_rows, [mu_rkv, mu_w, mu_a, mu_g] + pre_params,
        [(da, F32)] * 7, [], tr_pre, halos=pre_halos)

    y_rec, s0s = _rec_fwd(r_, lw_, k2_, v_, kk_, b_)

    rk_flat = r_k.reshape(1, da)
    post_params = [lnx_g, lnx_b, rk_flat, seg, seg_t]

    def post_fn(i, nb, yb, rb, kb_, vb_, gb, *params):
        return (_rwkv_post_math(yb, rb, kb_, vb_, gb, *params, dot=_raw_dot),), ()
    (ya,) = _rowwise("rwkv_post", post_fn, [y_rec, r_, k2_, v_, gate_], post_params, [(da, BF16)], [], tr_pre)

    onehot = _bucket_onehot()
    bias = _small_dot("rpb_gather", rpb_table, onehot, 0, 0)
    bias = bias.reshape(hq, 2 * ATT_BLOCK, ATT_BLOCK)
    yb = _attn_fwd(q, kb, vb, bias, attn_sinks)

    mix_in = jnp.concatenate([ya, yb], axis=1)
    w_out_f = late_weight(0, "w_out", mix_in).reshape(d, d)
    (mix,) = _matmul("out_proj", mix_in, w_out_f, "nn", [F32])

    def post1_fn(i, nb, xin, yv, gate, g, b, sc, sh):
        return _post_math(xin, yv, gate, g, b, sc, sh), ()
    x1, u2 = _rowwise("ln1_mod", post1_fn, [x0, mix], [g1, ln1_g, ln1_b, sc2, sh2],
                      [(d, F32), (d, BF16)], [], tr)

    def relu2(acc):
        rl = jnp.maximum(acc, 0.0)
        return acc, rl * rl
    w_up_f = late_weight(1, "w_up", u2)
    hpre, hact = _matmul("mlp_up", u2, w_up_f, "nn", [F32, BF16], epilogue=relu2)
    w_down_f = late_weight(2, "w_down", hact).reshape(-1, d)
    (hmlp,) = _matmul("mlp_down", hact, w_down_f, "nn", [F32])

    def loss_fn(i, nb, xin, hv, tg, gate, g, b):
        val, vjp = jax.vjp(_loss_math, xin, hv, tg, gate, g, b)
        dxin, dh, _, dgate, dg, db = vjp(jnp.ones((), F32))
        return (dxin, dh), (val, dgate, dg, db)
    dx1, dh, loss_acc, dg2, dln2g, dln2b = _rowwise(
        "ln2_loss", loss_fn, [x1, hmlp, tgt], [g2, ln2_g, ln2_b], [(d, F32), (d, BF16)],
        [(1, 1), (1, d), (1, d), (1, d)], _rtile(t, 128))

    def drelu2(acc, hp):
        return (acc * 2.0 * jnp.maximum(hp, 0.0),)
    (dhpre,) = _matmul("mlp_down_dgrad", dh, w_down_f, "nt", [BF16], epilogue=drelu2, extras=[hpre])
    (gw_down,) = _matmul("mlp_down_wgrad", hact, dh, "tn", [BF16])
    (du2,) = _matmul("mlp_up_dgrad", dhpre, w_up_f, "nt", [F32])
    (gw_up,) = _matmul("mlp_up_wgrad", u2, dhpre, "tn", [BF16])

    def landing(src):
        return lax.empty(src.shape, src.dtype)
    mlp_src = [gw_down.reshape(N_DEV, -1, d), gw_up]
    mlp_cols = [None, n_up]
    mlp_send, mlp_recv, mlp_src, mlp_land, mlp_token = _exchange_start(
        "scatter_mlp_grads_start", mlp_src, [landing(mlp_src[0]), lax.empty((N_DEV, d, n_up), BF16)], True,
        gw_up, mlp_cols)

    def post1_bwd(i, nb, xin, yv, dx1v, du2v, gate, g, b, sc, sh):
        _, vjp = jax.vjp(_post_math, xin, yv, gate, g, b, sc, sh)
        dxin, dy, dgate, dg, db, dsc, dsh = vjp((dx1v, du2v))
        return (dxin, dy), (dgate, dg, db, dsc, dsh)
    dx0, dmix, dg1, dln1g, dln1b, dsc2, dsh2 = _rowwise(
        "ln1_mod_bwd", post1_bwd, [x0, mix, dx1, du2], [g1 + mlp_token[0, 0], ln1_g, ln1_b, sc2, sh2],
        [(d, F32), (d, BF16)], [(1, d)] * 5, _rtile(t, 128))

    (dmix_in,) = _matmul("out_proj_dgrad", dmix, w_out_f, "nt", [F32])
    (gw_out,) = _matmul("out_proj_wgrad", mix_in, dmix, "tn", [BF16])
    dya = dmix_in[:, :da]
    out_src = [gw_out.reshape(N_DEV, d // N_DEV, d)]
    out_send, out_recv, out_src, out_land, out_token = _exchange_start(
        "scatter_out_grad_start", out_src, [landing(a) for a in out_src], True, gw_out)
    post_params_bwd = [lnx_g, lnx_b, rk_flat + out_token[0, 0], seg, seg_t]

    def post_bwd(i, nb, yb_, rb, kb_, vb_, gb, dyab, *params):
        _, vjp = jax.vjp(_rwkv_post_math, yb_, rb, kb_, vb_, gb, *params)
        dy, dr, dk, dv, dg, dlg, dlb, drk, _, _ = vjp(dyab)
        return (dy, dr, dk, dv, dg), (dlg, dlb, drk)
    dy_rec, dr_e, dk_e, dv_e, dgate, dlnxg, dlnxb, drk = _rowwise(
        "rwkv_post_bwd", post_bwd, [y_rec, r_, k2_, v_, gate_, dya], post_params_bwd,
        [(da, F32)] * 5, [(1, da)] * 3, tr_pre)

    dr_r, dlw_r, dk_r, dv_r, dkk_r, db_r = _rec_bwd(r_, lw_, k2_, v_, kk_, b_, s0s, dy_rec)

    def pre_bwd(i, nb, b_rkv, b_w, b_a, b_g, dr1, dr2, dlw, dk1, dk2, dv1, dv2, dkk, dbb, dgt,
                h_rkv, h_w, h_a_, h_g, m_rkv, m_w, m_a, m_g, *params):
        blocks = [b_rkv, b_w, b_a, b_g]
        prevs = [_shift_prev(xb, hb, i) for xb, hb in zip(blocks, [h_rkv, h_w, h_a_, h_g])]
        mus = [m_rkv, m_w, m_a, m_g]
        s_rkv, s_w, s_a, s_g = [xb + (pb - xb) * mb for xb, pb, mb in zip(blocks, prevs, mus)]
        _, vjp = jax.vjp(_rwkv_pre_math, *split3(s_rkv), s_w, s_a, s_g, *params)
        grads = vjp((dr1 + dr2, dlw, dk1 + dk2, dv1 + dv2, dkk, dbb, dgt))
        g_rkv = jnp.concatenate(grads[:3], axis=1)
        g_w, g_a, g_g = grads[3:6]
        dmu = [jnp.sum(gs * (pb - xb), axis=0, keepdims=True)
               for gs, pb, xb in zip([g_rkv, g_w, g_a, g_g], prevs, blocks)]
        dw0, dwd, da0, dwa, dwg, dkk_, dka = grads[6:13]
        return (g_rkv, g_w, g_a, g_g), (*dmu, dw0, dwd, da0, dwa, dwg, dkk_, dka)

    pre_out = _rowwise(
        "rwkv_pre_bwd", pre_bwd,
        pre_rows + [dr_r, dr_e, dlw_r, dk_r, dk_e, dv_r, dv_e, dkk_r, db_r, dgate],
        [mu_rkv, mu_w, mu_a, mu_g] + pre_params,
        [(o_w, F32), (l_w, F32), (l_a, F32), (l_g, F32)],
        [(1, o_w), (1, l_w), (1, l_a), (1, l_g), (1, da), (l_w, da), (1, da), (l_a, da), (l_g, da),
         (1, da), (1, da)],
        _rtile(t, 64), halos=pre_halos)
    gs_rkv, gs_w, gs_a, gs_g = pre_out[:4]
    dmu_parts = pre_out[4:8]
    dw0, dwd, da0, dwa, dwg, dk_k, dk_a = pre_out[8:]

    def unshift_fn(i, nb, a1, a2, a3, a4, n1, n2, n3, n4, m1, m2, m3, m4):
        outs = [gs * (1.0 - mb) + _shift_next(gs * mb, hb * mb, i, nb)
                for gs, hb, mb in zip([a1, a2, a3, a4], [n1, n2, n3, n4], [m1, m2, m3, m4])]
        return outs, ()
    gs_list = [gs_rkv, gs_w, gs_a, gs_g]
    dp_rkv, dp_w, dp_a, dp_g = _rowwise(
        "token_shift_bwd", unshift_fn, gs_list, [mu_rkv, mu_w, mu_a, mu_g],
        [(o_w, BF16), (l_w, BF16), (l_a, BF16), (l_g, BF16)], [], tr_pre,
        halos=[(a, "next") for a in gs_list])

    dq, dkp, dkc, dvp, dvc, dbias, dsinks = _attn_bwd(q, kb, vb, bias, attn_sinks, dmix_in, 1)
    zpad = jnp.zeros((ATT_BLOCK, kb.shape[1]), F32)
    dkb = (dkc + jnp.concatenate([dkp[ATT_BLOCK:], zpad], axis=0)).astype(BF16)
    dvb = (dvc + jnp.concatenate([dvp[ATT_BLOCK:], zpad], axis=0)).astype(BF16)
    d_rpb = _small_dot("rpb_scatter", onehot, dbias.reshape(hq, -1), 1, 1)

    dp = jnp.concatenate([dp_rkv, dp_w, dp_a, dp_g, dq, dkb, dvb], axis=1)
    (gw_in,) = _matmul("in_proj_wgrad", u1, dp, "tn", [BF16])
    in_names = ["w_in", "w_decay_up", "w_iclr_up", "w_gate_up"]
    in_parts = [_cols_to_shards(gw_in), _cols_to_shards(dwd), _cols_to_shards(dwa), _cols_to_shards(dwg)]
    in_sibling = _pair_exchange("pair_in_grads", in_parts)
    in_src = [_pair_sum("pair_sum_" + nm, a, b) for nm, a, b in zip(in_names, in_parts, in_sibling)]
    in_send, in_recv, in_src, in_land, in_token = _exchange_start(
        "scatter_in_grads_start", in_src, [landing(a) for a in in_src], "chips", in_src[0])
    (du1,) = _matmul("in_proj_dgrad", dp, w_in_f, "nt", [F32], after=in_token)

    def embed_bwd(i, nb, xb, dx0v, du1v, g, b, sc, sh):
        _, vjp = jax.vjp(_embed_math, xb, g, b, sc, sh)
        dxv, dg, db, dsc, dsh = vjp((dx0v, du1v))
        return (dxv,), (dg, db, dsc, dsh)
    grad_x, dlng, dlnb, dsc1, dsh1 = _rowwise(
        "embed_ln_mod_bwd", embed_bwd, [x2d, dx0, du1], [lng, lnb, sc1, sh1], [(d, F32)], [(1, d)] * 4,
        _rtile(t, 128))

    dmod = jnp.concatenate([dsh1, dsc1, dg1, dsh2, dsc2, dg2], axis=1)
    small = {"ln_emb_g": dlng, "ln_emb_b": dlnb, "rpb_table": d_rpb, "b_mod": dmod,
             "mu_shift": jnp.concatenate(dmu_parts, axis=1), "w0": dw0, "a0": da0, "k_k": dk_k, "k_a": dk_a,
             "r_k": drk, "lnx_g": dlnxg, "lnx_b": dlnxb, "attn_sinks": dsinks, "ln1_g": dln1g, "ln1_b": dln1b,
             "ln2_g": dln2g, "ln2_b": dln2b}
    small_names = list(small)
    packed = jnp.concatenate([small[nm].reshape(1, -1) for nm in small_names], axis=1)
    sm_send, sm_recv, sm_src, sm_land, sm_token = _exchange_start(
        "gather_small_grads_start", [packed], [lax.empty((N_DEV,) + packed.shape, F32)], False, grad_x)

    grads, deltas, new_m, new_v = {}, {}, {}, {}

    def put(nm, res):
        shape = weights[nm].shape
        grads[nm], deltas[nm], new_m[nm], new_v[nm] = [a.reshape(shape) for a in res]

    def big_update(nm, parts):
        put(nm, _adamw("adamw_" + nm, weights[nm][0], mom_m[nm][0], mom_v[nm][0], parts))

    behind = sm_token
    big_update("w_down", _exchange_wait("scatter_w_down_wait", mlp_send[0], mlp_recv[0], mlp_src[0], mlp_land[0],
                                        True, behind, mlp_cols[0]))
    big_update("w_up", _exchange_wait("scatter_w_up_wait", mlp_send[1], mlp_recv[1], mlp_src[1], mlp_land[1],
                                      True, behind, mlp_cols[1]))
    big_update("w_out", _exchange_wait("scatter_w_out_wait", out_send[0], out_recv[0], out_src[0], out_land[0],
                                       True, behind))

    packed_all = _exchange_wait("gather_small_grads_wait", sm_send[0], sm_recv[0], sm_src[0], sm_land[0], False,
                                deltas["w_out"]).reshape(N_DEV, -1)
    n_mod = w_mod.shape[2]
    dmod_cols = lax.dynamic_slice_in_dim(packed_all[:, _offset(small, small_names, "b_mod"):], me * n_mod, n_mod,
                                         axis=1)
    put("w_mod", _adamw_outer("adamw_w_mod", w_mod[0], m_w_mod[0], v_w_mod[0], cond_all.T, dmod_cols))

    off = 0
    for nm in small_names:
        size = small[nm].size
        wshape = weights[nm].shape
        two_d = (1, size) if nm != "rpb_table" else wshape
        parts = packed_all[:, off:off + size].reshape((N_DEV,) + two_d)
        off += size
        put(nm, _adamw("adamw_" + nm, weights[nm].reshape(two_d), mom_m[nm].reshape(two_d),
                       mom_v[nm].reshape(two_d), parts))

    behind = deltas[small_names[-1]]
    for i, nm in enumerate(in_names):
        big_update(nm, _exchange_wait("scatter_" + nm + "_wait", in_send[i], in_recv[i], in_src[i], in_land[i],
                                      "chips", behind))

    loss = lax.psum(loss_acc[0, 0], MESH_AXES)
    return (loss, grad_x[None], *[grads[nm] for nm in names], *[deltas[nm] for nm in names],
            *[new_m[nm] for nm in names], *[new_v[nm] for nm in names])


def _offset(small, small_names, name):
    off = 0
    for nm in small_names:
        if nm == name:
            return off
        off += small[nm].size
    raise KeyError(name)
```

```python
import functools
import math

import jax
import jax.numpy as jnp
from jax import lax
from jax.experimental import pallas as pl
from jax.experimental.pallas import tpu as pltpu

F32 = jnp.float32
BF16 = jnp.bfloat16
HI = lax.Precision.HIGHEST
MESH_AXES = ("x", "y", "c")
N_DEV = 8

HEAD = 64
GQA_RATIO = 8
ATT_BLOCK = 128
RPB_MAX_DIST = 128
LN_EPS = 1e-5
LNX_EPS = 64e-5
DEPTH = 1
ALPHA = (2.0 * DEPTH) ** 0.25
CHUNK = 64
REC_HEADS = 16

ADAM_LR = 0.001
ADAM_B1 = 0.9
ADAM_B2 = 0.999
ADAM_EPS = 1e-08
ADAM_WD = 0.01
ADAM_STEP = 10

VMEM_LIMIT = 60 * 1024 * 1024


def _cparams(sem=None):
    return pltpu.CompilerParams(dimension_semantics=sem, vmem_limit_bytes=VMEM_LIMIT)


def _tile(dim, cap):
    best = None
    t = 128
    while t <= min(dim, cap):
        if dim % t == 0:
            best = t
        t += 128
    return best or dim


def _rtile(dim, cap):
    best = None
    t = 8
    while t <= min(dim, cap):
        if dim % t == 0:
            best = t
        t += 8
    return best or dim


def _split2(a):
    hi = a.astype(BF16)
    return hi, (a - hi.astype(F32)).astype(BF16)


def _raw_dot(a, b, ca, cb, prec):
    dims = (((ca,), (cb,)), ((), ()))
    mm = lambda p, q: lax.dot_general(p, q, dims, preferred_element_type=F32)
    if prec == "bf16":
        return mm(a.astype(BF16), b.astype(BF16))
    if prec == "x3":
        (ah, al), (bh, bl) = _split2(a), _split2(b)
        return mm(ah, bh) + (mm(ah, bl) + mm(al, bh))
    if prec == "mask":
        ab = a.astype(BF16)
        b1, b2 = _split2(b)
        b3 = (b - b1.astype(F32) - b2.astype(F32)).astype(BF16)
        return mm(ab, b1) + (mm(ab, b2) + mm(ab, b3))
    if prec == "mb2":
        (ah, al), bb = _split2(a), b.astype(BF16)
        return mm(ah, bb) + mm(al, bb)
    return lax.dot_general(a, b, dims, precision=HI, preferred_element_type=F32)


@functools.partial(jax.custom_vjp, nondiff_argnums=(2, 3, 4))
def _bf16_dot(a, b, ca, cb, prec):
    return _raw_dot(a, b, ca, cb, prec)


def _bf16_dot_fwd(a, b, ca, cb, prec):
    return _raw_dot(a, b, ca, cb, prec), (a, b)


def _bf16_dot_bwd(ca, cb, prec, res, g):
    a, b = res
    if prec == "mask":
        return jnp.zeros_like(a), _bf16_dot(a, g, 1 - ca, 0, prec)
    if prec == "mb2":
        return _bf16_dot(g, b, 1, 1, prec), jnp.zeros_like(b)
    if ca == 1:
        da = _bf16_dot(g, b, 1, 1 - cb, prec)
    else:
        da = _bf16_dot(b, g, 1 - cb, 1, prec)
    if cb == 0:
        db = _bf16_dot(a, g, 1 - ca, 0, prec)
    else:
        db = _bf16_dot(g, a, 0, 1 - ca, prec)
    return da, db


_bf16_dot.defvjp(_bf16_dot_fwd, _bf16_dot_bwd)


def _dot(a, b, ca, cb, prec):
    return _raw_dot(a, b, ca, cb, prec) if prec == "hi" else _bf16_dot(a, b, ca, cb, prec)


def _sigmoid(z):
    return 1.0 / (1.0 + jnp.exp(-z))


def _softplus(z):
    return jnp.maximum(z, 0.0) + jnp.log(1.0 + jnp.exp(-jnp.abs(z)))


MATMUL_VMEM_BUDGET = 51 * 1024 * 1024


def _matmul_tiles(m, n, k, in_bytes, out_dtypes, n_extras):
    tm, tn = _tile(m, 1024), _tile(n, 1024)
    out_bytes = sum(jnp.dtype(dt).itemsize for dt in out_dtypes)
    for cap in (4096, 2048, 1024, 512, 256, 128):
        tk = _tile(k, cap)
        acc = 4 * tm * tn if tk < k else 0
        need = 2 * in_bytes * (tm + tn) * tk + 2 * tm * tn * (out_bytes + 4 * n_extras) + acc + 4 * tm * tn
        if need <= MATMUL_VMEM_BUDGET:
            break
    return tm, tn, tk


def _matmul(name, a, b, mode, out_dtypes, epilogue=None, extras=(), after=None):
    if mode == "nn":
        (m, k), n = a.shape, b.shape[1]
    elif mode == "nt":
        (m, k), n = a.shape, b.shape[0]
    else:
        (k, m), n = a.shape, b.shape[1]
    tm, tn, tk = _matmul_tiles(m, n, k, a.dtype.itemsize, out_dtypes, len(extras))
    nk = k // tk
    ne, no = len(extras), len(out_dtypes)
    ca, cb = {"nn": (1, 0), "nt": (1, 1), "tn": (0, 0)}[mode]

    n_after = 0 if after is None else 1

    def body(a_ref, b_ref, *rest):
        rest = rest[n_after:]
        extra_refs, out_refs = rest[:ne], rest[ne:ne + no]
        acc = rest[-1] if nk > 1 else None
        kk = pl.program_id(2)
        part = _raw_dot(a_ref[...], b_ref[...], ca, cb, "bf16")

        def finish(total):
            res = epilogue(total, *[e[...] for e in extra_refs]) if epilogue else (total,)
            for o, v in zip(out_refs, res):
                o[...] = v.astype(o.dtype)

        if nk == 1:
            finish(part)
            return

        @pl.when(kk == 0)
        def _():
            acc[...] = part

        @pl.when((kk > 0) & (kk < nk - 1))
        def _():
            acc[...] += part

        @pl.when(kk == nk - 1)
        def _():
            finish(acc[...] + part)

    a_spec = (pl.BlockSpec((tk, tm), lambda i, j, kk: (kk, i)) if mode == "tn"
              else pl.BlockSpec((tm, tk), lambda i, j, kk: (i, kk)))
    b_spec = (pl.BlockSpec((tn, tk), lambda i, j, kk: (j, kk)) if mode == "nt"
              else pl.BlockSpec((tk, tn), lambda i, j, kk: (kk, j)))
    mn_spec = pl.BlockSpec((tm, tn), lambda i, j, kk: (i, j))
    after_specs = [pl.BlockSpec(memory_space=pl.ANY)] * n_after
    outs = pl.pallas_call(
        body, name=name, grid=(m // tm, n // tn, nk),
        in_specs=[a_spec, b_spec] + after_specs + [mn_spec] * ne,
        out_specs=[mn_spec] * no,
        out_shape=[jax.ShapeDtypeStruct((m, n), dt) for dt in out_dtypes],
        scratch_shapes=[pltpu.VMEM((tm, tn), F32)] if nk > 1 else [],
        compiler_params=_cparams(("parallel", "parallel", "arbitrary")),
    )(a, b, *([after] * n_after), *extras)
    return outs


def _rowwise(name, fn, rows, bcasts, out_rows, out_accs, tr, halos=()):
    t = rows[0].shape[0]
    nb = t // tr
    n_in = len(rows) + len(halos) + len(bcasts)
    n_ro = len(out_rows)

    def body(*refs):
        ins = [r[...] for r in refs[:n_in]]
        o_refs = refs[n_in:]
        i = pl.program_id(0)
        routs, aouts = fn(i, nb, *ins)
        for ref, v in zip(o_refs[:n_ro], routs):
            ref[...] = v.astype(ref.dtype)
        for ref, v in zip(o_refs[n_ro:], aouts):
            @pl.when(i == 0)
            def _(ref=ref):
                ref[...] = jnp.zeros_like(ref)
            ref[...] += v.reshape(ref.shape)

    in_specs = [pl.BlockSpec((tr, r.shape[1]), lambda i: (i, 0)) for r in rows]
    for arr, which in halos:
        if which == "prev":
            in_specs.append(pl.BlockSpec((8, arr.shape[1]), lambda i: (jnp.maximum(i * (tr // 8) - 1, 0), 0)))
        else:
            in_specs.append(pl.BlockSpec((8, arr.shape[1]),
                                         lambda i: (jnp.minimum((i + 1) * (tr // 8), t // 8 - 1), 0)))
    for bc in bcasts:
        in_specs.append(pl.BlockSpec(bc.shape, lambda i, nd=bc.ndim: (0,) * nd))
    out_specs = [pl.BlockSpec((tr, c), lambda i: (i, 0)) for c, _ in out_rows]
    out_specs += [pl.BlockSpec(s, lambda i, nd=len(s): (0,) * nd) for s in out_accs]
    out_shape = [jax.ShapeDtypeStruct((t, c), dt) for c, dt in out_rows]
    out_shape += [jax.ShapeDtypeStruct(s, F32) for s in out_accs]
    return pl.pallas_call(
        body, name=name, grid=(nb,), in_specs=in_specs, out_specs=out_specs, out_shape=out_shape,
        compiler_params=_cparams(("arbitrary",)),
    )(*rows, *[h[0] for h in halos], *bcasts)


def _shift_prev(x, halo, i):
    rolled = pltpu.roll(x, 1, 0)
    first = jnp.where(i == 0, 0.0, halo[7:8, :])
    row = lax.broadcasted_iota(jnp.int32, x.shape, 0)
    return jnp.where(row == 0, first, rolled)


def _shift_next(x, halo, i, nb):
    rolled = pltpu.roll(x, x.shape[0] - 1, 0)
    last = jnp.where(i == nb - 1, 0.0, halo[0:1, :])
    row = lax.broadcasted_iota(jnp.int32, x.shape, 0)
    return jnp.where(row == x.shape[0] - 1, last, rolled)


def _ln(x, g, b, eps=LN_EPS):
    mu = jnp.mean(x, axis=-1, keepdims=True)
    xc = x - mu
    var = jnp.mean(xc * xc, axis=-1, keepdims=True)
    return xc * lax.rsqrt(var + eps) * g + b


def _embed_math(x, g, b, sc, sh):
    x0 = _ln(x, g, b)
    return x0, x0 * (1.0 + sc) + sh


def _post_math(xin, y, gate, g, b, sc, sh):
    x1 = _ln(ALPHA * xin + (1.0 + gate) * y, g, b)
    return x1, x1 * (1.0 + sc) + sh


def _loss_math(xin, h, tgt, gate, g, b):
    x2 = _ln(ALPHA * xin + (1.0 + gate) * h, g, b)
    err = x2 - tgt
    return 0.5 * jnp.sum(jnp.mean(err * err, axis=-1))


def _rwkv_pre_math(r, k, v, xw, xa, xg, w0, wd, a0, wa, wg, k_k, k_a, seg, seg_t, dot=_dot):
    wpre = -_softplus(-(w0 + dot(jnp.tanh(xw), wd, 1, 0, "x3"))) - 0.5
    lw = -jnp.exp(wpre)
    a = _sigmoid(a0 + dot(xa, wa, 1, 0, "x3"))
    g = dot(_sigmoid(xg), wg, 1, 0, "x3")
    kk = k * k_k
    norm = jnp.sqrt(dot(kk * kk, seg, 1, 0, "mb2"))
    kkn = kk * dot(1.0 / jnp.maximum(norm, 1e-12), seg_t, 1, 0, "mb2")
    k2 = k * (1.0 + (a - 1.0) * k_a)
    return r, lw, k2, v, kkn, kkn * a, g


def _rwkv_post_math(y, r, k2, v, g, lnx_g, lnx_b, r_k, seg, seg_t, dot=_dot):
    inv = 1.0 / HEAD
    spread = lambda z: dot(dot(z, seg, 1, 0, "mb2"), seg_t, 1, 0, "mb2")
    mu = spread(y) * inv
    yc = y - mu
    var = spread(yc * yc) * inv
    yn = yc * lax.rsqrt(var + LNX_EPS) * lnx_g + lnx_b
    bonus = spread(r * k2 * r_k) * v
    return (yn + bonus) * g


@jax.custom_vjp
def _known_inverse(low, inv):
    return inv


def _known_inverse_fwd(low, inv):
    return inv, inv


def _known_inverse_bwd(inv, g):
    left = [_raw_dot(t, gi, 0, 0, "bf16") for t, gi in zip(inv, g)]
    return [_raw_dot(x, t, 1, 1, "bf16") for x, t in zip(left, inv)], [jnp.zeros_like(t) for t in inv]


_known_inverse.defvjp(_known_inverse_fwd, _known_inverse_bwd)


def _chunk_math(s0, r, lw, k, v, kk, b, known_inv=None, dot=_dot):
    n = len(r)
    hs = range(n)
    c = r[0].shape[0]
    ti = lax.broadcasted_iota(jnp.int32, (2 * c, 2 * c), 0)
    tj = lax.broadcasted_iota(jnp.int32, (2 * c, 2 * c), 1)
    tt, jj = ti & (c - 1), tj & (c - 1)
    quad = jnp.where(ti < c, (tt > jj).astype(F32), (tt >= jj).astype(F32))
    incl = quad[c:, :c]
    eye = (ti[:c, :c] == tj[:c, :c]).astype(F32)
    cl = [dot(incl, lw[i], 1, 0, "mask") for i in hs]
    ge = [jnp.exp(cl[i]) for i in hs]
    gi = [jnp.exp(-cl[i]) for i in hs]
    ar = [jnp.concatenate([-kk[i] * jnp.exp(cl[i] - lw[i]), r[i] * ge[i]], axis=0) for i in hs]
    kb = [jnp.concatenate([k[i] * gi[i], b[i] * gi[i]], axis=0) for i in hs]
    m = [dot(ar[i], kb[i], 1, 1, "x3") * quad for i in hs]
    ars0 = [dot(ar[i], s0[i], 1, 1, "bf16") for i in hs]
    mv = [dot(m[i][:c, :c], v[i], 1, 0, "bf16") for i in hs]
    pw = [m[i][:c, c:] for i in hs]
    if known_inv is None:
        inv = [eye + pw[i] for i in hs]
        for _ in range(int(math.log2(c)) - 1):
            pw = [dot(pw[i], pw[i], 1, 0, "bf16") for i in hs]
            inv = [inv[i] + dot(inv[i], pw[i], 1, 0, "bf16") for i in hs]
    else:
        inv = _known_inverse(pw, known_inv)
    u = [dot(inv[i], ars0[i][:c] + mv[i], 1, 0, "bf16") for i in hs]
    vu = [jnp.concatenate([v[i], u[i]], axis=0) for i in hs]
    y = [ars0[i][c:] + dot(m[i][c:], vu[i], 1, 0, "bf16") for i in hs]
    s1 = [(s0[i] + dot(vu[i], kb[i], 0, 0, "x3")) * ge[i][c - 1:c, :] for i in hs]
    return y, s1, inv


def _attn_math(q, kp, kc, vp, vc, bias, sinks, first, dot=_dot):
    hq = q.shape[1] // HEAD
    hkv = kc.shape[1] // HEAD
    group = hq // hkv
    cols = group * ATT_BLOCK
    kj = lax.broadcasted_iota(jnp.int32, (2 * ATT_BLOCK, cols), 0)
    qi = lax.broadcasted_iota(jnp.int32, (2 * ATT_BLOCK, cols), 1) & (ATT_BLOCK - 1)
    dist = qi + ATT_BLOCK - kj
    valid = (dist >= 0) & (dist < ATT_BLOCK) & (jnp.logical_not(first) | (kj >= ATT_BLOCK))
    eye = (lax.broadcasted_iota(jnp.int32, (ATT_BLOCK, ATT_BLOCK), 0)
           == lax.broadcasted_iota(jnp.int32, (ATT_BLOCK, ATT_BLOCK), 1)).astype(F32)
    outs = []
    for j in range(hkv):
        heads = range(j * group, (j + 1) * group)
        kband = jnp.concatenate([kp[:, j * HEAD:(j + 1) * HEAD], kc[:, j * HEAD:(j + 1) * HEAD]], axis=0)
        vband = jnp.concatenate([vp[:, j * HEAD:(j + 1) * HEAD], vc[:, j * HEAD:(j + 1) * HEAD]], axis=0)
        qg = jnp.concatenate([q[:, h * HEAD:(h + 1) * HEAD] for h in heads], axis=0)
        bias_g = jnp.concatenate([bias[h] for h in heads], axis=1)
        sink = jnp.concatenate([jnp.broadcast_to(sinks[0:1, h:h + 1], (1, ATT_BLOCK)) for h in heads], axis=1)
        s = dot(kband, qg, 1, 1, "bf16") * (HEAD ** -0.5) + bias_g
        s = jnp.where(valid, s, -1e30)
        m = jnp.maximum(jnp.max(s, axis=0, keepdims=True), sink)
        e = jnp.exp(s - m)
        p = e / (jnp.sum(e, axis=0, keepdims=True) + jnp.exp(sink - m))
        o_t = dot(vband, p, 0, 0, "bf16")
        outs += [dot(eye, o_t[:, g * ATT_BLOCK:(g + 1) * ATT_BLOCK], 1, 1, "bf16") for g in range(group)]
    return jnp.concatenate(outs, axis=1)


def _rec_specs(t, da, gh, reverse):
    nc = t // CHUNK
    if reverse:
        return pl.BlockSpec((CHUNK, gh * HEAD), lambda hg, c: (nc - 1 - c, hg))
    return pl.BlockSpec((CHUNK, gh * HEAD), lambda hg, c: (c, hg))


def _rec_fwd(r, lw, k, v, kk, b):
    t, da = r.shape
    h = da // HEAD
    gh = min(REC_HEADS, h)
    nc = t // CHUNK

    def body(r_ref, lw_ref, k_ref, v_ref, kk_ref, b_ref, y_ref, s0_ref, inv_ref, state):
        @pl.when(pl.program_id(1) == 0)
        def _():
            state[...] = jnp.zeros_like(state)

        sls = [slice(i * HEAD, (i + 1) * HEAD) for i in range(gh)]
        heads = lambda ref: [ref[:, sl] for sl in sls]
        s0 = [state[i] for i in range(gh)]
        y, s1, inv = _chunk_math(s0, heads(r_ref), heads(lw_ref), heads(k_ref), heads(v_ref), heads(kk_ref),
                                 heads(b_ref), dot=_raw_dot)
        for i, sl in enumerate(sls):
            s0_ref[0, i] = s0[i]
            inv_ref[0, i] = inv[i]
            y_ref[:, sl] = y[i]
            state[i] = s1[i]

    spec = _rec_specs(t, da, gh, False)
    per_chunk = pl.BlockSpec((1, gh, HEAD, HEAD), lambda hg, c: (c, hg, 0, 0))
    return pl.pallas_call(
        body, name="rwkv_recurrence_fwd", grid=(h // gh, nc),
        in_specs=[spec] * 6,
        out_specs=[spec, per_chunk, per_chunk],
        out_shape=[jax.ShapeDtypeStruct((t, da), F32)] + [jax.ShapeDtypeStruct((nc, h, HEAD, HEAD), F32)] * 2,
        scratch_shapes=[pltpu.VMEM((gh, HEAD, HEAD), F32)],
        compiler_params=_cparams(("parallel", "arbitrary")),
    )(r, lw, k, v, kk, b)


def _rec_bwd(r, lw, k, v, kk, b, s0s, invs, dy):
    t, da = r.shape
    h = da // HEAD
    gh = min(REC_HEADS, h)
    nc = t // CHUNK

    def body(r_ref, lw_ref, k_ref, v_ref, kk_ref, b_ref, dy_ref, s0_ref, inv_ref,
             dr_ref, dlw_ref, dk_ref, dv_ref, dkk_ref, db_ref, dstate):
        @pl.when(pl.program_id(1) == 0)
        def _():
            dstate[...] = jnp.zeros_like(dstate)

        sls = [slice(i * HEAD, (i + 1) * HEAD) for i in range(gh)]
        heads = lambda ref: [ref[:, sl] for sl in sls]
        known = [inv_ref[0, i] for i in range(gh)]
        fn = lambda *args: _chunk_math(*args, known_inv=known)[:2]
        _, vjp = jax.vjp(fn, [s0_ref[0, i] for i in range(gh)], heads(r_ref), heads(lw_ref),
                         heads(k_ref), heads(v_ref), heads(kk_ref), heads(b_ref))
        grads = vjp((heads(dy_ref), [dstate[i] for i in range(gh)]))
        for i, sl in enumerate(sls):
            dstate[i] = grads[0][i]
            for ref, val in zip((dr_ref, dlw_ref, dk_ref, dv_ref, dkk_ref, db_ref), grads[1:]):
                ref[:, sl] = val[i]

    spec = _rec_specs(t, da, gh, True)
    return pl.pallas_call(
        body, name="rwkv_recurrence_bwd", grid=(h // gh, nc),
        in_specs=[spec] * 7 + [pl.BlockSpec((1, gh, HEAD, HEAD), lambda hg, c: (nc - 1 - c, hg, 0, 0))] * 2,
        out_specs=[spec] * 6,
        out_shape=[jax.ShapeDtypeStruct((t, da), F32)] * 6,
        scratch_shapes=[pltpu.VMEM((gh, HEAD, HEAD), F32)],
        compiler_params=_cparams(("parallel", "arbitrary")),
    )(r, lw, k, v, kk, b, dy, s0s, invs)


def _attn_specs(t, hq_w, hkv_w):
    nb = t // ATT_BLOCK
    cur = lambda w: pl.BlockSpec((ATT_BLOCK, w), lambda n: (n, 0))
    prev = lambda w: pl.BlockSpec((ATT_BLOCK, w), lambda n: (jnp.maximum(n - 1, 0), 0))
    return nb, cur, prev


def _attn_fwd(q, kb, vb, bias, sinks):
    t, qw = q.shape
    kw = kb.shape[1]
    nb, cur, prev = _attn_specs(t, qw, kw)

    def body(q_ref, kp_ref, kc_ref, vp_ref, vc_ref, bias_ref, sink_ref, o_ref):
        first = pl.program_id(0) == 0
        o = _attn_math(q_ref[...], kp_ref[...], kc_ref[...], vp_ref[...], vc_ref[...],
                       bias_ref[...], sink_ref[...], first, dot=_raw_dot)
        o_ref[...] = o.astype(o_ref.dtype)

    full = lambda a: pl.BlockSpec(a.shape, lambda n, nd=a.ndim: (0,) * nd)
    return pl.pallas_call(
        body, name="swa_attention_fwd", grid=(nb,),
        in_specs=[cur(qw), prev(kw), cur(kw), prev(kw), cur(kw), full(bias), full(sinks)],
        out_specs=cur(qw), out_shape=jax.ShapeDtypeStruct((t, qw), BF16),
        compiler_params=_cparams(("parallel",)),
    )(q, kb, kb, vb, vb, bias, sinks)


def _attn_bwd(q, kb, vb, bias, sinks, do, col_block):
    t, qw = q.shape
    kw = kb.shape[1]
    nb, cur, prev = _attn_specs(t, qw, kw)

    def body(q_ref, kp_ref, kc_ref, vp_ref, vc_ref, bias_ref, sink_ref, do_ref,
             dq_ref, dkp_ref, dkc_ref, dvp_ref, dvc_ref, dbias_ref, dsink_ref):
        n = pl.program_id(0)
        first = n == 0
        fn = functools.partial(_attn_math, first=first)
        _, vjp = jax.vjp(fn, q_ref[...], kp_ref[...], kc_ref[...], vp_ref[...], vc_ref[...],
                         bias_ref[...], sink_ref[...])
        dq, dkp, dkc, dvp, dvc, dbias, dsink = vjp(do_ref[...].astype(F32))
        dq_ref[...] = dq.astype(dq_ref.dtype)
        dkp_ref[...] = dkp
        dkc_ref[...] = dkc
        dvp_ref[...] = dvp
        dvc_ref[...] = dvc

        @pl.when(first)
        def _():
            dbias_ref[...] = jnp.zeros_like(dbias_ref)
            dsink_ref[...] = jnp.zeros_like(dsink_ref)

        dbias_ref[...] += dbias
        dsink_ref[...] += dsink

    full = lambda a: pl.BlockSpec(a.shape, lambda n, nd=a.ndim: (0,) * nd)
    kshape = jax.ShapeDtypeStruct((t, kw), F32)
    return pl.pallas_call(
        body, name="swa_attention_bwd", grid=(nb,),
        in_specs=[cur(qw), prev(kw), cur(kw), prev(kw), cur(kw), full(bias), full(sinks),
                  pl.BlockSpec((ATT_BLOCK, qw), lambda n: (n, col_block))],
        out_specs=[cur(qw), cur(kw), cur(kw), cur(kw), cur(kw), full(bias), full(sinks)],
        out_shape=[jax.ShapeDtypeStruct((t, qw), BF16), kshape, kshape, kshape, kshape,
                   jax.ShapeDtypeStruct(bias.shape, F32), jax.ShapeDtypeStruct(sinks.shape, F32)],
        compiler_params=_cparams(("arbitrary",)),
    )(q, kb, kb, vb, vb, bias, sinks, do)


def _bucket_onehot():
    qi = jnp.arange(ATT_BLOCK)[None, :]
    kj = jnp.arange(2 * ATT_BLOCK)[:, None]
    n = jnp.maximum(qi + ATT_BLOCK - kj, 0)
    buckets, max_exact = 32, 16
    nf = jnp.maximum(n, 1).astype(F32)
    large = max_exact + (jnp.log(nf / max_exact) / math.log(RPB_MAX_DIST / max_exact)
                         * (buckets - max_exact)).astype(jnp.int32)
    bucket = jnp.where(n < max_exact, n, jnp.minimum(large, buckets - 1)).reshape(-1)
    return (bucket[None, :] == jnp.arange(buckets)[:, None]).astype(F32)


def _small_dot(name, a, b, ca, cb):
    m = a.shape[1 - ca]
    n = b.shape[1 - cb]

    def body(a_ref, b_ref, o_ref):
        o_ref[...] = _raw_dot(a_ref[...], b_ref[...], ca, cb, "hi")

    return pl.pallas_call(body, name=name, out_shape=jax.ShapeDtypeStruct((m, n), F32),
                          compiler_params=_cparams())(a, b)


def _mod_fwd(c_all, w_mod):
    d, n = w_mod.shape
    tn = _tile(n, 512)

    def body(c_ref, w_ref, o_ref, cond_ref):
        cv = c_ref[...]
        cond = cv * _sigmoid(cv)
        cond_ref[...] = cond
        o_ref[...] = _raw_dot(cond, w_ref[...], 1, 0, "hi")

    return pl.pallas_call(
        body, name="adaln_mod_fwd", grid=(n // tn,),
        in_specs=[pl.BlockSpec(c_all.shape, lambda j: (0, 0)), pl.BlockSpec((d, tn), lambda j: (0, j))],
        out_specs=[pl.BlockSpec((c_all.shape[0], tn), lambda j: (0, j)),
                   pl.BlockSpec(c_all.shape, lambda j: (0, 0))],
        out_shape=[jax.ShapeDtypeStruct((c_all.shape[0], n), F32), jax.ShapeDtypeStruct(c_all.shape, F32)],
        compiler_params=_cparams(("arbitrary",)),
    )(c_all, w_mod)


def _adam_math(w, g, m, v):
    m = ADAM_B1 * m + (1.0 - ADAM_B1) * g
    v = ADAM_B2 * v + (1.0 - ADAM_B2) * (g * g)
    m_hat = m / (1.0 - ADAM_B1 ** ADAM_STEP)
    v_hat = v / (1.0 - ADAM_B2 ** ADAM_STEP)
    delta = -ADAM_LR * (m_hat / (jnp.sqrt(v_hat) + ADAM_EPS) + ADAM_WD * w)
    return delta, m, v


def _adamw(name, w, m, v, gparts):
    r, c = w.shape
    p = gparts.shape[0]
    tr = _rtile(r, max(8, (1 << 18) // max(c, 1) // 8 * 8))

    def body(w_ref, m_ref, v_ref, g_ref, go_ref, d_ref, mo_ref, vo_ref):
        g = g_ref[0].astype(F32)
        for s in range(1, p):
            g = g + g_ref[s].astype(F32)
        delta, mn, vn = _adam_math(w_ref[...], g, m_ref[...], v_ref[...])
        go_ref[...] = g
        d_ref[...] = delta
        mo_ref[...] = mn
        vo_ref[...] = vn

    spec = pl.BlockSpec((tr, c), lambda i: (i, 0))
    return pl.pallas_call(
        body, name=name, grid=(r // tr,),
        in_specs=[spec, spec, spec, pl.BlockSpec((p, tr, c), lambda i: (0, i, 0))],
        out_specs=[spec] * 4, out_shape=[jax.ShapeDtypeStruct((r, c), F32)] * 4,
        compiler_params=_cparams(("parallel",)),
    )(w, m, v, gparts)


def _adamw_outer(name, w, m, v, cond_t, dmod):
    d, n = w.shape
    tr, tn = _rtile(d, 512), _tile(n, 1024)

    def body(w_ref, m_ref, v_ref, c_ref, dm_ref, go_ref, d_ref, mo_ref, vo_ref):
        g = _raw_dot(c_ref[...], dm_ref[...], 1, 0, "hi")
        delta, mn, vn = _adam_math(w_ref[...], g, m_ref[...], v_ref[...])
        go_ref[...] = g
        d_ref[...] = delta
        mo_ref[...] = mn
        vo_ref[...] = vn

    spec = pl.BlockSpec((tr, tn), lambda i, j: (i, j))
    return pl.pallas_call(
        body, name=name, grid=(d // tr, n // tn),
        in_specs=[spec, spec, spec, pl.BlockSpec((tr, cond_t.shape[1]), lambda i, j: (i, 0)),
                  pl.BlockSpec((dmod.shape[0], tn), lambda i, j: (0, j))],
        out_specs=[spec] * 4, out_shape=[jax.ShapeDtypeStruct((d, n), F32)] * 4,
        compiler_params=_cparams(("parallel", "parallel")),
    )(w, m, v, cond_t, dmod)


def _all_gather(name, arrays):
    n = len(arrays)

    def body(*refs):
        ins, outs = refs[:n], refs[n:2 * n]
        send_sems, recv_sems, local_sems = refs[2 * n:]
        x, y, c = lax.axis_index("x"), lax.axis_index("y"), lax.axis_index("c")
        me, sibling = (x, y, c), (x, y, 1 - c)
        chips = [(1 - x, y), (x, 1 - y), (1 - x, 1 - y)]

        def copy(a, k, block, to, src=None):
            rows = outs[a].at[4 * block[0] + 2 * block[1] + block[2]]
            return pltpu.make_async_remote_copy(
                src_ref=rows if src is None else src, dst_ref=rows, send_sem=send_sems.at[a, k],
                recv_sem=recv_sems.at[a, k], device_id=to, device_id_type=pl.DeviceIdType.MESH)

        mine = [pltpu.make_async_copy(ins[a], outs[a].at[4 * x + 2 * y + c], local_sems.at[a]) for a in range(n)]
        for cp in mine:
            cp.start()
        first = []
        for a in range(n):
            first.append(copy(a, 0, me, sibling, src=ins[a]))
            first += [copy(a, 1 + j, me, (*chip, c), src=ins[a]) for j, chip in enumerate(chips)]
        for cp in first:
            cp.start()
        passed = []
        for a in range(n):
            for j, chip in enumerate(chips):
                copy(a, 1 + j, (*chip, c), me).wait_recv()
                passed.append(copy(a, 4 + j, (*chip, c), sibling))
                passed[-1].start()
        for a in range(n):
            copy(a, 0, sibling, me).wait_recv()
            for j, chip in enumerate(chips):
                copy(a, 4 + j, (*chip, 1 - c), me).wait_recv()
        for cp in first + passed:
            cp.wait_send()
        for cp in mine:
            cp.wait()

    any_spec = pl.BlockSpec(memory_space=pl.ANY)
    return pl.pallas_call(
        body, name=name, in_specs=[any_spec] * n, out_specs=[any_spec] * n,
        out_shape=[jax.ShapeDtypeStruct((N_DEV,) + a.shape, a.dtype) for a in arrays],
        scratch_shapes=[pltpu.SemaphoreType.DMA((n, N_DEV - 1)), pltpu.SemaphoreType.DMA((n, N_DEV - 1)),
                        pltpu.SemaphoreType.DMA((n,))],
    )(*arrays)


def _peer(p):
    x, y, c = lax.axis_index("x"), lax.axis_index("y"), lax.axis_index("c")
    px, py, pc = x ^ ((p >> 2) & 1), y ^ ((p >> 1) & 1), c ^ (p & 1)
    return (px, py, pc), 4 * px + 2 * py + pc


def _block(ref, d, cols, rows=None):
    if cols:
        r = slice(None) if rows is None else pl.ds(pl.multiple_of(rows[0], 8), rows[1])
        return ref.at[r, pl.ds(pl.multiple_of(d * cols, cols), cols)]
    return ref.at[d] if rows is None else ref.at[d, pl.ds(pl.multiple_of(rows[0], 8), rows[1])]


def _split_copy(src_ref, land_ref, send_sems, recv_sems, p, scatter, arriving, cols=None, halves=False):
    x, y, c = lax.axis_index("x"), lax.axis_index("y"), lax.axis_index("c")
    me = 4 * x + 2 * y + c
    dev, idx = _peer(p)
    if scatter == "chips":
        src, dst = src_ref.at[idx >> 1], land_ref.at[(idx if arriving else me) >> 1]
    elif scatter:
        src, dst = _block(src_ref, idx, cols), land_ref.at[idx if arriving else me]
    elif halves and p >= 2:
        half = src_ref.shape[0] // 2
        rows = ((c if arriving else dev[2]) * half, half)
        src = src_ref.at[pl.ds(pl.multiple_of(rows[0], 8), half)]
        dst = _block(land_ref, idx if arriving else me, cols, rows)
    else:
        src, dst = src_ref, _block(land_ref, idx if arriving else me, cols)
    return pltpu.make_async_remote_copy(
        src_ref=src, dst_ref=dst, send_sem=send_sems.at[p - 1], recv_sem=recv_sems.at[p - 1], device_id=dev,
        device_id_type=pl.DeviceIdType.MESH)


_HBM_SPEC = pl.BlockSpec(memory_space=pltpu.HBM)
_SEM_SPEC = pl.BlockSpec(memory_space=pltpu.SEMAPHORE)
_DATAFLOW = pltpu.SideEffectType.DATAFLOW_SIDE_EFFECTING


def _own_copy(src_ref, land_ref, send_sems, scatter, cols):
    me = 4 * lax.axis_index("x") + 2 * lax.axis_index("y") + lax.axis_index("c")
    if scatter == "chips":
        src, dst = src_ref.at[me >> 1], land_ref.at[me >> 1]
    elif scatter:
        src, dst = _block(src_ref, me, cols), land_ref.at[me]
    else:
        src, dst = src_ref, _block(land_ref, me, cols)
    return pltpu.make_async_copy(src, dst, send_sems.at[N_DEV - 1])


def _peers(scatter):
    return (2, 4, 6) if scatter == "chips" else tuple(range(1, N_DEV))


def _exchange_start(name, srcs, lands, scatter, after, cols=None, halves=False):
    n = len(srcs)
    cols = cols or [None] * n

    def body(*refs):
        src_refs, land_refs = refs[:n], refs[n:2 * n]
        outs = refs[2 * n + 1:]
        send, recv, token = outs[:n], outs[n:2 * n], outs[-1]
        for a in range(n):
            for p in _peers(scatter):
                _split_copy(src_refs[a], land_refs[a], send[a], recv[a], p, scatter, False, cols[a],
                            halves).start()
            _own_copy(src_refs[a], land_refs[a], send[a], scatter, cols[a]).start()
        token[...] = jnp.zeros_like(token)

    sems = [pltpu.SemaphoreType.DMA((N_DEV,))] * n + [pltpu.SemaphoreType.DMA((N_DEV - 1,))] * n
    hbm = [pltpu.HBM(a.shape, a.dtype) for a in list(srcs) + list(lands)]
    res = pl.pallas_call(
        body, name=name,
        out_shape=sems + hbm + [jax.ShapeDtypeStruct((8, 128), F32)],
        in_specs=[_HBM_SPEC] * (2 * n) + [pl.BlockSpec(memory_space=pl.ANY)],
        out_specs=[_SEM_SPEC] * (2 * n) + [_HBM_SPEC] * (2 * n) + [pl.BlockSpec(memory_space=pltpu.VMEM)],
        input_output_aliases={i: 2 * n + i for i in range(2 * n)},
        compiler_params=pltpu.CompilerParams(has_side_effects=_DATAFLOW),
    )(*[pltpu.with_memory_space_constraint(a, pltpu.HBM) for a in list(srcs) + list(lands)], after)
    return res[:n], res[n:2 * n], res[2 * n:3 * n], res[3 * n:4 * n], res[-1]


def _exchange_wait(name, send_sem, recv_sem, src, land, scatter, after, cols=None, halves=False):
    def body(src_ref, land_ref, send, recv, after_ref, src_out, land_out):
        for p in _peers(scatter):
            cp = _split_copy(src_ref, land_ref, send, recv, p, scatter, True, cols, halves)
            cp.wait_send()
            cp.wait_recv()
        _own_copy(src_ref, land_ref, send, scatter, cols).wait()

    return pl.pallas_call(
        body, name=name,
        out_shape=(pltpu.HBM(src.shape, src.dtype), pltpu.HBM(land.shape, land.dtype)),
        in_specs=[_HBM_SPEC, _HBM_SPEC, _SEM_SPEC, _SEM_SPEC, pl.BlockSpec(memory_space=pl.ANY)],
        out_specs=(_HBM_SPEC, _HBM_SPEC), input_output_aliases={0: 0, 1: 1},
        compiler_params=pltpu.CompilerParams(has_side_effects=_DATAFLOW),
    )(src, land, send_sem, recv_sem, after)[1]


def _sibling_fill(name, land, cols):
    rows = land.shape[0] if cols else land.shape[1]
    half = rows // 2

    def body(in_ref, out_ref, send_sems, recv_sems):
        x, y, c = lax.axis_index("x"), lax.axis_index("y"), lax.axis_index("c")

        def copy(p, core):
            _, idx = _peer(p)
            return pltpu.make_async_remote_copy(
                src_ref=_block(in_ref, idx, cols, (core * half, half)),
                dst_ref=_block(out_ref, idx, cols, (core * half, half)),
                send_sem=send_sems.at[idx], recv_sem=recv_sems.at[idx], device_id=(x, y, 1 - c),
                device_id_type=pl.DeviceIdType.MESH)

        sends = [copy(p, c) for p in range(2, N_DEV)]
        for cp in sends:
            cp.start()
        for p in range(2, N_DEV):
            copy(p, 1 - c).wait_recv()
        for cp in sends:
            cp.wait_send()

    any_spec = pl.BlockSpec(memory_space=pl.ANY)
    return pl.pallas_call(
        body, name=name, in_specs=[any_spec], out_specs=any_spec,
        out_shape=jax.ShapeDtypeStruct(land.shape, land.dtype), input_output_aliases={0: 0},
        scratch_shapes=[pltpu.SemaphoreType.DMA((N_DEV,)), pltpu.SemaphoreType.DMA((N_DEV,))],
    )(land)


def _pair_exchange(name, arrays):
    n = len(arrays)

    def body(*refs):
        ins, outs = refs[:n], refs[n:2 * n]
        send_sems, recv_sems = refs[2 * n:]
        x, y, c = lax.axis_index("x"), lax.axis_index("y"), lax.axis_index("c")
        copies = [pltpu.make_async_remote_copy(
            src_ref=ins[a].at[2 * k + (1 - c)], dst_ref=outs[a].at[k], send_sem=send_sems.at[a, k],
            recv_sem=recv_sems.at[a, k], device_id=(x, y, 1 - c), device_id_type=pl.DeviceIdType.MESH)
            for a in range(n) for k in range(N_DEV // 2)]
        for cp in copies:
            cp.start()
        for cp in copies:
            cp.wait()

    any_spec = pl.BlockSpec(memory_space=pl.ANY)
    return pl.pallas_call(
        body, name=name, in_specs=[any_spec] * n, out_specs=[any_spec] * n,
        out_shape=[jax.ShapeDtypeStruct((N_DEV // 2,) + a.shape[1:], a.dtype) for a in arrays],
        scratch_shapes=[pltpu.SemaphoreType.DMA((n, N_DEV // 2)), pltpu.SemaphoreType.DMA((n, N_DEV // 2))],
    )(*arrays)


def _pair_sum(name, mine, theirs):
    _, r, c_ = mine.shape
    tr = _rtile(r, max(8, (1 << 19) // c_ // 8 * 8))

    def body(mine_ref, theirs_ref, o_ref):
        core = lax.axis_index("c")
        o_ref[0] = (mine_ref[0, core].astype(F32) + theirs_ref[0].astype(F32)).astype(o_ref.dtype)

    return pl.pallas_call(
        body, name=name, grid=(N_DEV // 2, r // tr),
        in_specs=[pl.BlockSpec((1, 2, tr, c_), lambda k, i: (k, 0, i, 0)),
                  pl.BlockSpec((1, tr, c_), lambda k, i: (k, i, 0))],
        out_specs=pl.BlockSpec((1, tr, c_), lambda k, i: (k, i, 0)),
        out_shape=jax.ShapeDtypeStruct((N_DEV // 2, r, c_), mine.dtype),
        compiler_params=_cparams(("parallel", "parallel")),
    )(mine.reshape(N_DEV // 2, 2, r, c_), theirs)


def _cols_to_shards(a):
    r, c = a.shape
    return a.reshape(r, N_DEV, c // N_DEV).transpose(1, 0, 2)


def _shards_to_cols(a):
    d, r, n = a.shape
    return a.transpose(1, 0, 2).reshape(r, d * n)


def kernel(x, c, ln_emb_g, ln_emb_b, rpb_table, w_mod, b_mod, w_in, mu_shift, w0, w_decay_up, a0, w_iclr_up, w_gate_up, k_k, k_a, r_k, lnx_g, lnx_b, attn_sinks, w_out, ln1_g, ln1_b, w_up, w_down, ln2_g, ln2_b, loss_target, m_ln_emb_g, m_ln_emb_b, m_rpb_table, m_w_mod, m_b_mod, m_w_in, m_mu_shift, m_w0, m_w_decay_up, m_a0, m_w_iclr_up, m_w_gate_up, m_k_k, m_k_a, m_r_k, m_lnx_g, m_lnx_b, m_attn_sinks, m_w_out, m_ln1_g, m_ln1_b, m_w_up, m_w_down, m_ln2_g, m_ln2_b, v_ln_emb_g, v_ln_emb_b, v_rpb_table, v_w_mod, v_b_mod, v_w_in, v_mu_shift, v_w0, v_w_decay_up, v_a0, v_w_iclr_up, v_w_gate_up, v_k_k, v_k_a, v_r_k, v_lnx_g, v_lnx_b, v_attn_sinks, v_w_out, v_ln1_g, v_ln1_b, v_w_up, v_w_down, v_ln2_g, v_ln2_b):
    names = ["ln_emb_g", "ln_emb_b", "rpb_table", "w_mod", "b_mod", "w_in", "mu_shift", "w0", "w_decay_up",
             "a0", "w_iclr_up", "w_gate_up", "k_k", "k_a", "r_k", "lnx_g", "lnx_b", "attn_sinks", "w_out",
             "ln1_g", "ln1_b", "w_up", "w_down", "ln2_g", "ln2_b"]
    env = dict(locals())
    weights = {nm: env[nm] for nm in names}
    mom_m = {nm: env["m_" + nm] for nm in names}
    mom_v = {nm: env["v_" + nm] for nm in names}

    t, d = x.shape[1], x.shape[2]
    da = d // 2
    h_a = da // HEAD
    hq = (d - da) // HEAD
    hkv = hq // GQA_RATIO
    l_w, l_a, l_g = w_decay_up.shape[1], w_iclr_up.shape[1], w_gate_up.shape[1]
    o_w, o_a, o_g = 3 * da, 3 * da + l_w, 3 * da + l_w + l_a
    n_rwkv = o_g + l_g
    o_kb, o_vb = n_rwkv + hq * HEAD, n_rwkv + hq * HEAD + hkv * HEAD
    me = 4 * lax.axis_index("x") + 2 * lax.axis_index("y") + lax.axis_index("c")

    x2d, tgt = x[0], loss_target[0]
    row = lambda a: a.reshape(1, -1)
    seg = (jnp.arange(da)[:, None] // HEAD == jnp.arange(h_a)[None, :]).astype(F32)
    seg_t = seg.T

    (c_all,) = _all_gather("gather_cond", [c])
    c_all = c_all.reshape(N_DEV, d)
    mod_rows, cond_all = _mod_fwd(c_all, w_mod[0])
    gathered = _all_gather("gather_weights", [
        mod_rows, w_in[0].astype(BF16), w_decay_up[0], w_iclr_up[0], w_gate_up[0]])
    mod_all, win_g, wd_g, wa_g, wg_g = gathered
    late = [w_out[0].astype(BF16), w_up[0].astype(BF16), w_down[0].astype(BF16)]
    n_up = w_up.shape[2]
    late_cols = [None, n_up, None]
    late_lands = [lax.empty((N_DEV,) + late[0].shape, BF16), lax.empty((d, N_DEV * n_up), BF16),
                  lax.empty((N_DEV,) + late[2].shape, BF16)]
    late_send, late_recv, late_src, late_land, late_token = _exchange_start(
        "gather_late_weights_start", late, late_lands, False, mod_all, late_cols, halves=True)
    mod = lax.dynamic_index_in_dim(mod_all, me, axis=1, keepdims=False).reshape(1, -1) + b_mod
    mod = mod + late_token[0, 0]
    sh1, sc1, g1, sh2, sc2, g2 = [mod[:, i * d:(i + 1) * d] for i in range(6)]
    w_in_f = _shards_to_cols(win_g)
    wd_f, wa_f, wg_f = _shards_to_cols(wd_g), _shards_to_cols(wa_g), _shards_to_cols(wg_g)

    def late_weight(i, nm, after):
        land = _exchange_wait("gather_" + nm + "_wait", late_send[i], late_recv[i], late_src[i], late_land[i],
                              False, after, late_cols[i], halves=True)
        return _sibling_fill("gather_" + nm + "_fill", land, late_cols[i])

    tr = _rtile(t, 256)
    lng, lnb = row(ln_emb_g), row(ln_emb_b)

    def embed_fn(i, nb, xb, g, b, sc, sh):
        return _embed_math(xb, g, b, sc, sh), ()
    x0, u1 = _rowwise("embed_ln_mod", embed_fn, [x2d], [lng, lnb, sc1, sh1], [(d, F32), (d, BF16)], [], tr)

    (p_rkv,) = _matmul("in_proj_rkv", u1, w_in_f[:, :o_w], "nn", [F32])
    (p_lora,) = _matmul("in_proj_lora", u1, w_in_f[:, o_w:n_rwkv], "nn", [F32])
    (q,) = _matmul("in_proj_q", u1, w_in_f[:, n_rwkv:o_kb], "nn", [F32])
    (p_kv,) = _matmul("in_proj_kv", u1, w_in_f[:, o_kb:], "nn", [F32])
    p_w, p_a, p_g = p_lora[:, :l_w], p_lora[:, l_w:l_w + l_a], p_lora[:, l_w + l_a:]
    kb, vb = p_kv[:, :hkv * HEAD], p_kv[:, hkv * HEAD:]
    mu_rkv, mu_w, mu_a, mu_g = (mu_shift[:, :o_w], mu_shift[:, o_w:o_a], mu_shift[:, o_a:o_g],
                                mu_shift[:, o_g:n_rwkv])
    pre_params = [w0, wd_f, a0, wa_f, wg_f, k_k, k_a, seg, seg_t]
    tr_pre = _rtile(t, 128)

    def shifted(i, blocks, halos, mus):
        return [xb + (_shift_prev(xb, hb, i) - xb) * mb for xb, hb, mb in zip(blocks, halos, mus)]

    def split3(a):
        return a[:, :da], a[:, da:2 * da], a[:, 2 * da:]

    def pre_fn(i, nb, b_rkv, b_w, b_a, b_g, h_rkv, h_w, h_a_, h_g, m_rkv, m_w, m_a, m_g, *params):
        s_rkv, s_w, s_a, s_g = shifted(i, [b_rkv, b_w, b_a, b_g], [h_rkv, h_w, h_a_, h_g],
                                       [m_rkv, m_w, m_a, m_g])
        return _rwkv_pre_math(*split3(s_rkv), s_w, s_a, s_g, *params, dot=_raw_dot), ()

    pre_rows = [p_rkv, p_w, p_a, p_g]
    pre_halos = [(a, "prev") for a in pre_rows]
    r_, lw_, k2_, v_, kk_, b_, gate_ = _rowwise(
        "rwkv_pre", pre_fn, pre_rows, [mu_rkv, mu_w, mu_a, mu_g] + pre_params,
        [(da, F32)] * 7, [], tr_pre, halos=pre_halos)

    y_rec, s0s, invs = _rec_fwd(r_, lw_, k2_, v_, kk_, b_)

    rk_flat = r_k.reshape(1, da)
    post_params = [lnx_g, lnx_b, rk_flat, seg, seg_t]

    def post_fn(i, nb, yb, rb, kb_, vb_, gb, *params):
        return (_rwkv_post_math(yb, rb, kb_, vb_, gb, *params, dot=_raw_dot),), ()
    (ya,) = _rowwise("rwkv_post", post_fn, [y_rec, r_, k2_, v_, gate_], post_params, [(da, BF16)], [], tr_pre)

    onehot = _bucket_onehot()
    bias = _small_dot("rpb_gather", rpb_table, onehot, 0, 0)
    bias = bias.reshape(hq, 2 * ATT_BLOCK, ATT_BLOCK)
    yb = _attn_fwd(q, kb, vb, bias, attn_sinks)

    mix_in = jnp.concatenate([ya, yb], axis=1)
    w_out_f = late_weight(0, "w_out", mix_in).reshape(d, d)
    (mix,) = _matmul("out_proj", mix_in, w_out_f, "nn", [F32])

    def post1_fn(i, nb, xin, yv, gate, g, b, sc, sh):
        return _post_math(xin, yv, gate, g, b, sc, sh), ()
    x1, u2 = _rowwise("ln1_mod", post1_fn, [x0, mix], [g1, ln1_g, ln1_b, sc2, sh2],
                      [(d, F32), (d, BF16)], [], tr)

    def relu2(acc):
        rl = jnp.maximum(acc, 0.0)
        return acc, rl * rl
    w_up_f = late_weight(1, "w_up", u2)
    hpre, hact = _matmul("mlp_up", u2, w_up_f, "nn", [F32, BF16], epilogue=relu2)
    w_down_f = late_weight(2, "w_down", hact).reshape(-1, d)
    (hmlp,) = _matmul("mlp_down", hact, w_down_f, "nn", [F32])

    def loss_fn(i, nb, xin, hv, tg, gate, g, b):
        val, vjp = jax.vjp(_loss_math, xin, hv, tg, gate, g, b)
        dxin, dh, _, dgate, dg, db = vjp(jnp.ones((), F32))
        return (dxin, dh), (val, dgate, dg, db)
    dx1, dh, loss_acc, dg2, dln2g, dln2b = _rowwise(
        "ln2_loss", loss_fn, [x1, hmlp, tgt], [g2, ln2_g, ln2_b], [(d, F32), (d, BF16)],
        [(1, 1), (1, d), (1, d), (1, d)], _rtile(t, 128))

    def drelu2(acc, hp):
        return (acc * 2.0 * jnp.maximum(hp, 0.0),)
    (dhpre,) = _matmul("mlp_down_dgrad", dh, w_down_f, "nt", [BF16], epilogue=drelu2, extras=[hpre])
    (gw_down,) = _matmul("mlp_down_wgrad", hact, dh, "tn", [BF16])
    (du2,) = _matmul("mlp_up_dgrad", dhpre, w_up_f, "nt", [F32])
    (gw_up,) = _matmul("mlp_up_wgrad", u2, dhpre, "tn", [BF16])

    def landing(src):
        return lax.empty(src.shape, src.dtype)
    mlp_src = [gw_down.reshape(N_DEV, -1, d), gw_up]
    mlp_cols = [None, n_up]
    mlp_send, mlp_recv, mlp_src, mlp_land, mlp_token = _exchange_start(
        "scatter_mlp_grads_start", mlp_src, [landing(mlp_src[0]), lax.empty((N_DEV, d, n_up), BF16)], True,
        gw_up, mlp_cols)

    def post1_bwd(i, nb, xin, yv, dx1v, du2v, gate, g, b, sc, sh):
        _, vjp = jax.vjp(_post_math, xin, yv, gate, g, b, sc, sh)
        dxin, dy, dgate, dg, db, dsc, dsh = vjp((dx1v, du2v))
        return (dxin, dy), (dgate, dg, db, dsc, dsh)
    dx0, dmix, dg1, dln1g, dln1b, dsc2, dsh2 = _rowwise(
        "ln1_mod_bwd", post1_bwd, [x0, mix, dx1, du2], [g1 + mlp_token[0, 0], ln1_g, ln1_b, sc2, sh2],
        [(d, F32), (d, BF16)], [(1, d)] * 5, _rtile(t, 128))

    (dmix_in,) = _matmul("out_proj_dgrad", dmix, w_out_f, "nt", [F32])
    (gw_out,) = _matmul("out_proj_wgrad", mix_in, dmix, "tn", [BF16])
    dya = dmix_in[:, :da]
    out_src = [gw_out.reshape(N_DEV, d // N_DEV, d)]
    out_send, out_recv, out_src, out_land, out_token = _exchange_start(
        "scatter_out_grad_start", out_src, [landing(a) for a in out_src], True, gw_out)
    post_params_bwd = [lnx_g, lnx_b, rk_flat + out_token[0, 0], seg, seg_t]

    def post_bwd(i, nb, yb_, rb, kb_, vb_, gb, dyab, *params):
        _, vjp = jax.vjp(_rwkv_post_math, yb_, rb, kb_, vb_, gb, *params)
        dy, dr, dk, dv, dg, dlg, dlb, drk, _, _ = vjp(dyab)
        return (dy, dr, dk, dv, dg), (dlg, dlb, drk)
    dy_rec, dr_e, dk_e, dv_e, dgate, dlnxg, dlnxb, drk = _rowwise(
        "rwkv_post_bwd", post_bwd, [y_rec, r_, k2_, v_, gate_, dya], post_params_bwd,
        [(da, F32)] * 5, [(1, da)] * 3, tr_pre)

    dr_r, dlw_r, dk_r, dv_r, dkk_r, db_r = _rec_bwd(r_, lw_, k2_, v_, kk_, b_, s0s, invs, dy_rec)

    def pre_bwd(i, nb, b_rkv, b_w, b_a, b_g, dr1, dr2, dlw, dk1, dk2, dv1, dv2, dkk, dbb, dgt,
                h_rkv, h_w, h_a_, h_g, m_rkv, m_w, m_a, m_g, *params):
        blocks = [b_rkv, b_w, b_a, b_g]
        prevs = [_shift_prev(xb, hb, i) for xb, hb in zip(blocks, [h_rkv, h_w, h_a_, h_g])]
        mus = [m_rkv, m_w, m_a, m_g]
        s_rkv, s_w, s_a, s_g = [xb + (pb - xb) * mb for xb, pb, mb in zip(blocks, prevs, mus)]
        _, vjp = jax.vjp(_rwkv_pre_math, *split3(s_rkv), s_w, s_a, s_g, *params)
        grads = vjp((dr1 + dr2, dlw, dk1 + dk2, dv1 + dv2, dkk, dbb, dgt))
        g_rkv = jnp.concatenate(grads[:3], axis=1)
        g_w, g_a, g_g = grads[3:6]
        dmu = [jnp.sum(gs * (pb - xb), axis=0, keepdims=True)
               for gs, pb, xb in zip([g_rkv, g_w, g_a, g_g], prevs, blocks)]
        dw0, dwd, da0, dwa, dwg, dkk_, dka = grads[6:13]
        return (g_rkv, g_w, g_a, g_g), (*dmu, dw0, dwd, da0, dwa, dwg, dkk_, dka)

    pre_out = _rowwise(
        "rwkv_pre_bwd", pre_bwd,
        pre_rows + [dr_r, dr_e, dlw_r, dk_r, dk_e, dv_r, dv_e, dkk_r, db_r, dgate],
        [mu_rkv, mu_w, mu_a, mu_g] + pre_params,
        [(o_w, F32), (l_w, F32), (l_a, F32), (l_g, F32)],
        [(1, o_w), (1, l_w), (1, l_a), (1, l_g), (1, da), (l_w, da), (1, da), (l_a, da), (l_g, da),
         (1, da), (1, da)],
        _rtile(t, 64), halos=pre_halos)
    gs_rkv, gs_w, gs_a, gs_g = pre_out[:4]
    dmu_parts = pre_out[4:8]
    dw0, dwd, da0, dwa, dwg, dk_k, dk_a = pre_out[8:]

    def unshift_fn(i, nb, a1, a2, a3, a4, n1, n2, n3, n4, m1, m2, m3, m4):
        outs = [gs * (1.0 - mb) + _shift_next(gs * mb, hb * mb, i, nb)
                for gs, hb, mb in zip([a1, a2, a3, a4], [n1, n2, n3, n4], [m1, m2, m3, m4])]
        return outs, ()
    gs_list = [gs_rkv, gs_w, gs_a, gs_g]
    dp_rkv, dp_w, dp_a, dp_g = _rowwise(
        "token_shift_bwd", unshift_fn, gs_list, [mu_rkv, mu_w, mu_a, mu_g],
        [(o_w, BF16), (l_w, BF16), (l_a, BF16), (l_g, BF16)], [], tr_pre,
        halos=[(a, "next") for a in gs_list])

    dq, dkp, dkc, dvp, dvc, dbias, dsinks = _attn_bwd(q, kb, vb, bias, attn_sinks, dmix_in, 1)
    zpad = jnp.zeros((ATT_BLOCK, kb.shape[1]), F32)
    dkb = (dkc + jnp.concatenate([dkp[ATT_BLOCK:], zpad], axis=0)).astype(BF16)
    dvb = (dvc + jnp.concatenate([dvp[ATT_BLOCK:], zpad], axis=0)).astype(BF16)
    d_rpb = _small_dot("rpb_scatter", onehot, dbias.reshape(hq, -1), 1, 1)

    dp = jnp.concatenate([dp_rkv, dp_w, dp_a, dp_g, dq, dkb, dvb], axis=1)
    (gw_in,) = _matmul("in_proj_wgrad", u1, dp, "tn", [BF16])
    in_names = ["w_in", "w_decay_up", "w_iclr_up", "w_gate_up"]
    in_parts = [_cols_to_shards(gw_in), _cols_to_shards(dwd), _cols_to_shards(dwa), _cols_to_shards(dwg)]
    in_sibling = _pair_exchange("pair_in_grads", in_parts)
    in_src = [_pair_sum("pair_sum_" + nm, a, b) for nm, a, b in zip(in_names, in_parts, in_sibling)]
    in_send, in_recv, in_src, in_land, in_token = _exchange_start(
        "scatter_in_grads_start", in_src, [landing(a) for a in in_src], "chips", in_src[0])
    (du1,) = _matmul("in_proj_dgrad", dp, w_in_f, "nt", [F32], after=in_token)

    def embed_bwd(i, nb, xb, dx0v, du1v, g, b, sc, sh):
        _, vjp = jax.vjp(_embed_math, xb, g, b, sc, sh)
        dxv, dg, db, dsc, dsh = vjp((dx0v, du1v))
        return (dxv,), (dg, db, dsc, dsh)
    grad_x, dlng, dlnb, dsc1, dsh1 = _rowwise(
        "embed_ln_mod_bwd", embed_bwd, [x2d, dx0, du1], [lng, lnb, sc1, sh1], [(d, F32)], [(1, d)] * 4,
        _rtile(t, 128))

    dmod = jnp.concatenate([dsh1, dsc1, dg1, dsh2, dsc2, dg2], axis=1)
    small = {"ln_emb_g": dlng, "ln_emb_b": dlnb, "rpb_table": d_rpb, "b_mod": dmod,
             "mu_shift": jnp.concatenate(dmu_parts, axis=1), "w0": dw0, "a0": da0, "k_k": dk_k, "k_a": dk_a,
             "r_k": drk, "lnx_g": dlnxg, "lnx_b": dlnxb, "attn_sinks": dsinks, "ln1_g": dln1g, "ln1_b": dln1b,
             "ln2_g": dln2g, "ln2_b": dln2b}
    small_names = list(small)
    packed = jnp.concatenate([small[nm].reshape(1, -1) for nm in small_names], axis=1)
    sm_send, sm_recv, sm_src, sm_land, sm_token = _exchange_start(
        "gather_small_grads_start", [packed], [lax.empty((N_DEV,) + packed.shape, F32)], False, grad_x)

    grads, deltas, new_m, new_v = {}, {}, {}, {}

    def put(nm, res):
        shape = weights[nm].shape
        grads[nm], deltas[nm], new_m[nm], new_v[nm] = [a.reshape(shape) for a in res]

    def big_update(nm, parts):
        put(nm, _adamw("adamw_" + nm, weights[nm][0], mom_m[nm][0], mom_v[nm][0], parts))

    behind = sm_token
    big_update("w_down", _exchange_wait("scatter_w_down_wait", mlp_send[0], mlp_recv[0], mlp_src[0], mlp_land[0],
                                        True, behind, mlp_cols[0]))
    big_update("w_up", _exchange_wait("scatter_w_up_wait", mlp_send[1], mlp_recv[1], mlp_src[1], mlp_land[1],
                                      True, behind, mlp_cols[1]))
    big_update("w_out", _exchange_wait("scatter_w_out_wait", out_send[0], out_recv[0], out_src[0], out_land[0],
                                       True, behind))

    packed_all = _exchange_wait("gather_small_grads_wait", sm_send[0], sm_recv[0], sm_src[0], sm_land[0], False,
                                deltas["w_out"]).reshape(N_DEV, -1)
    n_mod = w_mod.shape[2]
    dmod_cols = lax.dynamic_slice_in_dim(packed_all[:, _offset(small, small_names, "b_mod"):], me * n_mod, n_mod,
                                         axis=1)
    put("w_mod", _adamw_outer("adamw_w_mod", w_mod[0], m_w_mod[0], v_w_mod[0], cond_all.T, dmod_cols))

    off = 0
    for nm in small_names:
        size = small[nm].size
        wshape = weights[nm].shape
        two_d = (1, size) if nm != "rpb_table" else wshape
        parts = packed_all[:, off:off + size].reshape((N_DEV,) + two_d)
        off += size
        put(nm, _adamw("adamw_" + nm, weights[nm].reshape(two_d), mom_m[nm].reshape(two_d),
                       mom_v[nm].reshape(two_d), parts))

    behind = deltas[small_names[-1]]
    for i, nm in enumerate(in_names):
        big_update(nm, _exchange_wait("scatter_" + nm + "_wait", in_send[i], in_recv[i], in_src[i], in_land[i],
                                      "chips", behind))

    loss = lax.psum(loss_acc[0, 0], MESH_AXES)
    return (loss, grad_x[None], *[grads[nm] for nm in names], *[deltas[nm] for nm in names],
            *[new_m[nm] for nm in names], *[new_v[nm] for nm in names])


def _offset(small, small_names, name):
    off = 0
    for nm in small_names:
        if nm == name:
            return off
        off += small[nm].size
    raise KeyError(name)
```

```python
import functools
import math

import jax
import jax.numpy as jnp
from jax import lax
from jax.experimental import pallas as pl
from jax.experimental.pallas import tpu as pltpu

F32 = jnp.float32
BF16 = jnp.bfloat16
HI = lax.Precision.HIGHEST
MESH_AXES = ("x", "y", "c")
N_DEV = 8

HEAD = 64
GQA_RATIO = 8
ATT_BLOCK = 128
RPB_MAX_DIST = 128
LN_EPS = 1e-5
LNX_EPS = 64e-5
DEPTH = 1
ALPHA = (2.0 * DEPTH) ** 0.25
CHUNK = 64
REC_HEADS = 16

ADAM_LR = 0.001
ADAM_B1 = 0.9
ADAM_B2 = 0.999
ADAM_EPS = 1e-08
ADAM_WD = 0.01
ADAM_STEP = 10

VMEM_LIMIT = 60 * 1024 * 1024


def _cparams(sem=None):
    return pltpu.CompilerParams(dimension_semantics=sem, vmem_limit_bytes=VMEM_LIMIT)


def _tile(dim, cap):
    best = None
    t = 128
    while t <= min(dim, cap):
        if dim % t == 0:
            best = t
        t += 128
    return best or dim


def _rtile(dim, cap):
    best = None
    t = 8
    while t <= min(dim, cap):
        if dim % t == 0:
            best = t
        t += 8
    return best or dim


def _split2(a):
    hi = a.astype(BF16)
    return hi, (a - hi.astype(F32)).astype(BF16)


def _raw_dot(a, b, ca, cb, prec):
    dims = (((ca,), (cb,)), ((), ()))
    mm = lambda p, q: lax.dot_general(p, q, dims, preferred_element_type=F32)
    if prec == "bf16":
        return mm(a.astype(BF16), b.astype(BF16))
    if prec == "x3":
        (ah, al), (bh, bl) = _split2(a), _split2(b)
        return mm(ah, bh) + (mm(ah, bl) + mm(al, bh))
    if prec == "mask":
        ab = a.astype(BF16)
        b1, b2 = _split2(b)
        b3 = (b - b1.astype(F32) - b2.astype(F32)).astype(BF16)
        return mm(ab, b1) + (mm(ab, b2) + mm(ab, b3))
    if prec == "mb2":
        (ah, al), bb = _split2(a), b.astype(BF16)
        return mm(ah, bb) + mm(al, bb)
    return lax.dot_general(a, b, dims, precision=HI, preferred_element_type=F32)


@functools.partial(jax.custom_vjp, nondiff_argnums=(2, 3, 4))
def _bf16_dot(a, b, ca, cb, prec):
    return _raw_dot(a, b, ca, cb, prec)


def _bf16_dot_fwd(a, b, ca, cb, prec):
    return _raw_dot(a, b, ca, cb, prec), (a, b)


def _bf16_dot_bwd(ca, cb, prec, res, g):
    a, b = res
    if prec == "mask":
        return jnp.zeros_like(a), _bf16_dot(a, g, 1 - ca, 0, prec)
    if prec == "mb2":
        return _bf16_dot(g, b, 1, 1, prec), jnp.zeros_like(b)
    if ca == 1:
        da = _bf16_dot(g, b, 1, 1 - cb, prec)
    else:
        da = _bf16_dot(b, g, 1 - cb, 1, prec)
    if cb == 0:
        db = _bf16_dot(a, g, 1 - ca, 0, prec)
    else:
        db = _bf16_dot(g, a, 0, 1 - ca, prec)
    return da, db


_bf16_dot.defvjp(_bf16_dot_fwd, _bf16_dot_bwd)


def _dot(a, b, ca, cb, prec):
    return _raw_dot(a, b, ca, cb, prec) if prec == "hi" else _bf16_dot(a, b, ca, cb, prec)


def _sigmoid(z):
    return 1.0 / (1.0 + jnp.exp(-z))


def _softplus(z):
    return jnp.maximum(z, 0.0) + jnp.log(1.0 + jnp.exp(-jnp.abs(z)))


MATMUL_VMEM_BUDGET = 51 * 1024 * 1024


def _matmul_tiles(m, n, k, in_bytes, out_dtypes, n_extras):
    tm, tn = _tile(m, 1024), _tile(n, 1024)
    out_bytes = sum(jnp.dtype(dt).itemsize for dt in out_dtypes)
    for cap in (4096, 2048, 1024, 512, 256, 128):
        tk = _tile(k, cap)
        acc = 4 * tm * tn if tk < k else 0
        need = 2 * in_bytes * (tm + tn) * tk + 2 * tm * tn * (out_bytes + 4 * n_extras) + acc + 4 * tm * tn
        if need <= MATMUL_VMEM_BUDGET:
            break
    return tm, tn, tk


def _matmul(name, a, b, mode, out_dtypes, epilogue=None, extras=(), after=None):
    if mode == "nn":
        (m, k), n = a.shape, b.shape[1]
    elif mode == "nt":
        (m, k), n = a.shape, b.shape[0]
    else:
        (k, m), n = a.shape, b.shape[1]
    tm, tn, tk = _matmul_tiles(m, n, k, a.dtype.itemsize, out_dtypes, len(extras))
    nk = k // tk
    ne, no = len(extras), len(out_dtypes)
    ca, cb = {"nn": (1, 0), "nt": (1, 1), "tn": (0, 0)}[mode]

    n_after = 0 if after is None else 1

    def body(a_ref, b_ref, *rest):
        rest = rest[n_after:]
        extra_refs, out_refs = rest[:ne], rest[ne:ne + no]
        acc = rest[-1] if nk > 1 else None
        kk = pl.program_id(2)
        part = _raw_dot(a_ref[...], b_ref[...], ca, cb, "bf16")

        def finish(total):
            res = epilogue(total, *[e[...] for e in extra_refs]) if epilogue else (total,)
            for o, v in zip(out_refs, res):
                o[...] = v.astype(o.dtype)

        if nk == 1:
            finish(part)
            return

        @pl.when(kk == 0)
        def _():
            acc[...] = part

        @pl.when((kk > 0) & (kk < nk - 1))
        def _():
            acc[...] += part

        @pl.when(kk == nk - 1)
        def _():
            finish(acc[...] + part)

    a_spec = (pl.BlockSpec((tk, tm), lambda i, j, kk: (kk, i)) if mode == "tn"
              else pl.BlockSpec((tm, tk), lambda i, j, kk: (i, kk)))
    b_spec = (pl.BlockSpec((tn, tk), lambda i, j, kk: (j, kk)) if mode == "nt"
              else pl.BlockSpec((tk, tn), lambda i, j, kk: (kk, j)))
    mn_spec = pl.BlockSpec((tm, tn), lambda i, j, kk: (i, j))
    after_specs = [pl.BlockSpec(memory_space=pl.ANY)] * n_after
    outs = pl.pallas_call(
        body, name=name, grid=(m // tm, n // tn, nk),
        in_specs=[a_spec, b_spec] + after_specs + [mn_spec] * ne,
        out_specs=[mn_spec] * no,
        out_shape=[jax.ShapeDtypeStruct((m, n), dt) for dt in out_dtypes],
        scratch_shapes=[pltpu.VMEM((tm, tn), F32)] if nk > 1 else [],
        compiler_params=_cparams(("parallel", "parallel", "arbitrary")),
    )(a, b, *([after] * n_after), *extras)
    return outs


def _rowwise(name, fn, rows, bcasts, out_rows, out_accs, tr, halos=()):
    t = rows[0].shape[0]
    nb = t // tr
    n_in = len(rows) + len(halos) + len(bcasts)
    n_ro = len(out_rows)

    def body(*refs):
        ins = [r[...] for r in refs[:n_in]]
        o_refs = refs[n_in:]
        i = pl.program_id(0)
        routs, aouts = fn(i, nb, *ins)
        for ref, v in zip(o_refs[:n_ro], routs):
            ref[...] = v.astype(ref.dtype)
        for ref, v in zip(o_refs[n_ro:], aouts):
            @pl.when(i == 0)
            def _(ref=ref):
                ref[...] = jnp.zeros_like(ref)
            ref[...] += v.reshape(ref.shape)

    in_specs = [pl.BlockSpec((tr, r.shape[1]), lambda i: (i, 0)) for r in rows]
    for arr, which in halos:
        if which == "prev":
            in_specs.append(pl.BlockSpec((8, arr.shape[1]), lambda i: (jnp.maximum(i * (tr // 8) - 1, 0), 0)))
        else:
            in_specs.append(pl.BlockSpec((8, arr.shape[1]),
                                         lambda i: (jnp.minimum((i + 1) * (tr // 8), t // 8 - 1), 0)))
    for bc in bcasts:
        in_specs.append(pl.BlockSpec(bc.shape, lambda i, nd=bc.ndim: (0,) * nd))
    out_specs = [pl.BlockSpec((tr, c), lambda i: (i, 0)) for c, _ in out_rows]
    out_specs += [pl.BlockSpec(s, lambda i, nd=len(s): (0,) * nd) for s in out_accs]
    out_shape = [jax.ShapeDtypeStruct((t, c), dt) for c, dt in out_rows]
    out_shape += [jax.ShapeDtypeStruct(s, F32) for s in out_accs]
    return pl.pallas_call(
        body, name=name, grid=(nb,), in_specs=in_specs, out_specs=out_specs, out_shape=out_shape,
        compiler_params=_cparams(("arbitrary",)),
    )(*rows, *[h[0] for h in halos], *bcasts)


def _shift_prev(x, halo, i):
    rolled = pltpu.roll(x, 1, 0)
    first = jnp.where(i == 0, 0.0, halo[7:8, :])
    row = lax.broadcasted_iota(jnp.int32, x.shape, 0)
    return jnp.where(row == 0, first, rolled)


def _shift_next(x, halo, i, nb):
    rolled = pltpu.roll(x, x.shape[0] - 1, 0)
    last = jnp.where(i == nb - 1, 0.0, halo[0:1, :])
    row = lax.broadcasted_iota(jnp.int32, x.shape, 0)
    return jnp.where(row == x.shape[0] - 1, last, rolled)


def _ln(x, g, b, eps=LN_EPS):
    mu = jnp.mean(x, axis=-1, keepdims=True)
    xc = x - mu
    var = jnp.mean(xc * xc, axis=-1, keepdims=True)
    return xc * lax.rsqrt(var + eps) * g + b


def _embed_math(x, g, b, sc, sh):
    x0 = _ln(x, g, b)
    return x0, x0 * (1.0 + sc) + sh


def _post_math(xin, y, gate, g, b, sc, sh):
    x1 = _ln(ALPHA * xin + (1.0 + gate) * y, g, b)
    return x1, x1 * (1.0 + sc) + sh


def _loss_math(xin, h, tgt, gate, g, b):
    x2 = _ln(ALPHA * xin + (1.0 + gate) * h, g, b)
    err = x2 - tgt
    return 0.5 * jnp.sum(jnp.mean(err * err, axis=-1))


def _rwkv_pre_math(r, k, v, xw, xa, xg, w0, wd, a0, wa, wg, k_k, k_a, seg, seg_t, dot=_dot):
    wpre = -_softplus(-(w0 + dot(jnp.tanh(xw), wd, 1, 0, "x3"))) - 0.5
    lw = -jnp.exp(wpre)
    a = _sigmoid(a0 + dot(xa, wa, 1, 0, "x3"))
    g = dot(_sigmoid(xg), wg, 1, 0, "x3")
    kk = k * k_k
    norm = jnp.sqrt(dot(kk * kk, seg, 1, 0, "mb2"))
    kkn = kk * dot(1.0 / jnp.maximum(norm, 1e-12), seg_t, 1, 0, "mb2")
    k2 = k * (1.0 + (a - 1.0) * k_a)
    return r, lw, k2, v, kkn, kkn * a, g


def _rwkv_post_math(y, r, k2, v, g, lnx_g, lnx_b, r_k, seg, seg_t, dot=_dot):
    inv = 1.0 / HEAD
    spread = lambda z: dot(dot(z, seg, 1, 0, "mb2"), seg_t, 1, 0, "mb2")
    mu = spread(y) * inv
    yc = y - mu
    var = spread(yc * yc) * inv
    yn = yc * lax.rsqrt(var + LNX_EPS) * lnx_g + lnx_b
    bonus = spread(r * k2 * r_k) * v
    return (yn + bonus) * g


@jax.custom_vjp
def _known_inverse(low, inv):
    return inv


def _known_inverse_fwd(low, inv):
    return inv, inv


def _known_inverse_bwd(inv, g):
    left = [_raw_dot(t, gi, 0, 0, "bf16") for t, gi in zip(inv, g)]
    return [_raw_dot(x, t, 1, 1, "bf16") for x, t in zip(left, inv)], [jnp.zeros_like(t) for t in inv]


_known_inverse.defvjp(_known_inverse_fwd, _known_inverse_bwd)


def _chunk_math(s0, r, lw, k, v, kk, b, known_inv=None, dot=_dot):
    n = len(r)
    hs = range(n)
    c = r[0].shape[0]
    ti = lax.broadcasted_iota(jnp.int32, (2 * c, 2 * c), 0)
    tj = lax.broadcasted_iota(jnp.int32, (2 * c, 2 * c), 1)
    tt, jj = ti & (c - 1), tj & (c - 1)
    quad = jnp.where(ti < c, (tt > jj).astype(F32), (tt >= jj).astype(F32))
    incl = quad[c:, :c]
    eye = (ti[:c, :c] == tj[:c, :c]).astype(F32)
    cl = [dot(incl, lw[i], 1, 0, "mask") for i in hs]
    ge = [jnp.exp(cl[i]) for i in hs]
    gi = [jnp.exp(-cl[i]) for i in hs]
    ar = [jnp.concatenate([-kk[i] * jnp.exp(cl[i] - lw[i]), r[i] * ge[i]], axis=0) for i in hs]
    kb = [jnp.concatenate([k[i] * gi[i], b[i] * gi[i]], axis=0) for i in hs]
    m = [dot(ar[i], kb[i], 1, 1, "x3") * quad for i in hs]
    ars0 = [dot(ar[i], s0[i], 1, 1, "bf16") for i in hs]
    mv = [dot(m[i][:c, :c], v[i], 1, 0, "bf16") for i in hs]
    pw = [m[i][:c, c:] for i in hs]
    if known_inv is None:
        inv = [eye + pw[i] for i in hs]
        for _ in range(int(math.log2(c)) - 1):
            pw = [dot(pw[i], pw[i], 1, 0, "bf16") for i in hs]
            inv = [inv[i] + dot(inv[i], pw[i], 1, 0, "bf16") for i in hs]
    else:
        inv = _known_inverse(pw, known_inv)
    u = [dot(inv[i], ars0[i][:c] + mv[i], 1, 0, "bf16") for i in hs]
    vu = [jnp.concatenate([v[i], u[i]], axis=0) for i in hs]
    y = [ars0[i][c:] + dot(m[i][c:], vu[i], 1, 0, "bf16") for i in hs]
    s1 = [(s0[i] + dot(vu[i], kb[i], 0, 0, "x3")) * ge[i][c - 1:c, :] for i in hs]
    return y, s1, inv


def _attn_math(q, kp, kc, vp, vc, bias, sinks, first, dot=_dot):
    hq = q.shape[1] // HEAD
    hkv = kc.shape[1] // HEAD
    group = hq // hkv
    cols = group * ATT_BLOCK
    kj = lax.broadcasted_iota(jnp.int32, (2 * ATT_BLOCK, cols), 0)
    qi = lax.broadcasted_iota(jnp.int32, (2 * ATT_BLOCK, cols), 1) & (ATT_BLOCK - 1)
    dist = qi + ATT_BLOCK - kj
    valid = (dist >= 0) & (dist < ATT_BLOCK) & (jnp.logical_not(first) | (kj >= ATT_BLOCK))
    eye = (lax.broadcasted_iota(jnp.int32, (ATT_BLOCK, ATT_BLOCK), 0)
           == lax.broadcasted_iota(jnp.int32, (ATT_BLOCK, ATT_BLOCK), 1)).astype(F32)
    outs = []
    for j in range(hkv):
        heads = range(j * group, (j + 1) * group)
        kband = jnp.concatenate([kp[:, j * HEAD:(j + 1) * HEAD], kc[:, j * HEAD:(j + 1) * HEAD]], axis=0)
        vband = jnp.concatenate([vp[:, j * HEAD:(j + 1) * HEAD], vc[:, j * HEAD:(j + 1) * HEAD]], axis=0)
        qg = jnp.concatenate([q[:, h * HEAD:(h + 1) * HEAD] for h in heads], axis=0)
        bias_g = jnp.concatenate([bias[h] for h in heads], axis=1)
        sink = jnp.concatenate([jnp.broadcast_to(sinks[0:1, h:h + 1], (1, ATT_BLOCK)) for h in heads], axis=1)
        s = dot(kband, qg, 1, 1, "bf16") * (HEAD ** -0.5) + bias_g
        s = jnp.where(valid, s, -1e30)
        m = jnp.maximum(jnp.max(s, axis=0, keepdims=True), sink)
        e = jnp.exp(s - m)
        p = e / (jnp.sum(e, axis=0, keepdims=True) + jnp.exp(sink - m))
        o_t = dot(vband, p, 0, 0, "bf16")
        outs += [dot(eye, o_t[:, g * ATT_BLOCK:(g + 1) * ATT_BLOCK], 1, 1, "bf16") for g in range(group)]
    return jnp.concatenate(outs, axis=1)


def _rec_specs(t, da, gh, reverse):
    nc = t // CHUNK
    if reverse:
        return pl.BlockSpec((CHUNK, gh * HEAD), lambda hg, c: (nc - 1 - c, hg))
    return pl.BlockSpec((CHUNK, gh * HEAD), lambda hg, c: (c, hg))


def _rec_fwd(r, lw, k, v, kk, b):
    t, da = r.shape
    h = da // HEAD
    gh = min(REC_HEADS, h)
    nc = t // CHUNK

    def body(r_ref, lw_ref, k_ref, v_ref, kk_ref, b_ref, y_ref, s0_ref, inv_ref, state):
        @pl.when(pl.program_id(1) == 0)
        def _():
            state[...] = jnp.zeros_like(state)

        sls = [slice(i * HEAD, (i + 1) * HEAD) for i in range(gh)]
        heads = lambda ref: [ref[:, sl] for sl in sls]
        s0 = [state[i] for i in range(gh)]
        y, s1, inv = _chunk_math(s0, heads(r_ref), heads(lw_ref), heads(k_ref), heads(v_ref), heads(kk_ref),
                                 heads(b_ref), dot=_raw_dot)
        for i, sl in enumerate(sls):
            s0_ref[0, i] = s0[i]
            inv_ref[0, i] = inv[i]
            y_ref[:, sl] = y[i]
            state[i] = s1[i]

    spec = _rec_specs(t, da, gh, False)
    per_chunk = pl.BlockSpec((1, gh, HEAD, HEAD), lambda hg, c: (c, hg, 0, 0))
    return pl.pallas_call(
        body, name="rwkv_recurrence_fwd", grid=(h // gh, nc),
        in_specs=[spec] * 6,
        out_specs=[spec, per_chunk, per_chunk],
        out_shape=[jax.ShapeDtypeStruct((t, da), F32)] + [jax.ShapeDtypeStruct((nc, h, HEAD, HEAD), F32)] * 2,
        scratch_shapes=[pltpu.VMEM((gh, HEAD, HEAD), F32)],
        compiler_params=_cparams(("parallel", "arbitrary")),
    )(r, lw, k, v, kk, b)


def _rec_bwd(r, lw, k, v, kk, b, s0s, invs, dy):
    t, da = r.shape
    h = da // HEAD
    gh = min(REC_HEADS, h)
    nc = t // CHUNK

    def body(r_ref, lw_ref, k_ref, v_ref, kk_ref, b_ref, dy_ref, s0_ref, inv_ref,
             dr_ref, dlw_ref, dk_ref, dv_ref, dkk_ref, db_ref, dstate):
        @pl.when(pl.program_id(1) == 0)
        def _():
            dstate[...] = jnp.zeros_like(dstate)

        sls = [slice(i * HEAD, (i + 1) * HEAD) for i in range(gh)]
        heads = lambda ref: [ref[:, sl] for sl in sls]
        known = [inv_ref[0, i] for i in range(gh)]
        fn = lambda *args: _chunk_math(*args, known_inv=known)[:2]
        _, vjp = jax.vjp(fn, [s0_ref[0, i] for i in range(gh)], heads(r_ref), heads(lw_ref),
                         heads(k_ref), heads(v_ref), heads(kk_ref), heads(b_ref))
        grads = vjp((heads(dy_ref), [dstate[i] for i in range(gh)]))
        for i, sl in enumerate(sls):
            dstate[i] = grads[0][i]
            for ref, val in zip((dr_ref, dlw_ref, dk_ref, dv_ref, dkk_ref, db_ref), grads[1:]):
                ref[:, sl] = val[i]

    spec = _rec_specs(t, da, gh, True)
    return pl.pallas_call(
        body, name="rwkv_recurrence_bwd", grid=(h // gh, nc),
        in_specs=[spec] * 7 + [pl.BlockSpec((1, gh, HEAD, HEAD), lambda hg, c: (nc - 1 - c, hg, 0, 0))] * 2,
        out_specs=[spec] * 6,
        out_shape=[jax.ShapeDtypeStruct((t, da), F32)] * 6,
        scratch_shapes=[pltpu.VMEM((gh, HEAD, HEAD), F32)],
        compiler_params=_cparams(("parallel", "arbitrary")),
    )(r, lw, k, v, kk, b, dy, s0s, invs)


def _attn_specs(t, hq_w, hkv_w):
    nb = t // ATT_BLOCK
    cur = lambda w: pl.BlockSpec((ATT_BLOCK, w), lambda n: (n, 0))
    prev = lambda w: pl.BlockSpec((ATT_BLOCK, w), lambda n: (jnp.maximum(n - 1, 0), 0))
    return nb, cur, prev


def _attn_fwd(q, kb, vb, bias, sinks):
    t, qw = q.shape
    kw = kb.shape[1]
    nb, cur, prev = _attn_specs(t, qw, kw)

    def body(q_ref, kp_ref, kc_ref, vp_ref, vc_ref, bias_ref, sink_ref, o_ref):
        first = pl.program_id(0) == 0
        o = _attn_math(q_ref[...], kp_ref[...], kc_ref[...], vp_ref[...], vc_ref[...],
                       bias_ref[...], sink_ref[...], first, dot=_raw_dot)
        o_ref[...] = o.astype(o_ref.dtype)

    full = lambda a: pl.BlockSpec(a.shape, lambda n, nd=a.ndim: (0,) * nd)
    return pl.pallas_call(
        body, name="swa_attention_fwd", grid=(nb,),
        in_specs=[cur(qw), prev(kw), cur(kw), prev(kw), cur(kw), full(bias), full(sinks)],
        out_specs=cur(qw), out_shape=jax.ShapeDtypeStruct((t, qw), BF16),
        compiler_params=_cparams(("parallel",)),
    )(q, kb, kb, vb, vb, bias, sinks)


def _attn_bwd(q, kb, vb, bias, sinks, do, col_block):
    t, qw = q.shape
    kw = kb.shape[1]
    nb, cur, prev = _attn_specs(t, qw, kw)

    def body(q_ref, kp_ref, kc_ref, vp_ref, vc_ref, bias_ref, sink_ref, do_ref,
             dq_ref, dkp_ref, dkc_ref, dvp_ref, dvc_ref, dbias_ref, dsink_ref):
        n = pl.program_id(0)
        first = n == 0
        fn = functools.partial(_attn_math, first=first)
        _, vjp = jax.vjp(fn, q_ref[...], kp_ref[...], kc_ref[...], vp_ref[...], vc_ref[...],
                         bias_ref[...], sink_ref[...])
        dq, dkp, dkc, dvp, dvc, dbias, dsink = vjp(do_ref[...].astype(F32))
        dq_ref[...] = dq.astype(dq_ref.dtype)
        dkp_ref[...] = dkp
        dkc_ref[...] = dkc
        dvp_ref[...] = dvp
        dvc_ref[...] = dvc

        @pl.when(first)
        def _():
            dbias_ref[...] = jnp.zeros_like(dbias_ref)
            dsink_ref[...] = jnp.zeros_like(dsink_ref)

        dbias_ref[...] += dbias
        dsink_ref[...] += dsink

    full = lambda a: pl.BlockSpec(a.shape, lambda n, nd=a.ndim: (0,) * nd)
    kshape = jax.ShapeDtypeStruct((t, kw), F32)
    return pl.pallas_call(
        body, name="swa_attention_bwd", grid=(nb,),
        in_specs=[cur(qw), prev(kw), cur(kw), prev(kw), cur(kw), full(bias), full(sinks),
                  pl.BlockSpec((ATT_BLOCK, qw), lambda n: (n, col_block))],
        out_specs=[cur(qw), cur(kw), cur(kw), cur(kw), cur(kw), full(bias), full(sinks)],
        out_shape=[jax.ShapeDtypeStruct((t, qw), BF16), kshape, kshape, kshape, kshape,
                   jax.ShapeDtypeStruct(bias.shape, F32), jax.ShapeDtypeStruct(sinks.shape, F32)],
        compiler_params=_cparams(("arbitrary",)),
    )(q, kb, kb, vb, vb, bias, sinks, do)


def _bucket_onehot():
    qi = jnp.arange(ATT_BLOCK)[None, :]
    kj = jnp.arange(2 * ATT_BLOCK)[:, None]
    n = jnp.maximum(qi + ATT_BLOCK - kj, 0)
    buckets, max_exact = 32, 16
    nf = jnp.maximum(n, 1).astype(F32)
    large = max_exact + (jnp.log(nf / max_exact) / math.log(RPB_MAX_DIST / max_exact)
                         * (buckets - max_exact)).astype(jnp.int32)
    bucket = jnp.where(n < max_exact, n, jnp.minimum(large, buckets - 1)).reshape(-1)
    return (bucket[None, :] == jnp.arange(buckets)[:, None]).astype(F32)


def _small_dot(name, a, b, ca, cb):
    m = a.shape[1 - ca]
    n = b.shape[1 - cb]

    def body(a_ref, b_ref, o_ref):
        o_ref[...] = _raw_dot(a_ref[...], b_ref[...], ca, cb, "hi")

    return pl.pallas_call(body, name=name, out_shape=jax.ShapeDtypeStruct((m, n), F32),
                          compiler_params=_cparams())(a, b)


def _mod_fwd(c_all, w_mod):
    d, n = w_mod.shape
    tn = _tile(n, 512)

    def body(c_ref, w_ref, o_ref, cond_ref):
        cv = c_ref[...]
        cond = cv * _sigmoid(cv)
        cond_ref[...] = cond
        o_ref[...] = _raw_dot(cond, w_ref[...], 1, 0, "hi")

    return pl.pallas_call(
        body, name="adaln_mod_fwd", grid=(n // tn,),
        in_specs=[pl.BlockSpec(c_all.shape, lambda j: (0, 0)), pl.BlockSpec((d, tn), lambda j: (0, j))],
        out_specs=[pl.BlockSpec((c_all.shape[0], tn), lambda j: (0, j)),
                   pl.BlockSpec(c_all.shape, lambda j: (0, 0))],
        out_shape=[jax.ShapeDtypeStruct((c_all.shape[0], n), F32), jax.ShapeDtypeStruct(c_all.shape, F32)],
        compiler_params=_cparams(("arbitrary",)),
    )(c_all, w_mod)


def _adam_math(w, g, m, v):
    m = ADAM_B1 * m + (1.0 - ADAM_B1) * g
    v = ADAM_B2 * v + (1.0 - ADAM_B2) * (g * g)
    m_hat = m / (1.0 - ADAM_B1 ** ADAM_STEP)
    v_hat = v / (1.0 - ADAM_B2 ** ADAM_STEP)
    delta = -ADAM_LR * (m_hat / (jnp.sqrt(v_hat) + ADAM_EPS) + ADAM_WD * w)
    return delta, m, v


def _adamw(name, w, m, v, gparts):
    r, c = w.shape
    p = gparts.shape[0]
    tr = _rtile(r, max(8, (1 << 18) // max(c, 1) // 8 * 8))

    def body(w_ref, m_ref, v_ref, g_ref, go_ref, d_ref, mo_ref, vo_ref):
        g = g_ref[0].astype(F32)
        for s in range(1, p):
            g = g + g_ref[s].astype(F32)
        delta, mn, vn = _adam_math(w_ref[...], g, m_ref[...], v_ref[...])
        go_ref[...] = g
        d_ref[...] = delta
        mo_ref[...] = mn
        vo_ref[...] = vn

    spec = pl.BlockSpec((tr, c), lambda i: (i, 0))
    return pl.pallas_call(
        body, name=name, grid=(r // tr,),
        in_specs=[spec, spec, spec, pl.BlockSpec((p, tr, c), lambda i: (0, i, 0))],
        out_specs=[spec] * 4, out_shape=[jax.ShapeDtypeStruct((r, c), F32)] * 4,
        compiler_params=_cparams(("parallel",)),
    )(w, m, v, gparts)


def _adamw_outer(name, w, m, v, cond_t, dmod):
    d, n = w.shape
    tr, tn = _rtile(d, 512), _tile(n, 1024)

    def body(w_ref, m_ref, v_ref, c_ref, dm_ref, go_ref, d_ref, mo_ref, vo_ref):
        g = _raw_dot(c_ref[...], dm_ref[...], 1, 0, "hi")
        delta, mn, vn = _adam_math(w_ref[...], g, m_ref[...], v_ref[...])
        go_ref[...] = g
        d_ref[...] = delta
        mo_ref[...] = mn
        vo_ref[...] = vn

    spec = pl.BlockSpec((tr, tn), lambda i, j: (i, j))
    return pl.pallas_call(
        body, name=name, grid=(d // tr, n // tn),
        in_specs=[spec, spec, spec, pl.BlockSpec((tr, cond_t.shape[1]), lambda i, j: (i, 0)),
                  pl.BlockSpec((dmod.shape[0], tn), lambda i, j: (0, j))],
        out_specs=[spec] * 4, out_shape=[jax.ShapeDtypeStruct((d, n), F32)] * 4,
        compiler_params=_cparams(("parallel", "parallel")),
    )(w, m, v, cond_t, dmod)


def _all_gather(name, arrays):
    n = len(arrays)

    def body(*refs):
        ins, outs = refs[:n], refs[n:2 * n]
        send_sems, recv_sems, local_sems = refs[2 * n:]
        x, y, c = lax.axis_index("x"), lax.axis_index("y"), lax.axis_index("c")
        me, sibling = (x, y, c), (x, y, 1 - c)
        chips = [(1 - x, y), (x, 1 - y), (1 - x, 1 - y)]

        def copy(a, k, block, to, src=None):
            rows = outs[a].at[4 * block[0] + 2 * block[1] + block[2]]
            return pltpu.make_async_remote_copy(
                src_ref=rows if src is None else src, dst_ref=rows, send_sem=send_sems.at[a, k],
                recv_sem=recv_sems.at[a, k], device_id=to, device_id_type=pl.DeviceIdType.MESH)

        mine = [pltpu.make_async_copy(ins[a], outs[a].at[4 * x + 2 * y + c], local_sems.at[a]) for a in range(n)]
        for cp in mine:
            cp.start()
        first = []
        for a in range(n):
            first.append(copy(a, 0, me, sibling, src=ins[a]))
            first += [copy(a, 1 + j, me, (*chip, c), src=ins[a]) for j, chip in enumerate(chips)]
        for cp in first:
            cp.start()
        passed = []
        for a in range(n):
            for j, chip in enumerate(chips):
                copy(a, 1 + j, (*chip, c), me).wait_recv()
                passed.append(copy(a, 4 + j, (*chip, c), sibling))
                passed[-1].start()
        for a in range(n):
            copy(a, 0, sibling, me).wait_recv()
            for j, chip in enumerate(chips):
                copy(a, 4 + j, (*chip, 1 - c), me).wait_recv()
        for cp in first + passed:
            cp.wait_send()
        for cp in mine:
            cp.wait()

    any_spec = pl.BlockSpec(memory_space=pl.ANY)
    return pl.pallas_call(
        body, name=name, in_specs=[any_spec] * n, out_specs=[any_spec] * n,
        out_shape=[jax.ShapeDtypeStruct((N_DEV,) + a.shape, a.dtype) for a in arrays],
        scratch_shapes=[pltpu.SemaphoreType.DMA((n, N_DEV - 1)), pltpu.SemaphoreType.DMA((n, N_DEV - 1)),
                        pltpu.SemaphoreType.DMA((n,))],
    )(*arrays)


def _peer(p):
    x, y, c = lax.axis_index("x"), lax.axis_index("y"), lax.axis_index("c")
    px, py, pc = x ^ ((p >> 2) & 1), y ^ ((p >> 1) & 1), c ^ (p & 1)
    return (px, py, pc), 4 * px + 2 * py + pc


def _block(ref, d, cols, rows=None):
    if cols:
        r = slice(None) if rows is None else pl.ds(pl.multiple_of(rows[0], 8), rows[1])
        return ref.at[r, pl.ds(pl.multiple_of(d * cols, cols), cols)]
    return ref.at[d] if rows is None else ref.at[d, pl.ds(pl.multiple_of(rows[0], 8), rows[1])]


def _split_copy(src_ref, land_ref, send_sems, recv_sems, p, scatter, arriving, cols=None, halves=False):
    x, y, c = lax.axis_index("x"), lax.axis_index("y"), lax.axis_index("c")
    me = 4 * x + 2 * y + c
    dev, idx = _peer(p)
    if scatter == "chips":
        src, dst = src_ref.at[idx >> 1], land_ref.at[(idx if arriving else me) >> 1]
    elif scatter:
        src, dst = _block(src_ref, idx, cols), land_ref.at[idx if arriving else me]
    elif halves and p >= 2:
        half = src_ref.shape[0] // 2
        rows = ((c if arriving else dev[2]) * half, half)
        src = src_ref.at[pl.ds(pl.multiple_of(rows[0], 8), half)]
        dst = _block(land_ref, idx if arriving else me, cols, rows)
    else:
        src, dst = src_ref, _block(land_ref, idx if arriving else me, cols)
    return pltpu.make_async_remote_copy(
        src_ref=src, dst_ref=dst, send_sem=send_sems.at[p - 1], recv_sem=recv_sems.at[p - 1], device_id=dev,
        device_id_type=pl.DeviceIdType.MESH)


_HBM_SPEC = pl.BlockSpec(memory_space=pltpu.HBM)
_SEM_SPEC = pl.BlockSpec(memory_space=pltpu.SEMAPHORE)
_DATAFLOW = pltpu.SideEffectType.DATAFLOW_SIDE_EFFECTING


def _own_copy(src_ref, land_ref, send_sems, scatter, cols):
    me = 4 * lax.axis_index("x") + 2 * lax.axis_index("y") + lax.axis_index("c")
    if scatter == "chips":
        src, dst = src_ref.at[me >> 1], land_ref.at[me >> 1]
    elif scatter:
        src, dst = _block(src_ref, me, cols), land_ref.at[me]
    else:
        src, dst = src_ref, _block(land_ref, me, cols)
    return pltpu.make_async_copy(src, dst, send_sems.at[N_DEV - 1])


def _peers(scatter):
    return (2, 4, 6) if scatter == "chips" else tuple(range(1, N_DEV))


def _exchange_start(name, srcs, lands, scatter, after, cols=None, halves=False):
    n = len(srcs)
    cols = cols or [None] * n

    def body(*refs):
        src_refs, land_refs = refs[:n], refs[n:2 * n]
        outs = refs[2 * n + 1:]
        send, recv, token = outs[:n], outs[n:2 * n], outs[-1]
        for a in range(n):
            for p in _peers(scatter):
                _split_copy(src_refs[a], land_refs[a], send[a], recv[a], p, scatter, False, cols[a],
                            halves).start()
            _own_copy(src_refs[a], land_refs[a], send[a], scatter, cols[a]).start()
        token[...] = jnp.zeros_like(token)

    sems = [pltpu.SemaphoreType.DMA((N_DEV,))] * n + [pltpu.SemaphoreType.DMA((N_DEV - 1,))] * n
    hbm = [pltpu.HBM(a.shape, a.dtype) for a in list(srcs) + list(lands)]
    res = pl.pallas_call(
        body, name=name,
        out_shape=sems + hbm + [jax.ShapeDtypeStruct((8, 128), F32)],
        in_specs=[_HBM_SPEC] * (2 * n) + [pl.BlockSpec(memory_space=pl.ANY)],
        out_specs=[_SEM_SPEC] * (2 * n) + [_HBM_SPEC] * (2 * n) + [pl.BlockSpec(memory_space=pltpu.VMEM)],
        input_output_aliases={i: 2 * n + i for i in range(2 * n)},
        compiler_params=pltpu.CompilerParams(has_side_effects=_DATAFLOW),
    )(*[pltpu.with_memory_space_constraint(a, pltpu.HBM) for a in list(srcs) + list(lands)], after)
    return res[:n], res[n:2 * n], res[2 * n:3 * n], res[3 * n:4 * n], res[-1]


def _exchange_wait(name, send_sem, recv_sem, src, land, scatter, after, cols=None, halves=False):
    def body(src_ref, land_ref, send, recv, after_ref, src_out, land_out):
        for p in _peers(scatter):
            cp = _split_copy(src_ref, land_ref, send, recv, p, scatter, True, cols, halves)
            cp.wait_send()
            cp.wait_recv()
        _own_copy(src_ref, land_ref, send, scatter, cols).wait()

    return pl.pallas_call(
        body, name=name,
        out_shape=(pltpu.HBM(src.shape, src.dtype), pltpu.HBM(land.shape, land.dtype)),
        in_specs=[_HBM_SPEC, _HBM_SPEC, _SEM_SPEC, _SEM_SPEC, pl.BlockSpec(memory_space=pl.ANY)],
        out_specs=(_HBM_SPEC, _HBM_SPEC), input_output_aliases={0: 0, 1: 1},
        compiler_params=pltpu.CompilerParams(has_side_effects=_DATAFLOW),
    )(src, land, send_sem, recv_sem, after)[1]


def _sibling_fill(name, land, cols):
    rows = land.shape[0] if cols else land.shape[1]
    half = rows // 2

    def body(in_ref, out_ref, send_sems, recv_sems):
        x, y, c = lax.axis_index("x"), lax.axis_index("y"), lax.axis_index("c")

        def copy(p, core):
            _, idx = _peer(p)
            return pltpu.make_async_remote_copy(
                src_ref=_block(in_ref, idx, cols, (core * half, half)),
                dst_ref=_block(out_ref, idx, cols, (core * half, half)),
                send_sem=send_sems.at[idx], recv_sem=recv_sems.at[idx], device_id=(x, y, 1 - c),
                device_id_type=pl.DeviceIdType.MESH)

        sends = [copy(p, c) for p in range(2, N_DEV)]
        for cp in sends:
            cp.start()
        for p in range(2, N_DEV):
            copy(p, 1 - c).wait_recv()
        for cp in sends:
            cp.wait_send()

    any_spec = pl.BlockSpec(memory_space=pl.ANY)
    return pl.pallas_call(
        body, name=name, in_specs=[any_spec], out_specs=any_spec,
        out_shape=jax.ShapeDtypeStruct(land.shape, land.dtype), input_output_aliases={0: 0},
        scratch_shapes=[pltpu.SemaphoreType.DMA((N_DEV,)), pltpu.SemaphoreType.DMA((N_DEV,))],
    )(land)


def _pair_exchange(name, arrays):
    n = len(arrays)

    def body(*refs):
        ins, outs = refs[:n], refs[n:2 * n]
        send_sems, recv_sems = refs[2 * n:]
        x, y, c = lax.axis_index("x"), lax.axis_index("y"), lax.axis_index("c")
        copies = [pltpu.make_async_remote_copy(
            src_ref=ins[a].at[2 * k + (1 - c)], dst_ref=outs[a].at[k], send_sem=send_sems.at[a, k],
            recv_sem=recv_sems.at[a, k], device_id=(x, y, 1 - c), device_id_type=pl.DeviceIdType.MESH)
            for a in range(n) for k in range(N_DEV // 2)]
        for cp in copies:
            cp.start()
        for cp in copies:
            cp.wait()

    any_spec = pl.BlockSpec(memory_space=pl.ANY)
    return pl.pallas_call(
        body, name=name, in_specs=[any_spec] * n, out_specs=[any_spec] * n,
        out_shape=[jax.ShapeDtypeStruct((N_DEV // 2,) + a.shape[1:], a.dtype) for a in arrays],
        scratch_shapes=[pltpu.SemaphoreType.DMA((n, N_DEV // 2)), pltpu.SemaphoreType.DMA((n, N_DEV // 2))],
    )(*arrays)


def _pair_sum(name, mine, theirs):
    _, r, c_ = mine.shape
    tr = _rtile(r, max(8, (1 << 19) // c_ // 8 * 8))

    def body(mine_ref, theirs_ref, o_ref):
        core = lax.axis_index("c")
        o_ref[0] = (mine_ref[0, core].astype(F32) + theirs_ref[0].astype(F32)).astype(o_ref.dtype)

    return pl.pallas_call(
        body, name=name, grid=(N_DEV // 2, r // tr),
        in_specs=[pl.BlockSpec((1, 2, tr, c_), lambda k, i: (k, 0, i, 0)),
                  pl.BlockSpec((1, tr, c_), lambda k, i: (k, i, 0))],
        out_specs=pl.BlockSpec((1, tr, c_), lambda k, i: (k, i, 0)),
        out_shape=jax.ShapeDtypeStruct((N_DEV // 2, r, c_), mine.dtype),
        compiler_params=_cparams(("parallel", "parallel")),
    )(mine.reshape(N_DEV // 2, 2, r, c_), theirs)


def _cols_to_shards(a):
    r, c = a.shape
    return a.reshape(r, N_DEV, c // N_DEV).transpose(1, 0, 2)


def _shards_to_cols(a):
    d, r, n = a.shape
    return a.transpose(1, 0, 2).reshape(r, d * n)


def kernel(x, c, ln_emb_g, ln_emb_b, rpb_table, w_mod, b_mod, w_in, mu_shift, w0, w_decay_up, a0, w_iclr_up, w_gate_up, k_k, k_a, r_k, lnx_g, lnx_b, attn_sinks, w_out, ln1_g, ln1_b, w_up, w_down, ln2_g, ln2_b, loss_target, m_ln_emb_g, m_ln_emb_b, m_rpb_table, m_w_mod, m_b_mod, m_w_in, m_mu_shift, m_w0, m_w_decay_up, m_a0, m_w_iclr_up, m_w_gate_up, m_k_k, m_k_a, m_r_k, m_lnx_g, m_lnx_b, m_attn_sinks, m_w_out, m_ln1_g, m_ln1_b, m_w_up, m_w_down, m_ln2_g, m_ln2_b, v_ln_emb_g, v_ln_emb_b, v_rpb_table, v_w_mod, v_b_mod, v_w_in, v_mu_shift, v_w0, v_w_decay_up, v_a0, v_w_iclr_up, v_w_gate_up, v_k_k, v_k_a, v_r_k, v_lnx_g, v_lnx_b, v_attn_sinks, v_w_out, v_ln1_g, v_ln1_b, v_w_up, v_w_down, v_ln2_g, v_ln2_b):
    names = ["ln_emb_g", "ln_emb_b", "rpb_table", "w_mod", "b_mod", "w_in", "mu_shift", "w0", "w_decay_up",
             "a0", "w_iclr_up", "w_gate_up", "k_k", "k_a", "r_k", "lnx_g", "lnx_b", "attn_sinks", "w_out",
             "ln1_g", "ln1_b", "w_up", "w_down", "ln2_g", "ln2_b"]
    env = dict(locals())
    weights = {nm: env[nm] for nm in names}
    mom_m = {nm: env["m_" + nm] for nm in names}
    mom_v = {nm: env["v_" + nm] for nm in names}

    t, d = x.shape[1], x.shape[2]
    da = d // 2
    h_a = da // HEAD
    hq = (d - da) // HEAD
    hkv = hq // GQA_RATIO
    l_w, l_a, l_g = w_decay_up.shape[1], w_iclr_up.shape[1], w_gate_up.shape[1]
    o_w, o_a, o_g = 3 * da, 3 * da + l_w, 3 * da + l_w + l_a
    n_rwkv = o_g + l_g
    o_kb, o_vb = n_rwkv + hq * HEAD, n_rwkv + hq * HEAD + hkv * HEAD
    me = 4 * lax.axis_index("x") + 2 * lax.axis_index("y") + lax.axis_index("c")

    x2d, tgt = x[0], loss_target[0]
    row = lambda a: a.reshape(1, -1)
    seg = (jnp.arange(da)[:, None] // HEAD == jnp.arange(h_a)[None, :]).astype(F32)
    seg_t = seg.T

    (c_all,) = _all_gather("gather_cond", [c])
    c_all = c_all.reshape(N_DEV, d)
    mod_rows, cond_all = _mod_fwd(c_all, w_mod[0])
    gathered = _all_gather("gather_weights", [
        mod_rows, w_in[0].astype(BF16).T, w_decay_up[0], w_iclr_up[0], w_gate_up[0]])
    mod_all, win_g, wd_g, wa_g, wg_g = gathered
    late = [w_out[0].astype(BF16), w_up[0].astype(BF16), w_down[0].astype(BF16)]
    n_up = w_up.shape[2]
    late_cols = [None, n_up, None]
    late_lands = [lax.empty((N_DEV,) + late[0].shape, BF16), lax.empty((d, N_DEV * n_up), BF16),
                  lax.empty((N_DEV,) + late[2].shape, BF16)]
    late_send, late_recv, late_src, late_land, late_token = _exchange_start(
        "gather_late_weights_start", late, late_lands, False, mod_all, late_cols, halves=True)
    mod = lax.dynamic_index_in_dim(mod_all, me, axis=1, keepdims=False).reshape(1, -1) + b_mod
    mod = mod + late_token[0, 0]
    sh1, sc1, g1, sh2, sc2, g2 = [mod[:, i * d:(i + 1) * d] for i in range(6)]
    w_in_t = win_g.reshape(-1, d)
    wd_f, wa_f, wg_f = _shards_to_cols(wd_g), _shards_to_cols(wa_g), _shards_to_cols(wg_g)

    def late_weight(i, nm, after):
        land = _exchange_wait("gather_" + nm + "_wait", late_send[i], late_recv[i], late_src[i], late_land[i],
                              False, after, late_cols[i], halves=True)
        return _sibling_fill("gather_" + nm + "_fill", land, late_cols[i])

    tr = _rtile(t, 256)
    lng, lnb = row(ln_emb_g), row(ln_emb_b)

    def embed_fn(i, nb, xb, g, b, sc, sh):
        return _embed_math(xb, g, b, sc, sh), ()
    x0, u1 = _rowwise("embed_ln_mod", embed_fn, [x2d], [lng, lnb, sc1, sh1], [(d, F32), (d, BF16)], [], tr)

    (p_rkv,) = _matmul("in_proj_rkv", u1, w_in_t[:o_w], "nt", [F32])
    (p_lora,) = _matmul("in_proj_lora", u1, w_in_t[o_w:n_rwkv], "nt", [F32])
    (q,) = _matmul("in_proj_q", u1, w_in_t[n_rwkv:o_kb], "nt", [F32])
    (p_kv,) = _matmul("in_proj_kv", u1, w_in_t[o_kb:], "nt", [F32])
    p_w, p_a, p_g = p_lora[:, :l_w], p_lora[:, l_w:l_w + l_a], p_lora[:, l_w + l_a:]
    kb, vb = p_kv[:, :hkv * HEAD], p_kv[:, hkv * HEAD:]
    mu_rkv, mu_w, mu_a, mu_g = (mu_shift[:, :o_w], mu_shift[:, o_w:o_a], mu_shift[:, o_a:o_g],
                                mu_shift[:, o_g:n_rwkv])
    pre_params = [w0, wd_f, a0, wa_f, wg_f, k_k, k_a, seg, seg_t]
    tr_pre = _rtile(t, 128)

    def shifted(i, blocks, halos, mus):
        return [xb + (_shift_prev(xb, hb, i) - xb) * mb for xb, hb, mb in zip(blocks, halos, mus)]

    def split3(a):
        return a[:, :da], a[:, da:2 * da], a[:, 2 * da:]

    def pre_fn(i, nb, b_rkv, b_w, b_a, b_g, h_rkv, h_w, h_a_, h_g, m_rkv, m_w, m_a, m_g, *params):
        s_rkv, s_w, s_a, s_g = shifted(i, [b_rkv, b_w, b_a, b_g], [h_rkv, h_w, h_a_, h_g],
                                       [m_rkv, m_w, m_a, m_g])
        return _rwkv_pre_math(*split3(s_rkv), s_w, s_a, s_g, *params, dot=_raw_dot), ()

    pre_rows = [p_rkv, p_w, p_a, p_g]
    pre_halos = [(a, "prev") for a in pre_rows]
    r_, lw_, k2_, v_, kk_, b_, gate_ = _rowwise(
        "rwkv_pre", pre_fn, pre_rows, [mu_rkv, mu_w, mu_a, mu_g] + pre_params,
        [(da, F32)] * 7, [], tr_pre, halos=pre_halos)

    y_rec, s0s, invs = _rec_fwd(r_, lw_, k2_, v_, kk_, b_)

    rk_flat = r_k.reshape(1, da)
    post_params = [lnx_g, lnx_b, rk_flat, seg, seg_t]

    def post_fn(i, nb, yb, rb, kb_, vb_, gb, *params):
        return (_rwkv_post_math(yb, rb, kb_, vb_, gb, *params, dot=_raw_dot),), ()
    (ya,) = _rowwise("rwkv_post", post_fn, [y_rec, r_, k2_, v_, gate_], post_params, [(da, BF16)], [], tr_pre)

    onehot = _bucket_onehot()
    bias = _small_dot("rpb_gather", rpb_table, onehot, 0, 0)
    bias = bias.reshape(hq, 2 * ATT_BLOCK, ATT_BLOCK)
    yb = _attn_fwd(q, kb, vb, bias, attn_sinks)

    mix_in = jnp.concatenate([ya, yb], axis=1)
    w_out_f = late_weight(0, "w_out", mix_in).reshape(d, d)
    (mix,) = _matmul("out_proj", mix_in, w_out_f, "nn", [F32])

    def post1_fn(i, nb, xin, yv, gate, g, b, sc, sh):
        return _post_math(xin, yv, gate, g, b, sc, sh), ()
    x1, u2 = _rowwise("ln1_mod", post1_fn, [x0, mix], [g1, ln1_g, ln1_b, sc2, sh2],
                      [(d, F32), (d, BF16)], [], tr)

    def relu2(acc):
        rl = jnp.maximum(acc, 0.0)
        return acc, rl * rl
    w_up_f = late_weight(1, "w_up", u2)
    hpre, hact = _matmul("mlp_up", u2, w_up_f, "nn", [F32, BF16], epilogue=relu2)
    w_down_f = late_weight(2, "w_down", hact).reshape(-1, d)
    (hmlp,) = _matmul("mlp_down", hact, w_down_f, "nn", [F32])

    def loss_fn(i, nb, xin, hv, tg, gate, g, b):
        val, vjp = jax.vjp(_loss_math, xin, hv, tg, gate, g, b)
        dxin, dh, _, dgate, dg, db = vjp(jnp.ones((), F32))
        return (dxin, dh), (val, dgate, dg, db)
    dx1, dh, loss_acc, dg2, dln2g, dln2b = _rowwise(
        "ln2_loss", loss_fn, [x1, hmlp, tgt], [g2, ln2_g, ln2_b], [(d, F32), (d, BF16)],
        [(1, 1), (1, d), (1, d), (1, d)], _rtile(t, 128))

    def drelu2(acc, hp):
        return (acc * 2.0 * jnp.maximum(hp, 0.0),)
    (dhpre,) = _matmul("mlp_down_dgrad", dh, w_down_f, "nt", [BF16], epilogue=drelu2, extras=[hpre])
    (gw_down,) = _matmul("mlp_down_wgrad", hact, dh, "tn", [BF16])
    (du2,) = _matmul("mlp_up_dgrad", dhpre, w_up_f, "nt", [F32])
    (gw_up,) = _matmul("mlp_up_wgrad", u2, dhpre, "tn", [BF16])

    def landing(src):
        return lax.empty(src.shape, src.dtype)
    mlp_src = [gw_down.reshape(N_DEV, -1, d), gw_up]
    mlp_cols = [None, n_up]
    mlp_send, mlp_recv, mlp_src, mlp_land, mlp_token = _exchange_start(
        "scatter_mlp_grads_start", mlp_src, [landing(mlp_src[0]), lax.empty((N_DEV, d, n_up), BF16)], True,
        gw_up, mlp_cols)

    def post1_bwd(i, nb, xin, yv, dx1v, du2v, gate, g, b, sc, sh):
        _, vjp = jax.vjp(_post_math, xin, yv, gate, g, b, sc, sh)
        dxin, dy, dgate, dg, db, dsc, dsh = vjp((dx1v, du2v))
        return (dxin, dy), (dgate, dg, db, dsc, dsh)
    dx0, dmix, dg1, dln1g, dln1b, dsc2, dsh2 = _rowwise(
        "ln1_mod_bwd", post1_bwd, [x0, mix, dx1, du2], [g1 + mlp_token[0, 0], ln1_g, ln1_b, sc2, sh2],
        [(d, F32), (d, BF16)], [(1, d)] * 5, _rtile(t, 128))

    (dmix_in,) = _matmul("out_proj_dgrad", dmix, w_out_f, "nt", [F32])
    (gw_out,) = _matmul("out_proj_wgrad", mix_in, dmix, "tn", [BF16])
    dya = dmix_in[:, :da]
    out_src = [gw_out.reshape(N_DEV, d // N_DEV, d)]
    out_send, out_recv, out_src, out_land, out_token = _exchange_start(
        "scatter_out_grad_start", out_src, [landing(a) for a in out_src], True, gw_out)
    post_params_bwd = [lnx_g, lnx_b, rk_flat + out_token[0, 0], seg, seg_t]

    def post_bwd(i, nb, yb_, rb, kb_, vb_, gb, dyab, *params):
        _, vjp = jax.vjp(_rwkv_post_math, yb_, rb, kb_, vb_, gb, *params)
        dy, dr, dk, dv, dg, dlg, dlb, drk, _, _ = vjp(dyab)
        return (dy, dr, dk, dv, dg), (dlg, dlb, drk)
    dy_rec, dr_e, dk_e, dv_e, dgate, dlnxg, dlnxb, drk = _rowwise(
        "rwkv_post_bwd", post_bwd, [y_rec, r_, k2_, v_, gate_, dya], post_params_bwd,
        [(da, F32)] * 5, [(1, da)] * 3, tr_pre)

    dr_r, dlw_r, dk_r, dv_r, dkk_r, db_r = _rec_bwd(r_, lw_, k2_, v_, kk_, b_, s0s, invs, dy_rec)

    def pre_bwd(i, nb, b_rkv, b_w, b_a, b_g, dr1, dr2, dlw, dk1, dk2, dv1, dv2, dkk, dbb, dgt,
                h_rkv, h_w, h_a_, h_g, m_rkv, m_w, m_a, m_g, *params):
        blocks = [b_rkv, b_w, b_a, b_g]
        prevs = [_shift_prev(xb, hb, i) for xb, hb in zip(blocks, [h_rkv, h_w, h_a_, h_g])]
        mus = [m_rkv, m_w, m_a, m_g]
        s_rkv, s_w, s_a, s_g = [xb + (pb - xb) * mb for xb, pb, mb in zip(blocks, prevs, mus)]
        _, vjp = jax.vjp(_rwkv_pre_math, *split3(s_rkv), s_w, s_a, s_g, *params)
        grads = vjp((dr1 + dr2, dlw, dk1 + dk2, dv1 + dv2, dkk, dbb, dgt))
        g_rkv = jnp.concatenate(grads[:3], axis=1)
        g_w, g_a, g_g = grads[3:6]
        dmu = [jnp.sum(gs * (pb - xb), axis=0, keepdims=True)
               for gs, pb, xb in zip([g_rkv, g_w, g_a, g_g], prevs, blocks)]
        dw0, dwd, da0, dwa, dwg, dkk_, dka = grads[6:13]
        return (g_rkv, g_w, g_a, g_g), (*dmu, dw0, dwd, da0, dwa, dwg, dkk_, dka)

    pre_out = _rowwise(
        "rwkv_pre_bwd", pre_bwd,
        pre_rows + [dr_r, dr_e, dlw_r, dk_r, dk_e, dv_r, dv_e, dkk_r, db_r, dgate],
        [mu_rkv, mu_w, mu_a, mu_g] + pre_params,
        [(o_w, F32), (l_w, F32), (l_a, F32), (l_g, F32)],
        [(1, o_w), (1, l_w), (1, l_a), (1, l_g), (1, da), (l_w, da), (1, da), (l_a, da), (l_g, da),
         (1, da), (1, da)],
        _rtile(t, 64), halos=pre_halos)
    gs_rkv, gs_w, gs_a, gs_g = pre_out[:4]
    dmu_parts = pre_out[4:8]
    dw0, dwd, da0, dwa, dwg, dk_k, dk_a = pre_out[8:]

    def unshift_fn(i, nb, a1, a2, a3, a4, n1, n2, n3, n4, m1, m2, m3, m4):
        outs = [gs * (1.0 - mb) + _shift_next(gs * mb, hb * mb, i, nb)
                for gs, hb, mb in zip([a1, a2, a3, a4], [n1, n2, n3, n4], [m1, m2, m3, m4])]
        return outs, ()
    gs_list = [gs_rkv, gs_w, gs_a, gs_g]
    dp_rkv, dp_w, dp_a, dp_g = _rowwise(
        "token_shift_bwd", unshift_fn, gs_list, [mu_rkv, mu_w, mu_a, mu_g],
        [(o_w, BF16), (l_w, BF16), (l_a, BF16), (l_g, BF16)], [], tr_pre,
        halos=[(a, "next") for a in gs_list])

    dq, dkp, dkc, dvp, dvc, dbias, dsinks = _attn_bwd(q, kb, vb, bias, attn_sinks, dmix_in, 1)
    zpad = jnp.zeros((ATT_BLOCK, kb.shape[1]), F32)
    dkb = (dkc + jnp.concatenate([dkp[ATT_BLOCK:], zpad], axis=0)).astype(BF16)
    dvb = (dvc + jnp.concatenate([dvp[ATT_BLOCK:], zpad], axis=0)).astype(BF16)
    d_rpb = _small_dot("rpb_scatter", onehot, dbias.reshape(hq, -1), 1, 1)

    dp = jnp.concatenate([dp_rkv, dp_w, dp_a, dp_g, dq, dkb, dvb], axis=1)
    (gw_in_t,) = _matmul("in_proj_wgrad", dp, u1, "tn", [BF16])
    in_names = ["w_in", "w_decay_up", "w_iclr_up", "w_gate_up"]
    in_parts = [gw_in_t.reshape(N_DEV, -1, d), _cols_to_shards(dwd), _cols_to_shards(dwa), _cols_to_shards(dwg)]
    in_sibling = _pair_exchange("pair_in_grads", in_parts)
    in_src = [_pair_sum("pair_sum_" + nm, a, b) for nm, a, b in zip(in_names, in_parts, in_sibling)]
    in_send, in_recv, in_src, in_land, in_token = _exchange_start(
        "scatter_in_grads_start", in_src, [landing(a) for a in in_src], "chips", in_src[0])
    (du1,) = _matmul("in_proj_dgrad", dp, w_in_t, "nn", [F32], after=in_token)

    def embed_bwd(i, nb, xb, dx0v, du1v, g, b, sc, sh):
        _, vjp = jax.vjp(_embed_math, xb, g, b, sc, sh)
        dxv, dg, db, dsc, dsh = vjp((dx0v, du1v))
        return (dxv,), (dg, db, dsc, dsh)
    grad_x, dlng, dlnb, dsc1, dsh1 = _rowwise(
        "embed_ln_mod_bwd", embed_bwd, [x2d, dx0, du1], [lng, lnb, sc1, sh1], [(d, F32)], [(1, d)] * 4,
        _rtile(t, 128))

    dmod = jnp.concatenate([dsh1, dsc1, dg1, dsh2, dsc2, dg2], axis=1)
    small = {"ln_emb_g": dlng, "ln_emb_b": dlnb, "rpb_table": d_rpb, "b_mod": dmod,
             "mu_shift": jnp.concatenate(dmu_parts, axis=1), "w0": dw0, "a0": da0, "k_k": dk_k, "k_a": dk_a,
             "r_k": drk, "lnx_g": dlnxg, "lnx_b": dlnxb, "attn_sinks": dsinks, "ln1_g": dln1g, "ln1_b": dln1b,
             "ln2_g": dln2g, "ln2_b": dln2b}
    small_names = list(small)
    packed = jnp.concatenate([small[nm].reshape(1, -1) for nm in small_names], axis=1)
    sm_send, sm_recv, sm_src, sm_land, sm_token = _exchange_start(
        "gather_small_grads_start", [packed], [lax.empty((N_DEV,) + packed.shape, F32)], False, grad_x)

    grads, deltas, new_m, new_v = {}, {}, {}, {}

    def put(nm, res):
        shape = weights[nm].shape
        grads[nm], deltas[nm], new_m[nm], new_v[nm] = [a.reshape(shape) for a in res]

    def big_update(nm, parts):
        put(nm, _adamw("adamw_" + nm, weights[nm][0], mom_m[nm][0], mom_v[nm][0], parts))

    behind = sm_token
    big_update("w_down", _exchange_wait("scatter_w_down_wait", mlp_send[0], mlp_recv[0], mlp_src[0], mlp_land[0],
                                        True, behind, mlp_cols[0]))
    big_update("w_up", _exchange_wait("scatter_w_up_wait", mlp_send[1], mlp_recv[1], mlp_src[1], mlp_land[1],
                                      True, behind, mlp_cols[1]))
    big_update("w_out", _exchange_wait("scatter_w_out_wait", out_send[0], out_recv[0], out_src[0], out_land[0],
                                       True, behind))

    packed_all = _exchange_wait("gather_small_grads_wait", sm_send[0], sm_recv[0], sm_src[0], sm_land[0], False,
                                deltas["w_out"]).reshape(N_DEV, -1)
    n_mod = w_mod.shape[2]
    dmod_cols = lax.dynamic_slice_in_dim(packed_all[:, _offset(small, small_names, "b_mod"):], me * n_mod, n_mod,
                                         axis=1)
    put("w_mod", _adamw_outer("adamw_w_mod", w_mod[0], m_w_mod[0], v_w_mod[0], cond_all.T, dmod_cols))

    off = 0
    for nm in small_names:
        size = small[nm].size
        wshape = weights[nm].shape
        two_d = (1, size) if nm != "rpb_table" else wshape
        parts = packed_all[:, off:off + size].reshape((N_DEV,) + two_d)
        off += size
        put(nm, _adamw("adamw_" + nm, weights[nm].reshape(two_d), mom_m[nm].reshape(two_d),
                       mom_v[nm].reshape(two_d), parts))

    behind = deltas[small_names[-1]]
    for i, nm in enumerate(in_names):
        parts = _exchange_wait("scatter_" + nm + "_wait", in_send[i], in_recv[i], in_src[i], in_land[i],
                               "chips", behind)
        big_update(nm, jnp.swapaxes(parts, 1, 2) if nm == "w_in" else parts)

    loss = lax.psum(loss_acc[0, 0], MESH_AXES)
    return (loss, grad_x[None], *[grads[nm] for nm in names], *[deltas[nm] for nm in names],
            *[new_m[nm] for nm in names], *[new_v[nm] for nm in names])


def _offset(small, small_names, name):
    off = 0
    for nm in small_names:
        if nm == name:
            return off
        off += small[nm].size
    raise KeyError(name)
```

```python
import functools
import math

import jax
import jax.numpy as jnp
from jax import lax
from jax.experimental import pallas as pl
from jax.experimental.pallas import tpu as pltpu

F32 = jnp.float32
BF16 = jnp.bfloat16
HI = lax.Precision.HIGHEST
MESH_AXES = ("x", "y", "c")
N_DEV = 8

HEAD = 64
GQA_RATIO = 8
ATT_BLOCK = 128
RPB_MAX_DIST = 128
LN_EPS = 1e-5
LNX_EPS = 64e-5
DEPTH = 1
ALPHA = (2.0 * DEPTH) ** 0.25
CHUNK = 64
REC_HEADS = 32

ADAM_LR = 0.001
ADAM_B1 = 0.9
ADAM_B2 = 0.999
ADAM_EPS = 1e-08
ADAM_WD = 0.01
ADAM_STEP = 10

VMEM_LIMIT = 60 * 1024 * 1024


def _cparams(sem=None):
    return pltpu.CompilerParams(dimension_semantics=sem, vmem_limit_bytes=VMEM_LIMIT)


def _tile(dim, cap):
    best = None
    t = 128
    while t <= min(dim, cap):
        if dim % t == 0:
            best = t
        t += 128
    return best or dim


def _rtile(dim, cap):
    best = None
    t = 8
    while t <= min(dim, cap):
        if dim % t == 0:
            best = t
        t += 8
    return best or dim


def _split2(a):
    hi = a.astype(BF16)
    return hi, (a - hi.astype(F32)).astype(BF16)


def _raw_dot(a, b, ca, cb, prec):
    dims = (((ca,), (cb,)), ((), ()))
    mm = lambda p, q: lax.dot_general(p, q, dims, preferred_element_type=F32)
    if prec == "bf16":
        return mm(a.astype(BF16), b.astype(BF16))
    if prec == "x3":
        (ah, al), (bh, bl) = _split2(a), _split2(b)
        return mm(ah, bh) + (mm(ah, bl) + mm(al, bh))
    if prec == "mask":
        ab = a.astype(BF16)
        b1, b2 = _split2(b)
        b3 = (b - b1.astype(F32) - b2.astype(F32)).astype(BF16)
        return mm(ab, b1) + (mm(ab, b2) + mm(ab, b3))
    if prec == "mb2":
        (ah, al), bb = _split2(a), b.astype(BF16)
        return mm(ah, bb) + mm(al, bb)
    return lax.dot_general(a, b, dims, precision=HI, preferred_element_type=F32)


@functools.partial(jax.custom_vjp, nondiff_argnums=(2, 3, 4))
def _bf16_dot(a, b, ca, cb, prec):
    return _raw_dot(a, b, ca, cb, prec)


def _bf16_dot_fwd(a, b, ca, cb, prec):
    return _raw_dot(a, b, ca, cb, prec), (a, b)


def _bf16_dot_bwd(ca, cb, prec, res, g):
    a, b = res
    if prec == "mask":
        return jnp.zeros_like(a), _bf16_dot(a, g, 1 - ca, 0, prec)
    if prec == "mb2":
        return _bf16_dot(g, b, 1, 1, prec), jnp.zeros_like(b)
    if ca == 1:
        da = _bf16_dot(g, b, 1, 1 - cb, prec)
    else:
        da = _bf16_dot(b, g, 1 - cb, 1, prec)
    if cb == 0:
        db = _bf16_dot(a, g, 1 - ca, 0, prec)
    else:
        db = _bf16_dot(g, a, 0, 1 - ca, prec)
    return da, db


_bf16_dot.defvjp(_bf16_dot_fwd, _bf16_dot_bwd)


def _dot(a, b, ca, cb, prec):
    return _raw_dot(a, b, ca, cb, prec) if prec == "hi" else _bf16_dot(a, b, ca, cb, prec)


def _sigmoid(z):
    return 1.0 / (1.0 + jnp.exp(-z))


def _softplus(z):
    return jnp.maximum(z, 0.0) + jnp.log(1.0 + jnp.exp(-jnp.abs(z)))


MATMUL_VMEM_BUDGET = 51 * 1024 * 1024


def _matmul_tiles(m, n, k, in_bytes, out_dtypes, n_extras):
    tm, tn = _tile(m, 1024), _tile(n, 1024)
    out_bytes = sum(jnp.dtype(dt).itemsize for dt in out_dtypes)
    for cap in (4096, 2048, 1024, 512, 256, 128):
        tk = _tile(k, cap)
        acc = 4 * tm * tn if tk < k else 0
        need = 2 * in_bytes * (tm + tn) * tk + 2 * tm * tn * (out_bytes + 4 * n_extras) + acc + 4 * tm * tn
        if need <= MATMUL_VMEM_BUDGET:
            break
    return tm, tn, tk


def _matmul(name, a, b, mode, out_dtypes, epilogue=None, extras=(), after=None):
    if mode == "nn":
        (m, k), n = a.shape, b.shape[1]
    elif mode == "nt":
        (m, k), n = a.shape, b.shape[0]
    else:
        (k, m), n = a.shape, b.shape[1]
    tm, tn, tk = _matmul_tiles(m, n, k, a.dtype.itemsize, out_dtypes, len(extras))
    nk = k // tk
    ne, no = len(extras), len(out_dtypes)
    ca, cb = {"nn": (1, 0), "nt": (1, 1), "tn": (0, 0)}[mode]

    n_after = 0 if after is None else 1

    def body(a_ref, b_ref, *rest):
        rest = rest[n_after:]
        extra_refs, out_refs = rest[:ne], rest[ne:ne + no]
        acc = rest[-1] if nk > 1 else None
        kk = pl.program_id(2)
        part = _raw_dot(a_ref[...], b_ref[...], ca, cb, "bf16")

        def finish(total):
            res = epilogue(total, *[e[...] for e in extra_refs]) if epilogue else (total,)
            for o, v in zip(out_refs, res):
                o[...] = v.astype(o.dtype)

        if nk == 1:
            finish(part)
            return

        @pl.when(kk == 0)
        def _():
            acc[...] = part

        @pl.when((kk > 0) & (kk < nk - 1))
        def _():
            acc[...] += part

        @pl.when(kk == nk - 1)
        def _():
            finish(acc[...] + part)

    a_spec = (pl.BlockSpec((tk, tm), lambda i, j, kk: (kk, i)) if mode == "tn"
              else pl.BlockSpec((tm, tk), lambda i, j, kk: (i, kk)))
    b_spec = (pl.BlockSpec((tn, tk), lambda i, j, kk: (j, kk)) if mode == "nt"
              else pl.BlockSpec((tk, tn), lambda i, j, kk: (kk, j)))
    mn_spec = pl.BlockSpec((tm, tn), lambda i, j, kk: (i, j))
    after_specs = [pl.BlockSpec(memory_space=pl.ANY)] * n_after
    outs = pl.pallas_call(
        body, name=name, grid=(m // tm, n // tn, nk),
        in_specs=[a_spec, b_spec] + after_specs + [mn_spec] * ne,
        out_specs=[mn_spec] * no,
        out_shape=[jax.ShapeDtypeStruct((m, n), dt) for dt in out_dtypes],
        scratch_shapes=[pltpu.VMEM((tm, tn), F32)] if nk > 1 else [],
        compiler_params=_cparams(("parallel", "parallel", "arbitrary")),
    )(a, b, *([after] * n_after), *extras)
    return outs


def _rowwise(name, fn, rows, bcasts, out_rows, out_accs, tr, halos=()):
    t = rows[0].shape[0]
    nb = t // tr
    n_in = len(rows) + len(halos) + len(bcasts)
    n_ro = len(out_rows)

    def body(*refs):
        ins = [r[...] for r in refs[:n_in]]
        o_refs = refs[n_in:]
        i = pl.program_id(0)
        routs, aouts = fn(i, nb, *ins)
        for ref, v in zip(o_refs[:n_ro], routs):
            ref[...] = v.astype(ref.dtype)
        for ref, v in zip(o_refs[n_ro:], aouts):
            @pl.when(i == 0)
            def _(ref=ref):
                ref[...] = jnp.zeros_like(ref)
            ref[...] += v.reshape(ref.shape)

    in_specs = [pl.BlockSpec((tr, r.shape[1]), lambda i: (i, 0)) for r in rows]
    for arr, which in halos:
        if which == "prev":
            in_specs.append(pl.BlockSpec((8, arr.shape[1]), lambda i: (jnp.maximum(i * (tr // 8) - 1, 0), 0)))
        else:
            in_specs.append(pl.BlockSpec((8, arr.shape[1]),
                                         lambda i: (jnp.minimum((i + 1) * (tr // 8), t // 8 - 1), 0)))
    for bc in bcasts:
        in_specs.append(pl.BlockSpec(bc.shape, lambda i, nd=bc.ndim: (0,) * nd))
    out_specs = [pl.BlockSpec((tr, c), lambda i: (i, 0)) for c, _ in out_rows]
    out_specs += [pl.BlockSpec(s, lambda i, nd=len(s): (0,) * nd) for s in out_accs]
    out_shape = [jax.ShapeDtypeStruct((t, c), dt) for c, dt in out_rows]
    out_shape += [jax.ShapeDtypeStruct(s, F32) for s in out_accs]
    return pl.pallas_call(
        body, name=name, grid=(nb,), in_specs=in_specs, out_specs=out_specs, out_shape=out_shape,
        compiler_params=_cparams(("arbitrary",)),
    )(*rows, *[h[0] for h in halos], *bcasts)


def _shift_prev(x, halo, i):
    rolled = pltpu.roll(x, 1, 0)
    first = jnp.where(i == 0, 0.0, halo[7:8, :])
    row = lax.broadcasted_iota(jnp.int32, x.shape, 0)
    return jnp.where(row == 0, first, rolled)


def _shift_next(x, halo, i, nb):
    rolled = pltpu.roll(x, x.shape[0] - 1, 0)
    last = jnp.where(i == nb - 1, 0.0, halo[0:1, :])
    row = lax.broadcasted_iota(jnp.int32, x.shape, 0)
    return jnp.where(row == x.shape[0] - 1, last, rolled)


def _ln(x, g, b, eps=LN_EPS):
    mu = jnp.mean(x, axis=-1, keepdims=True)
    xc = x - mu
    var = jnp.mean(xc * xc, axis=-1, keepdims=True)
    return xc * lax.rsqrt(var + eps) * g + b


def _embed_math(x, g, b, sc, sh):
    x0 = _ln(x, g, b)
    return x0, x0 * (1.0 + sc) + sh


def _post_math(xin, y, gate, g, b, sc, sh):
    x1 = _ln(ALPHA * xin + (1.0 + gate) * y, g, b)
    return x1, x1 * (1.0 + sc) + sh


def _loss_math(xin, h, tgt, gate, g, b):
    x2 = _ln(ALPHA * xin + (1.0 + gate) * h, g, b)
    err = x2 - tgt
    return 0.5 * jnp.sum(jnp.mean(err * err, axis=-1))


def _rwkv_pre_math(r, k, v, xw, xa, xg, w0, wd, a0, wa, wg, k_k, k_a, seg, seg_t, dot=_dot):
    wpre = -_softplus(-(w0 + dot(jnp.tanh(xw), wd, 1, 0, "x3"))) - 0.5
    lw = -jnp.exp(wpre)
    a = _sigmoid(a0 + dot(xa, wa, 1, 0, "x3"))
    g = dot(_sigmoid(xg), wg, 1, 0, "x3")
    kk = k * k_k
    norm = jnp.sqrt(dot(kk * kk, seg, 1, 0, "mb2"))
    kkn = kk * dot(1.0 / jnp.maximum(norm, 1e-12), seg_t, 1, 0, "mb2")
    k2 = k * (1.0 + (a - 1.0) * k_a)
    return r, lw, k2, v, kkn, kkn * a, g


def _rwkv_post_math(y, r, k2, v, g, lnx_g, lnx_b, r_k, seg, seg_t, dot=_dot):
    inv = 1.0 / HEAD
    spread = lambda z: dot(dot(z, seg, 1, 0, "mb2"), seg_t, 1, 0, "mb2")
    mu = spread(y) * inv
    yc = y - mu
    var = spread(yc * yc) * inv
    yn = yc * lax.rsqrt(var + LNX_EPS) * lnx_g + lnx_b
    bonus = spread(r * k2 * r_k) * v
    return (yn + bonus) * g


@jax.custom_vjp
def _known_inverse(low, inv):
    return inv


def _known_inverse_fwd(low, inv):
    return inv, inv


def _known_inverse_bwd(inv, g):
    left = [_raw_dot(t, gi, 0, 0, "bf16") for t, gi in zip(inv, g)]
    return [_raw_dot(x, t, 1, 1, "bf16") for x, t in zip(left, inv)], [jnp.zeros_like(t) for t in inv]


_known_inverse.defvjp(_known_inverse_fwd, _known_inverse_bwd)


def _chunk_math(s0, r, lw, k, v, kk, b, known_inv=None, dot=_dot):
    n = len(r)
    hs = range(n)
    c = r[0].shape[0]
    ti = lax.broadcasted_iota(jnp.int32, (2 * c, 2 * c), 0)
    tj = lax.broadcasted_iota(jnp.int32, (2 * c, 2 * c), 1)
    tt, jj = ti & (c - 1), tj & (c - 1)
    quad = jnp.where(ti < c, (tt > jj).astype(F32), (tt >= jj).astype(F32))
    incl = quad[c:, :c]
    eye = (ti[:c, :c] == tj[:c, :c]).astype(F32)
    cl = [dot(incl, lw[i], 1, 0, "mask") for i in hs]
    ge = [jnp.exp(cl[i]) for i in hs]
    gi = [jnp.exp(-cl[i]) for i in hs]
    ar = [jnp.concatenate([-kk[i] * jnp.exp(cl[i] - lw[i]), r[i] * ge[i]], axis=0) for i in hs]
    kb = [jnp.concatenate([k[i] * gi[i], b[i] * gi[i]], axis=0) for i in hs]
    m = [dot(ar[i], kb[i], 1, 1, "x3") * quad for i in hs]
    ars0 = [dot(ar[i], s0[i], 1, 1, "bf16") for i in hs]
    mv = [dot(m[i][:c, :c], v[i], 1, 0, "bf16") for i in hs]
    pw = [m[i][:c, c:] for i in hs]
    if known_inv is None:
        inv = [eye + pw[i] for i in hs]
        for _ in range(int(math.log2(c)) - 1):
            pw = [dot(pw[i], pw[i], 1, 0, "bf16") for i in hs]
            inv = [inv[i] + dot(inv[i], pw[i], 1, 0, "bf16") for i in hs]
    else:
        inv = _known_inverse(pw, known_inv)
    u = [dot(inv[i], ars0[i][:c] + mv[i], 1, 0, "bf16") for i in hs]
    vu = [jnp.concatenate([v[i], u[i]], axis=0) for i in hs]
    y = [ars0[i][c:] + dot(m[i][c:], vu[i], 1, 0, "bf16") for i in hs]
    s1 = [(s0[i] + dot(vu[i], kb[i], 0, 0, "x3")) * ge[i][c - 1:c, :] for i in hs]
    return y, s1, inv


def _attn_math(q, kp, kc, vp, vc, bias, sinks, first, dot=_dot):
    hq = q.shape[1] // HEAD
    hkv = kc.shape[1] // HEAD
    group = hq // hkv
    cols = group * ATT_BLOCK
    kj = lax.broadcasted_iota(jnp.int32, (2 * ATT_BLOCK, cols), 0)
    qi = lax.broadcasted_iota(jnp.int32, (2 * ATT_BLOCK, cols), 1) & (ATT_BLOCK - 1)
    dist = qi + ATT_BLOCK - kj
    valid = (dist >= 0) & (dist < ATT_BLOCK) & (jnp.logical_not(first) | (kj >= ATT_BLOCK))
    eye = (lax.broadcasted_iota(jnp.int32, (ATT_BLOCK, ATT_BLOCK), 0)
           == lax.broadcasted_iota(jnp.int32, (ATT_BLOCK, ATT_BLOCK), 1)).astype(F32)
    outs = []
    for j in range(hkv):
        heads = range(j * group, (j + 1) * group)
        kband = jnp.concatenate([kp[:, j * HEAD:(j + 1) * HEAD], kc[:, j * HEAD:(j + 1) * HEAD]], axis=0)
        vband = jnp.concatenate([vp[:, j * HEAD:(j + 1) * HEAD], vc[:, j * HEAD:(j + 1) * HEAD]], axis=0)
        qg = jnp.concatenate([q[:, h * HEAD:(h + 1) * HEAD] for h in heads], axis=0)
        bias_g = jnp.concatenate([bias[h] for h in heads], axis=1)
        sink = jnp.concatenate([jnp.broadcast_to(sinks[0:1, h:h + 1], (1, ATT_BLOCK)) for h in heads], axis=1)
        s = dot(kband, qg, 1, 1, "bf16") * (HEAD ** -0.5) + bias_g
        s = jnp.where(valid, s, -1e30)
        m = jnp.maximum(jnp.max(s, axis=0, keepdims=True), sink)
        e = jnp.exp(s - m)
        p = e / (jnp.sum(e, axis=0, keepdims=True) + jnp.exp(sink - m))
        o_t = dot(vband, p, 0, 0, "bf16")
        outs += [dot(eye, o_t[:, g * ATT_BLOCK:(g + 1) * ATT_BLOCK], 1, 1, "bf16") for g in range(group)]
    return jnp.concatenate(outs, axis=1)


def _rec_specs(t, da, gh, reverse):
    nc = t // CHUNK
    if reverse:
        return pl.BlockSpec((CHUNK, gh * HEAD), lambda hg, c: (nc - 1 - c, hg))
    return pl.BlockSpec((CHUNK, gh * HEAD), lambda hg, c: (c, hg))


def _rec_fwd(r, lw, k, v, kk, b):
    t, da = r.shape
    h = da // HEAD
    gh = min(REC_HEADS, h)
    nc = t // CHUNK

    def body(r_ref, lw_ref, k_ref, v_ref, kk_ref, b_ref, y_ref, s0_ref, inv_ref, state):
        @pl.when(pl.program_id(1) == 0)
        def _():
            state[...] = jnp.zeros_like(state)

        sls = [slice(i * HEAD, (i + 1) * HEAD) for i in range(gh)]
        heads = lambda ref: [ref[:, sl] for sl in sls]
        s0 = [state[i] for i in range(gh)]
        y, s1, inv = _chunk_math(s0, heads(r_ref), heads(lw_ref), heads(k_ref), heads(v_ref), heads(kk_ref),
                                 heads(b_ref), dot=_raw_dot)
        for i, sl in enumerate(sls):
            s0_ref[0, i] = s0[i]
            inv_ref[0, i] = inv[i]
            y_ref[:, sl] = y[i]
            state[i] = s1[i]

    spec = _rec_specs(t, da, gh, False)
    per_chunk = pl.BlockSpec((1, gh, HEAD, HEAD), lambda hg, c: (c, hg, 0, 0))
    return pl.pallas_call(
        body, name="rwkv_recurrence_fwd", grid=(h // gh, nc),
        in_specs=[spec] * 6,
        out_specs=[spec, per_chunk, per_chunk],
        out_shape=[jax.ShapeDtypeStruct((t, da), F32)] + [jax.ShapeDtypeStruct((nc, h, HEAD, HEAD), F32)] * 2,
        scratch_shapes=[pltpu.VMEM((gh, HEAD, HEAD), F32)],
        compiler_params=_cparams(("parallel", "arbitrary")),
    )(r, lw, k, v, kk, b)


def _rec_bwd(r, lw, k, v, kk, b, s0s, invs, dy):
    t, da = r.shape
    h = da // HEAD
    gh = min(REC_HEADS, h)
    nc = t // CHUNK

    def body(r_ref, lw_ref, k_ref, v_ref, kk_ref, b_ref, dy_ref, s0_ref, inv_ref,
             dr_ref, dlw_ref, dk_ref, dv_ref, dkk_ref, db_ref, dstate):
        @pl.when(pl.program_id(1) == 0)
        def _():
            dstate[...] = jnp.zeros_like(dstate)

        sls = [slice(i * HEAD, (i + 1) * HEAD) for i in range(gh)]
        heads = lambda ref: [ref[:, sl] for sl in sls]
        known = [inv_ref[0, i] for i in range(gh)]
        fn = lambda *args: _chunk_math(*args, known_inv=known)[:2]
        _, vjp = jax.vjp(fn, [s0_ref[0, i] for i in range(gh)], heads(r_ref), heads(lw_ref),
                         heads(k_ref), heads(v_ref), heads(kk_ref), heads(b_ref))
        grads = vjp((heads(dy_ref), [dstate[i] for i in range(gh)]))
        for i, sl in enumerate(sls):
            dstate[i] = grads[0][i]
            for ref, val in zip((dr_ref, dlw_ref, dk_ref, dv_ref, dkk_ref, db_ref), grads[1:]):
                ref[:, sl] = val[i]

    spec = _rec_specs(t, da, gh, True)
    return pl.pallas_call(
        body, name="rwkv_recurrence_bwd", grid=(h // gh, nc),
        in_specs=[spec] * 7 + [pl.BlockSpec((1, gh, HEAD, HEAD), lambda hg, c: (nc - 1 - c, hg, 0, 0))] * 2,
        out_specs=[spec] * 6,
        out_shape=[jax.ShapeDtypeStruct((t, da), F32)] * 6,
        scratch_shapes=[pltpu.VMEM((gh, HEAD, HEAD), F32)],
        compiler_params=_cparams(("parallel", "arbitrary")),
    )(r, lw, k, v, kk, b, dy, s0s, invs)


def _attn_specs(t, hq_w, hkv_w):
    nb = t // ATT_BLOCK
    cur = lambda w: pl.BlockSpec((ATT_BLOCK, w), lambda n: (n, 0))
    prev = lambda w: pl.BlockSpec((ATT_BLOCK, w), lambda n: (jnp.maximum(n - 1, 0), 0))
    return nb, cur, prev


def _attn_fwd(q, kb, vb, bias, sinks):
    t, qw = q.shape
    kw = kb.shape[1]
    nb, cur, prev = _attn_specs(t, qw, kw)

    def body(q_ref, kp_ref, kc_ref, vp_ref, vc_ref, bias_ref, sink_ref, o_ref):
        first = pl.program_id(0) == 0
        o = _attn_math(q_ref[...], kp_ref[...], kc_ref[...], vp_ref[...], vc_ref[...],
                       bias_ref[...], sink_ref[...], first, dot=_raw_dot)
        o_ref[...] = o.astype(o_ref.dtype)

    full = lambda a: pl.BlockSpec(a.shape, lambda n, nd=a.ndim: (0,) * nd)
    return pl.pallas_call(
        body, name="swa_attention_fwd", grid=(nb,),
        in_specs=[cur(qw), prev(kw), cur(kw), prev(kw), cur(kw), full(bias), full(sinks)],
        out_specs=cur(qw), out_shape=jax.ShapeDtypeStruct((t, qw), BF16),
        compiler_params=_cparams(("parallel",)),
    )(q, kb, kb, vb, vb, bias, sinks)


def _attn_bwd(q, kb, vb, bias, sinks, do, col_block):
    t, qw = q.shape
    kw = kb.shape[1]
    nb, cur, prev = _attn_specs(t, qw, kw)

    def body(q_ref, kp_ref, kc_ref, vp_ref, vc_ref, bias_ref, sink_ref, do_ref,
             dq_ref, dkp_ref, dkc_ref, dvp_ref, dvc_ref, dbias_ref, dsink_ref):
        n = pl.program_id(0)
        first = n == 0
        fn = functools.partial(_attn_math, first=first)
        _, vjp = jax.vjp(fn, q_ref[...], kp_ref[...], kc_ref[...], vp_ref[...], vc_ref[...],
                         bias_ref[...], sink_ref[...])
        dq, dkp, dkc, dvp, dvc, dbias, dsink = vjp(do_ref[...].astype(F32))
        dq_ref[...] = dq.astype(dq_ref.dtype)
        dkp_ref[...] = dkp
        dkc_ref[...] = dkc
        dvp_ref[...] = dvp
        dvc_ref[...] = dvc

        @pl.when(first)
        def _():
            dbias_ref[...] = jnp.zeros_like(dbias_ref)
            dsink_ref[...] = jnp.zeros_like(dsink_ref)

        dbias_ref[...] += dbias
        dsink_ref[...] += dsink

    full = lambda a: pl.BlockSpec(a.shape, lambda n, nd=a.ndim: (0,) * nd)
    kshape = jax.ShapeDtypeStruct((t, kw), F32)
    return pl.pallas_call(
        body, name="swa_attention_bwd", grid=(nb,),
        in_specs=[cur(qw), prev(kw), cur(kw), prev(kw), cur(kw), full(bias), full(sinks),
                  pl.BlockSpec((ATT_BLOCK, qw), lambda n: (n, col_block))],
        out_specs=[cur(qw), cur(kw), cur(kw), cur(kw), cur(kw), full(bias), full(sinks)],
        out_shape=[jax.ShapeDtypeStruct((t, qw), BF16), kshape, kshape, kshape, kshape,
                   jax.ShapeDtypeStruct(bias.shape, F32), jax.ShapeDtypeStruct(sinks.shape, F32)],
        compiler_params=_cparams(("arbitrary",)),
    )(q, kb, kb, vb, vb, bias, sinks, do)


def _bucket_onehot():
    qi = jnp.arange(ATT_BLOCK)[None, :]
    kj = jnp.arange(2 * ATT_BLOCK)[:, None]
    n = jnp.maximum(qi + ATT_BLOCK - kj, 0)
    buckets, max_exact = 32, 16
    nf = jnp.maximum(n, 1).astype(F32)
    large = max_exact + (jnp.log(nf / max_exact) / math.log(RPB_MAX_DIST / max_exact)
                         * (buckets - max_exact)).astype(jnp.int32)
    bucket = jnp.where(n < max_exact, n, jnp.minimum(large, buckets - 1)).reshape(-1)
    return (bucket[None, :] == jnp.arange(buckets)[:, None]).astype(F32)


def _small_dot(name, a, b, ca, cb):
    m = a.shape[1 - ca]
    n = b.shape[1 - cb]

    def body(a_ref, b_ref, o_ref):
        o_ref[...] = _raw_dot(a_ref[...], b_ref[...], ca, cb, "hi")

    return pl.pallas_call(body, name=name, out_shape=jax.ShapeDtypeStruct((m, n), F32),
                          compiler_params=_cparams())(a, b)


def _mod_fwd(c_all, w_mod):
    d, n = w_mod.shape
    tn = _tile(n, 512)

    def body(c_ref, w_ref, o_ref, cond_ref):
        cv = c_ref[...]
        cond = cv * _sigmoid(cv)
        cond_ref[...] = cond
        o_ref[...] = _raw_dot(cond, w_ref[...], 1, 0, "hi")

    return pl.pallas_call(
        body, name="adaln_mod_fwd", grid=(n // tn,),
        in_specs=[pl.BlockSpec(c_all.shape, lambda j: (0, 0)), pl.BlockSpec((d, tn), lambda j: (0, j))],
        out_specs=[pl.BlockSpec((c_all.shape[0], tn), lambda j: (0, j)),
                   pl.BlockSpec(c_all.shape, lambda j: (0, 0))],
        out_shape=[jax.ShapeDtypeStruct((c_all.shape[0], n), F32), jax.ShapeDtypeStruct(c_all.shape, F32)],
        compiler_params=_cparams(("arbitrary",)),
    )(c_all, w_mod)


def _adam_math(w, g, m, v):
    m = ADAM_B1 * m + (1.0 - ADAM_B1) * g
    v = ADAM_B2 * v + (1.0 - ADAM_B2) * (g * g)
    m_hat = m / (1.0 - ADAM_B1 ** ADAM_STEP)
    v_hat = v / (1.0 - ADAM_B2 ** ADAM_STEP)
    delta = -ADAM_LR * (m_hat / (jnp.sqrt(v_hat) + ADAM_EPS) + ADAM_WD * w)
    return delta, m, v


def _adamw(name, w, m, v, gparts):
    r, c = w.shape
    p = gparts.shape[0]
    tr = _rtile(r, max(8, (1 << 18) // max(c, 1) // 8 * 8))

    def body(w_ref, m_ref, v_ref, g_ref, go_ref, d_ref, mo_ref, vo_ref):
        g = g_ref[0].astype(F32)
        for s in range(1, p):
            g = g + g_ref[s].astype(F32)
        delta, mn, vn = _adam_math(w_ref[...], g, m_ref[...], v_ref[...])
        go_ref[...] = g
        d_ref[...] = delta
        mo_ref[...] = mn
        vo_ref[...] = vn

    spec = pl.BlockSpec((tr, c), lambda i: (i, 0))
    return pl.pallas_call(
        body, name=name, grid=(r // tr,),
        in_specs=[spec, spec, spec, pl.BlockSpec((p, tr, c), lambda i: (0, i, 0))],
        out_specs=[spec] * 4, out_shape=[jax.ShapeDtypeStruct((r, c), F32)] * 4,
        compiler_params=_cparams(("parallel",)),
    )(w, m, v, gparts)


def _adamw_outer(name, w, m, v, cond_t, dmod):
    d, n = w.shape
    tr, tn = _rtile(d, 512), _tile(n, 1024)

    def body(w_ref, m_ref, v_ref, c_ref, dm_ref, go_ref, d_ref, mo_ref, vo_ref):
        g = _raw_dot(c_ref[...], dm_ref[...], 1, 0, "hi")
        delta, mn, vn = _adam_math(w_ref[...], g, m_ref[...], v_ref[...])
        go_ref[...] = g
        d_ref[...] = delta
        mo_ref[...] = mn
        vo_ref[...] = vn

    spec = pl.BlockSpec((tr, tn), lambda i, j: (i, j))
    return pl.pallas_call(
        body, name=name, grid=(d // tr, n // tn),
        in_specs=[spec, spec, spec, pl.BlockSpec((tr, cond_t.shape[1]), lambda i, j: (i, 0)),
                  pl.BlockSpec((dmod.shape[0], tn), lambda i, j: (0, j))],
        out_specs=[spec] * 4, out_shape=[jax.ShapeDtypeStruct((d, n), F32)] * 4,
        compiler_params=_cparams(("parallel", "parallel")),
    )(w, m, v, cond_t, dmod)


def _all_gather(name, arrays):
    n = len(arrays)

    def body(*refs):
        ins, outs = refs[:n], refs[n:2 * n]
        send_sems, recv_sems, local_sems = refs[2 * n:]
        x, y, c = lax.axis_index("x"), lax.axis_index("y"), lax.axis_index("c")
        me, sibling = (x, y, c), (x, y, 1 - c)
        chips = [(1 - x, y), (x, 1 - y), (1 - x, 1 - y)]

        def copy(a, k, block, to, src=None):
            rows = outs[a].at[4 * block[0] + 2 * block[1] + block[2]]
            return pltpu.make_async_remote_copy(
                src_ref=rows if src is None else src, dst_ref=rows, send_sem=send_sems.at[a, k],
                recv_sem=recv_sems.at[a, k], device_id=to, device_id_type=pl.DeviceIdType.MESH)

        mine = [pltpu.make_async_copy(ins[a], outs[a].at[4 * x + 2 * y + c], local_sems.at[a]) for a in range(n)]
        for cp in mine:
            cp.start()
        first = []
        for a in range(n):
            first.append(copy(a, 0, me, sibling, src=ins[a]))
            first += [copy(a, 1 + j, me, (*chip, c), src=ins[a]) for j, chip in enumerate(chips)]
        for cp in first:
            cp.start()
        passed = []
        for a in range(n):
            for j, chip in enumerate(chips):
                copy(a, 1 + j, (*chip, c), me).wait_recv()
                passed.append(copy(a, 4 + j, (*chip, c), sibling))
                passed[-1].start()
        for a in range(n):
            copy(a, 0, sibling, me).wait_recv()
            for j, chip in enumerate(chips):
                copy(a, 4 + j, (*chip, 1 - c), me).wait_recv()
        for cp in first + passed:
            cp.wait_send()
        for cp in mine:
            cp.wait()

    any_spec = pl.BlockSpec(memory_space=pl.ANY)
    return pl.pallas_call(
        body, name=name, in_specs=[any_spec] * n, out_specs=[any_spec] * n,
        out_shape=[jax.ShapeDtypeStruct((N_DEV,) + a.shape, a.dtype) for a in arrays],
        scratch_shapes=[pltpu.SemaphoreType.DMA((n, N_DEV - 1)), pltpu.SemaphoreType.DMA((n, N_DEV - 1)),
                        pltpu.SemaphoreType.DMA((n,))],
    )(*arrays)


def _peer(p):
    x, y, c = lax.axis_index("x"), lax.axis_index("y"), lax.axis_index("c")
    px, py, pc = x ^ ((p >> 2) & 1), y ^ ((p >> 1) & 1), c ^ (p & 1)
    return (px, py, pc), 4 * px + 2 * py + pc


def _block(ref, d, cols, rows=None):
    if cols:
        r = slice(None) if rows is None else pl.ds(pl.multiple_of(rows[0], 8), rows[1])
        return ref.at[r, pl.ds(pl.multiple_of(d * cols, cols), cols)]
    return ref.at[d] if rows is None else ref.at[d, pl.ds(pl.multiple_of(rows[0], 8), rows[1])]


def _split_copy(src_ref, land_ref, send_sems, recv_sems, p, scatter, arriving, cols=None, halves=False):
    x, y, c = lax.axis_index("x"), lax.axis_index("y"), lax.axis_index("c")
    me = 4 * x + 2 * y + c
    dev, idx = _peer(p)
    if scatter == "chips":
        src, dst = src_ref.at[idx >> 1], land_ref.at[(idx if arriving else me) >> 1]
    elif scatter:
        src, dst = _block(src_ref, idx, cols), land_ref.at[idx if arriving else me]
    elif halves and p >= 2:
        half = src_ref.shape[0] // 2
        rows = ((c if arriving else dev[2]) * half, half)
        src = src_ref.at[pl.ds(pl.multiple_of(rows[0], 8), half)]
        dst = _block(land_ref, idx if arriving else me, cols, rows)
    else:
        src, dst = src_ref, _block(land_ref, idx if arriving else me, cols)
    return pltpu.make_async_remote_copy(
        src_ref=src, dst_ref=dst, send_sem=send_sems.at[p - 1], recv_sem=recv_sems.at[p - 1], device_id=dev,
        device_id_type=pl.DeviceIdType.MESH)


_HBM_SPEC = pl.BlockSpec(memory_space=pltpu.HBM)
_SEM_SPEC = pl.BlockSpec(memory_space=pltpu.SEMAPHORE)
_DATAFLOW = pltpu.SideEffectType.DATAFLOW_SIDE_EFFECTING


def _own_copy(src_ref, land_ref, send_sems, scatter, cols):
    me = 4 * lax.axis_index("x") + 2 * lax.axis_index("y") + lax.axis_index("c")
    if scatter == "chips":
        src, dst = src_ref.at[me >> 1], land_ref.at[me >> 1]
    elif scatter:
        src, dst = _block(src_ref, me, cols), land_ref.at[me]
    else:
        src, dst = src_ref, _block(land_ref, me, cols)
    return pltpu.make_async_copy(src, dst, send_sems.at[N_DEV - 1])


def _peers(scatter):
    return (2, 4, 6) if scatter == "chips" else tuple(range(1, N_DEV))


def _exchange_start(name, srcs, lands, scatter, after, cols=None, halves=False):
    n = len(srcs)
    cols = cols or [None] * n

    def body(*refs):
        src_refs, land_refs = refs[:n], refs[n:2 * n]
        outs = refs[2 * n + 1:]
        send, recv, token = outs[:n], outs[n:2 * n], outs[-1]
        for a in range(n):
            for p in _peers(scatter):
                _split_copy(src_refs[a], land_refs[a], send[a], recv[a], p, scatter, False, cols[a],
                            halves).start()
            _own_copy(src_refs[a], land_refs[a], send[a], scatter, cols[a]).start()
        token[...] = jnp.zeros_like(token)

    sems = [pltpu.SemaphoreType.DMA((N_DEV,))] * n + [pltpu.SemaphoreType.DMA((N_DEV - 1,))] * n
    hbm = [pltpu.HBM(a.shape, a.dtype) for a in list(srcs) + list(lands)]
    res = pl.pallas_call(
        body, name=name,
        out_shape=sems + hbm + [jax.ShapeDtypeStruct((8, 128), F32)],
        in_specs=[_HBM_SPEC] * (2 * n) + [pl.BlockSpec(memory_space=pl.ANY)],
        out_specs=[_SEM_SPEC] * (2 * n) + [_HBM_SPEC] * (2 * n) + [pl.BlockSpec(memory_space=pltpu.VMEM)],
        input_output_aliases={i: 2 * n + i for i in range(2 * n)},
        compiler_params=pltpu.CompilerParams(has_side_effects=_DATAFLOW),
    )(*[pltpu.with_memory_space_constraint(a, pltpu.HBM) for a in list(srcs) + list(lands)], after)
    return res[:n], res[n:2 * n], res[2 * n:3 * n], res[3 * n:4 * n], res[-1]


def _exchange_wait(name, send_sem, recv_sem, src, land, scatter, after, cols=None, halves=False):
    def body(src_ref, land_ref, send, recv, after_ref, src_out, land_out):
        for p in _peers(scatter):
            cp = _split_copy(src_ref, land_ref, send, recv, p, scatter, True, cols, halves)
            cp.wait_send()
            cp.wait_recv()
        _own_copy(src_ref, land_ref, send, scatter, cols).wait()

    return pl.pallas_call(
        body, name=name,
        out_shape=(pltpu.HBM(src.shape, src.dtype), pltpu.HBM(land.shape, land.dtype)),
        in_specs=[_HBM_SPEC, _HBM_SPEC, _SEM_SPEC, _SEM_SPEC, pl.BlockSpec(memory_space=pl.ANY)],
        out_specs=(_HBM_SPEC, _HBM_SPEC), input_output_aliases={0: 0, 1: 1},
        compiler_params=pltpu.CompilerParams(has_side_effects=_DATAFLOW),
    )(src, land, send_sem, recv_sem, after)[1]


def _sibling_fill(name, land, cols):
    rows = land.shape[0] if cols else land.shape[1]
    half = rows // 2

    def body(in_ref, out_ref, send_sems, recv_sems):
        x, y, c = lax.axis_index("x"), lax.axis_index("y"), lax.axis_index("c")

        def copy(p, core):
            _, idx = _peer(p)
            return pltpu.make_async_remote_copy(
                src_ref=_block(in_ref, idx, cols, (core * half, half)),
                dst_ref=_block(out_ref, idx, cols, (core * half, half)),
                send_sem=send_sems.at[idx], recv_sem=recv_sems.at[idx], device_id=(x, y, 1 - c),
                device_id_type=pl.DeviceIdType.MESH)

        sends = [copy(p, c) for p in range(2, N_DEV)]
        for cp in sends:
            cp.start()
        for p in range(2, N_DEV):
            copy(p, 1 - c).wait_recv()
        for cp in sends:
            cp.wait_send()

    any_spec = pl.BlockSpec(memory_space=pl.ANY)
    return pl.pallas_call(
        body, name=name, in_specs=[any_spec], out_specs=any_spec,
        out_shape=jax.ShapeDtypeStruct(land.shape, land.dtype), input_output_aliases={0: 0},
        scratch_shapes=[pltpu.SemaphoreType.DMA((N_DEV,)), pltpu.SemaphoreType.DMA((N_DEV,))],
    )(land)


def _pair_exchange(name, arrays):
    n = len(arrays)

    def body(*refs):
        ins, outs = refs[:n], refs[n:2 * n]
        send_sems, recv_sems = refs[2 * n:]
        x, y, c = lax.axis_index("x"), lax.axis_index("y"), lax.axis_index("c")
        copies = [pltpu.make_async_remote_copy(
            src_ref=ins[a].at[2 * k + (1 - c)], dst_ref=outs[a].at[k], send_sem=send_sems.at[a, k],
            recv_sem=recv_sems.at[a, k], device_id=(x, y, 1 - c), device_id_type=pl.DeviceIdType.MESH)
            for a in range(n) for k in range(N_DEV // 2)]
        for cp in copies:
            cp.start()
        for cp in copies:
            cp.wait()

    any_spec = pl.BlockSpec(memory_space=pl.ANY)
    return pl.pallas_call(
        body, name=name, in_specs=[any_spec] * n, out_specs=[any_spec] * n,
        out_shape=[jax.ShapeDtypeStruct((N_DEV // 2,) + a.shape[1:], a.dtype) for a in arrays],
        scratch_shapes=[pltpu.SemaphoreType.DMA((n, N_DEV // 2)), pltpu.SemaphoreType.DMA((n, N_DEV // 2))],
    )(*arrays)


def _pair_sum(name, mine, theirs):
    _, r, c_ = mine.shape
    tr = _rtile(r, max(8, (1 << 19) // c_ // 8 * 8))

    def body(mine_ref, theirs_ref, o_ref):
        core = lax.axis_index("c")
        o_ref[0] = (mine_ref[0, core].astype(F32) + theirs_ref[0].astype(F32)).astype(o_ref.dtype)

    return pl.pallas_call(
        body, name=name, grid=(N_DEV // 2, r // tr),
        in_specs=[pl.BlockSpec((1, 2, tr, c_), lambda k, i: (k, 0, i, 0)),
                  pl.BlockSpec((1, tr, c_), lambda k, i: (k, i, 0))],
        out_specs=pl.BlockSpec((1, tr, c_), lambda k, i: (k, i, 0)),
        out_shape=jax.ShapeDtypeStruct((N_DEV // 2, r, c_), mine.dtype),
        compiler_params=_cparams(("parallel", "parallel")),
    )(mine.reshape(N_DEV // 2, 2, r, c_), theirs)


def _cols_to_shards(a):
    r, c = a.shape
    return a.reshape(r, N_DEV, c // N_DEV).transpose(1, 0, 2)


def _shards_to_cols(a):
    d, r, n = a.shape
    return a.transpose(1, 0, 2).reshape(r, d * n)


def kernel(x, c, ln_emb_g, ln_emb_b, rpb_table, w_mod, b_mod, w_in, mu_shift, w0, w_decay_up, a0, w_iclr_up, w_gate_up, k_k, k_a, r_k, lnx_g, lnx_b, attn_sinks, w_out, ln1_g, ln1_b, w_up, w_down, ln2_g, ln2_b, loss_target, m_ln_emb_g, m_ln_emb_b, m_rpb_table, m_w_mod, m_b_mod, m_w_in, m_mu_shift, m_w0, m_w_decay_up, m_a0, m_w_iclr_up, m_w_gate_up, m_k_k, m_k_a, m_r_k, m_lnx_g, m_lnx_b, m_attn_sinks, m_w_out, m_ln1_g, m_ln1_b, m_w_up, m_w_down, m_ln2_g, m_ln2_b, v_ln_emb_g, v_ln_emb_b, v_rpb_table, v_w_mod, v_b_mod, v_w_in, v_mu_shift, v_w0, v_w_decay_up, v_a0, v_w_iclr_up, v_w_gate_up, v_k_k, v_k_a, v_r_k, v_lnx_g, v_lnx_b, v_attn_sinks, v_w_out, v_ln1_g, v_ln1_b, v_w_up, v_w_down, v_ln2_g, v_ln2_b):
    names = ["ln_emb_g", "ln_emb_b", "rpb_table", "w_mod", "b_mod", "w_in", "mu_shift", "w0", "w_decay_up",
             "a0", "w_iclr_up", "w_gate_up", "k_k", "k_a", "r_k", "lnx_g", "lnx_b", "attn_sinks", "w_out",
             "ln1_g", "ln1_b", "w_up", "w_down", "ln2_g", "ln2_b"]
    env = dict(locals())
    weights = {nm: env[nm] for nm in names}
    mom_m = {nm: env["m_" + nm] for nm in names}
    mom_v = {nm: env["v_" + nm] for nm in names}

    t, d = x.shape[1], x.shape[2]
    da = d // 2
    h_a = da // HEAD
    hq = (d - da) // HEAD
    hkv = hq // GQA_RATIO
    l_w, l_a, l_g = w_decay_up.shape[1], w_iclr_up.shape[1], w_gate_up.shape[1]
    o_w, o_a, o_g = 3 * da, 3 * da + l_w, 3 * da + l_w + l_a
    n_rwkv = o_g + l_g
    o_kb, o_vb = n_rwkv + hq * HEAD, n_rwkv + hq * HEAD + hkv * HEAD
    me = 4 * lax.axis_index("x") + 2 * lax.axis_index("y") + lax.axis_index("c")

    x2d, tgt = x[0], loss_target[0]
    row = lambda a: a.reshape(1, -1)
    seg = (jnp.arange(da)[:, None] // HEAD == jnp.arange(h_a)[None, :]).astype(F32)
    seg_t = seg.T

    (c_all,) = _all_gather("gather_cond", [c])
    c_all = c_all.reshape(N_DEV, d)
    mod_rows, cond_all = _mod_fwd(c_all, w_mod[0])
    gathered = _all_gather("gather_weights", [
        mod_rows, w_in[0].astype(BF16).T, w_decay_up[0], w_iclr_up[0], w_gate_up[0]])
    mod_all, win_g, wd_g, wa_g, wg_g = gathered
    late = [w_out[0].astype(BF16), w_up[0].astype(BF16), w_down[0].astype(BF16)]
    n_up = w_up.shape[2]
    late_cols = [None, n_up, None]
    late_lands = [lax.empty((N_DEV,) + late[0].shape, BF16), lax.empty((d, N_DEV * n_up), BF16),
                  lax.empty((N_DEV,) + late[2].shape, BF16)]
    late_send, late_recv, late_src, late_land, late_token = _exchange_start(
        "gather_late_weights_start", late, late_lands, False, mod_all, late_cols, halves=True)
    mod = lax.dynamic_index_in_dim(mod_all, me, axis=1, keepdims=False).reshape(1, -1) + b_mod
    mod = mod + late_token[0, 0]
    sh1, sc1, g1, sh2, sc2, g2 = [mod[:, i * d:(i + 1) * d] for i in range(6)]
    w_in_t = win_g.reshape(-1, d)
    wd_f, wa_f, wg_f = _shards_to_cols(wd_g), _shards_to_cols(wa_g), _shards_to_cols(wg_g)

    def late_weight(i, nm, after):
        land = _exchange_wait("gather_" + nm + "_wait", late_send[i], late_recv[i], late_src[i], late_land[i],
                              False, after, late_cols[i], halves=True)
        return _sibling_fill("gather_" + nm + "_fill", land, late_cols[i])

    tr = _rtile(t, 256)
    lng, lnb = row(ln_emb_g), row(ln_emb_b)

    def embed_fn(i, nb, xb, g, b, sc, sh):
        return _embed_math(xb, g, b, sc, sh), ()
    x0, u1 = _rowwise("embed_ln_mod", embed_fn, [x2d], [lng, lnb, sc1, sh1], [(d, F32), (d, BF16)], [], tr)

    (p_rkv,) = _matmul("in_proj_rkv", u1, w_in_t[:o_w], "nt", [F32])
    (p_lora,) = _matmul("in_proj_lora", u1, w_in_t[o_w:n_rwkv], "nt", [F32])
    (q,) = _matmul("in_proj_q", u1, w_in_t[n_rwkv:o_kb], "nt", [F32])
    (p_kv,) = _matmul("in_proj_kv", u1, w_in_t[o_kb:], "nt", [F32])
    p_w, p_a, p_g = p_lora[:, :l_w], p_lora[:, l_w:l_w + l_a], p_lora[:, l_w + l_a:]
    kb, vb = p_kv[:, :hkv * HEAD], p_kv[:, hkv * HEAD:]
    mu_rkv, mu_w, mu_a, mu_g = (mu_shift[:, :o_w], mu_shift[:, o_w:o_a], mu_shift[:, o_a:o_g],
                                mu_shift[:, o_g:n_rwkv])
    pre_params = [w0, wd_f, a0, wa_f, wg_f, k_k, k_a, seg, seg_t]
    tr_pre = _rtile(t, 128)

    def shifted(i, blocks, halos, mus):
        return [xb + (_shift_prev(xb, hb, i) - xb) * mb for xb, hb, mb in zip(blocks, halos, mus)]

    def split3(a):
        return a[:, :da], a[:, da:2 * da], a[:, 2 * da:]

    def pre_fn(i, nb, b_rkv, b_w, b_a, b_g, h_rkv, h_w, h_a_, h_g, m_rkv, m_w, m_a, m_g, *params):
        s_rkv, s_w, s_a, s_g = shifted(i, [b_rkv, b_w, b_a, b_g], [h_rkv, h_w, h_a_, h_g],
                                       [m_rkv, m_w, m_a, m_g])
        return _rwkv_pre_math(*split3(s_rkv), s_w, s_a, s_g, *params, dot=_raw_dot), ()

    pre_rows = [p_rkv, p_w, p_a, p_g]
    pre_halos = [(a, "prev") for a in pre_rows]
    r_, lw_, k2_, v_, kk_, b_, gate_ = _rowwise(
        "rwkv_pre", pre_fn, pre_rows, [mu_rkv, mu_w, mu_a, mu_g] + pre_params,
        [(da, F32)] * 7, [], tr_pre, halos=pre_halos)

    y_rec, s0s, invs = _rec_fwd(r_, lw_, k2_, v_, kk_, b_)

    rk_flat = r_k.reshape(1, da)
    post_params = [lnx_g, lnx_b, rk_flat, seg, seg_t]

    def post_fn(i, nb, yb, rb, kb_, vb_, gb, *params):
        return (_rwkv_post_math(yb, rb, kb_, vb_, gb, *params, dot=_raw_dot),), ()
    (ya,) = _rowwise("rwkv_post", post_fn, [y_rec, r_, k2_, v_, gate_], post_params, [(da, BF16)], [], tr_pre)

    onehot = _bucket_onehot()
    bias = _small_dot("rpb_gather", rpb_table, onehot, 0, 0)
    bias = bias.reshape(hq, 2 * ATT_BLOCK, ATT_BLOCK)
    yb = _attn_fwd(q, kb, vb, bias, attn_sinks)

    mix_in = jnp.concatenate([ya, yb], axis=1)
    w_out_f = late_weight(0, "w_out", mix_in).reshape(d, d)
    (mix,) = _matmul("out_proj", mix_in, w_out_f, "nn", [F32])

    def post1_fn(i, nb, xin, yv, gate, g, b, sc, sh):
        return _post_math(xin, yv, gate, g, b, sc, sh), ()
    x1, u2 = _rowwise("ln1_mod", post1_fn, [x0, mix], [g1, ln1_g, ln1_b, sc2, sh2],
                      [(d, F32), (d, BF16)], [], tr)

    def relu2(acc):
        rl = jnp.maximum(acc, 0.0)
        return acc, rl * rl
    w_up_f = late_weight(1, "w_up", u2)
    hpre, hact = _matmul("mlp_up", u2, w_up_f, "nn", [F32, BF16], epilogue=relu2)
    w_down_f = late_weight(2, "w_down", hact).reshape(-1, d)
    (hmlp,) = _matmul("mlp_down", hact, w_down_f, "nn", [F32])

    def loss_fn(i, nb, xin, hv, tg, gate, g, b):
        val, vjp = jax.vjp(_loss_math, xin, hv, tg, gate, g, b)
        dxin, dh, _, dgate, dg, db = vjp(jnp.ones((), F32))
        return (dxin, dh), (val, dgate, dg, db)
    dx1, dh, loss_acc, dg2, dln2g, dln2b = _rowwise(
        "ln2_loss", loss_fn, [x1, hmlp, tgt], [g2, ln2_g, ln2_b], [(d, F32), (d, BF16)],
        [(1, 1), (1, d), (1, d), (1, d)], _rtile(t, 128))

    def drelu2(acc, hp):
        return (acc * 2.0 * jnp.maximum(hp, 0.0),)
    (dhpre,) = _matmul("mlp_down_dgrad", dh, w_down_f, "nt", [BF16], epilogue=drelu2, extras=[hpre])
    (gw_down,) = _matmul("mlp_down_wgrad", hact, dh, "tn", [BF16])
    (du2,) = _matmul("mlp_up_dgrad", dhpre, w_up_f, "nt", [F32])
    (gw_up,) = _matmul("mlp_up_wgrad", u2, dhpre, "tn", [BF16])

    def landing(src):
        return lax.empty(src.shape, src.dtype)
    mlp_src = [gw_down.reshape(N_DEV, -1, d), gw_up]
    mlp_cols = [None, n_up]
    mlp_send, mlp_recv, mlp_src, mlp_land, mlp_token = _exchange_start(
        "scatter_mlp_grads_start", mlp_src, [landing(mlp_src[0]), lax.empty((N_DEV, d, n_up), BF16)], True,
        gw_up, mlp_cols)

    def post1_bwd(i, nb, xin, yv, dx1v, du2v, gate, g, b, sc, sh):
        _, vjp = jax.vjp(_post_math, xin, yv, gate, g, b, sc, sh)
        dxin, dy, dgate, dg, db, dsc, dsh = vjp((dx1v, du2v))
        return (dxin, dy), (dgate, dg, db, dsc, dsh)
    dx0, dmix, dg1, dln1g, dln1b, dsc2, dsh2 = _rowwise(
        "ln1_mod_bwd", post1_bwd, [x0, mix, dx1, du2], [g1 + mlp_token[0, 0], ln1_g, ln1_b, sc2, sh2],
        [(d, F32), (d, BF16)], [(1, d)] * 5, _rtile(t, 128))

    (dmix_in,) = _matmul("out_proj_dgrad", dmix, w_out_f, "nt", [F32])
    (gw_out,) = _matmul("out_proj_wgrad", mix_in, dmix, "tn", [BF16])
    dya = dmix_in[:, :da]
    out_src = [gw_out.reshape(N_DEV, d // N_DEV, d)]
    out_send, out_recv, out_src, out_land, out_token = _exchange_start(
        "scatter_out_grad_start", out_src, [landing(a) for a in out_src], True, gw_out)
    post_params_bwd = [lnx_g, lnx_b, rk_flat + out_token[0, 0], seg, seg_t]

    def post_bwd(i, nb, yb_, rb, kb_, vb_, gb, dyab, *params):
        _, vjp = jax.vjp(_rwkv_post_math, yb_, rb, kb_, vb_, gb, *params)
        dy, dr, dk, dv, dg, dlg, dlb, drk, _, _ = vjp(dyab)
        return (dy, dr, dk, dv, dg), (dlg, dlb, drk)
    dy_rec, dr_e, dk_e, dv_e, dgate, dlnxg, dlnxb, drk = _rowwise(
        "rwkv_post_bwd", post_bwd, [y_rec, r_, k2_, v_, gate_, dya], post_params_bwd,
        [(da, F32)] * 5, [(1, da)] * 3, tr_pre)

    dr_r, dlw_r, dk_r, dv_r, dkk_r, db_r = _rec_bwd(r_, lw_, k2_, v_, kk_, b_, s0s, invs, dy_rec)

    def pre_bwd(i, nb, b_rkv, b_w, b_a, b_g, dr1, dr2, dlw, dk1, dk2, dv1, dv2, dkk, dbb, dgt,
                h_rkv, h_w, h_a_, h_g, m_rkv, m_w, m_a, m_g, *params):
        blocks = [b_rkv, b_w, b_a, b_g]
        prevs = [_shift_prev(xb, hb, i) for xb, hb in zip(blocks, [h_rkv, h_w, h_a_, h_g])]
        mus = [m_rkv, m_w, m_a, m_g]
        s_rkv, s_w, s_a, s_g = [xb + (pb - xb) * mb for xb, pb, mb in zip(blocks, prevs, mus)]
        _, vjp = jax.vjp(_rwkv_pre_math, *split3(s_rkv), s_w, s_a, s_g, *params)
        grads = vjp((dr1 + dr2, dlw, dk1 + dk2, dv1 + dv2, dkk, dbb, dgt))
        g_rkv = jnp.concatenate(grads[:3], axis=1)
        g_w, g_a, g_g = grads[3:6]
        dmu = [jnp.sum(gs * (pb - xb), axis=0, keepdims=True)
               for gs, pb, xb in zip([g_rkv, g_w, g_a, g_g], prevs, blocks)]
        dw0, dwd, da0, dwa, dwg, dkk_, dka = grads[6:13]
        return (g_rkv, g_w, g_a, g_g), (*dmu, dw0, dwd, da0, dwa, dwg, dkk_, dka)

    pre_out = _rowwise(
        "rwkv_pre_bwd", pre_bwd,
        pre_rows + [dr_r, dr_e, dlw_r, dk_r, dk_e, dv_r, dv_e, dkk_r, db_r, dgate],
        [mu_rkv, mu_w, mu_a, mu_g] + pre_params,
        [(o_w, F32), (l_w, F32), (l_a, F32), (l_g, F32)],
        [(1, o_w), (1, l_w), (1, l_a), (1, l_g), (1, da), (l_w, da), (1, da), (l_a, da), (l_g, da),
         (1, da), (1, da)],
        _rtile(t, 64), halos=pre_halos)
    gs_rkv, gs_w, gs_a, gs_g = pre_out[:4]
    dmu_parts = pre_out[4:8]
    dw0, dwd, da0, dwa, dwg, dk_k, dk_a = pre_out[8:]

    def unshift_fn(i, nb, a1, a2, a3, a4, n1, n2, n3, n4, m1, m2, m3, m4):
        outs = [gs * (1.0 - mb) + _shift_next(gs * mb, hb * mb, i, nb)
                for gs, hb, mb in zip([a1, a2, a3, a4], [n1, n2, n3, n4], [m1, m2, m3, m4])]
        return outs, ()
    gs_list = [gs_rkv, gs_w, gs_a, gs_g]
    dp_rkv, dp_w, dp_a, dp_g = _rowwise(
        "token_shift_bwd", unshift_fn, gs_list, [mu_rkv, mu_w, mu_a, mu_g],
        [(o_w, BF16), (l_w, BF16), (l_a, BF16), (l_g, BF16)], [], tr_pre,
        halos=[(a, "next") for a in gs_list])

    dq, dkp, dkc, dvp, dvc, dbias, dsinks = _attn_bwd(q, kb, vb, bias, attn_sinks, dmix_in, 1)
    zpad = jnp.zeros((ATT_BLOCK, kb.shape[1]), F32)
    dkb = (dkc + jnp.concatenate([dkp[ATT_BLOCK:], zpad], axis=0)).astype(BF16)
    dvb = (dvc + jnp.concatenate([dvp[ATT_BLOCK:], zpad], axis=0)).astype(BF16)
    d_rpb = _small_dot("rpb_scatter", onehot, dbias.reshape(hq, -1), 1, 1)

    dp = jnp.concatenate([dp_rkv, dp_w, dp_a, dp_g, dq, dkb, dvb], axis=1)
    (du1,) = _matmul("in_proj_dgrad", dp, w_in_t, "nn", [F32])

    def embed_bwd(i, nb, xb, dx0v, du1v, g, b, sc, sh):
        _, vjp = jax.vjp(_embed_math, xb, g, b, sc, sh)
        dxv, dg, db, dsc, dsh = vjp((dx0v, du1v))
        return (dxv,), (dg, db, dsc, dsh)
    grad_x, dlng, dlnb, dsc1, dsh1 = _rowwise(
        "embed_ln_mod_bwd", embed_bwd, [x2d, dx0, du1], [lng, lnb, sc1, sh1], [(d, F32)], [(1, d)] * 4,
        _rtile(t, 128))

    dmod = jnp.concatenate([dsh1, dsc1, dg1, dsh2, dsc2, dg2], axis=1)
    small = {"ln_emb_g": dlng, "ln_emb_b": dlnb, "rpb_table": d_rpb, "b_mod": dmod,
             "mu_shift": jnp.concatenate(dmu_parts, axis=1), "w0": dw0, "a0": da0, "k_k": dk_k, "k_a": dk_a,
             "r_k": drk, "lnx_g": dlnxg, "lnx_b": dlnxb, "attn_sinks": dsinks, "ln1_g": dln1g, "ln1_b": dln1b,
             "ln2_g": dln2g, "ln2_b": dln2b}
    small_names = list(small)
    packed = jnp.concatenate([small[nm].reshape(1, -1) for nm in small_names], axis=1)
    sm_send, sm_recv, sm_src, sm_land, sm_token = _exchange_start(
        "gather_small_grads_start", [packed], [lax.empty((N_DEV,) + packed.shape, F32)], False, grad_x)

    (gw_in_t,) = _matmul("in_proj_wgrad", dp, u1, "tn", [BF16], after=sm_token)
    in_names = ["w_in", "w_decay_up", "w_iclr_up", "w_gate_up"]
    in_parts = [gw_in_t.reshape(N_DEV, -1, d), _cols_to_shards(dwd), _cols_to_shards(dwa), _cols_to_shards(dwg)]
    in_sibling = _pair_exchange("pair_in_grads", in_parts)
    in_src = [_pair_sum("pair_sum_" + nm, a, b) for nm, a, b in zip(in_names, in_parts, in_sibling)]
    in_send, in_recv, in_src, in_land, in_token = _exchange_start(
        "scatter_in_grads_start", in_src, [landing(a) for a in in_src], "chips", in_src[0])

    grads, deltas, new_m, new_v = {}, {}, {}, {}

    def put(nm, res):
        shape = weights[nm].shape
        grads[nm], deltas[nm], new_m[nm], new_v[nm] = [a.reshape(shape) for a in res]

    def big_update(nm, parts):
        put(nm, _adamw("adamw_" + nm, weights[nm][0], mom_m[nm][0], mom_v[nm][0], parts))

    behind = in_token
    big_update("w_down", _exchange_wait("scatter_w_down_wait", mlp_send[0], mlp_recv[0], mlp_src[0], mlp_land[0],
                                        True, behind, mlp_cols[0]))
    big_update("w_up", _exchange_wait("scatter_w_up_wait", mlp_send[1], mlp_recv[1], mlp_src[1], mlp_land[1],
                                      True, behind, mlp_cols[1]))
    big_update("w_out", _exchange_wait("scatter_w_out_wait", out_send[0], out_recv[0], out_src[0], out_land[0],
                                       True, behind))

    packed_all = _exchange_wait("gather_small_grads_wait", sm_send[0], sm_recv[0], sm_src[0], sm_land[0], False,
                                deltas["w_out"]).reshape(N_DEV, -1)
    n_mod = w_mod.shape[2]
    dmod_cols = lax.dynamic_slice_in_dim(packed_all[:, _offset(small, small_names, "b_mod"):], me * n_mod, n_mod,
                                         axis=1)
    put("w_mod", _adamw_outer("adamw_w_mod", w_mod[0], m_w_mod[0], v_w_mod[0], cond_all.T, dmod_cols))

    off = 0
    for nm in small_names:
        size = small[nm].size
        wshape = weights[nm].shape
        two_d = (1, size) if nm != "rpb_table" else wshape
        parts = packed_all[:, off:off + size].reshape((N_DEV,) + two_d)
        off += size
        put(nm, _adamw("adamw_" + nm, weights[nm].reshape(two_d), mom_m[nm].reshape(two_d),
                       mom_v[nm].reshape(two_d), parts))

    behind = deltas[small_names[-1]]
    for i, nm in enumerate(in_names):
        parts = _exchange_wait("scatter_" + nm + "_wait", in_send[i], in_recv[i], in_src[i], in_land[i],
                               "chips", behind)
        big_update(nm, jnp.swapaxes(parts, 1, 2) if nm == "w_in" else parts)

    loss = lax.psum(loss_acc[0, 0], MESH_AXES)
    return (loss, grad_x[None], *[grads[nm] for nm in names], *[deltas[nm] for nm in names],
            *[new_m[nm] for nm in names], *[new_v[nm] for nm in names])


def _offset(small, small_names, name):
    off = 0
    for nm in small_names:
        if nm == name:
            return off
        off += small[nm].size
    raise KeyError(name)
```

```python
import functools
import math

import jax
import jax.numpy as jnp
from jax import lax
from jax.experimental import pallas as pl
from jax.experimental.pallas import tpu as pltpu

F32 = jnp.float32
BF16 = jnp.bfloat16
HI = lax.Precision.HIGHEST
MESH_AXES = ("x", "y", "c")
N_DEV = 8

HEAD = 64
GQA_RATIO = 8
ATT_BLOCK = 128
RPB_MAX_DIST = 128
LN_EPS = 1e-5
LNX_EPS = 64e-5
DEPTH = 1
ALPHA = (2.0 * DEPTH) ** 0.25
CHUNK = 64
REC_HEADS = 32

ADAM_LR = 0.001
ADAM_B1 = 0.9
ADAM_B2 = 0.999
ADAM_EPS = 1e-08
ADAM_WD = 0.01
ADAM_STEP = 10

VMEM_LIMIT = 60 * 1024 * 1024


def _cparams(sem=None):
    return pltpu.CompilerParams(dimension_semantics=sem, vmem_limit_bytes=VMEM_LIMIT)


def _tile(dim, cap):
    best = None
    t = 128
    while t <= min(dim, cap):
        if dim % t == 0:
            best = t
        t += 128
    return best or dim


def _rtile(dim, cap):
    best = None
    t = 8
    while t <= min(dim, cap):
        if dim % t == 0:
            best = t
        t += 8
    return best or dim


def _split2(a):
    hi = a.astype(BF16)
    return hi, (a - hi.astype(F32)).astype(BF16)


def _raw_dot(a, b, ca, cb, prec):
    dims = (((ca,), (cb,)), ((), ()))
    mm = lambda p, q: lax.dot_general(p, q, dims, preferred_element_type=F32)
    if prec == "bf16":
        return mm(a.astype(BF16), b.astype(BF16))
    if prec == "x3":
        (ah, al), (bh, bl) = _split2(a), _split2(b)
        return mm(ah, bh) + (mm(ah, bl) + mm(al, bh))
    if prec == "mask":
        ab = a.astype(BF16)
        b1, b2 = _split2(b)
        b3 = (b - b1.astype(F32) - b2.astype(F32)).astype(BF16)
        return mm(ab, b1) + (mm(ab, b2) + mm(ab, b3))
    if prec == "mb2":
        (ah, al), bb = _split2(a), b.astype(BF16)
        return mm(ah, bb) + mm(al, bb)
    return lax.dot_general(a, b, dims, precision=HI, preferred_element_type=F32)


@functools.partial(jax.custom_vjp, nondiff_argnums=(2, 3, 4))
def _bf16_dot(a, b, ca, cb, prec):
    return _raw_dot(a, b, ca, cb, prec)


def _bf16_dot_fwd(a, b, ca, cb, prec):
    return _raw_dot(a, b, ca, cb, prec), (a, b)


def _bf16_dot_bwd(ca, cb, prec, res, g):
    a, b = res
    if prec == "mask":
        return jnp.zeros_like(a), _bf16_dot(a, g, 1 - ca, 0, prec)
    if prec == "mb2":
        return _bf16_dot(g, b, 1, 1, prec), jnp.zeros_like(b)
    if ca == 1:
        da = _bf16_dot(g, b, 1, 1 - cb, prec)
    else:
        da = _bf16_dot(b, g, 1 - cb, 1, prec)
    if cb == 0:
        db = _bf16_dot(a, g, 1 - ca, 0, prec)
    else:
        db = _bf16_dot(g, a, 0, 1 - ca, prec)
    return da, db


_bf16_dot.defvjp(_bf16_dot_fwd, _bf16_dot_bwd)


def _dot(a, b, ca, cb, prec):
    return _raw_dot(a, b, ca, cb, prec) if prec == "hi" else _bf16_dot(a, b, ca, cb, prec)


def _sigmoid(z):
    return 1.0 / (1.0 + jnp.exp(-z))


def _softplus(z):
    return jnp.maximum(z, 0.0) + jnp.log(1.0 + jnp.exp(-jnp.abs(z)))


MATMUL_VMEM_BUDGET = 51 * 1024 * 1024


def _matmul_tiles(m, n, k, in_bytes, out_dtypes, n_extras):
    tm, tn = _tile(m, 1024), _tile(n, 1024)
    out_bytes = sum(jnp.dtype(dt).itemsize for dt in out_dtypes)
    for cap in (4096, 2048, 1024, 512, 256, 128):
        tk = _tile(k, cap)
        acc = 4 * tm * tn if tk < k else 0
        need = 2 * in_bytes * (tm + tn) * tk + 2 * tm * tn * (out_bytes + 4 * n_extras) + acc + 4 * tm * tn
        if need <= MATMUL_VMEM_BUDGET:
            break
    return tm, tn, tk


def _matmul(name, a, b, mode, out_dtypes, epilogue=None, extras=(), after=None):
    if mode == "nn":
        (m, k), n = a.shape, b.shape[1]
    elif mode == "nt":
        (m, k), n = a.shape, b.shape[0]
    else:
        (k, m), n = a.shape, b.shape[1]
    tm, tn, tk = _matmul_tiles(m, n, k, a.dtype.itemsize, out_dtypes, len(extras))
    nk = k // tk
    ne, no = len(extras), len(out_dtypes)
    ca, cb = {"nn": (1, 0), "nt": (1, 1), "tn": (0, 0)}[mode]

    n_after = 0 if after is None else 1

    def body(a_ref, b_ref, *rest):
        rest = rest[n_after:]
        extra_refs, out_refs = rest[:ne], rest[ne:ne + no]
        acc = rest[-1] if nk > 1 else None
        kk = pl.program_id(2)
        part = _raw_dot(a_ref[...], b_ref[...], ca, cb, "bf16")

        def finish(total):
            res = epilogue(total, *[e[...] for e in extra_refs]) if epilogue else (total,)
            for o, v in zip(out_refs, res):
                o[...] = v.astype(o.dtype)

        if nk == 1:
            finish(part)
            return

        @pl.when(kk == 0)
        def _():
            acc[...] = part

        @pl.when((kk > 0) & (kk < nk - 1))
        def _():
            acc[...] += part

        @pl.when(kk == nk - 1)
        def _():
            finish(acc[...] + part)

    a_spec = (pl.BlockSpec((tk, tm), lambda i, j, kk: (kk, i)) if mode == "tn"
              else pl.BlockSpec((tm, tk), lambda i, j, kk: (i, kk)))
    b_spec = (pl.BlockSpec((tn, tk), lambda i, j, kk: (j, kk)) if mode == "nt"
              else pl.BlockSpec((tk, tn), lambda i, j, kk: (kk, j)))
    mn_spec = pl.BlockSpec((tm, tn), lambda i, j, kk: (i, j))
    after_specs = [pl.BlockSpec(memory_space=pl.ANY)] * n_after
    outs = pl.pallas_call(
        body, name=name, grid=(m // tm, n // tn, nk),
        in_specs=[a_spec, b_spec] + after_specs + [mn_spec] * ne,
        out_specs=[mn_spec] * no,
        out_shape=[jax.ShapeDtypeStruct((m, n), dt) for dt in out_dtypes],
        scratch_shapes=[pltpu.VMEM((tm, tn), F32)] if nk > 1 else [],
        compiler_params=_cparams(("parallel", "parallel", "arbitrary")),
    )(a, b, *([after] * n_after), *extras)
    return outs


def _rowwise(name, fn, rows, bcasts, out_rows, out_accs, tr, halos=()):
    t = rows[0].shape[0]
    nb = t // tr
    n_in = len(rows) + len(halos) + len(bcasts)
    n_ro = len(out_rows)

    def body(*refs):
        ins = [r[...] for r in refs[:n_in]]
        o_refs = refs[n_in:]
        i = pl.program_id(0)
        routs, aouts = fn(i, nb, *ins)
        for ref, v in zip(o_refs[:n_ro], routs):
            ref[...] = v.astype(ref.dtype)
        for ref, v in zip(o_refs[n_ro:], aouts):
            @pl.when(i == 0)
            def _(ref=ref):
                ref[...] = jnp.zeros_like(ref)
            ref[...] += v.reshape(ref.shape)

    in_specs = [pl.BlockSpec((tr, r.shape[1]), lambda i: (i, 0)) for r in rows]
    for arr, which in halos:
        if which == "prev":
            in_specs.append(pl.BlockSpec((8, arr.shape[1]), lambda i: (jnp.maximum(i * (tr // 8) - 1, 0), 0)))
        else:
            in_specs.append(pl.BlockSpec((8, arr.shape[1]),
                                         lambda i: (jnp.minimum((i + 1) * (tr // 8), t // 8 - 1), 0)))
    for bc in bcasts:
        in_specs.append(pl.BlockSpec(bc.shape, lambda i, nd=bc.ndim: (0,) * nd))
    out_specs = [pl.BlockSpec((tr, c), lambda i: (i, 0)) for c, _ in out_rows]
    out_specs += [pl.BlockSpec(s, lambda i, nd=len(s): (0,) * nd) for s in out_accs]
    out_shape = [jax.ShapeDtypeStruct((t, c), dt) for c, dt in out_rows]
    out_shape += [jax.ShapeDtypeStruct(s, F32) for s in out_accs]
    return pl.pallas_call(
        body, name=name, grid=(nb,), in_specs=in_specs, out_specs=out_specs, out_shape=out_shape,
        compiler_params=_cparams(("arbitrary",)),
    )(*rows, *[h[0] for h in halos], *bcasts)


def _shift_prev(x, halo, i):
    rolled = pltpu.roll(x, 1, 0)
    first = jnp.where(i == 0, 0.0, halo[7:8, :])
    row = lax.broadcasted_iota(jnp.int32, x.shape, 0)
    return jnp.where(row == 0, first, rolled)


def _shift_next(x, halo, i, nb):
    rolled = pltpu.roll(x, x.shape[0] - 1, 0)
    last = jnp.where(i == nb - 1, 0.0, halo[0:1, :])
    row = lax.broadcasted_iota(jnp.int32, x.shape, 0)
    return jnp.where(row == x.shape[0] - 1, last, rolled)


def _ln(x, g, b, eps=LN_EPS):
    mu = jnp.mean(x, axis=-1, keepdims=True)
    xc = x - mu
    var = jnp.mean(xc * xc, axis=-1, keepdims=True)
    return xc * lax.rsqrt(var + eps) * g + b


def _embed_math(x, g, b, sc, sh):
    x0 = _ln(x, g, b)
    return x0, x0 * (1.0 + sc) + sh


def _post_math(xin, y, gate, g, b, sc, sh):
    x1 = _ln(ALPHA * xin + (1.0 + gate) * y, g, b)
    return x1, x1 * (1.0 + sc) + sh


def _loss_math(xin, h, tgt, gate, g, b):
    x2 = _ln(ALPHA * xin + (1.0 + gate) * h, g, b)
    err = x2 - tgt
    return 0.5 * jnp.sum(jnp.mean(err * err, axis=-1))


def _rwkv_pre_math(r, k, v, xw, xa, xg, w0, wd, a0, wa, wg, k_k, k_a, seg, seg_t, dot=_dot):
    wpre = -_softplus(-(w0 + dot(jnp.tanh(xw), wd, 1, 0, "x3"))) - 0.5
    lw = -jnp.exp(wpre)
    a = _sigmoid(a0 + dot(xa, wa, 1, 0, "x3"))
    g = dot(_sigmoid(xg), wg, 1, 0, "x3")
    kk = k * k_k
    norm = jnp.sqrt(dot(kk * kk, seg, 1, 0, "mb2"))
    kkn = kk * dot(1.0 / jnp.maximum(norm, 1e-12), seg_t, 1, 0, "mb2")
    k2 = k * (1.0 + (a - 1.0) * k_a)
    return r, lw, k2, v, kkn, kkn * a, g


def _rwkv_post_math(y, r, k2, v, g, lnx_g, lnx_b, r_k, seg, seg_t, dot=_dot):
    inv = 1.0 / HEAD
    spread = lambda z: dot(dot(z, seg, 1, 0, "mb2"), seg_t, 1, 0, "mb2")
    mu = spread(y) * inv
    yc = y - mu
    var = spread(yc * yc) * inv
    yn = yc * lax.rsqrt(var + LNX_EPS) * lnx_g + lnx_b
    bonus = spread(r * k2 * r_k) * v
    return (yn + bonus) * g


@jax.custom_vjp
def _known_inverse(low, inv):
    return inv


def _known_inverse_fwd(low, inv):
    return inv, inv


def _known_inverse_bwd(inv, g):
    left = [_raw_dot(t, gi, 0, 0, "bf16") for t, gi in zip(inv, g)]
    return [_raw_dot(x, t, 1, 1, "bf16") for x, t in zip(left, inv)], [jnp.zeros_like(t) for t in inv]


_known_inverse.defvjp(_known_inverse_fwd, _known_inverse_bwd)


def _chunk_math(s0, r, lw, k, v, kk, b, known_inv=None, dot=_dot):
    n = len(r)
    hs = range(n)
    c = r[0].shape[0]
    ti = lax.broadcasted_iota(jnp.int32, (2 * c, 2 * c), 0)
    tj = lax.broadcasted_iota(jnp.int32, (2 * c, 2 * c), 1)
    tt, jj = ti & (c - 1), tj & (c - 1)
    quad = jnp.where(ti < c, (tt > jj).astype(F32), (tt >= jj).astype(F32))
    incl = quad[c:, :c]
    eye = (ti[:c, :c] == tj[:c, :c]).astype(F32)
    cl = [dot(incl, lw[i], 1, 0, "mask") for i in hs]
    ge = [jnp.exp(cl[i]) for i in hs]
    gi = [jnp.exp(-cl[i]) for i in hs]
    ar = [jnp.concatenate([-kk[i] * jnp.exp(cl[i] - lw[i]), r[i] * ge[i]], axis=0) for i in hs]
    kb = [jnp.concatenate([k[i] * gi[i], b[i] * gi[i]], axis=0) for i in hs]
    m = [dot(ar[i], kb[i], 1, 1, "x3") * quad for i in hs]
    ars0 = [dot(ar[i], s0[i], 1, 1, "bf16") for i in hs]
    mv = [dot(m[i][:c, :c], v[i], 1, 0, "bf16") for i in hs]
    pw = [m[i][:c, c:] for i in hs]
    if known_inv is None:
        inv = [eye + pw[i] for i in hs]
        for _ in range(int(math.log2(c)) - 1):
            pw = [dot(pw[i], pw[i], 1, 0, "bf16") for i in hs]
            inv = [inv[i] + dot(inv[i], pw[i], 1, 0, "bf16") for i in hs]
    else:
        inv = _known_inverse(pw, known_inv)
    u = [dot(inv[i], ars0[i][:c] + mv[i], 1, 0, "bf16") for i in hs]
    vu = [jnp.concatenate([v[i], u[i]], axis=0) for i in hs]
    y = [ars0[i][c:] + dot(m[i][c:], vu[i], 1, 0, "bf16") for i in hs]
    s1 = [(s0[i] + dot(vu[i], kb[i], 0, 0, "x3")) * ge[i][c - 1:c, :] for i in hs]
    return y, s1, inv


def _attn_math(q, kp, kc, vp, vc, bias, sinks, first, dot=_dot):
    hq = q.shape[1] // HEAD
    hkv = kc.shape[1] // HEAD
    group = hq // hkv
    cols = group * ATT_BLOCK
    kj = lax.broadcasted_iota(jnp.int32, (2 * ATT_BLOCK, cols), 0)
    qi = lax.broadcasted_iota(jnp.int32, (2 * ATT_BLOCK, cols), 1) & (ATT_BLOCK - 1)
    dist = qi + ATT_BLOCK - kj
    valid = (dist >= 0) & (dist < ATT_BLOCK) & (jnp.logical_not(first) | (kj >= ATT_BLOCK))
    eye = (lax.broadcasted_iota(jnp.int32, (ATT_BLOCK, ATT_BLOCK), 0)
           == lax.broadcasted_iota(jnp.int32, (ATT_BLOCK, ATT_BLOCK), 1)).astype(F32)
    outs = []
    for j in range(hkv):
        heads = range(j * group, (j + 1) * group)
        kband = jnp.concatenate([kp[:, j * HEAD:(j + 1) * HEAD], kc[:, j * HEAD:(j + 1) * HEAD]], axis=0)
        vband = jnp.concatenate([vp[:, j * HEAD:(j + 1) * HEAD], vc[:, j * HEAD:(j + 1) * HEAD]], axis=0)
        qg = jnp.concatenate([q[:, h * HEAD:(h + 1) * HEAD] for h in heads], axis=0)
        bias_g = jnp.concatenate([bias[h] for h in heads], axis=1)
        sink = jnp.concatenate([jnp.broadcast_to(sinks[0:1, h:h + 1], (1, ATT_BLOCK)) for h in heads], axis=1)
        s = dot(kband, qg, 1, 1, "bf16") * (HEAD ** -0.5) + bias_g
        s = jnp.where(valid, s, -1e30)
        m = jnp.maximum(jnp.max(s, axis=0, keepdims=True), sink)
        e = jnp.exp(s - m)
        p = e / (jnp.sum(e, axis=0, keepdims=True) + jnp.exp(sink - m))
        o_t = dot(vband, p, 0, 0, "bf16")
        outs += [dot(eye, o_t[:, g * ATT_BLOCK:(g + 1) * ATT_BLOCK], 1, 1, "bf16") for g in range(group)]
    return jnp.concatenate(outs, axis=1)


def _rec_specs(t, da, gh, reverse):
    nc = t // CHUNK
    if reverse:
        return pl.BlockSpec((CHUNK, gh * HEAD), lambda hg, c: (nc - 1 - c, hg))
    return pl.BlockSpec((CHUNK, gh * HEAD), lambda hg, c: (c, hg))


def _rec_fwd(r, lw, k, v, kk, b):
    t, da = r.shape
    h = da // HEAD
    gh = min(REC_HEADS, h)
    nc = t // CHUNK

    def body(r_ref, lw_ref, k_ref, v_ref, kk_ref, b_ref, y_ref, s0_ref, inv_ref, state):
        @pl.when(pl.program_id(1) == 0)
        def _():
            state[...] = jnp.zeros_like(state)

        sls = [slice(i * HEAD, (i + 1) * HEAD) for i in range(gh)]
        heads = lambda ref: [ref[:, sl] for sl in sls]
        s0 = [state[i] for i in range(gh)]
        y, s1, inv = _chunk_math(s0, heads(r_ref), heads(lw_ref), heads(k_ref), heads(v_ref), heads(kk_ref),
                                 heads(b_ref), dot=_raw_dot)
        for i, sl in enumerate(sls):
            s0_ref[0, i] = s0[i]
            inv_ref[0, i] = inv[i]
            y_ref[:, sl] = y[i]
            state[i] = s1[i]

    spec = _rec_specs(t, da, gh, False)
    per_chunk = pl.BlockSpec((1, gh, HEAD, HEAD), lambda hg, c: (c, hg, 0, 0))
    return pl.pallas_call(
        body, name="rwkv_recurrence_fwd", grid=(h // gh, nc),
        in_specs=[spec] * 6,
        out_specs=[spec, per_chunk, per_chunk],
        out_shape=[jax.ShapeDtypeStruct((t, da), F32)] + [jax.ShapeDtypeStruct((nc, h, HEAD, HEAD), F32)] * 2,
        scratch_shapes=[pltpu.VMEM((gh, HEAD, HEAD), F32)],
        compiler_params=_cparams(("parallel", "arbitrary")),
    )(r, lw, k, v, kk, b)


def _rec_bwd(r, lw, k, v, kk, b, s0s, invs, dy):
    t, da = r.shape
    h = da // HEAD
    gh = min(REC_HEADS, h)
    nc = t // CHUNK

    def body(r_ref, lw_ref, k_ref, v_ref, kk_ref, b_ref, dy_ref, s0_ref, inv_ref,
             dr_ref, dlw_ref, dk_ref, dv_ref, dkk_ref, db_ref, dstate):
        @pl.when(pl.program_id(1) == 0)
        def _():
            dstate[...] = jnp.zeros_like(dstate)

        sls = [slice(i * HEAD, (i + 1) * HEAD) for i in range(gh)]
        heads = lambda ref: [ref[:, sl] for sl in sls]
        known = [inv_ref[0, i] for i in range(gh)]
        fn = lambda *args: _chunk_math(*args, known_inv=known)[:2]
        _, vjp = jax.vjp(fn, [s0_ref[0, i] for i in range(gh)], heads(r_ref), heads(lw_ref),
                         heads(k_ref), heads(v_ref), heads(kk_ref), heads(b_ref))
        grads = vjp((heads(dy_ref), [dstate[i] for i in range(gh)]))
        for i, sl in enumerate(sls):
            dstate[i] = grads[0][i]
            for ref, val in zip((dr_ref, dlw_ref, dk_ref, dv_ref, dkk_ref, db_ref), grads[1:]):
                ref[:, sl] = val[i]

    spec = _rec_specs(t, da, gh, True)
    return pl.pallas_call(
        body, name="rwkv_recurrence_bwd", grid=(h // gh, nc),
        in_specs=[spec] * 7 + [pl.BlockSpec((1, gh, HEAD, HEAD), lambda hg, c: (nc - 1 - c, hg, 0, 0))] * 2,
        out_specs=[spec] * 6,
        out_shape=[jax.ShapeDtypeStruct((t, da), F32)] * 6,
        scratch_shapes=[pltpu.VMEM((gh, HEAD, HEAD), F32)],
        compiler_params=_cparams(("parallel", "arbitrary")),
    )(r, lw, k, v, kk, b, dy, s0s, invs)


def _attn_specs(t, hq_w, hkv_w):
    nb = t // ATT_BLOCK
    cur = lambda w: pl.BlockSpec((ATT_BLOCK, w), lambda n: (n, 0))
    prev = lambda w: pl.BlockSpec((ATT_BLOCK, w), lambda n: (jnp.maximum(n - 1, 0), 0))
    return nb, cur, prev


def _attn_fwd(q, kb, vb, bias, sinks):
    t, qw = q.shape
    kw = kb.shape[1]
    nb, cur, prev = _attn_specs(t, qw, kw)

    def body(q_ref, kp_ref, kc_ref, vp_ref, vc_ref, bias_ref, sink_ref, o_ref):
        first = pl.program_id(0) == 0
        o = _attn_math(q_ref[...], kp_ref[...], kc_ref[...], vp_ref[...], vc_ref[...],
                       bias_ref[...], sink_ref[...], first, dot=_raw_dot)
        o_ref[...] = o.astype(o_ref.dtype)

    full = lambda a: pl.BlockSpec(a.shape, lambda n, nd=a.ndim: (0,) * nd)
    return pl.pallas_call(
        body, name="swa_attention_fwd", grid=(nb,),
        in_specs=[cur(qw), prev(kw), cur(kw), prev(kw), cur(kw), full(bias), full(sinks)],
        out_specs=cur(qw), out_shape=jax.ShapeDtypeStruct((t, qw), BF16),
        compiler_params=_cparams(("parallel",)),
    )(q, kb, kb, vb, vb, bias, sinks)


def _attn_bwd(q, kb, vb, bias, sinks, do, col_block):
    t, qw = q.shape
    kw = kb.shape[1]
    nb, cur, prev = _attn_specs(t, qw, kw)

    def body(q_ref, kp_ref, kc_ref, vp_ref, vc_ref, bias_ref, sink_ref, do_ref,
             dq_ref, dkp_ref, dkc_ref, dvp_ref, dvc_ref, dbias_ref, dsink_ref):
        n = pl.program_id(0)
        first = n == 0
        fn = functools.partial(_attn_math, first=first)
        _, vjp = jax.vjp(fn, q_ref[...], kp_ref[...], kc_ref[...], vp_ref[...], vc_ref[...],
                         bias_ref[...], sink_ref[...])
        dq, dkp, dkc, dvp, dvc, dbias, dsink = vjp(do_ref[...].astype(F32))
        dq_ref[...] = dq.astype(dq_ref.dtype)
        dkp_ref[...] = dkp
        dkc_ref[...] = dkc
        dvp_ref[...] = dvp
        dvc_ref[...] = dvc

        @pl.when(first)
        def _():
            dbias_ref[...] = jnp.zeros_like(dbias_ref)
            dsink_ref[...] = jnp.zeros_like(dsink_ref)

        dbias_ref[...] += dbias
        dsink_ref[...] += dsink

    full = lambda a: pl.BlockSpec(a.shape, lambda n, nd=a.ndim: (0,) * nd)
    kshape = jax.ShapeDtypeStruct((t, kw), F32)
    return pl.pallas_call(
        body, name="swa_attention_bwd", grid=(nb,),
        in_specs=[cur(qw), prev(kw), cur(kw), prev(kw), cur(kw), full(bias), full(sinks),
                  pl.BlockSpec((ATT_BLOCK, qw), lambda n: (n, col_block))],
        out_specs=[cur(qw), cur(kw), cur(kw), cur(kw), cur(kw), full(bias), full(sinks)],
        out_shape=[jax.ShapeDtypeStruct((t, qw), BF16), kshape, kshape, kshape, kshape,
                   jax.ShapeDtypeStruct(bias.shape, F32), jax.ShapeDtypeStruct(sinks.shape, F32)],
        compiler_params=_cparams(("arbitrary",)),
    )(q, kb, kb, vb, vb, bias, sinks, do)


def _bucket_onehot():
    qi = jnp.arange(ATT_BLOCK)[None, :]
    kj = jnp.arange(2 * ATT_BLOCK)[:, None]
    n = jnp.maximum(qi + ATT_BLOCK - kj, 0)
    buckets, max_exact = 32, 16
    nf = jnp.maximum(n, 1).astype(F32)
    large = max_exact + (jnp.log(nf / max_exact) / math.log(RPB_MAX_DIST / max_exact)
                         * (buckets - max_exact)).astype(jnp.int32)
    bucket = jnp.where(n < max_exact, n, jnp.minimum(large, buckets - 1)).reshape(-1)
    return (bucket[None, :] == jnp.arange(buckets)[:, None]).astype(F32)


def _small_dot(name, a, b, ca, cb):
    m = a.shape[1 - ca]
    n = b.shape[1 - cb]

    def body(a_ref, b_ref, o_ref):
        o_ref[...] = _raw_dot(a_ref[...], b_ref[...], ca, cb, "hi")

    return pl.pallas_call(body, name=name, out_shape=jax.ShapeDtypeStruct((m, n), F32),
                          compiler_params=_cparams())(a, b)


def _mod_fwd(c_all, w_mod):
    d, n = w_mod.shape
    tn = _tile(n, 512)

    def body(c_ref, w_ref, o_ref, cond_ref):
        cv = c_ref[...]
        cond = cv * _sigmoid(cv)
        cond_ref[...] = cond
        o_ref[...] = _raw_dot(cond, w_ref[...], 1, 0, "hi")

    return pl.pallas_call(
        body, name="adaln_mod_fwd", grid=(n // tn,),
        in_specs=[pl.BlockSpec(c_all.shape, lambda j: (0, 0)), pl.BlockSpec((d, tn), lambda j: (0, j))],
        out_specs=[pl.BlockSpec((c_all.shape[0], tn), lambda j: (0, j)),
                   pl.BlockSpec(c_all.shape, lambda j: (0, 0))],
        out_shape=[jax.ShapeDtypeStruct((c_all.shape[0], n), F32), jax.ShapeDtypeStruct(c_all.shape, F32)],
        compiler_params=_cparams(("arbitrary",)),
    )(c_all, w_mod)


def _adam_math(w, g, m, v):
    m = ADAM_B1 * m + (1.0 - ADAM_B1) * g
    v = ADAM_B2 * v + (1.0 - ADAM_B2) * (g * g)
    m_hat = m / (1.0 - ADAM_B1 ** ADAM_STEP)
    v_hat = v / (1.0 - ADAM_B2 ** ADAM_STEP)
    delta = -ADAM_LR * (m_hat / (jnp.sqrt(v_hat) + ADAM_EPS) + ADAM_WD * w)
    return delta, m, v


def _adamw(name, w, m, v, gparts):
    r, c = w.shape
    p = gparts.shape[0]
    tr = _rtile(r, max(8, (1 << 18) // max(c, 1) // 8 * 8))

    def body(w_ref, m_ref, v_ref, g_ref, go_ref, d_ref, mo_ref, vo_ref):
        g = g_ref[0].astype(F32)
        for s in range(1, p):
            g = g + g_ref[s].astype(F32)
        delta, mn, vn = _adam_math(w_ref[...], g, m_ref[...], v_ref[...])
        go_ref[...] = g
        d_ref[...] = delta
        mo_ref[...] = mn
        vo_ref[...] = vn

    spec = pl.BlockSpec((tr, c), lambda i: (i, 0))
    return pl.pallas_call(
        body, name=name, grid=(r // tr,),
        in_specs=[spec, spec, spec, pl.BlockSpec((p, tr, c), lambda i: (0, i, 0))],
        out_specs=[spec] * 4, out_shape=[jax.ShapeDtypeStruct((r, c), F32)] * 4,
        compiler_params=_cparams(("parallel",)),
    )(w, m, v, gparts)


def _adamw_outer(name, w, m, v, cond_t, dmod):
    d, n = w.shape
    tr, tn = _rtile(d, 512), _tile(n, 1024)

    def body(w_ref, m_ref, v_ref, c_ref, dm_ref, go_ref, d_ref, mo_ref, vo_ref):
        g = _raw_dot(c_ref[...], dm_ref[...], 1, 0, "hi")
        delta, mn, vn = _adam_math(w_ref[...], g, m_ref[...], v_ref[...])
        go_ref[...] = g
        d_ref[...] = delta
        mo_ref[...] = mn
        vo_ref[...] = vn

    spec = pl.BlockSpec((tr, tn), lambda i, j: (i, j))
    return pl.pallas_call(
        body, name=name, grid=(d // tr, n // tn),
        in_specs=[spec, spec, spec, pl.BlockSpec((tr, cond_t.shape[1]), lambda i, j: (i, 0)),
                  pl.BlockSpec((dmod.shape[0], tn), lambda i, j: (0, j))],
        out_specs=[spec] * 4, out_shape=[jax.ShapeDtypeStruct((d, n), F32)] * 4,
        compiler_params=_cparams(("parallel", "parallel")),
    )(w, m, v, cond_t, dmod)


def _all_gather(name, arrays):
    n = len(arrays)

    def body(*refs):
        ins, outs = refs[:n], refs[n:2 * n]
        send_sems, recv_sems, local_sems = refs[2 * n:]
        x, y, c = lax.axis_index("x"), lax.axis_index("y"), lax.axis_index("c")
        me, sibling = (x, y, c), (x, y, 1 - c)
        chips = [(1 - x, y), (x, 1 - y), (1 - x, 1 - y)]

        def copy(a, k, block, to, src=None):
            rows = outs[a].at[4 * block[0] + 2 * block[1] + block[2]]
            return pltpu.make_async_remote_copy(
                src_ref=rows if src is None else src, dst_ref=rows, send_sem=send_sems.at[a, k],
                recv_sem=recv_sems.at[a, k], device_id=to, device_id_type=pl.DeviceIdType.MESH)

        mine = [pltpu.make_async_copy(ins[a], outs[a].at[4 * x + 2 * y + c], local_sems.at[a]) for a in range(n)]
        for cp in mine:
            cp.start()
        first = []
        for a in range(n):
            first.append(copy(a, 0, me, sibling, src=ins[a]))
            first += [copy(a, 1 + j, me, (*chip, c), src=ins[a]) for j, chip in enumerate(chips)]
        for cp in first:
            cp.start()
        passed = []
        for a in range(n):
            for j, chip in enumerate(chips):
                copy(a, 1 + j, (*chip, c), me).wait_recv()
                passed.append(copy(a, 4 + j, (*chip, c), sibling))
                passed[-1].start()
        for a in range(n):
            copy(a, 0, sibling, me).wait_recv()
            for j, chip in enumerate(chips):
                copy(a, 4 + j, (*chip, 1 - c), me).wait_recv()
        for cp in first + passed:
            cp.wait_send()
        for cp in mine:
            cp.wait()

    any_spec = pl.BlockSpec(memory_space=pl.ANY)
    return pl.pallas_call(
        body, name=name, in_specs=[any_spec] * n, out_specs=[any_spec] * n,
        out_shape=[jax.ShapeDtypeStruct((N_DEV,) + a.shape, a.dtype) for a in arrays],
        scratch_shapes=[pltpu.SemaphoreType.DMA((n, N_DEV - 1)), pltpu.SemaphoreType.DMA((n, N_DEV - 1)),
                        pltpu.SemaphoreType.DMA((n,))],
    )(*arrays)


def _peer(p):
    x, y, c = lax.axis_index("x"), lax.axis_index("y"), lax.axis_index("c")
    px, py, pc = x ^ ((p >> 2) & 1), y ^ ((p >> 1) & 1), c ^ (p & 1)
    return (px, py, pc), 4 * px + 2 * py + pc


def _block(ref, d, cols, rows=None):
    if cols:
        r = slice(None) if rows is None else pl.ds(pl.multiple_of(rows[0], 8), rows[1])
        return ref.at[r, pl.ds(pl.multiple_of(d * cols, cols), cols)]
    return ref.at[d] if rows is None else ref.at[d, pl.ds(pl.multiple_of(rows[0], 8), rows[1])]


def _split_copy(src_ref, land_ref, send_sems, recv_sems, p, scatter, arriving, cols=None, halves=False):
    x, y, c = lax.axis_index("x"), lax.axis_index("y"), lax.axis_index("c")
    me = 4 * x + 2 * y + c
    dev, idx = _peer(p)
    if scatter == "chips":
        src, dst = src_ref.at[idx >> 1], land_ref.at[(idx if arriving else me) >> 1]
    elif scatter:
        src, dst = _block(src_ref, idx, cols), land_ref.at[idx if arriving else me]
    elif halves and p >= 2:
        half = src_ref.shape[0] // 2
        rows = ((c if arriving else dev[2]) * half, half)
        src = src_ref.at[pl.ds(pl.multiple_of(rows[0], 8), half)]
        dst = _block(land_ref, idx if arriving else me, cols, rows)
    else:
        src, dst = src_ref, _block(land_ref, idx if arriving else me, cols)
    return pltpu.make_async_remote_copy(
        src_ref=src, dst_ref=dst, send_sem=send_sems.at[p - 1], recv_sem=recv_sems.at[p - 1], device_id=dev,
        device_id_type=pl.DeviceIdType.MESH)


_HBM_SPEC = pl.BlockSpec(memory_space=pltpu.HBM)
_SEM_SPEC = pl.BlockSpec(memory_space=pltpu.SEMAPHORE)
_DATAFLOW = pltpu.SideEffectType.DATAFLOW_SIDE_EFFECTING


def _own_copy(src_ref, land_ref, send_sems, scatter, cols):
    me = 4 * lax.axis_index("x") + 2 * lax.axis_index("y") + lax.axis_index("c")
    if scatter == "chips":
        src, dst = src_ref.at[me >> 1], land_ref.at[me >> 1]
    elif scatter:
        src, dst = _block(src_ref, me, cols), land_ref.at[me]
    else:
        src, dst = src_ref, _block(land_ref, me, cols)
    return pltpu.make_async_copy(src, dst, send_sems.at[N_DEV - 1])


def _peers(scatter):
    return (2, 4, 6) if scatter == "chips" else tuple(range(1, N_DEV))


def _exchange_start(name, srcs, lands, scatter, after, cols=None, halves=False):
    n = len(srcs)
    cols = cols or [None] * n

    def body(*refs):
        src_refs, land_refs = refs[:n], refs[n:2 * n]
        outs = refs[2 * n + 1:]
        send, recv, token = outs[:n], outs[n:2 * n], outs[-1]
        for a in range(n):
            for p in _peers(scatter):
                _split_copy(src_refs[a], land_refs[a], send[a], recv[a], p, scatter, False, cols[a],
                            halves).start()
            _own_copy(src_refs[a], land_refs[a], send[a], scatter, cols[a]).start()
        token[...] = jnp.zeros_like(token)

    sems = [pltpu.SemaphoreType.DMA((N_DEV,))] * n + [pltpu.SemaphoreType.DMA((N_DEV - 1,))] * n
    hbm = [pltpu.HBM(a.shape, a.dtype) for a in list(srcs) + list(lands)]
    res = pl.pallas_call(
        body, name=name,
        out_shape=sems + hbm + [jax.ShapeDtypeStruct((8, 128), F32)],
        in_specs=[_HBM_SPEC] * (2 * n) + [pl.BlockSpec(memory_space=pl.ANY)],
        out_specs=[_SEM_SPEC] * (2 * n) + [_HBM_SPEC] * (2 * n) + [pl.BlockSpec(memory_space=pltpu.VMEM)],
        input_output_aliases={i: 2 * n + i for i in range(2 * n)},
        compiler_params=pltpu.CompilerParams(has_side_effects=_DATAFLOW),
    )(*[pltpu.with_memory_space_constraint(a, pltpu.HBM) for a in list(srcs) + list(lands)], after)
    return res[:n], res[n:2 * n], res[2 * n:3 * n], res[3 * n:4 * n], res[-1]


def _exchange_wait(name, send_sem, recv_sem, src, land, scatter, after, cols=None, halves=False):
    def body(src_ref, land_ref, send, recv, after_ref, src_out, land_out):
        for p in _peers(scatter):
            cp = _split_copy(src_ref, land_ref, send, recv, p, scatter, True, cols, halves)
            cp.wait_send()
            cp.wait_recv()
        _own_copy(src_ref, land_ref, send, scatter, cols).wait()

    return pl.pallas_call(
        body, name=name,
        out_shape=(pltpu.HBM(src.shape, src.dtype), pltpu.HBM(land.shape, land.dtype)),
        in_specs=[_HBM_SPEC, _HBM_SPEC, _SEM_SPEC, _SEM_SPEC, pl.BlockSpec(memory_space=pl.ANY)],
        out_specs=(_HBM_SPEC, _HBM_SPEC), input_output_aliases={0: 0, 1: 1},
        compiler_params=pltpu.CompilerParams(has_side_effects=_DATAFLOW),
    )(src, land, send_sem, recv_sem, after)[1]


def _sibling_fill(name, land, cols):
    rows = land.shape[0] if cols else land.shape[1]
    half = rows // 2

    def body(in_ref, out_ref, send_sems, recv_sems):
        x, y, c = lax.axis_index("x"), lax.axis_index("y"), lax.axis_index("c")

        def copy(p, core):
            _, idx = _peer(p)
            return pltpu.make_async_remote_copy(
                src_ref=_block(in_ref, idx, cols, (core * half, half)),
                dst_ref=_block(out_ref, idx, cols, (core * half, half)),
                send_sem=send_sems.at[idx], recv_sem=recv_sems.at[idx], device_id=(x, y, 1 - c),
                device_id_type=pl.DeviceIdType.MESH)

        sends = [copy(p, c) for p in range(2, N_DEV)]
        for cp in sends:
            cp.start()
        for p in range(2, N_DEV):
            copy(p, 1 - c).wait_recv()
        for cp in sends:
            cp.wait_send()

    any_spec = pl.BlockSpec(memory_space=pl.ANY)
    return pl.pallas_call(
        body, name=name, in_specs=[any_spec], out_specs=any_spec,
        out_shape=jax.ShapeDtypeStruct(land.shape, land.dtype), input_output_aliases={0: 0},
        scratch_shapes=[pltpu.SemaphoreType.DMA((N_DEV,)), pltpu.SemaphoreType.DMA((N_DEV,))],
    )(land)


def _pair_exchange(name, arrays):
    n = len(arrays)

    def body(*refs):
        ins, outs = refs[:n], refs[n:2 * n]
        send_sems, recv_sems = refs[2 * n:]
        x, y, c = lax.axis_index("x"), lax.axis_index("y"), lax.axis_index("c")
        copies = [pltpu.make_async_remote_copy(
            src_ref=ins[a].at[2 * k + (1 - c)], dst_ref=outs[a].at[k], send_sem=send_sems.at[a, k],
            recv_sem=recv_sems.at[a, k], device_id=(x, y, 1 - c), device_id_type=pl.DeviceIdType.MESH)
            for a in range(n) for k in range(N_DEV // 2)]
        for cp in copies:
            cp.start()
        for cp in copies:
            cp.wait()

    any_spec = pl.BlockSpec(memory_space=pl.ANY)
    return pl.pallas_call(
        body, name=name, in_specs=[any_spec] * n, out_specs=[any_spec] * n,
        out_shape=[jax.ShapeDtypeStruct((N_DEV // 2,) + a.shape[1:], a.dtype) for a in arrays],
        scratch_shapes=[pltpu.SemaphoreType.DMA((n, N_DEV // 2)), pltpu.SemaphoreType.DMA((n, N_DEV // 2))],
    )(*arrays)


def _pair_sum(name, mine, theirs):
    _, r, c_ = mine.shape
    tr = _rtile(r, max(8, (1 << 19) // c_ // 8 * 8))

    def body(mine_ref, theirs_ref, o_ref):
        core = lax.axis_index("c")
        o_ref[0] = (mine_ref[0, core].astype(F32) + theirs_ref[0].astype(F32)).astype(o_ref.dtype)

    return pl.pallas_call(
        body, name=name, grid=(N_DEV // 2, r // tr),
        in_specs=[pl.BlockSpec((1, 2, tr, c_), lambda k, i: (k, 0, i, 0)),
                  pl.BlockSpec((1, tr, c_), lambda k, i: (k, i, 0))],
        out_specs=pl.BlockSpec((1, tr, c_), lambda k, i: (k, i, 0)),
        out_shape=jax.ShapeDtypeStruct((N_DEV // 2, r, c_), mine.dtype),
        compiler_params=_cparams(("parallel", "parallel")),
    )(mine.reshape(N_DEV // 2, 2, r, c_), theirs)


def _cols_to_shards(a):
    r, c = a.shape
    return a.reshape(r, N_DEV, c // N_DEV).transpose(1, 0, 2)


def _shards_to_cols(a):
    d, r, n = a.shape
    return a.transpose(1, 0, 2).reshape(r, d * n)


def kernel(x, c, ln_emb_g, ln_emb_b, rpb_table, w_mod, b_mod, w_in, mu_shift, w0, w_decay_up, a0, w_iclr_up, w_gate_up, k_k, k_a, r_k, lnx_g, lnx_b, attn_sinks, w_out, ln1_g, ln1_b, w_up, w_down, ln2_g, ln2_b, loss_target, m_ln_emb_g, m_ln_emb_b, m_rpb_table, m_w_mod, m_b_mod, m_w_in, m_mu_shift, m_w0, m_w_decay_up, m_a0, m_w_iclr_up, m_w_gate_up, m_k_k, m_k_a, m_r_k, m_lnx_g, m_lnx_b, m_attn_sinks, m_w_out, m_ln1_g, m_ln1_b, m_w_up, m_w_down, m_ln2_g, m_ln2_b, v_ln_emb_g, v_ln_emb_b, v_rpb_table, v_w_mod, v_b_mod, v_w_in, v_mu_shift, v_w0, v_w_decay_up, v_a0, v_w_iclr_up, v_w_gate_up, v_k_k, v_k_a, v_r_k, v_lnx_g, v_lnx_b, v_attn_sinks, v_w_out, v_ln1_g, v_ln1_b, v_w_up, v_w_down, v_ln2_g, v_ln2_b):
    names = ["ln_emb_g", "ln_emb_b", "rpb_table", "w_mod", "b_mod", "w_in", "mu_shift", "w0", "w_decay_up",
             "a0", "w_iclr_up", "w_gate_up", "k_k", "k_a", "r_k", "lnx_g", "lnx_b", "attn_sinks", "w_out",
             "ln1_g", "ln1_b", "w_up", "w_down", "ln2_g", "ln2_b"]
    env = dict(locals())
    weights = {nm: env[nm] for nm in names}
    mom_m = {nm: env["m_" + nm] for nm in names}
    mom_v = {nm: env["v_" + nm] for nm in names}

    t, d = x.shape[1], x.shape[2]
    da = d // 2
    h_a = da // HEAD
    hq = (d - da) // HEAD
    hkv = hq // GQA_RATIO
    l_w, l_a, l_g = w_decay_up.shape[1], w_iclr_up.shape[1], w_gate_up.shape[1]
    o_w, o_a, o_g = 3 * da, 3 * da + l_w, 3 * da + l_w + l_a
    n_rwkv = o_g + l_g
    o_kb, o_vb = n_rwkv + hq * HEAD, n_rwkv + hq * HEAD + hkv * HEAD
    me = 4 * lax.axis_index("x") + 2 * lax.axis_index("y") + lax.axis_index("c")

    x2d, tgt = x[0], loss_target[0]
    row = lambda a: a.reshape(1, -1)
    seg = (jnp.arange(da)[:, None] // HEAD == jnp.arange(h_a)[None, :]).astype(F32)
    seg_t = seg.T

    (c_all,) = _all_gather("gather_cond", [c])
    c_all = c_all.reshape(N_DEV, d)
    mod_rows, cond_all = _mod_fwd(c_all, w_mod[0])
    gathered = _all_gather("gather_weights", [
        mod_rows, w_in[0].astype(BF16).T, w_decay_up[0], w_iclr_up[0], w_gate_up[0]])
    mod_all, win_g, wd_g, wa_g, wg_g = gathered
    late = [w_out[0].astype(BF16), w_up[0].astype(BF16), w_down[0].astype(BF16)]
    n_up = w_up.shape[2]
    late_cols = [None, n_up, None]
    late_lands = [lax.empty((N_DEV,) + late[0].shape, BF16), lax.empty((d, N_DEV * n_up), BF16),
                  lax.empty((N_DEV,) + late[2].shape, BF16)]
    late_send, late_recv, late_src, late_land, late_token = _exchange_start(
        "gather_late_weights_start", late, late_lands, False, mod_all, late_cols, halves=True)
    mod = lax.dynamic_index_in_dim(mod_all, me, axis=1, keepdims=False).reshape(1, -1) + b_mod
    mod = mod + late_token[0, 0]
    sh1, sc1, g1, sh2, sc2, g2 = [mod[:, i * d:(i + 1) * d] for i in range(6)]
    w_in_t = win_g.reshape(-1, d)
    wd_f, wa_f, wg_f = _shards_to_cols(wd_g), _shards_to_cols(wa_g), _shards_to_cols(wg_g)

    def late_weight(i, nm, after):
        land = _exchange_wait("gather_" + nm + "_wait", late_send[i], late_recv[i], late_src[i], late_land[i],
                              False, after, late_cols[i], halves=True)
        return _sibling_fill("gather_" + nm + "_fill", land, late_cols[i])

    tr = _rtile(t, 256)
    lng, lnb = row(ln_emb_g), row(ln_emb_b)

    def embed_fn(i, nb, xb, g, b, sc, sh):
        return _embed_math(xb, g, b, sc, sh), ()
    x0, u1 = _rowwise("embed_ln_mod", embed_fn, [x2d], [lng, lnb, sc1, sh1], [(d, F32), (d, BF16)], [], tr)

    (p_rkv,) = _matmul("in_proj_rkv", u1, w_in_t[:o_w], "nt", [F32])
    (p_lora,) = _matmul("in_proj_lora", u1, w_in_t[o_w:n_rwkv], "nt", [F32])
    (q,) = _matmul("in_proj_q", u1, w_in_t[n_rwkv:o_kb], "nt", [F32])
    (p_kv,) = _matmul("in_proj_kv", u1, w_in_t[o_kb:], "nt", [F32])
    p_w, p_a, p_g = p_lora[:, :l_w], p_lora[:, l_w:l_w + l_a], p_lora[:, l_w + l_a:]
    kb, vb = p_kv[:, :hkv * HEAD], p_kv[:, hkv * HEAD:]
    mu_rkv, mu_w, mu_a, mu_g = (mu_shift[:, :o_w], mu_shift[:, o_w:o_a], mu_shift[:, o_a:o_g],
                                mu_shift[:, o_g:n_rwkv])
    pre_params = [w0, wd_f, a0, wa_f, wg_f, k_k, k_a, seg, seg_t]
    tr_pre = _rtile(t, 128)

    def shifted(i, blocks, halos, mus):
        return [xb + (_shift_prev(xb, hb, i) - xb) * mb for xb, hb, mb in zip(blocks, halos, mus)]

    def split3(a):
        return a[:, :da], a[:, da:2 * da], a[:, 2 * da:]

    def pre_fn(i, nb, b_rkv, b_w, b_a, b_g, h_rkv, h_w, h_a_, h_g, m_rkv, m_w, m_a, m_g, *params):
        s_rkv, s_w, s_a, s_g = shifted(i, [b_rkv, b_w, b_a, b_g], [h_rkv, h_w, h_a_, h_g],
                                       [m_rkv, m_w, m_a, m_g])
        return _rwkv_pre_math(*split3(s_rkv), s_w, s_a, s_g, *params, dot=_raw_dot), ()

    pre_rows = [p_rkv, p_w, p_a, p_g]
    pre_halos = [(a, "prev") for a in pre_rows]
    r_, lw_, k2_, v_, kk_, b_, gate_ = _rowwise(
        "rwkv_pre", pre_fn, pre_rows, [mu_rkv, mu_w, mu_a, mu_g] + pre_params,
        [(da, F32)] * 7, [], tr_pre, halos=pre_halos)

    y_rec, s0s, invs = _rec_fwd(r_, lw_, k2_, v_, kk_, b_)

    rk_flat = r_k.reshape(1, da)
    post_params = [lnx_g, lnx_b, rk_flat, seg, seg_t]

    def post_fn(i, nb, yb, rb, kb_, vb_, gb, *params):
        return (_rwkv_post_math(yb, rb, kb_, vb_, gb, *params, dot=_raw_dot),), ()
    (ya,) = _rowwise("rwkv_post", post_fn, [y_rec, r_, k2_, v_, gate_], post_params, [(da, BF16)], [], tr_pre)

    onehot = _bucket_onehot()
    bias = _small_dot("rpb_gather", rpb_table, onehot, 0, 0)
    bias = bias.reshape(hq, 2 * ATT_BLOCK, ATT_BLOCK)
    yb = _attn_fwd(q, kb, vb, bias, attn_sinks)

    mix_in = jnp.concatenate([ya, yb], axis=1)
    w_out_f = late_weight(0, "w_out", mix_in).reshape(d, d)
    (mix,) = _matmul("out_proj", mix_in, w_out_f, "nn", [F32])

    def post1_fn(i, nb, xin, yv, gate, g, b, sc, sh):
        return _post_math(xin, yv, gate, g, b, sc, sh), ()
    x1, u2 = _rowwise("ln1_mod", post1_fn, [x0, mix], [g1, ln1_g, ln1_b, sc2, sh2],
                      [(d, F32), (d, BF16)], [], tr)

    def relu2(acc):
        rl = jnp.maximum(acc, 0.0)
        return acc, rl * rl
    w_up_f = late_weight(1, "w_up", u2)
    hpre, hact = _matmul("mlp_up", u2, w_up_f, "nn", [F32, BF16], epilogue=relu2)
    w_down_f = late_weight(2, "w_down", hact).reshape(-1, d)
    (hmlp,) = _matmul("mlp_down", hact, w_down_f, "nn", [F32])

    def loss_fn(i, nb, xin, hv, tg, gate, g, b):
        val, vjp = jax.vjp(_loss_math, xin, hv, tg, gate, g, b)
        dxin, dh, _, dgate, dg, db = vjp(jnp.ones((), F32))
        return (dxin, dh), (val, dgate, dg, db)
    dx1, dh, loss_acc, dg2, dln2g, dln2b = _rowwise(
        "ln2_loss", loss_fn, [x1, hmlp, tgt], [g2, ln2_g, ln2_b], [(d, F32), (d, BF16)],
        [(1, 1), (1, d), (1, d), (1, d)], _rtile(t, 128))

    def drelu2(acc, hp):
        return (acc * 2.0 * jnp.maximum(hp, 0.0),)
    (dhpre,) = _matmul("mlp_down_dgrad", dh, w_down_f, "nt", [BF16], epilogue=drelu2, extras=[hpre])
    (gw_down,) = _matmul("mlp_down_wgrad", hact, dh, "tn", [BF16])
    (du2,) = _matmul("mlp_up_dgrad", dhpre, w_up_f, "nt", [F32])
    (gw_up,) = _matmul("mlp_up_wgrad", u2, dhpre, "tn", [BF16])

    def landing(src):
        return lax.empty(src.shape, src.dtype)
    mlp_src = [gw_down.reshape(N_DEV, -1, d), gw_up]
    mlp_cols = [None, n_up]
    mlp_send, mlp_recv, mlp_src, mlp_land, mlp_token = _exchange_start(
        "scatter_mlp_grads_start", mlp_src, [landing(mlp_src[0]), lax.empty((N_DEV, d, n_up), BF16)], True,
        gw_up, mlp_cols)

    def post1_bwd(i, nb, xin, yv, dx1v, du2v, gate, g, b, sc, sh):
        _, vjp = jax.vjp(_post_math, xin, yv, gate, g, b, sc, sh)
        dxin, dy, dgate, dg, db, dsc, dsh = vjp((dx1v, du2v))
        return (dxin, dy), (dgate, dg, db, dsc, dsh)
    dx0, dmix, dg1, dln1g, dln1b, dsc2, dsh2 = _rowwise(
        "ln1_mod_bwd", post1_bwd, [x0, mix, dx1, du2], [g1 + mlp_token[0, 0], ln1_g, ln1_b, sc2, sh2],
        [(d, F32), (d, BF16)], [(1, d)] * 5, _rtile(t, 128))

    (dmix_in,) = _matmul("out_proj_dgrad", dmix, w_out_f, "nt", [F32])
    (gw_out,) = _matmul("out_proj_wgrad", mix_in, dmix, "tn", [BF16])
    dya = dmix_in[:, :da]
    out_src = [gw_out.reshape(N_DEV, d // N_DEV, d)]
    out_send, out_recv, out_src, out_land, out_token = _exchange_start(
        "scatter_out_grad_start", out_src, [landing(a) for a in out_src], True, gw_out)
    post_params_bwd = [lnx_g, lnx_b, rk_flat + out_token[0, 0], seg, seg_t]

    def post_bwd(i, nb, yb_, rb, kb_, vb_, gb, dyab, *params):
        _, vjp = jax.vjp(_rwkv_post_math, yb_, rb, kb_, vb_, gb, *params)
        dy, dr, dk, dv, dg, dlg, dlb, drk, _, _ = vjp(dyab)
        return (dy, dr, dk, dv, dg), (dlg, dlb, drk)
    dy_rec, dr_e, dk_e, dv_e, dgate, dlnxg, dlnxb, drk = _rowwise(
        "rwkv_post_bwd", post_bwd, [y_rec, r_, k2_, v_, gate_, dya], post_params_bwd,
        [(da, F32)] * 5, [(1, da)] * 3, tr_pre)

    dr_r, dlw_r, dk_r, dv_r, dkk_r, db_r = _rec_bwd(r_, lw_, k2_, v_, kk_, b_, s0s, invs, dy_rec)

    def pre_bwd(i, nb, b_rkv, b_w, b_a, b_g, dr1, dr2, dlw, dk1, dk2, dv1, dv2, dkk, dbb, dgt,
                h_rkv, h_w, h_a_, h_g, m_rkv, m_w, m_a, m_g, *params):
        blocks = [b_rkv, b_w, b_a, b_g]
        prevs = [_shift_prev(xb, hb, i) for xb, hb in zip(blocks, [h_rkv, h_w, h_a_, h_g])]
        mus = [m_rkv, m_w, m_a, m_g]
        s_rkv, s_w, s_a, s_g = [xb + (pb - xb) * mb for xb, pb, mb in zip(blocks, prevs, mus)]
        _, vjp = jax.vjp(_rwkv_pre_math, *split3(s_rkv), s_w, s_a, s_g, *params)
        grads = vjp((dr1 + dr2, dlw, dk1 + dk2, dv1 + dv2, dkk, dbb, dgt))
        g_rkv = jnp.concatenate(grads[:3], axis=1)
        g_w, g_a, g_g = grads[3:6]
        dmu = [jnp.sum(gs * (pb - xb), axis=0, keepdims=True)
               for gs, pb, xb in zip([g_rkv, g_w, g_a, g_g], prevs, blocks)]
        dw0, dwd, da0, dwa, dwg, dkk_, dka = grads[6:13]
        return (g_rkv, g_w, g_a, g_g), (*dmu, dw0, dwd, da0, dwa, dwg, dkk_, dka)

    pre_out = _rowwise(
        "rwkv_pre_bwd", pre_bwd,
        pre_rows + [dr_r, dr_e, dlw_r, dk_r, dk_e, dv_r, dv_e, dkk_r, db_r, dgate],
        [mu_rkv, mu_w, mu_a, mu_g] + pre_params,
        [(o_w, F32), (l_w, F32), (l_a, F32), (l_g, F32)],
        [(1, o_w), (1, l_w), (1, l_a), (1, l_g), (1, da), (l_w, da), (1, da), (l_a, da), (l_g, da),
         (1, da), (1, da)],
        _rtile(t, 64), halos=pre_halos)
    gs_rkv, gs_w, gs_a, gs_g = pre_out[:4]
    dmu_parts = pre_out[4:8]
    dw0, dwd, da0, dwa, dwg, dk_k, dk_a = pre_out[8:]

    def unshift_fn(i, nb, a1, a2, a3, a4, n1, n2, n3, n4, m1, m2, m3, m4):
        outs = [gs * (1.0 - mb) + _shift_next(gs * mb, hb * mb, i, nb)
                for gs, hb, mb in zip([a1, a2, a3, a4], [n1, n2, n3, n4], [m1, m2, m3, m4])]
        return outs, ()
    gs_list = [gs_rkv, gs_w, gs_a, gs_g]
    dp_rkv, dp_w, dp_a, dp_g = _rowwise(
        "token_shift_bwd", unshift_fn, gs_list, [mu_rkv, mu_w, mu_a, mu_g],
        [(o_w, BF16), (l_w, BF16), (l_a, BF16), (l_g, BF16)], [], tr_pre,
        halos=[(a, "next") for a in gs_list])

    dq, dkp, dkc, dvp, dvc, dbias, dsinks = _attn_bwd(q, kb, vb, bias, attn_sinks, dmix_in, 1)
    zpad = jnp.zeros((ATT_BLOCK, kb.shape[1]), F32)
    dkb = (dkc + jnp.concatenate([dkp[ATT_BLOCK:], zpad], axis=0)).astype(BF16)
    dvb = (dvc + jnp.concatenate([dvp[ATT_BLOCK:], zpad], axis=0)).astype(BF16)
    d_rpb = _small_dot("rpb_scatter", onehot, dbias.reshape(hq, -1), 1, 1)

    dp = jnp.concatenate([dp_rkv, dp_w, dp_a, dp_g, dq, dkb, dvb], axis=1)
    (gw_in_t,) = _matmul("in_proj_wgrad", dp, u1, "tn", [BF16])
    in_names = ["w_in", "w_decay_up", "w_iclr_up", "w_gate_up"]
    in_parts = [gw_in_t.reshape(N_DEV, -1, d), _cols_to_shards(dwd), _cols_to_shards(dwa), _cols_to_shards(dwg)]
    in_sibling = _pair_exchange("pair_in_grads", in_parts)
    in_src = [_pair_sum("pair_sum_" + nm, a, b) for nm, a, b in zip(in_names, in_parts, in_sibling)]
    in_send, in_recv, in_src, in_land, in_token = _exchange_start(
        "scatter_in_grads_start", in_src, [landing(a) for a in in_src], "chips", in_src[0])
    (du1,) = _matmul("in_proj_dgrad", dp, w_in_t, "nn", [F32], after=in_token)

    def embed_bwd(i, nb, xb, dx0v, du1v, g, b, sc, sh):
        _, vjp = jax.vjp(_embed_math, xb, g, b, sc, sh)
        dxv, dg, db, dsc, dsh = vjp((dx0v, du1v))
        return (dxv,), (dg, db, dsc, dsh)
    grad_x, dlng, dlnb, dsc1, dsh1 = _rowwise(
        "embed_ln_mod_bwd", embed_bwd, [x2d, dx0, du1], [lng, lnb, sc1, sh1], [(d, F32)], [(1, d)] * 4,
        _rtile(t, 128))

    dmod = jnp.concatenate([dsh1, dsc1, dg1, dsh2, dsc2, dg2], axis=1)
    small = {"ln_emb_g": dlng, "ln_emb_b": dlnb, "rpb_table": d_rpb, "b_mod": dmod,
             "mu_shift": jnp.concatenate(dmu_parts, axis=1), "w0": dw0, "a0": da0, "k_k": dk_k, "k_a": dk_a,
             "r_k": drk, "lnx_g": dlnxg, "lnx_b": dlnxb, "attn_sinks": dsinks, "ln1_g": dln1g, "ln1_b": dln1b,
             "ln2_g": dln2g, "ln2_b": dln2b}
    small_names = list(small)
    packed = jnp.concatenate([small[nm].reshape(1, -1) for nm in small_names], axis=1)
    sm_send, sm_recv, sm_src, sm_land, sm_token = _exchange_start(
        "gather_small_grads_start", [packed], [lax.empty((N_DEV,) + packed.shape, F32)], False, grad_x)

    grads, deltas, new_m, new_v = {}, {}, {}, {}

    def put(nm, res):
        shape = weights[nm].shape
        grads[nm], deltas[nm], new_m[nm], new_v[nm] = [a.reshape(shape) for a in res]

    def big_update(nm, parts):
        put(nm, _adamw("adamw_" + nm, weights[nm][0], mom_m[nm][0], mom_v[nm][0], parts))

    behind = sm_token
    big_update("w_down", _exchange_wait("scatter_w_down_wait", mlp_send[0], mlp_recv[0], mlp_src[0], mlp_land[0],
                                        True, behind, mlp_cols[0]))
    big_update("w_up", _exchange_wait("scatter_w_up_wait", mlp_send[1], mlp_recv[1], mlp_src[1], mlp_land[1],
                                      True, behind, mlp_cols[1]))
    big_update("w_out", _exchange_wait("scatter_w_out_wait", out_send[0], out_recv[0], out_src[0], out_land[0],
                                       True, behind))

    packed_all = _exchange_wait("gather_small_grads_wait", sm_send[0], sm_recv[0], sm_src[0], sm_land[0], False,
                                deltas["w_out"]).reshape(N_DEV, -1)
    n_mod = w_mod.shape[2]
    dmod_cols = lax.dynamic_slice_in_dim(packed_all[:, _offset(small, small_names, "b_mod"):], me * n_mod, n_mod,
                                         axis=1)
    put("w_mod", _adamw_outer("adamw_w_mod", w_mod[0], m_w_mod[0], v_w_mod[0], cond_all.T, dmod_cols))

    off = 0
    for nm in small_names:
        size = small[nm].size
        wshape = weights[nm].shape
        two_d = (1, size) if nm != "rpb_table" else wshape
        parts = packed_all[:, off:off + size].reshape((N_DEV,) + two_d)
        off += size
        put(nm, _adamw("adamw_" + nm, weights[nm].reshape(two_d), mom_m[nm].reshape(two_d),
                       mom_v[nm].reshape(two_d), parts))

    behind = deltas[small_names[-1]]
    for i, nm in enumerate(in_names):
        parts = _exchange_wait("scatter_" + nm + "_wait", in_send[i], in_recv[i], in_src[i], in_land[i],
                               "chips", behind)
        big_update(nm, jnp.swapaxes(parts, 1, 2) if nm == "w_in" else parts)

    loss = lax.psum(loss_acc[0, 0], MESH_AXES)
    return (loss, grad_x[None], *[grads[nm] for nm in names], *[deltas[nm] for nm in names],
            *[new_m[nm] for nm in names], *[new_v[nm] for nm in names])


def _offset(small, small_names, name):
    off = 0
    for nm in small_names:
        if nm == name:
            return off
        off += small[nm].size
    raise KeyError(name)
```

```python
import functools
import math

import jax
import jax.numpy as jnp
from jax import lax
from jax.experimental import pallas as pl
from jax.experimental.pallas import tpu as pltpu

F32 = jnp.float32
BF16 = jnp.bfloat16
HI = lax.Precision.HIGHEST
MESH_AXES = ("x", "y", "c")
N_DEV = 8

HEAD = 64
GQA_RATIO = 8
ATT_BLOCK = 128
RPB_MAX_DIST = 128
LN_EPS = 1e-5
LNX_EPS = 64e-5
DEPTH = 1
ALPHA = (2.0 * DEPTH) ** 0.25
CHUNK = 64
REC_HEADS = 32

ADAM_LR = 0.001
ADAM_B1 = 0.9
ADAM_B2 = 0.999
ADAM_EPS = 1e-08
ADAM_WD = 0.01
ADAM_STEP = 10

VMEM_LIMIT = 60 * 1024 * 1024


def _cparams(sem=None):
    return pltpu.CompilerParams(dimension_semantics=sem, vmem_limit_bytes=VMEM_LIMIT)


def _tile(dim, cap):
    best = None
    t = 128
    while t <= min(dim, cap):
        if dim % t == 0:
            best = t
        t += 128
    return best or dim


def _rtile(dim, cap):
    best = None
    t = 8
    while t <= min(dim, cap):
        if dim % t == 0:
            best = t
        t += 8
    return best or dim


def _split2(a):
    hi = a.astype(BF16)
    return hi, (a - hi.astype(F32)).astype(BF16)


def _raw_dot(a, b, ca, cb, prec):
    dims = (((ca,), (cb,)), ((), ()))
    mm = lambda p, q: lax.dot_general(p, q, dims, preferred_element_type=F32)
    if prec == "bf16":
        return mm(a.astype(BF16), b.astype(BF16))
    if prec == "x3":
        (ah, al), (bh, bl) = _split2(a), _split2(b)
        return mm(ah, bh) + (mm(ah, bl) + mm(al, bh))
    if prec == "mask":
        ab = a.astype(BF16)
        b1, b2 = _split2(b)
        b3 = (b - b1.astype(F32) - b2.astype(F32)).astype(BF16)
        return mm(ab, b1) + (mm(ab, b2) + mm(ab, b3))
    if prec == "mb2":
        (ah, al), bb = _split2(a), b.astype(BF16)
        return mm(ah, bb) + mm(al, bb)
    return lax.dot_general(a, b, dims, precision=HI, preferred_element_type=F32)


@functools.partial(jax.custom_vjp, nondiff_argnums=(2, 3, 4))
def _bf16_dot(a, b, ca, cb, prec):
    return _raw_dot(a, b, ca, cb, prec)


def _bf16_dot_fwd(a, b, ca, cb, prec):
    return _raw_dot(a, b, ca, cb, prec), (a, b)


def _bf16_dot_bwd(ca, cb, prec, res, g):
    a, b = res
    if prec == "mask":
        return jnp.zeros_like(a), _bf16_dot(a, g, 1 - ca, 0, prec)
    if prec == "mb2":
        return _bf16_dot(g, b, 1, 1, prec), jnp.zeros_like(b)
    if ca == 1:
        da = _bf16_dot(g, b, 1, 1 - cb, prec)
    else:
        da = _bf16_dot(b, g, 1 - cb, 1, prec)
    if cb == 0:
        db = _bf16_dot(a, g, 1 - ca, 0, prec)
    else:
        db = _bf16_dot(g, a, 0, 1 - ca, prec)
    return da, db


_bf16_dot.defvjp(_bf16_dot_fwd, _bf16_dot_bwd)


def _dot(a, b, ca, cb, prec):
    return _raw_dot(a, b, ca, cb, prec) if prec == "hi" else _bf16_dot(a, b, ca, cb, prec)


def _sigmoid(z):
    return 1.0 / (1.0 + jnp.exp(-z))


def _softplus(z):
    return jnp.maximum(z, 0.0) + jnp.log(1.0 + jnp.exp(-jnp.abs(z)))


MATMUL_VMEM_BUDGET = 51 * 1024 * 1024


def _matmul_tiles(m, n, k, in_bytes, out_dtypes, n_extras):
    tm, tn = _tile(m, 1024), _tile(n, 1024)
    out_bytes = sum(jnp.dtype(dt).itemsize for dt in out_dtypes)
    for cap in (4096, 2048, 1024, 512, 256, 128):
        tk = _tile(k, cap)
        acc = 4 * tm * tn if tk < k else 0
        need = 2 * in_bytes * (tm + tn) * tk + 2 * tm * tn * (out_bytes + 4 * n_extras) + acc + 4 * tm * tn
        if need <= MATMUL_VMEM_BUDGET:
            break
    return tm, tn, tk


def _matmul(name, a, b, mode, out_dtypes, epilogue=None, extras=(), after=None):
    if mode == "nn":
        (m, k), n = a.shape, b.shape[1]
    elif mode == "nt":
        (m, k), n = a.shape, b.shape[0]
    else:
        (k, m), n = a.shape, b.shape[1]
    tm, tn, tk = _matmul_tiles(m, n, k, a.dtype.itemsize, out_dtypes, len(extras))
    nk = k // tk
    ne, no = len(extras), len(out_dtypes)
    ca, cb = {"nn": (1, 0), "nt": (1, 1), "tn": (0, 0)}[mode]

    n_after = 0 if after is None else 1

    def body(a_ref, b_ref, *rest):
        rest = rest[n_after:]
        extra_refs, out_refs = rest[:ne], rest[ne:ne + no]
        acc = rest[-1] if nk > 1 else None
        kk = pl.program_id(2)
        part = _raw_dot(a_ref[...], b_ref[...], ca, cb, "bf16")

        def finish(total):
            res = epilogue(total, *[e[...] for e in extra_refs]) if epilogue else (total,)
            for o, v in zip(out_refs, res):
                o[...] = v.astype(o.dtype)

        if nk == 1:
            finish(part)
            return

        @pl.when(kk == 0)
        def _():
            acc[...] = part

        @pl.when((kk > 0) & (kk < nk - 1))
        def _():
            acc[...] += part

        @pl.when(kk == nk - 1)
        def _():
            finish(acc[...] + part)

    a_spec = (pl.BlockSpec((tk, tm), lambda i, j, kk: (kk, i)) if mode == "tn"
              else pl.BlockSpec((tm, tk), lambda i, j, kk: (i, kk)))
    b_spec = (pl.BlockSpec((tn, tk), lambda i, j, kk: (j, kk)) if mode == "nt"
              else pl.BlockSpec((tk, tn), lambda i, j, kk: (kk, j)))
    mn_spec = pl.BlockSpec((tm, tn), lambda i, j, kk: (i, j))
    after_specs = [pl.BlockSpec(memory_space=pl.ANY)] * n_after
    outs = pl.pallas_call(
        body, name=name, grid=(m // tm, n // tn, nk),
        in_specs=[a_spec, b_spec] + after_specs + [mn_spec] * ne,
        out_specs=[mn_spec] * no,
        out_shape=[jax.ShapeDtypeStruct((m, n), dt) for dt in out_dtypes],
        scratch_shapes=[pltpu.VMEM((tm, tn), F32)] if nk > 1 else [],
        compiler_params=_cparams(("parallel", "parallel", "arbitrary")),
    )(a, b, *([after] * n_after), *extras)
    return outs


def _rowwise(name, fn, rows, bcasts, out_rows, out_accs, tr, halos=()):
    t = rows[0].shape[0]
    nb = t // tr
    n_in = len(rows) + len(halos) + len(bcasts)
    n_ro = len(out_rows)

    def body(*refs):
        ins = [r[...] for r in refs[:n_in]]
        o_refs = refs[n_in:]
        i = pl.program_id(0)
        routs, aouts = fn(i, nb, *ins)
        for ref, v in zip(o_refs[:n_ro], routs):
            ref[...] = v.astype(ref.dtype)
        for ref, v in zip(o_refs[n_ro:], aouts):
            @pl.when(i == 0)
            def _(ref=ref):
                ref[...] = jnp.zeros_like(ref)
            ref[...] += v.reshape(ref.shape)

    in_specs = [pl.BlockSpec((tr, r.shape[1]), lambda i: (i, 0)) for r in rows]
    for arr, which in halos:
        if which == "prev":
            in_specs.append(pl.BlockSpec((8, arr.shape[1]), lambda i: (jnp.maximum(i * (tr // 8) - 1, 0), 0)))
        else:
            in_specs.append(pl.BlockSpec((8, arr.shape[1]),
                                         lambda i: (jnp.minimum((i + 1) * (tr // 8), t // 8 - 1), 0)))
    for bc in bcasts:
        in_specs.append(pl.BlockSpec(bc.shape, lambda i, nd=bc.ndim: (0,) * nd))
    out_specs = [pl.BlockSpec((tr, c), lambda i: (i, 0)) for c, _ in out_rows]
    out_specs += [pl.BlockSpec(s, lambda i, nd=len(s): (0,) * nd) for s in out_accs]
    out_shape = [jax.ShapeDtypeStruct((t, c), dt) for c, dt in out_rows]
    out_shape += [jax.ShapeDtypeStruct(s, F32) for s in out_accs]
    return pl.pallas_call(
        body, name=name, grid=(nb,), in_specs=in_specs, out_specs=out_specs, out_shape=out_shape,
        compiler_params=_cparams(("arbitrary",)),
    )(*rows, *[h[0] for h in halos], *bcasts)


def _shift_prev(x, halo, i):
    rolled = pltpu.roll(x, 1, 0)
    first = jnp.where(i == 0, 0.0, halo[7:8, :])
    row = lax.broadcasted_iota(jnp.int32, x.shape, 0)
    return jnp.where(row == 0, first, rolled)


def _shift_next(x, halo, i, nb):
    rolled = pltpu.roll(x, x.shape[0] - 1, 0)
    last = jnp.where(i == nb - 1, 0.0, halo[0:1, :])
    row = lax.broadcasted_iota(jnp.int32, x.shape, 0)
    return jnp.where(row == x.shape[0] - 1, last, rolled)


def _ln(x, g, b, eps=LN_EPS):
    mu = jnp.mean(x, axis=-1, keepdims=True)
    xc = x - mu
    var = jnp.mean(xc * xc, axis=-1, keepdims=True)
    return xc * lax.rsqrt(var + eps) * g + b


def _embed_math(x, g, b, sc, sh):
    x0 = _ln(x, g, b)
    return x0, x0 * (1.0 + sc) + sh


def _post_math(xin, y, gate, g, b, sc, sh):
    x1 = _ln(ALPHA * xin + (1.0 + gate) * y, g, b)
    return x1, x1 * (1.0 + sc) + sh


def _loss_math(xin, h, tgt, gate, g, b):
    x2 = _ln(ALPHA * xin + (1.0 + gate) * h, g, b)
    err = x2 - tgt
    return 0.5 * jnp.sum(jnp.mean(err * err, axis=-1))


def _rwkv_pre_math(r, k, v, xw, xa, xg, w0, wd, a0, wa, wg, k_k, k_a, seg, seg_t, dot=_dot):
    wpre = -_softplus(-(w0 + dot(jnp.tanh(xw), wd, 1, 0, "x3"))) - 0.5
    lw = -jnp.exp(wpre)
    a = _sigmoid(a0 + dot(xa, wa, 1, 0, "x3"))
    g = dot(_sigmoid(xg), wg, 1, 0, "x3")
    kk = k * k_k
    norm = jnp.sqrt(dot(kk * kk, seg, 1, 0, "mb2"))
    kkn = kk * dot(1.0 / jnp.maximum(norm, 1e-12), seg_t, 1, 0, "mb2")
    k2 = k * (1.0 + (a - 1.0) * k_a)
    return r, lw, k2, v, kkn, kkn * a, g


def _rwkv_post_math(y, r, k2, v, g, lnx_g, lnx_b, r_k, seg, seg_t, dot=_dot):
    inv = 1.0 / HEAD
    spread = lambda z: dot(dot(z, seg, 1, 0, "mb2"), seg_t, 1, 0, "mb2")
    mu = spread(y) * inv
    yc = y - mu
    var = spread(yc * yc) * inv
    yn = yc * lax.rsqrt(var + LNX_EPS) * lnx_g + lnx_b
    bonus = spread(r * k2 * r_k) * v
    return (yn + bonus) * g


@jax.custom_vjp
def _known_inverse(low, inv):
    return inv


def _known_inverse_fwd(low, inv):
    return inv, inv


def _known_inverse_bwd(inv, g):
    left = [_raw_dot(t, gi, 0, 0, "bf16") for t, gi in zip(inv, g)]
    return [_raw_dot(x, t, 1, 1, "bf16") for x, t in zip(left, inv)], [jnp.zeros_like(t) for t in inv]


_known_inverse.defvjp(_known_inverse_fwd, _known_inverse_bwd)


def _chunk_math(s0, r, lw, k, v, kk, b, known_inv=None, dot=_dot):
    n = len(r)
    hs = range(n)
    c = r[0].shape[0]
    ti = lax.broadcasted_iota(jnp.int32, (2 * c, 2 * c), 0)
    tj = lax.broadcasted_iota(jnp.int32, (2 * c, 2 * c), 1)
    tt, jj = ti & (c - 1), tj & (c - 1)
    quad = jnp.where(ti < c, (tt > jj).astype(F32), (tt >= jj).astype(F32))
    incl = quad[c:, :c]
    eye = (ti[:c, :c] == tj[:c, :c]).astype(F32)
    cl = [dot(incl, lw[i], 1, 0, "mask") for i in hs]
    ge = [jnp.exp(cl[i]) for i in hs]
    gi = [jnp.exp(-cl[i]) for i in hs]
    ar = [jnp.concatenate([-kk[i] * jnp.exp(cl[i] - lw[i]), r[i] * ge[i]], axis=0) for i in hs]
    kb = [jnp.concatenate([k[i] * gi[i], b[i] * gi[i]], axis=0) for i in hs]
    m = [dot(ar[i], kb[i], 1, 1, "x3") * quad for i in hs]
    ars0 = [dot(ar[i], s0[i], 1, 1, "bf16") for i in hs]
    mv = [dot(m[i][:c, :c], v[i], 1, 0, "bf16") for i in hs]
    pw = [m[i][:c, c:] for i in hs]
    if known_inv is None:
        inv = [eye + pw[i] for i in hs]
        for _ in range(int(math.log2(c)) - 1):
            pw = [dot(pw[i], pw[i], 1, 0, "bf16") for i in hs]
            inv = [inv[i] + dot(inv[i], pw[i], 1, 0, "bf16") for i in hs]
    else:
        inv = _known_inverse(pw, known_inv)
    u = [dot(inv[i], ars0[i][:c] + mv[i], 1, 0, "bf16") for i in hs]
    vu = [jnp.concatenate([v[i], u[i]], axis=0) for i in hs]
    y = [ars0[i][c:] + dot(m[i][c:], vu[i], 1, 0, "bf16") for i in hs]
    s1 = [(s0[i] + dot(vu[i], kb[i], 0, 0, "x3")) * ge[i][c - 1:c, :] for i in hs]
    return y, s1, inv


def _attn_math(q, kp, kc, vp, vc, bias, sinks, first, dot=_dot):
    hq = q.shape[1] // HEAD
    hkv = kc.shape[1] // HEAD
    group = hq // hkv
    cols = group * ATT_BLOCK
    kj = lax.broadcasted_iota(jnp.int32, (2 * ATT_BLOCK, cols), 0)
    qi = lax.broadcasted_iota(jnp.int32, (2 * ATT_BLOCK, cols), 1) & (ATT_BLOCK - 1)
    dist = qi + ATT_BLOCK - kj
    valid = (dist >= 0) & (dist < ATT_BLOCK) & (jnp.logical_not(first) | (kj >= ATT_BLOCK))
    eye = (lax.broadcasted_iota(jnp.int32, (ATT_BLOCK, ATT_BLOCK), 0)
           == lax.broadcasted_iota(jnp.int32, (ATT_BLOCK, ATT_BLOCK), 1)).astype(F32)
    outs = []
    for j in range(hkv):
        heads = range(j * group, (j + 1) * group)
        kband = jnp.concatenate([kp[:, j * HEAD:(j + 1) * HEAD], kc[:, j * HEAD:(j + 1) * HEAD]], axis=0)
        vband = jnp.concatenate([vp[:, j * HEAD:(j + 1) * HEAD], vc[:, j * HEAD:(j + 1) * HEAD]], axis=0)
        qg = jnp.concatenate([q[:, h * HEAD:(h + 1) * HEAD] for h in heads], axis=0)
        bias_g = jnp.concatenate([bias[h] for h in heads], axis=1)
        sink = jnp.concatenate([jnp.broadcast_to(sinks[0:1, h:h + 1], (1, ATT_BLOCK)) for h in heads], axis=1)
        s = dot(kband, qg, 1, 1, "bf16") * (HEAD ** -0.5) + bias_g
        s = jnp.where(valid, s, -1e30)
        m = jnp.maximum(jnp.max(s, axis=0, keepdims=True), sink)
        e = jnp.exp(s - m)
        p = e / (jnp.sum(e, axis=0, keepdims=True) + jnp.exp(sink - m))
        o_t = dot(vband, p, 0, 0, "bf16")
        outs += [dot(eye, o_t[:, g * ATT_BLOCK:(g + 1) * ATT_BLOCK], 1, 1, "bf16") for g in range(group)]
    return jnp.concatenate(outs, axis=1)


def _rec_specs(t, da, gh, reverse):
    nc = t // CHUNK
    if reverse:
        return pl.BlockSpec((CHUNK, gh * HEAD), lambda hg, c: (nc - 1 - c, hg))
    return pl.BlockSpec((CHUNK, gh * HEAD), lambda hg, c: (c, hg))


def _rec_fwd(r, lw, k, v, kk, b):
    t, da = r.shape
    h = da // HEAD
    gh = min(REC_HEADS, h)
    nc = t // CHUNK

    def body(r_ref, lw_ref, k_ref, v_ref, kk_ref, b_ref, y_ref, s0_ref, inv_ref, state):
        @pl.when(pl.program_id(1) == 0)
        def _():
            state[...] = jnp.zeros_like(state)

        sls = [slice(i * HEAD, (i + 1) * HEAD) for i in range(gh)]
        heads = lambda ref: [ref[:, sl] for sl in sls]
        s0 = [state[i] for i in range(gh)]
        y, s1, inv = _chunk_math(s0, heads(r_ref), heads(lw_ref), heads(k_ref), heads(v_ref), heads(kk_ref),
                                 heads(b_ref), dot=_raw_dot)
        for i, sl in enumerate(sls):
            s0_ref[0, i] = s0[i]
            inv_ref[0, i] = inv[i]
            y_ref[:, sl] = y[i]
            state[i] = s1[i]

    spec = _rec_specs(t, da, gh, False)
    per_chunk = pl.BlockSpec((1, gh, HEAD, HEAD), lambda hg, c: (c, hg, 0, 0))
    return pl.pallas_call(
        body, name="rwkv_recurrence_fwd", grid=(h // gh, nc),
        in_specs=[spec] * 6,
        out_specs=[spec, per_chunk, per_chunk],
        out_shape=[jax.ShapeDtypeStruct((t, da), F32)] + [jax.ShapeDtypeStruct((nc, h, HEAD, HEAD), F32)] * 2,
        scratch_shapes=[pltpu.VMEM((gh, HEAD, HEAD), F32)],
        compiler_params=_cparams(("parallel", "arbitrary")),
    )(r, lw, k, v, kk, b)


def _rec_bwd(r, lw, k, v, kk, b, s0s, invs, dy):
    t, da = r.shape
    h = da // HEAD
    gh = min(REC_HEADS, h)
    nc = t // CHUNK

    def body(r_ref, lw_ref, k_ref, v_ref, kk_ref, b_ref, dy_ref, s0_ref, inv_ref,
             dr_ref, dlw_ref, dk_ref, dv_ref, dkk_ref, db_ref, dstate):
        @pl.when(pl.program_id(1) == 0)
        def _():
            dstate[...] = jnp.zeros_like(dstate)

        sls = [slice(i * HEAD, (i + 1) * HEAD) for i in range(gh)]
        heads = lambda ref: [ref[:, sl] for sl in sls]
        known = [inv_ref[0, i] for i in range(gh)]
        fn = lambda *args: _chunk_math(*args, known_inv=known)[:2]
        _, vjp = jax.vjp(fn, [s0_ref[0, i] for i in range(gh)], heads(r_ref), heads(lw_ref),
                         heads(k_ref), heads(v_ref), heads(kk_ref), heads(b_ref))
        grads = vjp((heads(dy_ref), [dstate[i] for i in range(gh)]))
        for i, sl in enumerate(sls):
            dstate[i] = grads[0][i]
            for ref, val in zip((dr_ref, dlw_ref, dk_ref, dv_ref, dkk_ref, db_ref), grads[1:]):
                ref[:, sl] = val[i]

    spec = _rec_specs(t, da, gh, True)
    return pl.pallas_call(
        body, name="rwkv_recurrence_bwd", grid=(h // gh, nc),
        in_specs=[spec] * 7 + [pl.BlockSpec((1, gh, HEAD, HEAD), lambda hg, c: (nc - 1 - c, hg, 0, 0))] * 2,
        out_specs=[spec] * 6,
        out_shape=[jax.ShapeDtypeStruct((t, da), F32)] * 6,
        scratch_shapes=[pltpu.VMEM((gh, HEAD, HEAD), F32)],
        compiler_params=_cparams(("parallel", "arbitrary")),
    )(r, lw, k, v, kk, b, dy, s0s, invs)


def _attn_specs(t, hq_w, hkv_w):
    nb = t // ATT_BLOCK
    cur = lambda w: pl.BlockSpec((ATT_BLOCK, w), lambda n: (n, 0))
    prev = lambda w: pl.BlockSpec((ATT_BLOCK, w), lambda n: (jnp.maximum(n - 1, 0), 0))
    return nb, cur, prev


def _attn_fwd(q, kb, vb, bias, sinks):
    t, qw = q.shape
    kw = kb.shape[1]
    nb, cur, prev = _attn_specs(t, qw, kw)

    def body(q_ref, kp_ref, kc_ref, vp_ref, vc_ref, bias_ref, sink_ref, o_ref):
        first = pl.program_id(0) == 0
        o = _attn_math(q_ref[...], kp_ref[...], kc_ref[...], vp_ref[...], vc_ref[...],
                       bias_ref[...], sink_ref[...], first, dot=_raw_dot)
        o_ref[...] = o.astype(o_ref.dtype)

    full = lambda a: pl.BlockSpec(a.shape, lambda n, nd=a.ndim: (0,) * nd)
    return pl.pallas_call(
        body, name="swa_attention_fwd", grid=(nb,),
        in_specs=[cur(qw), prev(kw), cur(kw), prev(kw), cur(kw), full(bias), full(sinks)],
        out_specs=cur(qw), out_shape=jax.ShapeDtypeStruct((t, qw), BF16),
        compiler_params=_cparams(("parallel",)),
    )(q, kb, kb, vb, vb, bias, sinks)


def _attn_bwd(q, kb, vb, bias, sinks, do, col_block):
    t, qw = q.shape
    kw = kb.shape[1]
    nb, cur, prev = _attn_specs(t, qw, kw)

    def body(q_ref, kp_ref, kc_ref, vp_ref, vc_ref, bias_ref, sink_ref, do_ref,
             dq_ref, dkp_ref, dkc_ref, dvp_ref, dvc_ref, dbias_ref, dsink_ref):
        n = pl.program_id(0)
        first = n == 0
        fn = functools.partial(_attn_math, first=first)
        _, vjp = jax.vjp(fn, q_ref[...], kp_ref[...], kc_ref[...], vp_ref[...], vc_ref[...],
                         bias_ref[...], sink_ref[...])
        dq, dkp, dkc, dvp, dvc, dbias, dsink = vjp(do_ref[...].astype(F32))
        dq_ref[...] = dq.astype(dq_ref.dtype)
        dkp_ref[...] = dkp
        dkc_ref[...] = dkc
        dvp_ref[...] = dvp
        dvc_ref[...] = dvc

        @pl.when(first)
        def _():
            dbias_ref[...] = jnp.zeros_like(dbias_ref)
            dsink_ref[...] = jnp.zeros_like(dsink_ref)

        dbias_ref[...] += dbias
        dsink_ref[...] += dsink

    full = lambda a: pl.BlockSpec(a.shape, lambda n, nd=a.ndim: (0,) * nd)
    kshape = jax.ShapeDtypeStruct((t, kw), F32)
    return pl.pallas_call(
        body, name="swa_attention_bwd", grid=(nb,),
        in_specs=[cur(qw), prev(kw), cur(kw), prev(kw), cur(kw), full(bias), full(sinks),
                  pl.BlockSpec((ATT_BLOCK, qw), lambda n: (n, col_block))],
        out_specs=[cur(qw), cur(kw), cur(kw), cur(kw), cur(kw), full(bias), full(sinks)],
        out_shape=[jax.ShapeDtypeStruct((t, qw), BF16), kshape, kshape, kshape, kshape,
                   jax.ShapeDtypeStruct(bias.shape, F32), jax.ShapeDtypeStruct(sinks.shape, F32)],
        compiler_params=_cparams(("arbitrary",)),
    )(q, kb, kb, vb, vb, bias, sinks, do)


def _bucket_onehot():
    qi = jnp.arange(ATT_BLOCK)[None, :]
    kj = jnp.arange(2 * ATT_BLOCK)[:, None]
    n = jnp.maximum(qi + ATT_BLOCK - kj, 0)
    buckets, max_exact = 32, 16
    nf = jnp.maximum(n, 1).astype(F32)
    large = max_exact + (jnp.log(nf / max_exact) / math.log(RPB_MAX_DIST / max_exact)
                         * (buckets - max_exact)).astype(jnp.int32)
    bucket = jnp.where(n < max_exact, n, jnp.minimum(large, buckets - 1)).reshape(-1)
    return (bucket[None, :] == jnp.arange(buckets)[:, None]).astype(F32)


def _small_dot(name, a, b, ca, cb):
    m = a.shape[1 - ca]
    n = b.shape[1 - cb]

    def body(a_ref, b_ref, o_ref):
        o_ref[...] = _raw_dot(a_ref[...], b_ref[...], ca, cb, "hi")

    return pl.pallas_call(body, name=name, out_shape=jax.ShapeDtypeStruct((m, n), F32),
                          compiler_params=_cparams())(a, b)


def _mod_fwd(c_all, w_mod):
    d, n = w_mod.shape
    tn = _tile(n, 512)

    def body(c_ref, w_ref, o_ref, cond_ref):
        cv = c_ref[...]
        cond = cv * _sigmoid(cv)
        cond_ref[...] = cond
        o_ref[...] = _raw_dot(cond, w_ref[...], 1, 0, "hi")

    return pl.pallas_call(
        body, name="adaln_mod_fwd", grid=(n // tn,),
        in_specs=[pl.BlockSpec(c_all.shape, lambda j: (0, 0)), pl.BlockSpec((d, tn), lambda j: (0, j))],
        out_specs=[pl.BlockSpec((c_all.shape[0], tn), lambda j: (0, j)),
                   pl.BlockSpec(c_all.shape, lambda j: (0, 0))],
        out_shape=[jax.ShapeDtypeStruct((c_all.shape[0], n), F32), jax.ShapeDtypeStruct(c_all.shape, F32)],
        compiler_params=_cparams(("arbitrary",)),
    )(c_all, w_mod)


def _adam_math(w, g, m, v):
    m = ADAM_B1 * m + (1.0 - ADAM_B1) * g
    v = ADAM_B2 * v + (1.0 - ADAM_B2) * (g * g)
    m_hat = m / (1.0 - ADAM_B1 ** ADAM_STEP)
    v_hat = v / (1.0 - ADAM_B2 ** ADAM_STEP)
    delta = -ADAM_LR * (m_hat / (jnp.sqrt(v_hat) + ADAM_EPS) + ADAM_WD * w)
    return delta, m, v


def _adamw(name, w, m, v, gparts):
    r, c = w.shape
    p = gparts.shape[0]
    tr = _rtile(r, max(8, (1 << 18) // max(c, 1) // 8 * 8))

    def body(w_ref, m_ref, v_ref, g_ref, go_ref, d_ref, mo_ref, vo_ref):
        g = g_ref[0].astype(F32)
        for s in range(1, p):
            g = g + g_ref[s].astype(F32)
        delta, mn, vn = _adam_math(w_ref[...], g, m_ref[...], v_ref[...])
        go_ref[...] = g
        d_ref[...] = delta
        mo_ref[...] = mn
        vo_ref[...] = vn

    spec = pl.BlockSpec((tr, c), lambda i: (i, 0))
    return pl.pallas_call(
        body, name=name, grid=(r // tr,),
        in_specs=[spec, spec, spec, pl.BlockSpec((p, tr, c), lambda i: (0, i, 0))],
        out_specs=[spec] * 4, out_shape=[jax.ShapeDtypeStruct((r, c), F32)] * 4,
        compiler_params=_cparams(("parallel",)),
    )(w, m, v, gparts)


def _adamw_outer(name, w, m, v, cond_t, dmod):
    d, n = w.shape
    tr, tn = _rtile(d, 512), _tile(n, 1024)

    def body(w_ref, m_ref, v_ref, c_ref, dm_ref, go_ref, d_ref, mo_ref, vo_ref):
        g = _raw_dot(c_ref[...], dm_ref[...], 1, 0, "hi")
        delta, mn, vn = _adam_math(w_ref[...], g, m_ref[...], v_ref[...])
        go_ref[...] = g
        d_ref[...] = delta
        mo_ref[...] = mn
        vo_ref[...] = vn

    spec = pl.BlockSpec((tr, tn), lambda i, j: (i, j))
    return pl.pallas_call(
        body, name=name, grid=(d // tr, n // tn),
        in_specs=[spec, spec, spec, pl.BlockSpec((tr, cond_t.shape[1]), lambda i, j: (i, 0)),
                  pl.BlockSpec((dmod.shape[0], tn), lambda i, j: (0, j))],
        out_specs=[spec] * 4, out_shape=[jax.ShapeDtypeStruct((d, n), F32)] * 4,
        compiler_params=_cparams(("parallel", "parallel")),
    )(w, m, v, cond_t, dmod)


def _all_gather(name, arrays):
    n = len(arrays)

    def body(*refs):
        ins, outs = refs[:n], refs[n:2 * n]
        send_sems, recv_sems, local_sems = refs[2 * n:]
        x, y, c = lax.axis_index("x"), lax.axis_index("y"), lax.axis_index("c")
        me, sibling = (x, y, c), (x, y, 1 - c)
        chips = [(1 - x, y), (x, 1 - y), (1 - x, 1 - y)]

        def copy(a, k, block, to, src=None):
            rows = outs[a].at[4 * block[0] + 2 * block[1] + block[2]]
            return pltpu.make_async_remote_copy(
                src_ref=rows if src is None else src, dst_ref=rows, send_sem=send_sems.at[a, k],
                recv_sem=recv_sems.at[a, k], device_id=to, device_id_type=pl.DeviceIdType.MESH)

        mine = [pltpu.make_async_copy(ins[a], outs[a].at[4 * x + 2 * y + c], local_sems.at[a]) for a in range(n)]
        for cp in mine:
            cp.start()
        first = []
        for a in range(n):
            first.append(copy(a, 0, me, sibling, src=ins[a]))
            first += [copy(a, 1 + j, me, (*chip, c), src=ins[a]) for j, chip in enumerate(chips)]
        for cp in first:
            cp.start()
        passed = []
        for a in range(n):
            for j, chip in enumerate(chips):
                copy(a, 1 + j, (*chip, c), me).wait_recv()
                passed.append(copy(a, 4 + j, (*chip, c), sibling))
                passed[-1].start()
        for a in range(n):
            copy(a, 0, sibling, me).wait_recv()
            for j, chip in enumerate(chips):
                copy(a, 4 + j, (*chip, 1 - c), me).wait_recv()
        for cp in first + passed:
            cp.wait_send()
        for cp in mine:
            cp.wait()

    any_spec = pl.BlockSpec(memory_space=pl.ANY)
    return pl.pallas_call(
        body, name=name, in_specs=[any_spec] * n, out_specs=[any_spec] * n,
        out_shape=[jax.ShapeDtypeStruct((N_DEV,) + a.shape, a.dtype) for a in arrays],
        scratch_shapes=[pltpu.SemaphoreType.DMA((n, N_DEV - 1)), pltpu.SemaphoreType.DMA((n, N_DEV - 1)),
                        pltpu.SemaphoreType.DMA((n,))],
    )(*arrays)


def _peer(p):
    x, y, c = lax.axis_index("x"), lax.axis_index("y"), lax.axis_index("c")
    px, py, pc = x ^ ((p >> 2) & 1), y ^ ((p >> 1) & 1), c ^ (p & 1)
    return (px, py, pc), 4 * px + 2 * py + pc


def _block(ref, d, cols, rows=None):
    if cols:
        r = slice(None) if rows is None else pl.ds(pl.multiple_of(rows[0], 8), rows[1])
        return ref.at[r, pl.ds(pl.multiple_of(d * cols, cols), cols)]
    return ref.at[d] if rows is None else ref.at[d, pl.ds(pl.multiple_of(rows[0], 8), rows[1])]


def _split_copy(src_ref, land_ref, send_sems, recv_sems, p, scatter, arriving, cols=None, halves=False):
    x, y, c = lax.axis_index("x"), lax.axis_index("y"), lax.axis_index("c")
    me = 4 * x + 2 * y + c
    dev, idx = _peer(p)
    if scatter == "chips":
        src, dst = src_ref.at[idx >> 1], land_ref.at[(idx if arriving else me) >> 1]
    elif scatter:
        src, dst = _block(src_ref, idx, cols), land_ref.at[idx if arriving else me]
    elif halves and p >= 2:
        half = src_ref.shape[0] // 2
        rows = ((c if arriving else dev[2]) * half, half)
        src = src_ref.at[pl.ds(pl.multiple_of(rows[0], 8), half)]
        dst = _block(land_ref, idx if arriving else me, cols, rows)
    else:
        src, dst = src_ref, _block(land_ref, idx if arriving else me, cols)
    return pltpu.make_async_remote_copy(
        src_ref=src, dst_ref=dst, send_sem=send_sems.at[p - 1], recv_sem=recv_sems.at[p - 1], device_id=dev,
        device_id_type=pl.DeviceIdType.MESH)


_HBM_SPEC = pl.BlockSpec(memory_space=pltpu.HBM)
_SEM_SPEC = pl.BlockSpec(memory_space=pltpu.SEMAPHORE)
_DATAFLOW = pltpu.SideEffectType.DATAFLOW_SIDE_EFFECTING


def _own_copy(src_ref, land_ref, send_sems, scatter, cols):
    me = 4 * lax.axis_index("x") + 2 * lax.axis_index("y") + lax.axis_index("c")
    if scatter == "chips":
        src, dst = src_ref.at[me >> 1], land_ref.at[me >> 1]
    elif scatter:
        src, dst = _block(src_ref, me, cols), land_ref.at[me]
    else:
        src, dst = src_ref, _block(land_ref, me, cols)
    return pltpu.make_async_copy(src, dst, send_sems.at[N_DEV - 1])


def _peers(scatter):
    return (2, 4, 6) if scatter == "chips" else tuple(range(1, N_DEV))


def _exchange_start(name, srcs, lands, scatter, after, cols=None, halves=False):
    n = len(srcs)
    cols = cols or [None] * n

    def body(*refs):
        src_refs, land_refs = refs[:n], refs[n:2 * n]
        outs = refs[2 * n + 1:]
        send, recv, token = outs[:n], outs[n:2 * n], outs[-1]
        for a in range(n):
            for p in _peers(scatter):
                _split_copy(src_refs[a], land_refs[a], send[a], recv[a], p, scatter, False, cols[a],
                            halves).start()
            _own_copy(src_refs[a], land_refs[a], send[a], scatter, cols[a]).start()
        token[...] = jnp.zeros_like(token)

    sems = [pltpu.SemaphoreType.DMA((N_DEV,))] * n + [pltpu.SemaphoreType.DMA((N_DEV - 1,))] * n
    hbm = [pltpu.HBM(a.shape, a.dtype) for a in list(srcs) + list(lands)]
    res = pl.pallas_call(
        body, name=name,
        out_shape=sems + hbm + [jax.ShapeDtypeStruct((8, 128), F32)],
        in_specs=[_HBM_SPEC] * (2 * n) + [pl.BlockSpec(memory_space=pl.ANY)],
        out_specs=[_SEM_SPEC] * (2 * n) + [_HBM_SPEC] * (2 * n) + [pl.BlockSpec(memory_space=pltpu.VMEM)],
        input_output_aliases={i: 2 * n + i for i in range(2 * n)},
        compiler_params=pltpu.CompilerParams(has_side_effects=_DATAFLOW),
    )(*[pltpu.with_memory_space_constraint(a, pltpu.HBM) for a in list(srcs) + list(lands)], after)
    return res[:n], res[n:2 * n], res[2 * n:3 * n], res[3 * n:4 * n], res[-1]


def _exchange_wait(name, send_sem, recv_sem, src, land, scatter, after, cols=None, halves=False):
    def body(src_ref, land_ref, send, recv, after_ref, src_out, land_out):
        for p in _peers(scatter):
            cp = _split_copy(src_ref, land_ref, send, recv, p, scatter, True, cols, halves)
            cp.wait_send()
            cp.wait_recv()
        _own_copy(src_ref, land_ref, send, scatter, cols).wait()

    return pl.pallas_call(
        body, name=name,
        out_shape=(pltpu.HBM(src.shape, src.dtype), pltpu.HBM(land.shape, land.dtype)),
        in_specs=[_HBM_SPEC, _HBM_SPEC, _SEM_SPEC, _SEM_SPEC, pl.BlockSpec(memory_space=pl.ANY)],
        out_specs=(_HBM_SPEC, _HBM_SPEC), input_output_aliases={0: 0, 1: 1},
        compiler_params=pltpu.CompilerParams(has_side_effects=_DATAFLOW),
    )(src, land, send_sem, recv_sem, after)[1]


def _sibling_fill(name, land, cols):
    rows = land.shape[0] if cols else land.shape[1]
    half = rows // 2

    def body(in_ref, out_ref, send_sems, recv_sems):
        x, y, c = lax.axis_index("x"), lax.axis_index("y"), lax.axis_index("c")

        def copy(p, core):
            _, idx = _peer(p)
            return pltpu.make_async_remote_copy(
                src_ref=_block(in_ref, idx, cols, (core * half, half)),
                dst_ref=_block(out_ref, idx, cols, (core * half, half)),
                send_sem=send_sems.at[idx], recv_sem=recv_sems.at[idx], device_id=(x, y, 1 - c),
                device_id_type=pl.DeviceIdType.MESH)

        sends = [copy(p, c) for p in range(2, N_DEV)]
        for cp in sends:
            cp.start()
        for p in range(2, N_DEV):
            copy(p, 1 - c).wait_recv()
        for cp in sends:
            cp.wait_send()

    any_spec = pl.BlockSpec(memory_space=pl.ANY)
    return pl.pallas_call(
        body, name=name, in_specs=[any_spec], out_specs=any_spec,
        out_shape=jax.ShapeDtypeStruct(land.shape, land.dtype), input_output_aliases={0: 0},
        scratch_shapes=[pltpu.SemaphoreType.DMA((N_DEV,)), pltpu.SemaphoreType.DMA((N_DEV,))],
    )(land)


def _pair_exchange(name, arrays):
    n = len(arrays)

    def body(*refs):
        ins, outs = refs[:n], refs[n:2 * n]
        send_sems, recv_sems = refs[2 * n:]
        x, y, c = lax.axis_index("x"), lax.axis_index("y"), lax.axis_index("c")
        copies = [pltpu.make_async_remote_copy(
            src_ref=ins[a].at[2 * k + (1 - c)], dst_ref=outs[a].at[k], send_sem=send_sems.at[a, k],
            recv_sem=recv_sems.at[a, k], device_id=(x, y, 1 - c), device_id_type=pl.DeviceIdType.MESH)
            for a in range(n) for k in range(N_DEV // 2)]
        for cp in copies:
            cp.start()
        for cp in copies:
            cp.wait()

    any_spec = pl.BlockSpec(memory_space=pl.ANY)
    return pl.pallas_call(
        body, name=name, in_specs=[any_spec] * n, out_specs=[any_spec] * n,
        out_shape=[jax.ShapeDtypeStruct((N_DEV // 2,) + a.shape[1:], a.dtype) for a in arrays],
        scratch_shapes=[pltpu.SemaphoreType.DMA((n, N_DEV // 2)), pltpu.SemaphoreType.DMA((n, N_DEV // 2))],
    )(*arrays)


def _pair_sum(name, mine, theirs):
    _, r, c_ = mine.shape
    tr = _rtile(r, max(8, (1 << 19) // c_ // 8 * 8))

    def body(mine_ref, theirs_ref, o_ref):
        core = lax.axis_index("c")
        o_ref[0] = (mine_ref[0, core].astype(F32) + theirs_ref[0].astype(F32)).astype(o_ref.dtype)

    return pl.pallas_call(
        body, name=name, grid=(N_DEV // 2, r // tr),
        in_specs=[pl.BlockSpec((1, 2, tr, c_), lambda k, i: (k, 0, i, 0)),
                  pl.BlockSpec((1, tr, c_), lambda k, i: (k, i, 0))],
        out_specs=pl.BlockSpec((1, tr, c_), lambda k, i: (k, i, 0)),
        out_shape=jax.ShapeDtypeStruct((N_DEV // 2, r, c_), mine.dtype),
        compiler_params=_cparams(("parallel", "parallel")),
    )(mine.reshape(N_DEV // 2, 2, r, c_), theirs)


def _cols_to_shards(a):
    r, c = a.shape
    return a.reshape(r, N_DEV, c // N_DEV).transpose(1, 0, 2)


def _shards_to_cols(a):
    d, r, n = a.shape
    return a.transpose(1, 0, 2).reshape(r, d * n)


def kernel(x, c, ln_emb_g, ln_emb_b, rpb_table, w_mod, b_mod, w_in, mu_shift, w0, w_decay_up, a0, w_iclr_up, w_gate_up, k_k, k_a, r_k, lnx_g, lnx_b, attn_sinks, w_out, ln1_g, ln1_b, w_up, w_down, ln2_g, ln2_b, loss_target, m_ln_emb_g, m_ln_emb_b, m_rpb_table, m_w_mod, m_b_mod, m_w_in, m_mu_shift, m_w0, m_w_decay_up, m_a0, m_w_iclr_up, m_w_gate_up, m_k_k, m_k_a, m_r_k, m_lnx_g, m_lnx_b, m_attn_sinks, m_w_out, m_ln1_g, m_ln1_b, m_w_up, m_w_down, m_ln2_g, m_ln2_b, v_ln_emb_g, v_ln_emb_b, v_rpb_table, v_w_mod, v_b_mod, v_w_in, v_mu_shift, v_w0, v_w_decay_up, v_a0, v_w_iclr_up, v_w_gate_up, v_k_k, v_k_a, v_r_k, v_lnx_g, v_lnx_b, v_attn_sinks, v_w_out, v_ln1_g, v_ln1_b, v_w_up, v_w_down, v_ln2_g, v_ln2_b):
    names = ["ln_emb_g", "ln_emb_b", "rpb_table", "w_mod", "b_mod", "w_in", "mu_shift", "w0", "w_decay_up",
             "a0", "w_iclr_up", "w_gate_up", "k_k", "k_a", "r_k", "lnx_g", "lnx_b", "attn_sinks", "w_out",
             "ln1_g", "ln1_b", "w_up", "w_down", "ln2_g", "ln2_b"]
    env = dict(locals())
    weights = {nm: env[nm] for nm in names}
    mom_m = {nm: env["m_" + nm] for nm in names}
    mom_v = {nm: env["v_" + nm] for nm in names}

    t, d = x.shape[1], x.shape[2]
    da = d // 2
    h_a = da // HEAD
    hq = (d - da) // HEAD
    hkv = hq // GQA_RATIO
    l_w, l_a, l_g = w_decay_up.shape[1], w_iclr_up.shape[1], w_gate_up.shape[1]
    o_w, o_a, o_g = 3 * da, 3 * da + l_w, 3 * da + l_w + l_a
    n_rwkv = o_g + l_g
    o_kb, o_vb = n_rwkv + hq * HEAD, n_rwkv + hq * HEAD + hkv * HEAD
    me = 4 * lax.axis_index("x") + 2 * lax.axis_index("y") + lax.axis_index("c")

    x2d, tgt = x[0], loss_target[0]
    row = lambda a: a.reshape(1, -1)
    seg = (jnp.arange(da)[:, None] // HEAD == jnp.arange(h_a)[None, :]).astype(F32)
    seg_t = seg.T

    (c_all,) = _all_gather("gather_cond", [c])
    c_all = c_all.reshape(N_DEV, d)
    mod_rows, cond_all = _mod_fwd(c_all, w_mod[0])
    gathered = _all_gather("gather_weights", [
        mod_rows, w_in[0].astype(BF16).T, w_decay_up[0], w_iclr_up[0], w_gate_up[0]])
    mod_all, win_g, wd_g, wa_g, wg_g = gathered
    late = [w_out[0].astype(BF16), w_up[0].astype(BF16), w_down[0].astype(BF16)]
    n_up = w_up.shape[2]
    late_cols = [None, n_up, None]
    late_lands = [lax.empty((N_DEV,) + late[0].shape, BF16), lax.empty((d, N_DEV * n_up), BF16),
                  lax.empty((N_DEV,) + late[2].shape, BF16)]
    late_send, late_recv, late_src, late_land, late_token = _exchange_start(
        "gather_late_weights_start", late, late_lands, False, mod_all, late_cols, halves=True)
    mod = lax.dynamic_index_in_dim(mod_all, me, axis=1, keepdims=False).reshape(1, -1) + b_mod
    mod = mod + late_token[0, 0]
    sh1, sc1, g1, sh2, sc2, g2 = [mod[:, i * d:(i + 1) * d] for i in range(6)]
    w_in_t = win_g.reshape(-1, d)
    wd_f, wa_f, wg_f = _shards_to_cols(wd_g), _shards_to_cols(wa_g), _shards_to_cols(wg_g)

    def late_weight(i, nm, after):
        land = _exchange_wait("gather_" + nm + "_wait", late_send[i], late_recv[i], late_src[i], late_land[i],
                              False, after, late_cols[i], halves=True)
        return _sibling_fill("gather_" + nm + "_fill", land, late_cols[i])

    tr = _rtile(t, 256)
    lng, lnb = row(ln_emb_g), row(ln_emb_b)

    def embed_fn(i, nb, xb, g, b, sc, sh):
        return _embed_math(xb, g, b, sc, sh), ()
    x0, u1 = _rowwise("embed_ln_mod", embed_fn, [x2d], [lng, lnb, sc1, sh1], [(d, F32), (d, BF16)], [], tr)

    (p_rkv,) = _matmul("in_proj_rkv", u1, w_in_t[:o_w], "nt", [F32])
    (p_lora,) = _matmul("in_proj_lora", u1, w_in_t[o_w:n_rwkv], "nt", [F32])
    (q,) = _matmul("in_proj_q", u1, w_in_t[n_rwkv:o_kb], "nt", [F32])
    (p_kv,) = _matmul("in_proj_kv", u1, w_in_t[o_kb:], "nt", [F32])
    p_w, p_a, p_g = p_lora[:, :l_w], p_lora[:, l_w:l_w + l_a], p_lora[:, l_w + l_a:]
    kb, vb = p_kv[:, :hkv * HEAD], p_kv[:, hkv * HEAD:]
    mu_rkv, mu_w, mu_a, mu_g = (mu_shift[:, :o_w], mu_shift[:, o_w:o_a], mu_shift[:, o_a:o_g],
                                mu_shift[:, o_g:n_rwkv])
    pre_params = [w0, wd_f, a0, wa_f, wg_f, k_k, k_a, seg, seg_t]
    tr_pre = _rtile(t, 128)

    def shifted(i, blocks, halos, mus):
        return [xb + (_shift_prev(xb, hb, i) - xb) * mb for xb, hb, mb in zip(blocks, halos, mus)]

    def split3(a):
        return a[:, :da], a[:, da:2 * da], a[:, 2 * da:]

    def pre_fn(i, nb, b_rkv, b_w, b_a, b_g, h_rkv, h_w, h_a_, h_g, m_rkv, m_w, m_a, m_g, *params):
        s_rkv, s_w, s_a, s_g = shifted(i, [b_rkv, b_w, b_a, b_g], [h_rkv, h_w, h_a_, h_g],
                                       [m_rkv, m_w, m_a, m_g])
        return _rwkv_pre_math(*split3(s_rkv), s_w, s_a, s_g, *params, dot=_raw_dot), ()

    pre_rows = [p_rkv, p_w, p_a, p_g]
    pre_halos = [(a, "prev") for a in pre_rows]
    r_, lw_, k2_, v_, kk_, b_, gate_ = _rowwise(
        "rwkv_pre", pre_fn, pre_rows, [mu_rkv, mu_w, mu_a, mu_g] + pre_params,
        [(da, F32)] * 7, [], tr_pre, halos=pre_halos)

    y_rec, s0s, invs = _rec_fwd(r_, lw_, k2_, v_, kk_, b_)

    rk_flat = r_k.reshape(1, da)
    post_params = [lnx_g, lnx_b, rk_flat, seg, seg_t]

    def post_fn(i, nb, yb, rb, kb_, vb_, gb, *params):
        return (_rwkv_post_math(yb, rb, kb_, vb_, gb, *params, dot=_raw_dot),), ()
    (ya,) = _rowwise("rwkv_post", post_fn, [y_rec, r_, k2_, v_, gate_], post_params, [(da, BF16)], [], tr_pre)

    onehot = _bucket_onehot()
    bias = _small_dot("rpb_gather", rpb_table, onehot, 0, 0)
    bias = bias.reshape(hq, 2 * ATT_BLOCK, ATT_BLOCK)
    yb = _attn_fwd(q, kb, vb, bias, attn_sinks)

    mix_in = jnp.concatenate([ya, yb], axis=1)
    w_out_f = late_weight(0, "w_out", mix_in).reshape(d, d)
    (mix,) = _matmul("out_proj", mix_in, w_out_f, "nn", [F32])

    def post1_fn(i, nb, xin, yv, gate, g, b, sc, sh):
        return _post_math(xin, yv, gate, g, b, sc, sh), ()
    x1, u2 = _rowwise("ln1_mod", post1_fn, [x0, mix], [g1, ln1_g, ln1_b, sc2, sh2],
                      [(d, F32), (d, BF16)], [], tr)

    def relu2(acc):
        rl = jnp.maximum(acc, 0.0)
        return acc, rl * rl
    w_up_f = late_weight(1, "w_up", u2)
    hpre, hact = _matmul("mlp_up", u2, w_up_f, "nn", [F32, BF16], epilogue=relu2)
    w_down_f = late_weight(2, "w_down", hact).reshape(-1, d)
    (hmlp,) = _matmul("mlp_down", hact, w_down_f, "nn", [F32])

    def loss_fn(i, nb, xin, hv, tg, gate, g, b):
        val, vjp = jax.vjp(_loss_math, xin, hv, tg, gate, g, b)
        dxin, dh, _, dgate, dg, db = vjp(jnp.ones((), F32))
        return (dxin, dh), (val, dgate, dg, db)
    dx1, dh, loss_acc, dg2, dln2g, dln2b = _rowwise(
        "ln2_loss", loss_fn, [x1, hmlp, tgt], [g2, ln2_g, ln2_b], [(d, F32), (d, BF16)],
        [(1, 1), (1, d), (1, d), (1, d)], _rtile(t, 128))

    def drelu2(acc, hp):
        return (acc * 2.0 * jnp.maximum(hp, 0.0),)
    (dhpre,) = _matmul("mlp_down_dgrad", dh, w_down_f, "nt", [BF16], epilogue=drelu2, extras=[hpre])
    (gw_down,) = _matmul("mlp_down_wgrad", hact, dh, "tn", [BF16])
    (du2,) = _matmul("mlp_up_dgrad", dhpre, w_up_f, "nt", [F32])
    (gw_up,) = _matmul("mlp_up_wgrad", u2, dhpre, "tn", [BF16])

    def landing(src):
        return lax.empty(src.shape, src.dtype)
    mlp_src = [gw_down.reshape(N_DEV, -1, d), gw_up]
    mlp_cols = [None, n_up]
    mlp_send, mlp_recv, mlp_src, mlp_land, mlp_token = _exchange_start(
        "scatter_mlp_grads_start", mlp_src, [landing(mlp_src[0]), lax.empty((N_DEV, d, n_up), BF16)], True,
        gw_up, mlp_cols)

    def post1_bwd(i, nb, xin, yv, dx1v, du2v, gate, g, b, sc, sh):
        _, vjp = jax.vjp(_post_math, xin, yv, gate, g, b, sc, sh)
        dxin, dy, dgate, dg, db, dsc, dsh = vjp((dx1v, du2v))
        return (dxin, dy), (dgate, dg, db, dsc, dsh)
    dx0, dmix, dg1, dln1g, dln1b, dsc2, dsh2 = _rowwise(
        "ln1_mod_bwd", post1_bwd, [x0, mix, dx1, du2], [g1 + mlp_token[0, 0], ln1_g, ln1_b, sc2, sh2],
        [(d, F32), (d, BF16)], [(1, d)] * 5, _rtile(t, 128))

    (dmix_in,) = _matmul("out_proj_dgrad", dmix, w_out_f, "nt", [F32])
    (gw_out,) = _matmul("out_proj_wgrad", mix_in, dmix, "tn", [BF16])
    dya = dmix_in[:, :da]
    out_src = [gw_out.reshape(N_DEV, d // N_DEV, d)]
    out_send, out_recv, out_src, out_land, out_token = _exchange_start(
        "scatter_out_grad_start", out_src, [landing(a) for a in out_src], True, gw_out)
    post_params_bwd = [lnx_g, lnx_b, rk_flat + out_token[0, 0], seg, seg_t]

    def post_bwd(i, nb, yb_, rb, kb_, vb_, gb, dyab, *params):
        _, vjp = jax.vjp(_rwkv_post_math, yb_, rb, kb_, vb_, gb, *params)
        dy, dr, dk, dv, dg, dlg, dlb, drk, _, _ = vjp(dyab)
        return (dy, dr, dk, dv, dg), (dlg, dlb, drk)
    dy_rec, dr_e, dk_e, dv_e, dgate, dlnxg, dlnxb, drk = _rowwise(
        "rwkv_post_bwd", post_bwd, [y_rec, r_, k2_, v_, gate_, dya], post_params_bwd,
        [(da, F32)] * 5, [(1, da)] * 3, tr_pre)

    dr_r, dlw_r, dk_r, dv_r, dkk_r, db_r = _rec_bwd(r_, lw_, k2_, v_, kk_, b_, s0s, invs, dy_rec)

    def pre_bwd(i, nb, b_rkv, b_w, b_a, b_g, dr1, dr2, dlw, dk1, dk2, dv1, dv2, dkk, dbb, dgt,
                h_rkv, h_w, h_a_, h_g, m_rkv, m_w, m_a, m_g, *params):
        blocks = [b_rkv, b_w, b_a, b_g]
        prevs = [_shift_prev(xb, hb, i) for xb, hb in zip(blocks, [h_rkv, h_w, h_a_, h_g])]
        mus = [m_rkv, m_w, m_a, m_g]
        s_rkv, s_w, s_a, s_g = [xb + (pb - xb) * mb for xb, pb, mb in zip(blocks, prevs, mus)]
        _, vjp = jax.vjp(_rwkv_pre_math, *split3(s_rkv), s_w, s_a, s_g, *params)
        grads = vjp((dr1 + dr2, dlw, dk1 + dk2, dv1 + dv2, dkk, dbb, dgt))
        g_rkv = jnp.concatenate(grads[:3], axis=1)
        g_w, g_a, g_g = grads[3:6]
        dmu = [jnp.sum(gs * (pb - xb), axis=0, keepdims=True)
               for gs, pb, xb in zip([g_rkv, g_w, g_a, g_g], prevs, blocks)]
        dw0, dwd, da0, dwa, dwg, dkk_, dka = grads[6:13]
        return (g_rkv, g_w, g_a, g_g), (*dmu, dw0, dwd, da0, dwa, dwg, dkk_, dka)

    pre_out = _rowwise(
        "rwkv_pre_bwd", pre_bwd,
        pre_rows + [dr_r, dr_e, dlw_r, dk_r, dk_e, dv_r, dv_e, dkk_r, db_r, dgate],
        [mu_rkv, mu_w, mu_a, mu_g] + pre_params,
        [(o_w, F32), (l_w, F32), (l_a, F32), (l_g, F32)],
        [(1, o_w), (1, l_w), (1, l_a), (1, l_g), (1, da), (l_w, da), (1, da), (l_a, da), (l_g, da),
         (1, da), (1, da)],
        _rtile(t, 64), halos=pre_halos)
    gs_rkv, gs_w, gs_a, gs_g = pre_out[:4]
    dmu_parts = pre_out[4:8]
    dw0, dwd, da0, dwa, dwg, dk_k, dk_a = pre_out[8:]

    def unshift_fn(i, nb, a1, a2, a3, a4, n1, n2, n3, n4, m1, m2, m3, m4):
        outs = [gs * (1.0 - mb) + _shift_next(gs * mb, hb * mb, i, nb)
                for gs, hb, mb in zip([a1, a2, a3, a4], [n1, n2, n3, n4], [m1, m2, m3, m4])]
        return outs, ()
    gs_list = [gs_rkv, gs_w, gs_a, gs_g]
    dp_rkv, dp_w, dp_a, dp_g = _rowwise(
        "token_shift_bwd", unshift_fn, gs_list, [mu_rkv, mu_w, mu_a, mu_g],
        [(o_w, BF16), (l_w, BF16), (l_a, BF16), (l_g, BF16)], [], tr_pre,
        halos=[(a, "next") for a in gs_list])

    dq, dkp, dkc, dvp, dvc, dbias, dsinks = _attn_bwd(q, kb, vb, bias, attn_sinks, dmix_in, 1)
    zpad = jnp.zeros((ATT_BLOCK, kb.shape[1]), F32)
    dkb = (dkc + jnp.concatenate([dkp[ATT_BLOCK:], zpad], axis=0)).astype(BF16)
    dvb = (dvc + jnp.concatenate([dvp[ATT_BLOCK:], zpad], axis=0)).astype(BF16)
    d_rpb = _small_dot("rpb_scatter", onehot, dbias.reshape(hq, -1), 1, 1)

    dp = jnp.concatenate([dp_rkv, dp_w, dp_a, dp_g, dq, dkb, dvb], axis=1)
    (gw_in_t,) = _matmul("in_proj_wgrad", dp, u1, "tn", [BF16])
    in_names = ["w_in", "w_decay_up", "w_iclr_up", "w_gate_up"]
    in_parts = [gw_in_t.reshape(N_DEV, -1, d), _cols_to_shards(dwd), _cols_to_shards(dwa), _cols_to_shards(dwg)]
    in_sibling = _pair_exchange("pair_in_grads", in_parts)
    in_src = [_pair_sum("pair_sum_" + nm, a, b) for nm, a, b in zip(in_names, in_parts, in_sibling)]
    cut = in_src[0].shape[1] // 32 * 16
    second_src = [in_src[0][:, cut:]]
    in_src[0] = in_src[0][:, :cut]
    in_send, in_recv, in_src, in_land, in_token = _exchange_start(
        "scatter_in_grads_start", in_src, [landing(a) for a in in_src], "chips", in_src[0])
    (du1,) = _matmul("in_proj_dgrad", dp, w_in_t, "nn", [F32], after=in_token)

    def embed_bwd(i, nb, xb, dx0v, du1v, g, b, sc, sh):
        _, vjp = jax.vjp(_embed_math, xb, g, b, sc, sh)
        dxv, dg, db, dsc, dsh = vjp((dx0v, du1v))
        return (dxv,), (dg, db, dsc, dsh)
    grad_x, dlng, dlnb, dsc1, dsh1 = _rowwise(
        "embed_ln_mod_bwd", embed_bwd, [x2d, dx0, du1], [lng, lnb, sc1, sh1], [(d, F32)], [(1, d)] * 4,
        _rtile(t, 128))

    dmod = jnp.concatenate([dsh1, dsc1, dg1, dsh2, dsc2, dg2], axis=1)
    small = {"ln_emb_g": dlng, "ln_emb_b": dlnb, "rpb_table": d_rpb, "b_mod": dmod,
             "mu_shift": jnp.concatenate(dmu_parts, axis=1), "w0": dw0, "a0": da0, "k_k": dk_k, "k_a": dk_a,
             "r_k": drk, "lnx_g": dlnxg, "lnx_b": dlnxb, "attn_sinks": dsinks, "ln1_g": dln1g, "ln1_b": dln1b,
             "ln2_g": dln2g, "ln2_b": dln2b}
    small_names = list(small)
    packed = jnp.concatenate([small[nm].reshape(1, -1) for nm in small_names], axis=1)
    sm_send, sm_recv, sm_src, sm_land, sm_token = _exchange_start(
        "gather_small_grads_start", [packed], [lax.empty((N_DEV,) + packed.shape, F32)], False, grad_x)
    second_send, second_recv, second_src, second_land, second_token = _exchange_start(
        "scatter_in_grads_second_start", second_src, [landing(a) for a in second_src], "chips", sm_token)

    grads, deltas, new_m, new_v = {}, {}, {}, {}

    def put(nm, res):
        shape = weights[nm].shape
        grads[nm], deltas[nm], new_m[nm], new_v[nm] = [a.reshape(shape) for a in res]

    def big_update(nm, parts):
        put(nm, _adamw("adamw_" + nm, weights[nm][0], mom_m[nm][0], mom_v[nm][0], parts))

    behind = second_token
    big_update("w_down", _exchange_wait("scatter_w_down_wait", mlp_send[0], mlp_recv[0], mlp_src[0], mlp_land[0],
                                        True, behind, mlp_cols[0]))
    big_update("w_up", _exchange_wait("scatter_w_up_wait", mlp_send[1], mlp_recv[1], mlp_src[1], mlp_land[1],
                                      True, behind, mlp_cols[1]))
    big_update("w_out", _exchange_wait("scatter_w_out_wait", out_send[0], out_recv[0], out_src[0], out_land[0],
                                       True, behind))

    packed_all = _exchange_wait("gather_small_grads_wait", sm_send[0], sm_recv[0], sm_src[0], sm_land[0], False,
                                deltas["w_out"]).reshape(N_DEV, -1)
    n_mod = w_mod.shape[2]
    dmod_cols = lax.dynamic_slice_in_dim(packed_all[:, _offset(small, small_names, "b_mod"):], me * n_mod, n_mod,
                                         axis=1)
    put("w_mod", _adamw_outer("adamw_w_mod", w_mod[0], m_w_mod[0], v_w_mod[0], cond_all.T, dmod_cols))

    off = 0
    for nm in small_names:
        size = small[nm].size
        wshape = weights[nm].shape
        two_d = (1, size) if nm != "rpb_table" else wshape
        parts = packed_all[:, off:off + size].reshape((N_DEV,) + two_d)
        off += size
        put(nm, _adamw("adamw_" + nm, weights[nm].reshape(two_d), mom_m[nm].reshape(two_d),
                       mom_v[nm].reshape(two_d), parts))

    behind = deltas[small_names[-1]]
    for i, nm in enumerate(in_names):
        parts = _exchange_wait("scatter_" + nm + "_wait", in_send[i], in_recv[i], in_src[i], in_land[i],
                               "chips", behind)
        if nm == "w_in":
            second = _exchange_wait("scatter_w_in_second_wait", second_send[0], second_recv[0], second_src[0],
                                    second_land[0], "chips", behind)
            parts = jnp.swapaxes(jnp.concatenate([parts, second], axis=1), 1, 2)
        big_update(nm, parts)

    loss = lax.psum(loss_acc[0, 0], MESH_AXES)
    return (loss, grad_x[None], *[grads[nm] for nm in names], *[deltas[nm] for nm in names],
            *[new_m[nm] for nm in names], *[new_v[nm] for nm in names])


def _offset(small, small_names, name):
    off = 0
    for nm in small_names:
        if nm == name:
            return off
        off += small[nm].size
    raise KeyError(name)
```

```python
import functools
import math

import jax
import jax.numpy as jnp
from jax import lax
from jax.experimental import pallas as pl
from jax.experimental.pallas import tpu as pltpu

F32 = jnp.float32
BF16 = jnp.bfloat16
HI = lax.Precision.HIGHEST
MESH_AXES = ("x", "y", "c")
N_DEV = 8

HEAD = 64
GQA_RATIO = 8
ATT_BLOCK = 128
RPB_MAX_DIST = 128
LN_EPS = 1e-5
LNX_EPS = 64e-5
DEPTH = 1
ALPHA = (2.0 * DEPTH) ** 0.25
CHUNK = 64
REC_HEADS = 32

ADAM_LR = 0.001
ADAM_B1 = 0.9
ADAM_B2 = 0.999
ADAM_EPS = 1e-08
ADAM_WD = 0.01
ADAM_STEP = 10

VMEM_LIMIT = 60 * 1024 * 1024


def _cparams(sem=None):
    return pltpu.CompilerParams(dimension_semantics=sem, vmem_limit_bytes=VMEM_LIMIT)


def _tile(dim, cap):
    best = None
    t = 128
    while t <= min(dim, cap):
        if dim % t == 0:
            best = t
        t += 128
    return best or dim


def _rtile(dim, cap):
    best = None
    t = 8
    while t <= min(dim, cap):
        if dim % t == 0:
            best = t
        t += 8
    return best or dim


def _split2(a):
    hi = a.astype(BF16)
    return hi, (a - hi.astype(F32)).astype(BF16)


def _raw_dot(a, b, ca, cb, prec):
    dims = (((ca,), (cb,)), ((), ()))
    mm = lambda p, q: lax.dot_general(p, q, dims, preferred_element_type=F32)
    if prec == "bf16":
        return mm(a.astype(BF16), b.astype(BF16))
    if prec == "x3":
        (ah, al), (bh, bl) = _split2(a), _split2(b)
        return mm(ah, bh) + (mm(ah, bl) + mm(al, bh))
    if prec == "mask":
        ab = a.astype(BF16)
        b1, b2 = _split2(b)
        b3 = (b - b1.astype(F32) - b2.astype(F32)).astype(BF16)
        return mm(ab, b1) + (mm(ab, b2) + mm(ab, b3))
    if prec == "mb2":
        (ah, al), bb = _split2(a), b.astype(BF16)
        return mm(ah, bb) + mm(al, bb)
    return lax.dot_general(a, b, dims, precision=HI, preferred_element_type=F32)


@functools.partial(jax.custom_vjp, nondiff_argnums=(2, 3, 4))
def _bf16_dot(a, b, ca, cb, prec):
    return _raw_dot(a, b, ca, cb, prec)


def _bf16_dot_fwd(a, b, ca, cb, prec):
    return _raw_dot(a, b, ca, cb, prec), (a, b)


def _bf16_dot_bwd(ca, cb, prec, res, g):
    a, b = res
    if prec == "mask":
        return jnp.zeros_like(a), _bf16_dot(a, g, 1 - ca, 0, prec)
    if prec == "mb2":
        return _bf16_dot(g, b, 1, 1, prec), jnp.zeros_like(b)
    if ca == 1:
        da = _bf16_dot(g, b, 1, 1 - cb, prec)
    else:
        da = _bf16_dot(b, g, 1 - cb, 1, prec)
    if cb == 0:
        db = _bf16_dot(a, g, 1 - ca, 0, prec)
    else:
        db = _bf16_dot(g, a, 0, 1 - ca, prec)
    return da, db


_bf16_dot.defvjp(_bf16_dot_fwd, _bf16_dot_bwd)


def _dot(a, b, ca, cb, prec):
    return _raw_dot(a, b, ca, cb, prec) if prec == "hi" else _bf16_dot(a, b, ca, cb, prec)


def _sigmoid(z):
    return 1.0 / (1.0 + jnp.exp(-z))


def _softplus(z):
    return jnp.maximum(z, 0.0) + jnp.log(1.0 + jnp.exp(-jnp.abs(z)))


MATMUL_VMEM_BUDGET = 51 * 1024 * 1024


def _matmul_tiles(m, n, k, in_bytes, out_dtypes, n_extras):
    tm, tn = _tile(m, 1024), _tile(n, 1024)
    out_bytes = sum(jnp.dtype(dt).itemsize for dt in out_dtypes)
    for cap in (4096, 2048, 1024, 512, 256, 128):
        tk = _tile(k, cap)
        acc = 4 * tm * tn if tk < k else 0
        need = 2 * in_bytes * (tm + tn) * tk + 2 * tm * tn * (out_bytes + 4 * n_extras) + acc + 4 * tm * tn
        if need <= MATMUL_VMEM_BUDGET:
            break
    return tm, tn, tk


def _matmul(name, a, b, mode, out_dtypes, epilogue=None, extras=(), after=None):
    if mode == "nn":
        (m, k), n = a.shape, b.shape[1]
    elif mode == "nt":
        (m, k), n = a.shape, b.shape[0]
    else:
        (k, m), n = a.shape, b.shape[1]
    tm, tn, tk = _matmul_tiles(m, n, k, a.dtype.itemsize, out_dtypes, len(extras))
    nk = k // tk
    ne, no = len(extras), len(out_dtypes)
    ca, cb = {"nn": (1, 0), "nt": (1, 1), "tn": (0, 0)}[mode]

    n_after = 0 if after is None else 1

    def body(a_ref, b_ref, *rest):
        rest = rest[n_after:]
        extra_refs, out_refs = rest[:ne], rest[ne:ne + no]
        acc = rest[-1] if nk > 1 else None
        kk = pl.program_id(2)
        part = _raw_dot(a_ref[...], b_ref[...], ca, cb, "bf16")

        def finish(total):
            res = epilogue(total, *[e[...] for e in extra_refs]) if epilogue else (total,)
            for o, v in zip(out_refs, res):
                o[...] = v.astype(o.dtype)

        if nk == 1:
            finish(part)
            return

        @pl.when(kk == 0)
        def _():
            acc[...] = part

        @pl.when((kk > 0) & (kk < nk - 1))
        def _():
            acc[...] += part

        @pl.when(kk == nk - 1)
        def _():
            finish(acc[...] + part)

    a_spec = (pl.BlockSpec((tk, tm), lambda i, j, kk: (kk, i)) if mode == "tn"
              else pl.BlockSpec((tm, tk), lambda i, j, kk: (i, kk)))
    b_spec = (pl.BlockSpec((tn, tk), lambda i, j, kk: (j, kk)) if mode == "nt"
              else pl.BlockSpec((tk, tn), lambda i, j, kk: (kk, j)))
    mn_spec = pl.BlockSpec((tm, tn), lambda i, j, kk: (i, j))
    after_specs = [pl.BlockSpec(memory_space=pl.ANY)] * n_after
    outs = pl.pallas_call(
        body, name=name, grid=(m // tm, n // tn, nk),
        in_specs=[a_spec, b_spec] + after_specs + [mn_spec] * ne,
        out_specs=[mn_spec] * no,
        out_shape=[jax.ShapeDtypeStruct((m, n), dt) for dt in out_dtypes],
        scratch_shapes=[pltpu.VMEM((tm, tn), F32)] if nk > 1 else [],
        compiler_params=_cparams(("parallel", "parallel", "arbitrary")),
    )(a, b, *([after] * n_after), *extras)
    return outs


def _rowwise(name, fn, rows, bcasts, out_rows, out_accs, tr, halos=()):
    t = rows[0].shape[0]
    nb = t // tr
    n_in = len(rows) + len(halos) + len(bcasts)
    n_ro = len(out_rows)

    def body(*refs):
        ins = [r[...] for r in refs[:n_in]]
        o_refs = refs[n_in:]
        i = pl.program_id(0)
        routs, aouts = fn(i, nb, *ins)
        for ref, v in zip(o_refs[:n_ro], routs):
            ref[...] = v.astype(ref.dtype)
        for ref, v in zip(o_refs[n_ro:], aouts):
            @pl.when(i == 0)
            def _(ref=ref):
                ref[...] = jnp.zeros_like(ref)
            ref[...] += v.reshape(ref.shape)

    in_specs = [pl.BlockSpec((tr, r.shape[1]), lambda i: (i, 0)) for r in rows]
    for arr, which in halos:
        if which == "prev":
            in_specs.append(pl.BlockSpec((8, arr.shape[1]), lambda i: (jnp.maximum(i * (tr // 8) - 1, 0), 0)))
        else:
            in_specs.append(pl.BlockSpec((8, arr.shape[1]),
                                         lambda i: (jnp.minimum((i + 1) * (tr // 8), t // 8 - 1), 0)))
    for bc in bcasts:
        in_specs.append(pl.BlockSpec(bc.shape, lambda i, nd=bc.ndim: (0,) * nd))
    out_specs = [pl.BlockSpec((tr, c), lambda i: (i, 0)) for c, _ in out_rows]
    out_specs += [pl.BlockSpec(s, lambda i, nd=len(s): (0,) * nd) for s in out_accs]
    out_shape = [jax.ShapeDtypeStruct((t, c), dt) for c, dt in out_rows]
    out_shape += [jax.ShapeDtypeStruct(s, F32) for s in out_accs]
    return pl.pallas_call(
        body, name=name, grid=(nb,), in_specs=in_specs, out_specs=out_specs, out_shape=out_shape,
        compiler_params=_cparams(("arbitrary",)),
    )(*rows, *[h[0] for h in halos], *bcasts)


def _shift_prev(x, halo, i):
    rolled = pltpu.roll(x, 1, 0)
    first = jnp.where(i == 0, 0.0, halo[7:8, :])
    row = lax.broadcasted_iota(jnp.int32, x.shape, 0)
    return jnp.where(row == 0, first, rolled)


def _shift_next(x, halo, i, nb):
    rolled = pltpu.roll(x, x.shape[0] - 1, 0)
    last = jnp.where(i == nb - 1, 0.0, halo[0:1, :])
    row = lax.broadcasted_iota(jnp.int32, x.shape, 0)
    return jnp.where(row == x.shape[0] - 1, last, rolled)


def _ln(x, g, b, eps=LN_EPS):
    mu = jnp.mean(x, axis=-1, keepdims=True)
    xc = x - mu
    var = jnp.mean(xc * xc, axis=-1, keepdims=True)
    return xc * lax.rsqrt(var + eps) * g + b


def _embed_math(x, g, b, sc, sh):
    x0 = _ln(x, g, b)
    return x0, x0 * (1.0 + sc) + sh


def _post_math(xin, y, gate, g, b, sc, sh):
    x1 = _ln(ALPHA * xin + (1.0 + gate) * y, g, b)
    return x1, x1 * (1.0 + sc) + sh


def _loss_math(xin, h, tgt, gate, g, b):
    x2 = _ln(ALPHA * xin + (1.0 + gate) * h, g, b)
    err = x2 - tgt
    return 0.5 * jnp.sum(jnp.mean(err * err, axis=-1))


def _rwkv_pre_math(r, k, v, xw, xa, xg, w0, wd, a0, wa, wg, k_k, k_a, seg, seg_t, dot=_dot):
    wpre = -_softplus(-(w0 + dot(jnp.tanh(xw), wd, 1, 0, "x3"))) - 0.5
    lw = -jnp.exp(wpre)
    a = _sigmoid(a0 + dot(xa, wa, 1, 0, "x3"))
    g = dot(_sigmoid(xg), wg, 1, 0, "x3")
    kk = k * k_k
    norm = jnp.sqrt(dot(kk * kk, seg, 1, 0, "mb2"))
    kkn = kk * dot(1.0 / jnp.maximum(norm, 1e-12), seg_t, 1, 0, "mb2")
    k2 = k * (1.0 + (a - 1.0) * k_a)
    return r, lw, k2, v, kkn, kkn * a, g


def _rwkv_post_math(y, r, k2, v, g, lnx_g, lnx_b, r_k, seg, seg_t, dot=_dot):
    inv = 1.0 / HEAD
    spread = lambda z: dot(dot(z, seg, 1, 0, "mb2"), seg_t, 1, 0, "mb2")
    mu = spread(y) * inv
    yc = y - mu
    var = spread(yc * yc) * inv
    yn = yc * lax.rsqrt(var + LNX_EPS) * lnx_g + lnx_b
    bonus = spread(r * k2 * r_k) * v
    return (yn + bonus) * g


@jax.custom_vjp
def _known_inverse(low, inv):
    return inv


def _known_inverse_fwd(low, inv):
    return inv, inv


def _known_inverse_bwd(inv, g):
    left = [_raw_dot(t, gi, 0, 0, "bf16") for t, gi in zip(inv, g)]
    return [_raw_dot(x, t, 1, 1, "bf16") for x, t in zip(left, inv)], [jnp.zeros_like(t) for t in inv]


_known_inverse.defvjp(_known_inverse_fwd, _known_inverse_bwd)


def _chunk_math(s0, r, lw, k, v, kk, b, known_inv=None, dot=_dot):
    n = len(r)
    hs = range(n)
    c = r[0].shape[0]
    ti = lax.broadcasted_iota(jnp.int32, (2 * c, 2 * c), 0)
    tj = lax.broadcasted_iota(jnp.int32, (2 * c, 2 * c), 1)
    tt, jj = ti & (c - 1), tj & (c - 1)
    quad = jnp.where(ti < c, (tt > jj).astype(F32), (tt >= jj).astype(F32))
    incl = quad[c:, :c]
    eye = (ti[:c, :c] == tj[:c, :c]).astype(F32)
    cl = [dot(incl, lw[i], 1, 0, "mask") for i in hs]
    ge = [jnp.exp(cl[i]) for i in hs]
    gi = [jnp.exp(-cl[i]) for i in hs]
    ar = [jnp.concatenate([-kk[i] * jnp.exp(cl[i] - lw[i]), r[i] * ge[i]], axis=0) for i in hs]
    kb = [jnp.concatenate([k[i] * gi[i], b[i] * gi[i]], axis=0) for i in hs]
    m = [dot(ar[i], kb[i], 1, 1, "x3") * quad for i in hs]
    ars0 = [dot(ar[i], s0[i], 1, 1, "bf16") for i in hs]
    mv = [dot(m[i][:c, :c], v[i], 1, 0, "bf16") for i in hs]
    pw = [m[i][:c, c:] for i in hs]
    if known_inv is None:
        inv = [eye + pw[i] for i in hs]
        for _ in range(int(math.log2(c)) - 1):
            pw = [dot(pw[i], pw[i], 1, 0, "bf16") for i in hs]
            inv = [inv[i] + dot(inv[i], pw[i], 1, 0, "bf16") for i in hs]
    else:
        inv = _known_inverse(pw, known_inv)
    u = [dot(inv[i], ars0[i][:c] + mv[i], 1, 0, "bf16") for i in hs]
    vu = [jnp.concatenate([v[i], u[i]], axis=0) for i in hs]
    y = [ars0[i][c:] + dot(m[i][c:], vu[i], 1, 0, "bf16") for i in hs]
    s1 = [(s0[i] + dot(vu[i], kb[i], 0, 0, "x3")) * ge[i][c - 1:c, :] for i in hs]
    return y, s1, inv


def _attn_math(q, kp, kc, vp, vc, bias, sinks, first, dot=_dot):
    hq = q.shape[1] // HEAD
    hkv = kc.shape[1] // HEAD
    group = hq // hkv
    cols = group * ATT_BLOCK
    kj = lax.broadcasted_iota(jnp.int32, (2 * ATT_BLOCK, cols), 0)
    qi = lax.broadcasted_iota(jnp.int32, (2 * ATT_BLOCK, cols), 1) & (ATT_BLOCK - 1)
    dist = qi + ATT_BLOCK - kj
    valid = (dist >= 0) & (dist < ATT_BLOCK) & (jnp.logical_not(first) | (kj >= ATT_BLOCK))
    eye = (lax.broadcasted_iota(jnp.int32, (ATT_BLOCK, ATT_BLOCK), 0)
           == lax.broadcasted_iota(jnp.int32, (ATT_BLOCK, ATT_BLOCK), 1)).astype(F32)
    outs = []
    for j in range(hkv):
        heads = range(j * group, (j + 1) * group)
        kband = jnp.concatenate([kp[:, j * HEAD:(j + 1) * HEAD], kc[:, j * HEAD:(j + 1) * HEAD]], axis=0)
        vband = jnp.concatenate([vp[:, j * HEAD:(j + 1) * HEAD], vc[:, j * HEAD:(j + 1) * HEAD]], axis=0)
        qg = jnp.concatenate([q[:, h * HEAD:(h + 1) * HEAD] for h in heads], axis=0)
        bias_g = jnp.concatenate([bias[h] for h in heads], axis=1)
        sink = jnp.concatenate([jnp.broadcast_to(sinks[0:1, h:h + 1], (1, ATT_BLOCK)) for h in heads], axis=1)
        s = dot(kband, qg, 1, 1, "bf16") * (HEAD ** -0.5) + bias_g
        s = jnp.where(valid, s, -1e30)
        m = jnp.maximum(jnp.max(s, axis=0, keepdims=True), sink)
        e = jnp.exp(s - m)
        p = e / (jnp.sum(e, axis=0, keepdims=True) + jnp.exp(sink - m))
        o_t = dot(vband, p, 0, 0, "bf16")
        outs += [dot(eye, o_t[:, g * ATT_BLOCK:(g + 1) * ATT_BLOCK], 1, 1, "bf16") for g in range(group)]
    return jnp.concatenate(outs, axis=1)


def _rec_specs(t, da, gh, reverse):
    nc = t // CHUNK
    if reverse:
        return pl.BlockSpec((CHUNK, gh * HEAD), lambda hg, c: (nc - 1 - c, hg))
    return pl.BlockSpec((CHUNK, gh * HEAD), lambda hg, c: (c, hg))


def _rec_fwd(r, lw, k, v, kk, b):
    t, da = r.shape
    h = da // HEAD
    gh = min(REC_HEADS, h)
    nc = t // CHUNK

    def body(r_ref, lw_ref, k_ref, v_ref, kk_ref, b_ref, y_ref, s0_ref, inv_ref, state):
        @pl.when(pl.program_id(1) == 0)
        def _():
            state[...] = jnp.zeros_like(state)

        sls = [slice(i * HEAD, (i + 1) * HEAD) for i in range(gh)]
        heads = lambda ref: [ref[:, sl] for sl in sls]
        s0 = [state[i] for i in range(gh)]
        y, s1, inv = _chunk_math(s0, heads(r_ref), heads(lw_ref), heads(k_ref), heads(v_ref), heads(kk_ref),
                                 heads(b_ref), dot=_raw_dot)
        for i, sl in enumerate(sls):
            s0_ref[0, i] = s0[i]
            inv_ref[0, i] = inv[i]
            y_ref[:, sl] = y[i]
            state[i] = s1[i]

    spec = _rec_specs(t, da, gh, False)
    per_chunk = pl.BlockSpec((1, gh, HEAD, HEAD), lambda hg, c: (c, hg, 0, 0))
    return pl.pallas_call(
        body, name="rwkv_recurrence_fwd", grid=(h // gh, nc),
        in_specs=[spec] * 6,
        out_specs=[spec, per_chunk, per_chunk],
        out_shape=[jax.ShapeDtypeStruct((t, da), F32)] + [jax.ShapeDtypeStruct((nc, h, HEAD, HEAD), F32)] * 2,
        scratch_shapes=[pltpu.VMEM((gh, HEAD, HEAD), F32)],
        compiler_params=_cparams(("parallel", "arbitrary")),
    )(r, lw, k, v, kk, b)


def _rec_bwd(r, lw, k, v, kk, b, s0s, invs, dy):
    t, da = r.shape
    h = da // HEAD
    gh = min(REC_HEADS, h)
    nc = t // CHUNK

    def body(r_ref, lw_ref, k_ref, v_ref, kk_ref, b_ref, dy_ref, s0_ref, inv_ref,
             dr_ref, dlw_ref, dk_ref, dv_ref, dkk_ref, db_ref, dstate):
        @pl.when(pl.program_id(1) == 0)
        def _():
            dstate[...] = jnp.zeros_like(dstate)

        sls = [slice(i * HEAD, (i + 1) * HEAD) for i in range(gh)]
        heads = lambda ref: [ref[:, sl] for sl in sls]
        known = [inv_ref[0, i] for i in range(gh)]
        fn = lambda *args: _chunk_math(*args, known_inv=known)[:2]
        _, vjp = jax.vjp(fn, [s0_ref[0, i] for i in range(gh)], heads(r_ref), heads(lw_ref),
                         heads(k_ref), heads(v_ref), heads(kk_ref), heads(b_ref))
        grads = vjp((heads(dy_ref), [dstate[i] for i in range(gh)]))
        for i, sl in enumerate(sls):
            dstate[i] = grads[0][i]
            for ref, val in zip((dr_ref, dlw_ref, dk_ref, dv_ref, dkk_ref, db_ref), grads[1:]):
                ref[:, sl] = val[i]

    spec = _rec_specs(t, da, gh, True)
    return pl.pallas_call(
        body, name="rwkv_recurrence_bwd", grid=(h // gh, nc),
        in_specs=[spec] * 7 + [pl.BlockSpec((1, gh, HEAD, HEAD), lambda hg, c: (nc - 1 - c, hg, 0, 0))] * 2,
        out_specs=[spec] * 6,
        out_shape=[jax.ShapeDtypeStruct((t, da), F32)] * 6,
        scratch_shapes=[pltpu.VMEM((gh, HEAD, HEAD), F32)],
        compiler_params=_cparams(("parallel", "arbitrary")),
    )(r, lw, k, v, kk, b, dy, s0s, invs)


def _attn_specs(t, hq_w, hkv_w):
    nb = t // ATT_BLOCK
    cur = lambda w: pl.BlockSpec((ATT_BLOCK, w), lambda n: (n, 0))
    prev = lambda w: pl.BlockSpec((ATT_BLOCK, w), lambda n: (jnp.maximum(n - 1, 0), 0))
    return nb, cur, prev


def _attn_fwd(q, kb, vb, bias, sinks):
    t, qw = q.shape
    kw = kb.shape[1]
    nb, cur, prev = _attn_specs(t, qw, kw)

    def body(q_ref, kp_ref, kc_ref, vp_ref, vc_ref, bias_ref, sink_ref, o_ref):
        first = pl.program_id(0) == 0
        o = _attn_math(q_ref[...], kp_ref[...], kc_ref[...], vp_ref[...], vc_ref[...],
                       bias_ref[...], sink_ref[...], first, dot=_raw_dot)
        o_ref[...] = o.astype(o_ref.dtype)

    full = lambda a: pl.BlockSpec(a.shape, lambda n, nd=a.ndim: (0,) * nd)
    return pl.pallas_call(
        body, name="swa_attention_fwd", grid=(nb,),
        in_specs=[cur(qw), prev(kw), cur(kw), prev(kw), cur(kw), full(bias), full(sinks)],
        out_specs=cur(qw), out_shape=jax.ShapeDtypeStruct((t, qw), BF16),
        compiler_params=_cparams(("parallel",)),
    )(q, kb, kb, vb, vb, bias, sinks)


def _attn_bwd(q, kb, vb, bias, sinks, do, col_block):
    t, qw = q.shape
    kw = kb.shape[1]
    nb, cur, prev = _attn_specs(t, qw, kw)

    def body(q_ref, kp_ref, kc_ref, vp_ref, vc_ref, bias_ref, sink_ref, do_ref,
             dq_ref, dkp_ref, dkc_ref, dvp_ref, dvc_ref, dbias_ref, dsink_ref):
        n = pl.program_id(0)
        first = n == 0
        fn = functools.partial(_attn_math, first=first)
        _, vjp = jax.vjp(fn, q_ref[...], kp_ref[...], kc_ref[...], vp_ref[...], vc_ref[...],
                         bias_ref[...], sink_ref[...])
        dq, dkp, dkc, dvp, dvc, dbias, dsink = vjp(do_ref[...].astype(F32))
        dq_ref[...] = dq.astype(dq_ref.dtype)
        dkp_ref[...] = dkp
        dkc_ref[...] = dkc
        dvp_ref[...] = dvp
        dvc_ref[...] = dvc

        @pl.when(first)
        def _():
            dbias_ref[...] = jnp.zeros_like(dbias_ref)
            dsink_ref[...] = jnp.zeros_like(dsink_ref)

        dbias_ref[...] += dbias
        dsink_ref[...] += dsink

    full = lambda a: pl.BlockSpec(a.shape, lambda n, nd=a.ndim: (0,) * nd)
    kshape = jax.ShapeDtypeStruct((t, kw), F32)
    return pl.pallas_call(
        body, name="swa_attention_bwd", grid=(nb,),
        in_specs=[cur(qw), prev(kw), cur(kw), prev(kw), cur(kw), full(bias), full(sinks),
                  pl.BlockSpec((ATT_BLOCK, qw), lambda n: (n, col_block))],
        out_specs=[cur(qw), cur(kw), cur(kw), cur(kw), cur(kw), full(bias), full(sinks)],
        out_shape=[jax.ShapeDtypeStruct((t, qw), BF16), kshape, kshape, kshape, kshape,
                   jax.ShapeDtypeStruct(bias.shape, F32), jax.ShapeDtypeStruct(sinks.shape, F32)],
        compiler_params=_cparams(("arbitrary",)),
    )(q, kb, kb, vb, vb, bias, sinks, do)


def _bucket_onehot():
    qi = jnp.arange(ATT_BLOCK)[None, :]
    kj = jnp.arange(2 * ATT_BLOCK)[:, None]
    n = jnp.maximum(qi + ATT_BLOCK - kj, 0)
    buckets, max_exact = 32, 16
    nf = jnp.maximum(n, 1).astype(F32)
    large = max_exact + (jnp.log(nf / max_exact) / math.log(RPB_MAX_DIST / max_exact)
                         * (buckets - max_exact)).astype(jnp.int32)
    bucket = jnp.where(n < max_exact, n, jnp.minimum(large, buckets - 1)).reshape(-1)
    return (bucket[None, :] == jnp.arange(buckets)[:, None]).astype(F32)


def _small_dot(name, a, b, ca, cb):
    m = a.shape[1 - ca]
    n = b.shape[1 - cb]

    def body(a_ref, b_ref, o_ref):
        o_ref[...] = _raw_dot(a_ref[...], b_ref[...], ca, cb, "hi")

    return pl.pallas_call(body, name=name, out_shape=jax.ShapeDtypeStruct((m, n), F32),
                          compiler_params=_cparams())(a, b)


def _mod_fwd(c_all, w_mod):
    d, n = w_mod.shape
    tn = _tile(n, 512)

    def body(c_ref, w_ref, o_ref, cond_ref):
        cv = c_ref[...]
        cond = cv * _sigmoid(cv)
        cond_ref[...] = cond
        o_ref[...] = _raw_dot(cond, w_ref[...], 1, 0, "hi")

    return pl.pallas_call(
        body, name="adaln_mod_fwd", grid=(n // tn,),
        in_specs=[pl.BlockSpec(c_all.shape, lambda j: (0, 0)), pl.BlockSpec((d, tn), lambda j: (0, j))],
        out_specs=[pl.BlockSpec((c_all.shape[0], tn), lambda j: (0, j)),
                   pl.BlockSpec(c_all.shape, lambda j: (0, 0))],
        out_shape=[jax.ShapeDtypeStruct((c_all.shape[0], n), F32), jax.ShapeDtypeStruct(c_all.shape, F32)],
        compiler_params=_cparams(("arbitrary",)),
    )(c_all, w_mod)


def _adam_math(w, g, m, v):
    m = ADAM_B1 * m + (1.0 - ADAM_B1) * g
    v = ADAM_B2 * v + (1.0 - ADAM_B2) * (g * g)
    m_hat = m / (1.0 - ADAM_B1 ** ADAM_STEP)
    v_hat = v / (1.0 - ADAM_B2 ** ADAM_STEP)
    delta = -ADAM_LR * (m_hat / (jnp.sqrt(v_hat) + ADAM_EPS) + ADAM_WD * w)
    return delta, m, v


def _adamw(name, w, m, v, gparts):
    r, c = w.shape
    p = gparts.shape[0]
    tr = _rtile(r, max(8, (1 << 18) // max(c, 1) // 8 * 8))

    def body(w_ref, m_ref, v_ref, g_ref, go_ref, d_ref, mo_ref, vo_ref):
        g = g_ref[0].astype(F32)
        for s in range(1, p):
            g = g + g_ref[s].astype(F32)
        delta, mn, vn = _adam_math(w_ref[...], g, m_ref[...], v_ref[...])
        go_ref[...] = g
        d_ref[...] = delta
        mo_ref[...] = mn
        vo_ref[...] = vn

    spec = pl.BlockSpec((tr, c), lambda i: (i, 0))
    return pl.pallas_call(
        body, name=name, grid=(r // tr,),
        in_specs=[spec, spec, spec, pl.BlockSpec((p, tr, c), lambda i: (0, i, 0))],
        out_specs=[spec] * 4, out_shape=[jax.ShapeDtypeStruct((r, c), F32)] * 4,
        compiler_params=_cparams(("parallel",)),
    )(w, m, v, gparts)


def _adamw_outer(name, w, m, v, cond_t, dmod):
    d, n = w.shape
    tr, tn = _rtile(d, 512), _tile(n, 1024)

    def body(w_ref, m_ref, v_ref, c_ref, dm_ref, go_ref, d_ref, mo_ref, vo_ref):
        g = _raw_dot(c_ref[...], dm_ref[...], 1, 0, "hi")
        delta, mn, vn = _adam_math(w_ref[...], g, m_ref[...], v_ref[...])
        go_ref[...] = g
        d_ref[...] = delta
        mo_ref[...] = mn
        vo_ref[...] = vn

    spec = pl.BlockSpec((tr, tn), lambda i, j: (i, j))
    return pl.pallas_call(
        body, name=name, grid=(d // tr, n // tn),
        in_specs=[spec, spec, spec, pl.BlockSpec((tr, cond_t.shape[1]), lambda i, j: (i, 0)),
                  pl.BlockSpec((dmod.shape[0], tn), lambda i, j: (0, j))],
        out_specs=[spec] * 4, out_shape=[jax.ShapeDtypeStruct((d, n), F32)] * 4,
        compiler_params=_cparams(("parallel", "parallel")),
    )(w, m, v, cond_t, dmod)


def _all_gather(name, arrays):
    n = len(arrays)

    def body(*refs):
        ins, outs = refs[:n], refs[n:2 * n]
        send_sems, recv_sems, local_sems = refs[2 * n:]
        x, y, c = lax.axis_index("x"), lax.axis_index("y"), lax.axis_index("c")
        me, sibling = (x, y, c), (x, y, 1 - c)
        chips = [(1 - x, y), (x, 1 - y), (1 - x, 1 - y)]

        def copy(a, k, block, to, src=None):
            rows = outs[a].at[4 * block[0] + 2 * block[1] + block[2]]
            return pltpu.make_async_remote_copy(
                src_ref=rows if src is None else src, dst_ref=rows, send_sem=send_sems.at[a, k],
                recv_sem=recv_sems.at[a, k], device_id=to, device_id_type=pl.DeviceIdType.MESH)

        mine = [pltpu.make_async_copy(ins[a], outs[a].at[4 * x + 2 * y + c], local_sems.at[a]) for a in range(n)]
        for cp in mine:
            cp.start()
        first = []
        for a in range(n):
            first.append(copy(a, 0, me, sibling, src=ins[a]))
            first += [copy(a, 1 + j, me, (*chip, c), src=ins[a]) for j, chip in enumerate(chips)]
        for cp in first:
            cp.start()
        passed = []
        for a in range(n):
            for j, chip in enumerate(chips):
                copy(a, 1 + j, (*chip, c), me).wait_recv()
                passed.append(copy(a, 4 + j, (*chip, c), sibling))
                passed[-1].start()
        for a in range(n):
            copy(a, 0, sibling, me).wait_recv()
            for j, chip in enumerate(chips):
                copy(a, 4 + j, (*chip, 1 - c), me).wait_recv()
        for cp in first + passed:
            cp.wait_send()
        for cp in mine:
            cp.wait()

    any_spec = pl.BlockSpec(memory_space=pl.ANY)
    return pl.pallas_call(
        body, name=name, in_specs=[any_spec] * n, out_specs=[any_spec] * n,
        out_shape=[jax.ShapeDtypeStruct((N_DEV,) + a.shape, a.dtype) for a in arrays],
        scratch_shapes=[pltpu.SemaphoreType.DMA((n, N_DEV - 1)), pltpu.SemaphoreType.DMA((n, N_DEV - 1)),
                        pltpu.SemaphoreType.DMA((n,))],
    )(*arrays)


def _peer(p):
    x, y, c = lax.axis_index("x"), lax.axis_index("y"), lax.axis_index("c")
    px, py, pc = x ^ ((p >> 2) & 1), y ^ ((p >> 1) & 1), c ^ (p & 1)
    return (px, py, pc), 4 * px + 2 * py + pc


def _block(ref, d, cols, rows=None):
    if cols:
        r = slice(None) if rows is None else pl.ds(pl.multiple_of(rows[0], 8), rows[1])
        return ref.at[r, pl.ds(pl.multiple_of(d * cols, cols), cols)]
    return ref.at[d] if rows is None else ref.at[d, pl.ds(pl.multiple_of(rows[0], 8), rows[1])]


def _split_copy(src_ref, land_ref, send_sems, recv_sems, p, scatter, arriving, cols=None, halves=False):
    x, y, c = lax.axis_index("x"), lax.axis_index("y"), lax.axis_index("c")
    me = 4 * x + 2 * y + c
    dev, idx = _peer(p)
    if scatter == "chips":
        src, dst = src_ref.at[idx >> 1], land_ref.at[(idx if arriving else me) >> 1]
    elif scatter:
        src, dst = _block(src_ref, idx, cols), land_ref.at[idx if arriving else me]
    elif halves and p >= 2:
        half = src_ref.shape[0] // 2
        rows = ((c if arriving else dev[2]) * half, half)
        src = src_ref.at[pl.ds(pl.multiple_of(rows[0], 8), half)]
        dst = _block(land_ref, idx if arriving else me, cols, rows)
    else:
        src, dst = src_ref, _block(land_ref, idx if arriving else me, cols)
    return pltpu.make_async_remote_copy(
        src_ref=src, dst_ref=dst, send_sem=send_sems.at[p - 1], recv_sem=recv_sems.at[p - 1], device_id=dev,
        device_id_type=pl.DeviceIdType.MESH)


_HBM_SPEC = pl.BlockSpec(memory_space=pltpu.HBM)
_SEM_SPEC = pl.BlockSpec(memory_space=pltpu.SEMAPHORE)
_DATAFLOW = pltpu.SideEffectType.DATAFLOW_SIDE_EFFECTING


def _own_copy(src_ref, land_ref, send_sems, scatter, cols):
    me = 4 * lax.axis_index("x") + 2 * lax.axis_index("y") + lax.axis_index("c")
    if scatter == "chips":
        src, dst = src_ref.at[me >> 1], land_ref.at[me >> 1]
    elif scatter:
        src, dst = _block(src_ref, me, cols), land_ref.at[me]
    else:
        src, dst = src_ref, _block(land_ref, me, cols)
    return pltpu.make_async_copy(src, dst, send_sems.at[N_DEV - 1])


def _peers(scatter):
    return (2, 4, 6) if scatter == "chips" else tuple(range(1, N_DEV))


def _exchange_start(name, srcs, lands, scatter, after, cols=None, halves=False):
    n = len(srcs)
    cols = cols or [None] * n

    def body(*refs):
        src_refs, land_refs = refs[:n], refs[n:2 * n]
        outs = refs[2 * n + 1:]
        send, recv, token = outs[:n], outs[n:2 * n], outs[-1]
        for a in range(n):
            for p in _peers(scatter):
                _split_copy(src_refs[a], land_refs[a], send[a], recv[a], p, scatter, False, cols[a],
                            halves).start()
            _own_copy(src_refs[a], land_refs[a], send[a], scatter, cols[a]).start()
        token[...] = jnp.zeros_like(token)

    sems = [pltpu.SemaphoreType.DMA((N_DEV,))] * n + [pltpu.SemaphoreType.DMA((N_DEV - 1,))] * n
    hbm = [pltpu.HBM(a.shape, a.dtype) for a in list(srcs) + list(lands)]
    res = pl.pallas_call(
        body, name=name,
        out_shape=sems + hbm + [jax.ShapeDtypeStruct((8, 128), F32)],
        in_specs=[_HBM_SPEC] * (2 * n) + [pl.BlockSpec(memory_space=pl.ANY)],
        out_specs=[_SEM_SPEC] * (2 * n) + [_HBM_SPEC] * (2 * n) + [pl.BlockSpec(memory_space=pltpu.VMEM)],
        input_output_aliases={i: 2 * n + i for i in range(2 * n)},
        compiler_params=pltpu.CompilerParams(has_side_effects=_DATAFLOW),
    )(*[pltpu.with_memory_space_constraint(a, pltpu.HBM) for a in list(srcs) + list(lands)], after)
    return res[:n], res[n:2 * n], res[2 * n:3 * n], res[3 * n:4 * n], res[-1]


def _exchange_wait(name, send_sem, recv_sem, src, land, scatter, after, cols=None, halves=False):
    def body(src_ref, land_ref, send, recv, after_ref, src_out, land_out):
        for p in _peers(scatter):
            cp = _split_copy(src_ref, land_ref, send, recv, p, scatter, True, cols, halves)
            cp.wait_send()
            cp.wait_recv()
        _own_copy(src_ref, land_ref, send, scatter, cols).wait()

    return pl.pallas_call(
        body, name=name,
        out_shape=(pltpu.HBM(src.shape, src.dtype), pltpu.HBM(land.shape, land.dtype)),
        in_specs=[_HBM_SPEC, _HBM_SPEC, _SEM_SPEC, _SEM_SPEC, pl.BlockSpec(memory_space=pl.ANY)],
        out_specs=(_HBM_SPEC, _HBM_SPEC), input_output_aliases={0: 0, 1: 1},
        compiler_params=pltpu.CompilerParams(has_side_effects=_DATAFLOW),
    )(src, land, send_sem, recv_sem, after)[1]


def _sibling_fill(name, land, cols):
    rows = land.shape[0] if cols else land.shape[1]
    half = rows // 2

    def body(in_ref, out_ref, send_sems, recv_sems):
        x, y, c = lax.axis_index("x"), lax.axis_index("y"), lax.axis_index("c")

        def copy(p, core):
            _, idx = _peer(p)
            return pltpu.make_async_remote_copy(
                src_ref=_block(in_ref, idx, cols, (core * half, half)),
                dst_ref=_block(out_ref, idx, cols, (core * half, half)),
                send_sem=send_sems.at[idx], recv_sem=recv_sems.at[idx], device_id=(x, y, 1 - c),
                device_id_type=pl.DeviceIdType.MESH)

        sends = [copy(p, c) for p in range(2, N_DEV)]
        for cp in sends:
            cp.start()
        for p in range(2, N_DEV):
            copy(p, 1 - c).wait_recv()
        for cp in sends:
            cp.wait_send()

    any_spec = pl.BlockSpec(memory_space=pl.ANY)
    return pl.pallas_call(
        body, name=name, in_specs=[any_spec], out_specs=any_spec,
        out_shape=jax.ShapeDtypeStruct(land.shape, land.dtype), input_output_aliases={0: 0},
        scratch_shapes=[pltpu.SemaphoreType.DMA((N_DEV,)), pltpu.SemaphoreType.DMA((N_DEV,))],
    )(land)


def _pair_exchange(name, arrays):
    n = len(arrays)

    def body(*refs):
        ins, outs = refs[:n], refs[n:2 * n]
        send_sems, recv_sems = refs[2 * n:]
        x, y, c = lax.axis_index("x"), lax.axis_index("y"), lax.axis_index("c")
        copies = [pltpu.make_async_remote_copy(
            src_ref=ins[a].at[2 * k + (1 - c)], dst_ref=outs[a].at[k], send_sem=send_sems.at[a, k],
            recv_sem=recv_sems.at[a, k], device_id=(x, y, 1 - c), device_id_type=pl.DeviceIdType.MESH)
            for a in range(n) for k in range(N_DEV // 2)]
        for cp in copies:
            cp.start()
        for cp in copies:
            cp.wait()

    any_spec = pl.BlockSpec(memory_space=pl.ANY)
    return pl.pallas_call(
        body, name=name, in_specs=[any_spec] * n, out_specs=[any_spec] * n,
        out_shape=[jax.ShapeDtypeStruct((N_DEV // 2,) + a.shape[1:], a.dtype) for a in arrays],
        scratch_shapes=[pltpu.SemaphoreType.DMA((n, N_DEV // 2)), pltpu.SemaphoreType.DMA((n, N_DEV // 2))],
    )(*arrays)


def _pair_sum(name, mine, theirs):
    _, r, c_ = mine.shape
    tr = _rtile(r, max(8, (1 << 19) // c_ // 8 * 8))

    def body(mine_ref, theirs_ref, o_ref):
        core = lax.axis_index("c")
        o_ref[0] = (mine_ref[0, core].astype(F32) + theirs_ref[0].astype(F32)).astype(o_ref.dtype)

    return pl.pallas_call(
        body, name=name, grid=(N_DEV // 2, r // tr),
        in_specs=[pl.BlockSpec((1, 2, tr, c_), lambda k, i: (k, 0, i, 0)),
                  pl.BlockSpec((1, tr, c_), lambda k, i: (k, i, 0))],
        out_specs=pl.BlockSpec((1, tr, c_), lambda k, i: (k, i, 0)),
        out_shape=jax.ShapeDtypeStruct((N_DEV // 2, r, c_), mine.dtype),
        compiler_params=_cparams(("parallel", "parallel")),
    )(mine.reshape(N_DEV // 2, 2, r, c_), theirs)


def _cols_to_shards(a):
    r, c = a.shape
    return a.reshape(r, N_DEV, c // N_DEV).transpose(1, 0, 2)


def _shards_to_cols(a):
    d, r, n = a.shape
    return a.transpose(1, 0, 2).reshape(r, d * n)


def kernel(x, c, ln_emb_g, ln_emb_b, rpb_table, w_mod, b_mod, w_in, mu_shift, w0, w_decay_up, a0, w_iclr_up, w_gate_up, k_k, k_a, r_k, lnx_g, lnx_b, attn_sinks, w_out, ln1_g, ln1_b, w_up, w_down, ln2_g, ln2_b, loss_target, m_ln_emb_g, m_ln_emb_b, m_rpb_table, m_w_mod, m_b_mod, m_w_in, m_mu_shift, m_w0, m_w_decay_up, m_a0, m_w_iclr_up, m_w_gate_up, m_k_k, m_k_a, m_r_k, m_lnx_g, m_lnx_b, m_attn_sinks, m_w_out, m_ln1_g, m_ln1_b, m_w_up, m_w_down, m_ln2_g, m_ln2_b, v_ln_emb_g, v_ln_emb_b, v_rpb_table, v_w_mod, v_b_mod, v_w_in, v_mu_shift, v_w0, v_w_decay_up, v_a0, v_w_iclr_up, v_w_gate_up, v_k_k, v_k_a, v_r_k, v_lnx_g, v_lnx_b, v_attn_sinks, v_w_out, v_ln1_g, v_ln1_b, v_w_up, v_w_down, v_ln2_g, v_ln2_b):
    names = ["ln_emb_g", "ln_emb_b", "rpb_table", "w_mod", "b_mod", "w_in", "mu_shift", "w0", "w_decay_up",
             "a0", "w_iclr_up", "w_gate_up", "k_k", "k_a", "r_k", "lnx_g", "lnx_b", "attn_sinks", "w_out",
             "ln1_g", "ln1_b", "w_up", "w_down", "ln2_g", "ln2_b"]
    env = dict(locals())
    weights = {nm: env[nm] for nm in names}
    mom_m = {nm: env["m_" + nm] for nm in names}
    mom_v = {nm: env["v_" + nm] for nm in names}

    t, d = x.shape[1], x.shape[2]
    da = d // 2
    h_a = da // HEAD
    hq = (d - da) // HEAD
    hkv = hq // GQA_RATIO
    l_w, l_a, l_g = w_decay_up.shape[1], w_iclr_up.shape[1], w_gate_up.shape[1]
    o_w, o_a, o_g = 3 * da, 3 * da + l_w, 3 * da + l_w + l_a
    n_rwkv = o_g + l_g
    o_kb, o_vb = n_rwkv + hq * HEAD, n_rwkv + hq * HEAD + hkv * HEAD
    me = 4 * lax.axis_index("x") + 2 * lax.axis_index("y") + lax.axis_index("c")

    x2d, tgt = x[0], loss_target[0]
    row = lambda a: a.reshape(1, -1)
    seg = (jnp.arange(da)[:, None] // HEAD == jnp.arange(h_a)[None, :]).astype(F32)
    seg_t = seg.T

    (c_all,) = _all_gather("gather_cond", [c])
    c_all = c_all.reshape(N_DEV, d)
    mod_rows, cond_all = _mod_fwd(c_all, w_mod[0])
    gathered = _all_gather("gather_weights", [
        mod_rows, w_in[0].astype(BF16).T, w_decay_up[0], w_iclr_up[0], w_gate_up[0]])
    mod_all, win_g, wd_g, wa_g, wg_g = gathered
    late = [w_out[0].astype(BF16), w_up[0].astype(BF16), w_down[0].astype(BF16)]
    n_up = w_up.shape[2]
    late_cols = [None, n_up, None]
    late_lands = [lax.empty((N_DEV,) + late[0].shape, BF16), lax.empty((d, N_DEV * n_up), BF16),
                  lax.empty((N_DEV,) + late[2].shape, BF16)]
    late_send, late_recv, late_src, late_land, late_token = _exchange_start(
        "gather_late_weights_start", late, late_lands, False, mod_all, late_cols, halves=True)
    mod = lax.dynamic_index_in_dim(mod_all, me, axis=1, keepdims=False).reshape(1, -1) + b_mod
    mod = mod + late_token[0, 0]
    sh1, sc1, g1, sh2, sc2, g2 = [mod[:, i * d:(i + 1) * d] for i in range(6)]
    w_in_t = win_g.reshape(-1, d)
    wd_f, wa_f, wg_f = _shards_to_cols(wd_g), _shards_to_cols(wa_g), _shards_to_cols(wg_g)

    def late_weight(i, nm, after):
        land = _exchange_wait("gather_" + nm + "_wait", late_send[i], late_recv[i], late_src[i], late_land[i],
                              False, after, late_cols[i], halves=True)
        return _sibling_fill("gather_" + nm + "_fill", land, late_cols[i])

    tr = _rtile(t, 256)
    lng, lnb = row(ln_emb_g), row(ln_emb_b)

    def embed_fn(i, nb, xb, g, b, sc, sh):
        return _embed_math(xb, g, b, sc, sh), ()
    x0, u1 = _rowwise("embed_ln_mod", embed_fn, [x2d], [lng, lnb, sc1, sh1], [(d, F32), (d, BF16)], [], tr)

    (p_rkv,) = _matmul("in_proj_rkv", u1, w_in_t[:o_w], "nt", [F32])
    (p_lora,) = _matmul("in_proj_lora", u1, w_in_t[o_w:n_rwkv], "nt", [F32])
    (q,) = _matmul("in_proj_q", u1, w_in_t[n_rwkv:o_kb], "nt", [F32])
    (p_kv,) = _matmul("in_proj_kv", u1, w_in_t[o_kb:], "nt", [F32])
    p_w, p_a, p_g = p_lora[:, :l_w], p_lora[:, l_w:l_w + l_a], p_lora[:, l_w + l_a:]
    kb, vb = p_kv[:, :hkv * HEAD], p_kv[:, hkv * HEAD:]
    mu_rkv, mu_w, mu_a, mu_g = (mu_shift[:, :o_w], mu_shift[:, o_w:o_a], mu_shift[:, o_a:o_g],
                                mu_shift[:, o_g:n_rwkv])
    pre_params = [w0, wd_f, a0, wa_f, wg_f, k_k, k_a, seg, seg_t]
    tr_pre = _rtile(t, 128)

    def shifted(i, blocks, halos, mus):
        return [xb + (_shift_prev(xb, hb, i) - xb) * mb for xb, hb, mb in zip(blocks, halos, mus)]

    def split3(a):
        return a[:, :da], a[:, da:2 * da], a[:, 2 * da:]

    def pre_fn(i, nb, b_rkv, b_w, b_a, b_g, h_rkv, h_w, h_a_, h_g, m_rkv, m_w, m_a, m_g, *params):
        s_rkv, s_w, s_a, s_g = shifted(i, [b_rkv, b_w, b_a, b_g], [h_rkv, h_w, h_a_, h_g],
                                       [m_rkv, m_w, m_a, m_g])
        return _rwkv_pre_math(*split3(s_rkv), s_w, s_a, s_g, *params, dot=_raw_dot), ()

    pre_rows = [p_rkv, p_w, p_a, p_g]
    pre_halos = [(a, "prev") for a in pre_rows]
    r_, lw_, k2_, v_, kk_, b_, gate_ = _rowwise(
        "rwkv_pre", pre_fn, pre_rows, [mu_rkv, mu_w, mu_a, mu_g] + pre_params,
        [(da, F32)] * 7, [], tr_pre, halos=pre_halos)

    y_rec, s0s, invs = _rec_fwd(r_, lw_, k2_, v_, kk_, b_)

    rk_flat = r_k.reshape(1, da)
    post_params = [lnx_g, lnx_b, rk_flat, seg, seg_t]

    def post_fn(i, nb, yb, rb, kb_, vb_, gb, *params):
        return (_rwkv_post_math(yb, rb, kb_, vb_, gb, *params, dot=_raw_dot),), ()
    (ya,) = _rowwise("rwkv_post", post_fn, [y_rec, r_, k2_, v_, gate_], post_params, [(da, BF16)], [], tr_pre)

    onehot = _bucket_onehot()
    bias = _small_dot("rpb_gather", rpb_table, onehot, 0, 0)
    bias = bias.reshape(hq, 2 * ATT_BLOCK, ATT_BLOCK)
    yb = _attn_fwd(q, kb, vb, bias, attn_sinks)

    mix_in = jnp.concatenate([ya, yb], axis=1)
    w_out_f = late_weight(0, "w_out", mix_in).reshape(d, d)
    (mix,) = _matmul("out_proj", mix_in, w_out_f, "nn", [F32])

    def post1_fn(i, nb, xin, yv, gate, g, b, sc, sh):
        return _post_math(xin, yv, gate, g, b, sc, sh), ()
    x1, u2 = _rowwise("ln1_mod", post1_fn, [x0, mix], [g1, ln1_g, ln1_b, sc2, sh2],
                      [(d, F32), (d, BF16)], [], tr)

    def relu2(acc):
        rl = jnp.maximum(acc, 0.0)
        return acc, rl * rl
    w_up_f = late_weight(1, "w_up", u2)
    hpre, hact = _matmul("mlp_up", u2, w_up_f, "nn", [F32, BF16], epilogue=relu2)
    w_down_f = late_weight(2, "w_down", hact).reshape(-1, d)
    (hmlp,) = _matmul("mlp_down", hact, w_down_f, "nn", [F32])

    def loss_fn(i, nb, xin, hv, tg, gate, g, b):
        val, vjp = jax.vjp(_loss_math, xin, hv, tg, gate, g, b)
        dxin, dh, _, dgate, dg, db = vjp(jnp.ones((), F32))
        return (dxin, dh), (val, dgate, dg, db)
    dx1, dh, loss_acc, dg2, dln2g, dln2b = _rowwise(
        "ln2_loss", loss_fn, [x1, hmlp, tgt], [g2, ln2_g, ln2_b], [(d, F32), (d, BF16)],
        [(1, 1), (1, d), (1, d), (1, d)], _rtile(t, 128))

    def drelu2(acc, hp):
        return (acc * 2.0 * jnp.maximum(hp, 0.0),)
    (dhpre,) = _matmul("mlp_down_dgrad", dh, w_down_f, "nt", [BF16], epilogue=drelu2, extras=[hpre])
    (gw_down,) = _matmul("mlp_down_wgrad", hact, dh, "tn", [BF16])
    (du2,) = _matmul("mlp_up_dgrad", dhpre, w_up_f, "nt", [F32])
    (gw_up,) = _matmul("mlp_up_wgrad", u2, dhpre, "tn", [BF16])

    def landing(src):
        return lax.empty(src.shape, src.dtype)
    mlp_src = [gw_down.reshape(N_DEV, -1, d), gw_up]
    mlp_cols = [None, n_up]
    mlp_send, mlp_recv, mlp_src, mlp_land, mlp_token = _exchange_start(
        "scatter_mlp_grads_start", mlp_src, [landing(mlp_src[0]), lax.empty((N_DEV, d, n_up), BF16)], True,
        gw_up, mlp_cols)

    def post1_bwd(i, nb, xin, yv, dx1v, du2v, gate, g, b, sc, sh):
        _, vjp = jax.vjp(_post_math, xin, yv, gate, g, b, sc, sh)
        dxin, dy, dgate, dg, db, dsc, dsh = vjp((dx1v, du2v))
        return (dxin, dy), (dgate, dg, db, dsc, dsh)
    dx0, dmix, dg1, dln1g, dln1b, dsc2, dsh2 = _rowwise(
        "ln1_mod_bwd", post1_bwd, [x0, mix, dx1, du2], [g1 + mlp_token[0, 0], ln1_g, ln1_b, sc2, sh2],
        [(d, F32), (d, BF16)], [(1, d)] * 5, _rtile(t, 128))

    (dmix_in,) = _matmul("out_proj_dgrad", dmix, w_out_f, "nt", [F32])
    (gw_out,) = _matmul("out_proj_wgrad", mix_in, dmix, "tn", [BF16])
    dya = dmix_in[:, :da]
    out_src = [gw_out.reshape(N_DEV, d // N_DEV, d)]
    out_send, out_recv, out_src, out_land, out_token = _exchange_start(
        "scatter_out_grad_start", out_src, [landing(a) for a in out_src], True, gw_out)
    post_params_bwd = [lnx_g, lnx_b, rk_flat + out_token[0, 0], seg, seg_t]

    def post_bwd(i, nb, yb_, rb, kb_, vb_, gb, dyab, *params):
        _, vjp = jax.vjp(_rwkv_post_math, yb_, rb, kb_, vb_, gb, *params)
        dy, dr, dk, dv, dg, dlg, dlb, drk, _, _ = vjp(dyab)
        return (dy, dr, dk, dv, dg), (dlg, dlb, drk)
    dy_rec, dr_e, dk_e, dv_e, dgate, dlnxg, dlnxb, drk = _rowwise(
        "rwkv_post_bwd", post_bwd, [y_rec, r_, k2_, v_, gate_, dya], post_params_bwd,
        [(da, F32)] * 5, [(1, da)] * 3, tr_pre)

    dr_r, dlw_r, dk_r, dv_r, dkk_r, db_r = _rec_bwd(r_, lw_, k2_, v_, kk_, b_, s0s, invs, dy_rec)

    def pre_bwd(i, nb, b_rkv, b_w, b_a, b_g, dr1, dr2, dlw, dk1, dk2, dv1, dv2, dkk, dbb, dgt,
                h_rkv, h_w, h_a_, h_g, m_rkv, m_w, m_a, m_g, *params):
        blocks = [b_rkv, b_w, b_a, b_g]
        prevs = [_shift_prev(xb, hb, i) for xb, hb in zip(blocks, [h_rkv, h_w, h_a_, h_g])]
        mus = [m_rkv, m_w, m_a, m_g]
        s_rkv, s_w, s_a, s_g = [xb + (pb - xb) * mb for xb, pb, mb in zip(blocks, prevs, mus)]
        _, vjp = jax.vjp(_rwkv_pre_math, *split3(s_rkv), s_w, s_a, s_g, *params)
        grads = vjp((dr1 + dr2, dlw, dk1 + dk2, dv1 + dv2, dkk, dbb, dgt))
        g_rkv = jnp.concatenate(grads[:3], axis=1)
        g_w, g_a, g_g = grads[3:6]
        dmu = [jnp.sum(gs * (pb - xb), axis=0, keepdims=True)
               for gs, pb, xb in zip([g_rkv, g_w, g_a, g_g], prevs, blocks)]
        dw0, dwd, da0, dwa, dwg, dkk_, dka = grads[6:13]
        return (g_rkv, g_w, g_a, g_g), (*dmu, dw0, dwd, da0, dwa, dwg, dkk_, dka)

    pre_out = _rowwise(
        "rwkv_pre_bwd", pre_bwd,
        pre_rows + [dr_r, dr_e, dlw_r, dk_r, dk_e, dv_r, dv_e, dkk_r, db_r, dgate],
        [mu_rkv, mu_w, mu_a, mu_g] + pre_params,
        [(o_w, F32), (l_w, F32), (l_a, F32), (l_g, F32)],
        [(1, o_w), (1, l_w), (1, l_a), (1, l_g), (1, da), (l_w, da), (1, da), (l_a, da), (l_g, da),
         (1, da), (1, da)],
        _rtile(t, 64), halos=pre_halos)
    gs_rkv, gs_w, gs_a, gs_g = pre_out[:4]
    dmu_parts = pre_out[4:8]
    dw0, dwd, da0, dwa, dwg, dk_k, dk_a = pre_out[8:]

    def unshift_fn(i, nb, a1, a2, a3, a4, n1, n2, n3, n4, m1, m2, m3, m4):
        outs = [gs * (1.0 - mb) + _shift_next(gs * mb, hb * mb, i, nb)
                for gs, hb, mb in zip([a1, a2, a3, a4], [n1, n2, n3, n4], [m1, m2, m3, m4])]
        return outs, ()
    gs_list = [gs_rkv, gs_w, gs_a, gs_g]
    dp_rkv, dp_w, dp_a, dp_g = _rowwise(
        "token_shift_bwd", unshift_fn, gs_list, [mu_rkv, mu_w, mu_a, mu_g],
        [(o_w, BF16), (l_w, BF16), (l_a, BF16), (l_g, BF16)], [], tr_pre,
        halos=[(a, "next") for a in gs_list])

    dq, dkp, dkc, dvp, dvc, dbias, dsinks = _attn_bwd(q, kb, vb, bias, attn_sinks, dmix_in, 1)
    zpad = jnp.zeros((ATT_BLOCK, kb.shape[1]), F32)
    dkb = (dkc + jnp.concatenate([dkp[ATT_BLOCK:], zpad], axis=0)).astype(BF16)
    dvb = (dvc + jnp.concatenate([dvp[ATT_BLOCK:], zpad], axis=0)).astype(BF16)
    d_rpb = _small_dot("rpb_scatter", onehot, dbias.reshape(hq, -1), 1, 1)

    dp = jnp.concatenate([dp_rkv, dp_w, dp_a, dp_g, dq, dkb, dvb], axis=1)
    (gw_in_t,) = _matmul("in_proj_wgrad", dp, u1, "tn", [BF16])
    in_names = ["w_in", "w_decay_up", "w_iclr_up", "w_gate_up"]
    in_parts = [gw_in_t.reshape(N_DEV, -1, d), _cols_to_shards(dwd), _cols_to_shards(dwa), _cols_to_shards(dwg)]
    in_sibling = _pair_exchange("pair_in_grads", in_parts)
    in_src = [_pair_sum("pair_sum_" + nm, a, b) for nm, a, b in zip(in_names, in_parts, in_sibling)]
    cut = in_src[0].shape[1] * 3 // 64 * 16
    second_src = [in_src[0][:, cut:]]
    in_src[0] = in_src[0][:, :cut]
    in_send, in_recv, in_src, in_land, in_token = _exchange_start(
        "scatter_in_grads_start", in_src, [landing(a) for a in in_src], "chips", in_src[0])
    (du1,) = _matmul("in_proj_dgrad", dp, w_in_t, "nn", [F32], after=in_token)

    def embed_bwd(i, nb, xb, dx0v, du1v, g, b, sc, sh):
        _, vjp = jax.vjp(_embed_math, xb, g, b, sc, sh)
        dxv, dg, db, dsc, dsh = vjp((dx0v, du1v))
        return (dxv,), (dg, db, dsc, dsh)
    grad_x, dlng, dlnb, dsc1, dsh1 = _rowwise(
        "embed_ln_mod_bwd", embed_bwd, [x2d, dx0, du1], [lng, lnb, sc1, sh1], [(d, F32)], [(1, d)] * 4,
        _rtile(t, 128))

    dmod = jnp.concatenate([dsh1, dsc1, dg1, dsh2, dsc2, dg2], axis=1)
    small = {"ln_emb_g": dlng, "ln_emb_b": dlnb, "rpb_table": d_rpb, "b_mod": dmod,
             "mu_shift": jnp.concatenate(dmu_parts, axis=1), "w0": dw0, "a0": da0, "k_k": dk_k, "k_a": dk_a,
             "r_k": drk, "lnx_g": dlnxg, "lnx_b": dlnxb, "attn_sinks": dsinks, "ln1_g": dln1g, "ln1_b": dln1b,
             "ln2_g": dln2g, "ln2_b": dln2b}
    small_names = list(small)
    packed = jnp.concatenate([small[nm].reshape(1, -1) for nm in small_names], axis=1)
    sm_send, sm_recv, sm_src, sm_land, sm_token = _exchange_start(
        "gather_small_grads_start", [packed], [lax.empty((N_DEV,) + packed.shape, F32)], False, grad_x)
    second_send, second_recv, second_src, second_land, second_token = _exchange_start(
        "scatter_in_grads_second_start", second_src, [landing(a) for a in second_src], "chips", sm_token)

    grads, deltas, new_m, new_v = {}, {}, {}, {}

    def put(nm, res):
        shape = weights[nm].shape
        grads[nm], deltas[nm], new_m[nm], new_v[nm] = [a.reshape(shape) for a in res]

    def big_update(nm, parts):
        put(nm, _adamw("adamw_" + nm, weights[nm][0], mom_m[nm][0], mom_v[nm][0], parts))

    behind = second_token
    big_update("w_down", _exchange_wait("scatter_w_down_wait", mlp_send[0], mlp_recv[0], mlp_src[0], mlp_land[0],
                                        True, behind, mlp_cols[0]))
    big_update("w_up", _exchange_wait("scatter_w_up_wait", mlp_send[1], mlp_recv[1], mlp_src[1], mlp_land[1],
                                      True, behind, mlp_cols[1]))
    big_update("w_out", _exchange_wait("scatter_w_out_wait", out_send[0], out_recv[0], out_src[0], out_land[0],
                                       True, behind))

    packed_all = _exchange_wait("gather_small_grads_wait", sm_send[0], sm_recv[0], sm_src[0], sm_land[0], False,
                                deltas["w_out"]).reshape(N_DEV, -1)
    n_mod = w_mod.shape[2]
    dmod_cols = lax.dynamic_slice_in_dim(packed_all[:, _offset(small, small_names, "b_mod"):], me * n_mod, n_mod,
                                         axis=1)
    put("w_mod", _adamw_outer("adamw_w_mod", w_mod[0], m_w_mod[0], v_w_mod[0], cond_all.T, dmod_cols))

    off = 0
    for nm in small_names:
        size = small[nm].size
        wshape = weights[nm].shape
        two_d = (1, size) if nm != "rpb_table" else wshape
        parts = packed_all[:, off:off + size].reshape((N_DEV,) + two_d)
        off += size
        put(nm, _adamw("adamw_" + nm, weights[nm].reshape(two_d), mom_m[nm].reshape(two_d),
                       mom_v[nm].reshape(two_d), parts))

    behind = deltas[small_names[-1]]
    for i, nm in enumerate(in_names):
        parts = _exchange_wait("scatter_" + nm + "_wait", in_send[i], in_recv[i], in_src[i], in_land[i],
                               "chips", behind)
        if nm == "w_in":
            second = _exchange_wait("scatter_w_in_second_wait", second_send[0], second_recv[0], second_src[0],
                                    second_land[0], "chips", behind)
            parts = jnp.swapaxes(jnp.concatenate([parts, second], axis=1), 1, 2)
        big_update(nm, parts)

    loss = lax.psum(loss_acc[0, 0], MESH_AXES)
    return (loss, grad_x[None], *[grads[nm] for nm in names], *[deltas[nm] for nm in names],
            *[new_m[nm] for nm in names], *[new_v[nm] for nm in names])


def _offset(small, small_names, name):
    off = 0
    for nm in small_names:
        if nm == name:
            return off
        off += small[nm].size
    raise KeyError(name)
```

```python
import functools
import math

import jax
import jax.numpy as jnp
from jax import lax
from jax.experimental import pallas as pl
from jax.experimental.pallas import tpu as pltpu

F32 = jnp.float32
BF16 = jnp.bfloat16
HI = lax.Precision.HIGHEST
MESH_AXES = ("x", "y", "c")
N_DEV = 8

HEAD = 64
GQA_RATIO = 8
ATT_BLOCK = 128
RPB_MAX_DIST = 128
LN_EPS = 1e-5
LNX_EPS = 64e-5
DEPTH = 1
ALPHA = (2.0 * DEPTH) ** 0.25
CHUNK = 64
REC_HEADS = 32

ADAM_LR = 0.001
ADAM_B1 = 0.9
ADAM_B2 = 0.999
ADAM_EPS = 1e-08
ADAM_WD = 0.01
ADAM_STEP = 10

VMEM_LIMIT = 60 * 1024 * 1024


def _cparams(sem=None):
    return pltpu.CompilerParams(dimension_semantics=sem, vmem_limit_bytes=VMEM_LIMIT)


def _tile(dim, cap):
    best = None
    t = 128
    while t <= min(dim, cap):
        if dim % t == 0:
            best = t
        t += 128
    return best or dim


def _rtile(dim, cap):
    best = None
    t = 8
    while t <= min(dim, cap):
        if dim % t == 0:
            best = t
        t += 8
    return best or dim


def _split2(a):
    hi = a.astype(BF16)
    return hi, (a - hi.astype(F32)).astype(BF16)


def _raw_dot(a, b, ca, cb, prec):
    dims = (((ca,), (cb,)), ((), ()))
    mm = lambda p, q: lax.dot_general(p, q, dims, preferred_element_type=F32)
    if prec == "bf16":
        return mm(a.astype(BF16), b.astype(BF16))
    if prec == "x3":
        (ah, al), (bh, bl) = _split2(a), _split2(b)
        return mm(ah, bh) + (mm(ah, bl) + mm(al, bh))
    if prec == "mask":
        ab = a.astype(BF16)
        b1, b2 = _split2(b)
        b3 = (b - b1.astype(F32) - b2.astype(F32)).astype(BF16)
        return mm(ab, b1) + (mm(ab, b2) + mm(ab, b3))
    if prec == "mb2":
        (ah, al), bb = _split2(a), b.astype(BF16)
        return mm(ah, bb) + mm(al, bb)
    return lax.dot_general(a, b, dims, precision=HI, preferred_element_type=F32)


@functools.partial(jax.custom_vjp, nondiff_argnums=(2, 3, 4))
def _bf16_dot(a, b, ca, cb, prec):
    return _raw_dot(a, b, ca, cb, prec)


def _bf16_dot_fwd(a, b, ca, cb, prec):
    return _raw_dot(a, b, ca, cb, prec), (a, b)


def _bf16_dot_bwd(ca, cb, prec, res, g):
    a, b = res
    if prec == "mask":
        return jnp.zeros_like(a), _bf16_dot(a, g, 1 - ca, 0, prec)
    if prec == "mb2":
        return _bf16_dot(g, b, 1, 1, prec), jnp.zeros_like(b)
    if ca == 1:
        da = _bf16_dot(g, b, 1, 1 - cb, prec)
    else:
        da = _bf16_dot(b, g, 1 - cb, 1, prec)
    if cb == 0:
        db = _bf16_dot(a, g, 1 - ca, 0, prec)
    else:
        db = _bf16_dot(g, a, 0, 1 - ca, prec)
    return da, db


_bf16_dot.defvjp(_bf16_dot_fwd, _bf16_dot_bwd)


def _dot(a, b, ca, cb, prec):
    return _raw_dot(a, b, ca, cb, prec) if prec == "hi" else _bf16_dot(a, b, ca, cb, prec)


def _sigmoid(z):
    return 1.0 / (1.0 + jnp.exp(-z))


def _softplus(z):
    return jnp.maximum(z, 0.0) + jnp.log(1.0 + jnp.exp(-jnp.abs(z)))


MATMUL_VMEM_BUDGET = 51 * 1024 * 1024


def _matmul_tiles(m, n, k, in_bytes, out_dtypes, n_extras):
    tm, tn = _tile(m, 1024), _tile(n, 1024)
    out_bytes = sum(jnp.dtype(dt).itemsize for dt in out_dtypes)
    for cap in (4096, 2048, 1024, 512, 256, 128):
        tk = _tile(k, cap)
        acc = 4 * tm * tn if tk < k else 0
        need = 2 * in_bytes * (tm + tn) * tk + 2 * tm * tn * (out_bytes + 4 * n_extras) + acc + 4 * tm * tn
        if need <= MATMUL_VMEM_BUDGET:
            break
    return tm, tn, tk


def _matmul(name, a, b, mode, out_dtypes, epilogue=None, extras=(), after=None):
    if mode == "nn":
        (m, k), n = a.shape, b.shape[1]
    elif mode == "nt":
        (m, k), n = a.shape, b.shape[0]
    else:
        (k, m), n = a.shape, b.shape[1]
    tm, tn, tk = _matmul_tiles(m, n, k, a.dtype.itemsize, out_dtypes, len(extras))
    nk = k // tk
    ne, no = len(extras), len(out_dtypes)
    ca, cb = {"nn": (1, 0), "nt": (1, 1), "tn": (0, 0)}[mode]

    n_after = 0 if after is None else 1

    def body(a_ref, b_ref, *rest):
        rest = rest[n_after:]
        extra_refs, out_refs = rest[:ne], rest[ne:ne + no]
        acc = rest[-1] if nk > 1 else None
        kk = pl.program_id(2)
        part = _raw_dot(a_ref[...], b_ref[...], ca, cb, "bf16")

        def finish(total):
            res = epilogue(total, *[e[...] for e in extra_refs]) if epilogue else (total,)
            for o, v in zip(out_refs, res):
                o[...] = v.astype(o.dtype)

        if nk == 1:
            finish(part)
            return

        @pl.when(kk == 0)
        def _():
            acc[...] = part

        @pl.when((kk > 0) & (kk < nk - 1))
        def _():
            acc[...] += part

        @pl.when(kk == nk - 1)
        def _():
            finish(acc[...] + part)

    a_spec = (pl.BlockSpec((tk, tm), lambda i, j, kk: (kk, i)) if mode == "tn"
              else pl.BlockSpec((tm, tk), lambda i, j, kk: (i, kk)))
    b_spec = (pl.BlockSpec((tn, tk), lambda i, j, kk: (j, kk)) if mode == "nt"
              else pl.BlockSpec((tk, tn), lambda i, j, kk: (kk, j)))
    mn_spec = pl.BlockSpec((tm, tn), lambda i, j, kk: (i, j))
    after_specs = [pl.BlockSpec(memory_space=pl.ANY)] * n_after
    outs = pl.pallas_call(
        body, name=name, grid=(m // tm, n // tn, nk),
        in_specs=[a_spec, b_spec] + after_specs + [mn_spec] * ne,
        out_specs=[mn_spec] * no,
        out_shape=[jax.ShapeDtypeStruct((m, n), dt) for dt in out_dtypes],
        scratch_shapes=[pltpu.VMEM((tm, tn), F32)] if nk > 1 else [],
        compiler_params=_cparams(("parallel", "parallel", "arbitrary")),
    )(a, b, *([after] * n_after), *extras)
    return outs


def _rowwise(name, fn, rows, bcasts, out_rows, out_accs, tr, halos=()):
    t = rows[0].shape[0]
    nb = t // tr
    n_in = len(rows) + len(halos) + len(bcasts)
    n_ro = len(out_rows)

    def body(*refs):
        ins = [r[...] for r in refs[:n_in]]
        o_refs = refs[n_in:]
        i = pl.program_id(0)
        routs, aouts = fn(i, nb, *ins)
        for ref, v in zip(o_refs[:n_ro], routs):
            ref[...] = v.astype(ref.dtype)
        for ref, v in zip(o_refs[n_ro:], aouts):
            @pl.when(i == 0)
            def _(ref=ref):
                ref[...] = jnp.zeros_like(ref)
            ref[...] += v.reshape(ref.shape)

    in_specs = [pl.BlockSpec((tr, r.shape[1]), lambda i: (i, 0)) for r in rows]
    for arr, which in halos:
        if which == "prev":
            in_specs.append(pl.BlockSpec((8, arr.shape[1]), lambda i: (jnp.maximum(i * (tr // 8) - 1, 0), 0)))
        else:
            in_specs.append(pl.BlockSpec((8, arr.shape[1]),
                                         lambda i: (jnp.minimum((i + 1) * (tr // 8), t // 8 - 1), 0)))
    for bc in bcasts:
        in_specs.append(pl.BlockSpec(bc.shape, lambda i, nd=bc.ndim: (0,) * nd))
    out_specs = [pl.BlockSpec((tr, c), lambda i: (i, 0)) for c, _ in out_rows]
    out_specs += [pl.BlockSpec(s, lambda i, nd=len(s): (0,) * nd) for s in out_accs]
    out_shape = [jax.ShapeDtypeStruct((t, c), dt) for c, dt in out_rows]
    out_shape += [jax.ShapeDtypeStruct(s, F32) for s in out_accs]
    return pl.pallas_call(
        body, name=name, grid=(nb,), in_specs=in_specs, out_specs=out_specs, out_shape=out_shape,
        compiler_params=_cparams(("arbitrary",)),
    )(*rows, *[h[0] for h in halos], *bcasts)


def _shift_prev(x, halo, i):
    rolled = pltpu.roll(x, 1, 0)
    first = jnp.where(i == 0, 0.0, halo[7:8, :])
    row = lax.broadcasted_iota(jnp.int32, x.shape, 0)
    return jnp.where(row == 0, first, rolled)


def _shift_next(x, halo, i, nb):
    rolled = pltpu.roll(x, x.shape[0] - 1, 0)
    last = jnp.where(i == nb - 1, 0.0, halo[0:1, :])
    row = lax.broadcasted_iota(jnp.int32, x.shape, 0)
    return jnp.where(row == x.shape[0] - 1, last, rolled)


def _ln(x, g, b, eps=LN_EPS):
    mu = jnp.mean(x, axis=-1, keepdims=True)
    xc = x - mu
    var = jnp.mean(xc * xc, axis=-1, keepdims=True)
    return xc * lax.rsqrt(var + eps) * g + b


def _embed_math(x, g, b, sc, sh):
    x0 = _ln(x, g, b)
    return x0, x0 * (1.0 + sc) + sh


def _post_math(xin, y, gate, g, b, sc, sh):
    x1 = _ln(ALPHA * xin + (1.0 + gate) * y, g, b)
    return x1, x1 * (1.0 + sc) + sh


def _loss_math(xin, h, tgt, gate, g, b):
    x2 = _ln(ALPHA * xin + (1.0 + gate) * h, g, b)
    err = x2 - tgt
    return 0.5 * jnp.sum(jnp.mean(err * err, axis=-1))


def _rwkv_pre_math(r, k, v, xw, xa, xg, w0, wd, a0, wa, wg, k_k, k_a, seg, seg_t, dot=_dot):
    wpre = -_softplus(-(w0 + dot(jnp.tanh(xw), wd, 1, 0, "x3"))) - 0.5
    lw = -jnp.exp(wpre)
    a = _sigmoid(a0 + dot(xa, wa, 1, 0, "x3"))
    g = dot(_sigmoid(xg), wg, 1, 0, "x3")
    kk = k * k_k
    norm = jnp.sqrt(dot(kk * kk, seg, 1, 0, "mb2"))
    kkn = kk * dot(1.0 / jnp.maximum(norm, 1e-12), seg_t, 1, 0, "mb2")
    k2 = k * (1.0 + (a - 1.0) * k_a)
    return r, lw, k2, v, kkn, kkn * a, g


def _rwkv_post_math(y, r, k2, v, g, lnx_g, lnx_b, r_k, seg, seg_t, dot=_dot):
    inv = 1.0 / HEAD
    spread = lambda z: dot(dot(z, seg, 1, 0, "mb2"), seg_t, 1, 0, "mb2")
    mu = spread(y) * inv
    yc = y - mu
    var = spread(yc * yc) * inv
    yn = yc * lax.rsqrt(var + LNX_EPS) * lnx_g + lnx_b
    bonus = spread(r * k2 * r_k) * v
    return (yn + bonus) * g


@jax.custom_vjp
def _known_inverse(low, inv):
    return inv


def _known_inverse_fwd(low, inv):
    return inv, inv


def _known_inverse_bwd(inv, g):
    left = [_raw_dot(t, gi, 0, 0, "bf16") for t, gi in zip(inv, g)]
    return [_raw_dot(x, t, 1, 1, "bf16") for x, t in zip(left, inv)], [jnp.zeros_like(t) for t in inv]


_known_inverse.defvjp(_known_inverse_fwd, _known_inverse_bwd)


def _chunk_math(s0, r, lw, k, v, kk, b, known_inv=None, dot=_dot):
    n = len(r)
    hs = range(n)
    c = r[0].shape[0]
    ti = lax.broadcasted_iota(jnp.int32, (2 * c, 2 * c), 0)
    tj = lax.broadcasted_iota(jnp.int32, (2 * c, 2 * c), 1)
    tt, jj = ti & (c - 1), tj & (c - 1)
    quad = jnp.where(ti < c, (tt > jj).astype(F32), (tt >= jj).astype(F32))
    incl = quad[c:, :c]
    eye = (ti[:c, :c] == tj[:c, :c]).astype(F32)
    cl = [dot(incl, lw[i], 1, 0, "mask") for i in hs]
    ge = [jnp.exp(cl[i]) for i in hs]
    gi = [jnp.exp(-cl[i]) for i in hs]
    ar = [jnp.concatenate([-kk[i] * jnp.exp(cl[i] - lw[i]), r[i] * ge[i]], axis=0) for i in hs]
    kb = [jnp.concatenate([k[i] * gi[i], b[i] * gi[i]], axis=0) for i in hs]
    m = [dot(ar[i], kb[i], 1, 1, "x3") * quad for i in hs]
    ars0 = [dot(ar[i], s0[i], 1, 1, "bf16") for i in hs]
    mv = [dot(m[i][:c, :c], v[i], 1, 0, "bf16") for i in hs]
    pw = [m[i][:c, c:] for i in hs]
    if known_inv is None:
        inv = [eye + pw[i] for i in hs]
        for _ in range(int(math.log2(c)) - 1):
            pw = [dot(pw[i], pw[i], 1, 0, "bf16") for i in hs]
            inv = [inv[i] + dot(inv[i], pw[i], 1, 0, "bf16") for i in hs]
    else:
        inv = _known_inverse(pw, known_inv)
    u = [dot(inv[i], ars0[i][:c] + mv[i], 1, 0, "bf16") for i in hs]
    vu = [jnp.concatenate([v[i], u[i]], axis=0) for i in hs]
    y = [ars0[i][c:] + dot(m[i][c:], vu[i], 1, 0, "bf16") for i in hs]
    s1 = [(s0[i] + dot(vu[i], kb[i], 0, 0, "x3")) * ge[i][c - 1:c, :] for i in hs]
    return y, s1, inv


def _attn_math(q, kp, kc, vp, vc, bias, sinks, first, dot=_dot):
    hq = q.shape[1] // HEAD
    hkv = kc.shape[1] // HEAD
    group = hq // hkv
    cols = group * ATT_BLOCK
    kj = lax.broadcasted_iota(jnp.int32, (2 * ATT_BLOCK, cols), 0)
    qi = lax.broadcasted_iota(jnp.int32, (2 * ATT_BLOCK, cols), 1) & (ATT_BLOCK - 1)
    dist = qi + ATT_BLOCK - kj
    valid = (dist >= 0) & (dist < ATT_BLOCK) & (jnp.logical_not(first) | (kj >= ATT_BLOCK))
    eye = (lax.broadcasted_iota(jnp.int32, (ATT_BLOCK, ATT_BLOCK), 0)
           == lax.broadcasted_iota(jnp.int32, (ATT_BLOCK, ATT_BLOCK), 1)).astype(F32)
    outs = []
    for j in range(hkv):
        heads = range(j * group, (j + 1) * group)
        kband = jnp.concatenate([kp[:, j * HEAD:(j + 1) * HEAD], kc[:, j * HEAD:(j + 1) * HEAD]], axis=0)
        vband = jnp.concatenate([vp[:, j * HEAD:(j + 1) * HEAD], vc[:, j * HEAD:(j + 1) * HEAD]], axis=0)
        qg = jnp.concatenate([q[:, h * HEAD:(h + 1) * HEAD] for h in heads], axis=0)
        bias_g = jnp.concatenate([bias[h] for h in heads], axis=1)
        sink = jnp.concatenate([jnp.broadcast_to(sinks[0:1, h:h + 1], (1, ATT_BLOCK)) for h in heads], axis=1)
        s = dot(kband, qg, 1, 1, "bf16") * (HEAD ** -0.5) + bias_g
        s = jnp.where(valid, s, -1e30)
        m = jnp.maximum(jnp.max(s, axis=0, keepdims=True), sink)
        e = jnp.exp(s - m)
        p = e / (jnp.sum(e, axis=0, keepdims=True) + jnp.exp(sink - m))
        o_t = dot(vband, p, 0, 0, "bf16")
        outs += [dot(eye, o_t[:, g * ATT_BLOCK:(g + 1) * ATT_BLOCK], 1, 1, "bf16") for g in range(group)]
    return jnp.concatenate(outs, axis=1)


def _rec_specs(t, da, gh, reverse):
    nc = t // CHUNK
    if reverse:
        return pl.BlockSpec((CHUNK, gh * HEAD), lambda hg, c: (nc - 1 - c, hg))
    return pl.BlockSpec((CHUNK, gh * HEAD), lambda hg, c: (c, hg))


def _rec_fwd(r, lw, k, v, kk, b):
    t, da = r.shape
    h = da // HEAD
    gh = min(REC_HEADS, h)
    nc = t // CHUNK

    def body(r_ref, lw_ref, k_ref, v_ref, kk_ref, b_ref, y_ref, s0_ref, inv_ref, state):
        @pl.when(pl.program_id(1) == 0)
        def _():
            state[...] = jnp.zeros_like(state)

        sls = [slice(i * HEAD, (i + 1) * HEAD) for i in range(gh)]
        heads = lambda ref: [ref[:, sl] for sl in sls]
        s0 = [state[i] for i in range(gh)]
        y, s1, inv = _chunk_math(s0, heads(r_ref), heads(lw_ref), heads(k_ref), heads(v_ref), heads(kk_ref),
                                 heads(b_ref), dot=_raw_dot)
        for i, sl in enumerate(sls):
            s0_ref[0, i] = s0[i]
            inv_ref[0, i] = inv[i]
            y_ref[:, sl] = y[i]
            state[i] = s1[i]

    spec = _rec_specs(t, da, gh, False)
    per_chunk = pl.BlockSpec((1, gh, HEAD, HEAD), lambda hg, c: (c, hg, 0, 0))
    return pl.pallas_call(
        body, name="rwkv_recurrence_fwd", grid=(h // gh, nc),
        in_specs=[spec] * 6,
        out_specs=[spec, per_chunk, per_chunk],
        out_shape=[jax.ShapeDtypeStruct((t, da), F32)] + [jax.ShapeDtypeStruct((nc, h, HEAD, HEAD), F32)] * 2,
        scratch_shapes=[pltpu.VMEM((gh, HEAD, HEAD), F32)],
        compiler_params=_cparams(("parallel", "arbitrary")),
    )(r, lw, k, v, kk, b)


def _rec_bwd(r, lw, k, v, kk, b, s0s, invs, dy):
    t, da = r.shape
    h = da // HEAD
    gh = min(REC_HEADS, h)
    nc = t // CHUNK

    def body(r_ref, lw_ref, k_ref, v_ref, kk_ref, b_ref, dy_ref, s0_ref, inv_ref,
             dr_ref, dlw_ref, dk_ref, dv_ref, dkk_ref, db_ref, dstate):
        @pl.when(pl.program_id(1) == 0)
        def _():
            dstate[...] = jnp.zeros_like(dstate)

        sls = [slice(i * HEAD, (i + 1) * HEAD) for i in range(gh)]
        heads = lambda ref: [ref[:, sl] for sl in sls]
        known = [inv_ref[0, i] for i in range(gh)]
        fn = lambda *args: _chunk_math(*args, known_inv=known)[:2]
        _, vjp = jax.vjp(fn, [s0_ref[0, i] for i in range(gh)], heads(r_ref), heads(lw_ref),
                         heads(k_ref), heads(v_ref), heads(kk_ref), heads(b_ref))
        grads = vjp((heads(dy_ref), [dstate[i] for i in range(gh)]))
        for i, sl in enumerate(sls):
            dstate[i] = grads[0][i]
            for ref, val in zip((dr_ref, dlw_ref, dk_ref, dv_ref, dkk_ref, db_ref), grads[1:]):
                ref[:, sl] = val[i]

    spec = _rec_specs(t, da, gh, True)
    return pl.pallas_call(
        body, name="rwkv_recurrence_bwd", grid=(h // gh, nc),
        in_specs=[spec] * 7 + [pl.BlockSpec((1, gh, HEAD, HEAD), lambda hg, c: (nc - 1 - c, hg, 0, 0))] * 2,
        out_specs=[spec] * 6,
        out_shape=[jax.ShapeDtypeStruct((t, da), F32)] * 6,
        scratch_shapes=[pltpu.VMEM((gh, HEAD, HEAD), F32)],
        compiler_params=_cparams(("parallel", "arbitrary")),
    )(r, lw, k, v, kk, b, dy, s0s, invs)


def _attn_specs(t, hq_w, hkv_w):
    nb = t // ATT_BLOCK
    cur = lambda w: pl.BlockSpec((ATT_BLOCK, w), lambda n: (n, 0))
    prev = lambda w: pl.BlockSpec((ATT_BLOCK, w), lambda n: (jnp.maximum(n - 1, 0), 0))
    return nb, cur, prev


def _attn_fwd(q, kb, vb, bias, sinks):
    t, qw = q.shape
    kw = kb.shape[1]
    nb, cur, prev = _attn_specs(t, qw, kw)

    def body(q_ref, kp_ref, kc_ref, vp_ref, vc_ref, bias_ref, sink_ref, o_ref):
        first = pl.program_id(0) == 0
        o = _attn_math(q_ref[...], kp_ref[...], kc_ref[...], vp_ref[...], vc_ref[...],
                       bias_ref[...], sink_ref[...], first, dot=_raw_dot)
        o_ref[...] = o.astype(o_ref.dtype)

    full = lambda a: pl.BlockSpec(a.shape, lambda n, nd=a.ndim: (0,) * nd)
    return pl.pallas_call(
        body, name="swa_attention_fwd", grid=(nb,),
        in_specs=[cur(qw), prev(kw), cur(kw), prev(kw), cur(kw), full(bias), full(sinks)],
        out_specs=cur(qw), out_shape=jax.ShapeDtypeStruct((t, qw), BF16),
        compiler_params=_cparams(("parallel",)),
    )(q, kb, kb, vb, vb, bias, sinks)


def _attn_bwd(q, kb, vb, bias, sinks, do, col_block):
    t, qw = q.shape
    kw = kb.shape[1]
    nb, cur, prev = _attn_specs(t, qw, kw)

    def body(q_ref, kp_ref, kc_ref, vp_ref, vc_ref, bias_ref, sink_ref, do_ref,
             dq_ref, dkp_ref, dkc_ref, dvp_ref, dvc_ref, dbias_ref, dsink_ref):
        n = pl.program_id(0)
        first = n == 0
        fn = functools.partial(_attn_math, first=first)
        _, vjp = jax.vjp(fn, q_ref[...], kp_ref[...], kc_ref[...], vp_ref[...], vc_ref[...],
                         bias_ref[...], sink_ref[...])
        dq, dkp, dkc, dvp, dvc, dbias, dsink = vjp(do_ref[...].astype(F32))
        dq_ref[...] = dq.astype(dq_ref.dtype)
        dkp_ref[...] = dkp
        dkc_ref[...] = dkc
        dvp_ref[...] = dvp
        dvc_ref[...] = dvc

        @pl.when(first)
        def _():
            dbias_ref[...] = jnp.zeros_like(dbias_ref)
            dsink_ref[...] = jnp.zeros_like(dsink_ref)

        dbias_ref[...] += dbias
        dsink_ref[...] += dsink

    full = lambda a: pl.BlockSpec(a.shape, lambda n, nd=a.ndim: (0,) * nd)
    kshape = jax.ShapeDtypeStruct((t, kw), F32)
    return pl.pallas_call(
        body, name="swa_attention_bwd", grid=(nb,),
        in_specs=[cur(qw), prev(kw), cur(kw), prev(kw), cur(kw), full(bias), full(sinks),
                  pl.BlockSpec((ATT_BLOCK, qw), lambda n: (n, col_block))],
        out_specs=[cur(qw), cur(kw), cur(kw), cur(kw), cur(kw), full(bias), full(sinks)],
        out_shape=[jax.ShapeDtypeStruct((t, qw), BF16), kshape, kshape, kshape, kshape,
                   jax.ShapeDtypeStruct(bias.shape, F32), jax.ShapeDtypeStruct(sinks.shape, F32)],
        compiler_params=_cparams(("arbitrary",)),
    )(q, kb, kb, vb, vb, bias, sinks, do)


def _bucket_onehot():
    qi = jnp.arange(ATT_BLOCK)[None, :]
    kj = jnp.arange(2 * ATT_BLOCK)[:, None]
    n = jnp.maximum(qi + ATT_BLOCK - kj, 0)
    buckets, max_exact = 32, 16
    nf = jnp.maximum(n, 1).astype(F32)
    large = max_exact + (jnp.log(nf / max_exact) / math.log(RPB_MAX_DIST / max_exact)
                         * (buckets - max_exact)).astype(jnp.int32)
    bucket = jnp.where(n < max_exact, n, jnp.minimum(large, buckets - 1)).reshape(-1)
    return (bucket[None, :] == jnp.arange(buckets)[:, None]).astype(F32)


def _small_dot(name, a, b, ca, cb):
    m = a.shape[1 - ca]
    n = b.shape[1 - cb]

    def body(a_ref, b_ref, o_ref):
        o_ref[...] = _raw_dot(a_ref[...], b_ref[...], ca, cb, "hi")

    return pl.pallas_call(body, name=name, out_shape=jax.ShapeDtypeStruct((m, n), F32),
                          compiler_params=_cparams())(a, b)


def _mod_fwd(c_all, w_mod):
    d, n = w_mod.shape
    tn = _tile(n, 512)

    def body(c_ref, w_ref, o_ref, cond_ref):
        cv = c_ref[...]
        cond = cv * _sigmoid(cv)
        cond_ref[...] = cond
        o_ref[...] = _raw_dot(cond, w_ref[...], 1, 0, "hi")

    return pl.pallas_call(
        body, name="adaln_mod_fwd", grid=(n // tn,),
        in_specs=[pl.BlockSpec(c_all.shape, lambda j: (0, 0)), pl.BlockSpec((d, tn), lambda j: (0, j))],
        out_specs=[pl.BlockSpec((c_all.shape[0], tn), lambda j: (0, j)),
                   pl.BlockSpec(c_all.shape, lambda j: (0, 0))],
        out_shape=[jax.ShapeDtypeStruct((c_all.shape[0], n), F32), jax.ShapeDtypeStruct(c_all.shape, F32)],
        compiler_params=_cparams(("arbitrary",)),
    )(c_all, w_mod)


def _adam_math(w, g, m, v):
    m = ADAM_B1 * m + (1.0 - ADAM_B1) * g
    v = ADAM_B2 * v + (1.0 - ADAM_B2) * (g * g)
    m_hat = m / (1.0 - ADAM_B1 ** ADAM_STEP)
    v_hat = v / (1.0 - ADAM_B2 ** ADAM_STEP)
    delta = -ADAM_LR * (m_hat / (jnp.sqrt(v_hat) + ADAM_EPS) + ADAM_WD * w)
    return delta, m, v


def _adamw(name, w, m, v, gparts):
    r, c = w.shape
    p = gparts.shape[0]
    tr = _rtile(r, max(8, (1 << 18) // max(c, 1) // 8 * 8))

    def body(w_ref, m_ref, v_ref, g_ref, go_ref, d_ref, mo_ref, vo_ref):
        g = g_ref[0].astype(F32)
        for s in range(1, p):
            g = g + g_ref[s].astype(F32)
        delta, mn, vn = _adam_math(w_ref[...], g, m_ref[...], v_ref[...])
        go_ref[...] = g
        d_ref[...] = delta
        mo_ref[...] = mn
        vo_ref[...] = vn

    spec = pl.BlockSpec((tr, c), lambda i: (i, 0))
    return pl.pallas_call(
        body, name=name, grid=(r // tr,),
        in_specs=[spec, spec, spec, pl.BlockSpec((p, tr, c), lambda i: (0, i, 0))],
        out_specs=[spec] * 4, out_shape=[jax.ShapeDtypeStruct((r, c), F32)] * 4,
        compiler_params=_cparams(("parallel",)),
    )(w, m, v, gparts)


def _adamw_outer(name, w, m, v, cond_t, dmod):
    d, n = w.shape
    tr, tn = _rtile(d, 512), _tile(n, 1024)

    def body(w_ref, m_ref, v_ref, c_ref, dm_ref, go_ref, d_ref, mo_ref, vo_ref):
        g = _raw_dot(c_ref[...], dm_ref[...], 1, 0, "hi")
        delta, mn, vn = _adam_math(w_ref[...], g, m_ref[...], v_ref[...])
        go_ref[...] = g
        d_ref[...] = delta
        mo_ref[...] = mn
        vo_ref[...] = vn

    spec = pl.BlockSpec((tr, tn), lambda i, j: (i, j))
    return pl.pallas_call(
        body, name=name, grid=(d // tr, n // tn),
        in_specs=[spec, spec, spec, pl.BlockSpec((tr, cond_t.shape[1]), lambda i, j: (i, 0)),
                  pl.BlockSpec((dmod.shape[0], tn), lambda i, j: (0, j))],
        out_specs=[spec] * 4, out_shape=[jax.ShapeDtypeStruct((d, n), F32)] * 4,
        compiler_params=_cparams(("parallel", "parallel")),
    )(w, m, v, cond_t, dmod)


def _all_gather(name, arrays):
    n = len(arrays)

    def body(*refs):
        ins, outs = refs[:n], refs[n:2 * n]
        send_sems, recv_sems, local_sems = refs[2 * n:]
        x, y, c = lax.axis_index("x"), lax.axis_index("y"), lax.axis_index("c")
        me, sibling = (x, y, c), (x, y, 1 - c)
        chips = [(1 - x, y), (x, 1 - y), (1 - x, 1 - y)]

        def copy(a, k, block, to, src=None):
            rows = outs[a].at[4 * block[0] + 2 * block[1] + block[2]]
            return pltpu.make_async_remote_copy(
                src_ref=rows if src is None else src, dst_ref=rows, send_sem=send_sems.at[a, k],
                recv_sem=recv_sems.at[a, k], device_id=to, device_id_type=pl.DeviceIdType.MESH)

        mine = [pltpu.make_async_copy(ins[a], outs[a].at[4 * x + 2 * y + c], local_sems.at[a]) for a in range(n)]
        for cp in mine:
            cp.start()
        first = []
        for a in range(n):
            first.append(copy(a, 0, me, sibling, src=ins[a]))
            first += [copy(a, 1 + j, me, (*chip, c), src=ins[a]) for j, chip in enumerate(chips)]
        for cp in first:
            cp.start()
        passed = []
        for a in range(n):
            for j, chip in enumerate(chips):
                copy(a, 1 + j, (*chip, c), me).wait_recv()
                passed.append(copy(a, 4 + j, (*chip, c), sibling))
                passed[-1].start()
        for a in range(n):
            copy(a, 0, sibling, me).wait_recv()
            for j, chip in enumerate(chips):
                copy(a, 4 + j, (*chip, 1 - c), me).wait_recv()
        for cp in first + passed:
            cp.wait_send()
        for cp in mine:
            cp.wait()

    any_spec = pl.BlockSpec(memory_space=pl.ANY)
    return pl.pallas_call(
        body, name=name, in_specs=[any_spec] * n, out_specs=[any_spec] * n,
        out_shape=[jax.ShapeDtypeStruct((N_DEV,) + a.shape, a.dtype) for a in arrays],
        scratch_shapes=[pltpu.SemaphoreType.DMA((n, N_DEV - 1)), pltpu.SemaphoreType.DMA((n, N_DEV - 1)),
                        pltpu.SemaphoreType.DMA((n,))],
    )(*arrays)


def _peer(p):
    x, y, c = lax.axis_index("x"), lax.axis_index("y"), lax.axis_index("c")
    px, py, pc = x ^ ((p >> 2) & 1), y ^ ((p >> 1) & 1), c ^ (p & 1)
    return (px, py, pc), 4 * px + 2 * py + pc


def _block(ref, d, cols, rows=None):
    if cols:
        r = slice(None) if rows is None else pl.ds(pl.multiple_of(rows[0], 8), rows[1])
        return ref.at[r, pl.ds(pl.multiple_of(d * cols, cols), cols)]
    return ref.at[d] if rows is None else ref.at[d, pl.ds(pl.multiple_of(rows[0], 8), rows[1])]


def _split_copy(src_ref, land_ref, send_sems, recv_sems, p, scatter, arriving, cols=None, halves=False):
    x, y, c = lax.axis_index("x"), lax.axis_index("y"), lax.axis_index("c")
    me = 4 * x + 2 * y + c
    dev, idx = _peer(p)
    if scatter == "chips":
        src, dst = src_ref.at[idx >> 1], land_ref.at[(idx if arriving else me) >> 1]
    elif scatter:
        src, dst = _block(src_ref, idx, cols), land_ref.at[idx if arriving else me]
    elif halves and p >= 2:
        half = src_ref.shape[0] // 2
        rows = ((c if arriving else dev[2]) * half, half)
        src = src_ref.at[pl.ds(pl.multiple_of(rows[0], 8), half)]
        dst = _block(land_ref, idx if arriving else me, cols, rows)
    else:
        src, dst = src_ref, _block(land_ref, idx if arriving else me, cols)
    return pltpu.make_async_remote_copy(
        src_ref=src, dst_ref=dst, send_sem=send_sems.at[p - 1], recv_sem=recv_sems.at[p - 1], device_id=dev,
        device_id_type=pl.DeviceIdType.MESH)


_HBM_SPEC = pl.BlockSpec(memory_space=pltpu.HBM)
_SEM_SPEC = pl.BlockSpec(memory_space=pltpu.SEMAPHORE)
_DATAFLOW = pltpu.SideEffectType.DATAFLOW_SIDE_EFFECTING


def _own_copy(src_ref, land_ref, send_sems, scatter, cols):
    me = 4 * lax.axis_index("x") + 2 * lax.axis_index("y") + lax.axis_index("c")
    if scatter == "chips":
        src, dst = src_ref.at[me >> 1], land_ref.at[me >> 1]
    elif scatter:
        src, dst = _block(src_ref, me, cols), land_ref.at[me]
    else:
        src, dst = src_ref, _block(land_ref, me, cols)
    return pltpu.make_async_copy(src, dst, send_sems.at[N_DEV - 1])


def _peers(scatter):
    return (2, 4, 6) if scatter == "chips" else tuple(range(1, N_DEV))


def _exchange_start(name, srcs, lands, scatter, after, cols=None, halves=False):
    n = len(srcs)
    cols = cols or [None] * n

    def body(*refs):
        src_refs, land_refs = refs[:n], refs[n:2 * n]
        outs = refs[2 * n + 1:]
        send, recv, token = outs[:n], outs[n:2 * n], outs[-1]
        for a in range(n):
            for p in _peers(scatter):
                _split_copy(src_refs[a], land_refs[a], send[a], recv[a], p, scatter, False, cols[a],
                            halves).start()
            _own_copy(src_refs[a], land_refs[a], send[a], scatter, cols[a]).start()
        token[...] = jnp.zeros_like(token)

    sems = [pltpu.SemaphoreType.DMA((N_DEV,))] * n + [pltpu.SemaphoreType.DMA((N_DEV - 1,))] * n
    hbm = [pltpu.HBM(a.shape, a.dtype) for a in list(srcs) + list(lands)]
    res = pl.pallas_call(
        body, name=name,
        out_shape=sems + hbm + [jax.ShapeDtypeStruct((8, 128), F32)],
        in_specs=[_HBM_SPEC] * (2 * n) + [pl.BlockSpec(memory_space=pl.ANY)],
        out_specs=[_SEM_SPEC] * (2 * n) + [_HBM_SPEC] * (2 * n) + [pl.BlockSpec(memory_space=pltpu.VMEM)],
        input_output_aliases={i: 2 * n + i for i in range(2 * n)},
        compiler_params=pltpu.CompilerParams(has_side_effects=_DATAFLOW),
    )(*[pltpu.with_memory_space_constraint(a, pltpu.HBM) for a in list(srcs) + list(lands)], after)
    return res[:n], res[n:2 * n], res[2 * n:3 * n], res[3 * n:4 * n], res[-1]


def _exchange_wait(name, send_sem, recv_sem, src, land, scatter, after, cols=None, halves=False):
    def body(src_ref, land_ref, send, recv, after_ref, src_out, land_out):
        for p in _peers(scatter):
            cp = _split_copy(src_ref, land_ref, send, recv, p, scatter, True, cols, halves)
            cp.wait_send()
            cp.wait_recv()
        _own_copy(src_ref, land_ref, send, scatter, cols).wait()

    return pl.pallas_call(
        body, name=name,
        out_shape=(pltpu.HBM(src.shape, src.dtype), pltpu.HBM(land.shape, land.dtype)),
        in_specs=[_HBM_SPEC, _HBM_SPEC, _SEM_SPEC, _SEM_SPEC, pl.BlockSpec(memory_space=pl.ANY)],
        out_specs=(_HBM_SPEC, _HBM_SPEC), input_output_aliases={0: 0, 1: 1},
        compiler_params=pltpu.CompilerParams(has_side_effects=_DATAFLOW),
    )(src, land, send_sem, recv_sem, after)[1]


def _sibling_fill(name, land, cols):
    rows = land.shape[0] if cols else land.shape[1]
    half = rows // 2

    def body(in_ref, out_ref, send_sems, recv_sems):
        x, y, c = lax.axis_index("x"), lax.axis_index("y"), lax.axis_index("c")

        def copy(p, core):
            _, idx = _peer(p)
            return pltpu.make_async_remote_copy(
                src_ref=_block(in_ref, idx, cols, (core * half, half)),
                dst_ref=_block(out_ref, idx, cols, (core * half, half)),
                send_sem=send_sems.at[idx], recv_sem=recv_sems.at[idx], device_id=(x, y, 1 - c),
                device_id_type=pl.DeviceIdType.MESH)

        sends = [copy(p, c) for p in range(2, N_DEV)]
        for cp in sends:
            cp.start()
        for p in range(2, N_DEV):
            copy(p, 1 - c).wait_recv()
        for cp in sends:
            cp.wait_send()

    any_spec = pl.BlockSpec(memory_space=pl.ANY)
    return pl.pallas_call(
        body, name=name, in_specs=[any_spec], out_specs=any_spec,
        out_shape=jax.ShapeDtypeStruct(land.shape, land.dtype), input_output_aliases={0: 0},
        scratch_shapes=[pltpu.SemaphoreType.DMA((N_DEV,)), pltpu.SemaphoreType.DMA((N_DEV,))],
    )(land)


def _pair_exchange(name, arrays):
    n = len(arrays)

    def body(*refs):
        ins, outs = refs[:n], refs[n:2 * n]
        send_sems, recv_sems = refs[2 * n:]
        x, y, c = lax.axis_index("x"), lax.axis_index("y"), lax.axis_index("c")
        copies = [pltpu.make_async_remote_copy(
            src_ref=ins[a].at[2 * k + (1 - c)], dst_ref=outs[a].at[k], send_sem=send_sems.at[a, k],
            recv_sem=recv_sems.at[a, k], device_id=(x, y, 1 - c), device_id_type=pl.DeviceIdType.MESH)
            for a in range(n) for k in range(N_DEV // 2)]
        for cp in copies:
            cp.start()
        for cp in copies:
            cp.wait()

    any_spec = pl.BlockSpec(memory_space=pl.ANY)
    return pl.pallas_call(
        body, name=name, in_specs=[any_spec] * n, out_specs=[any_spec] * n,
        out_shape=[jax.ShapeDtypeStruct((N_DEV // 2,) + a.shape[1:], a.dtype) for a in arrays],
        scratch_shapes=[pltpu.SemaphoreType.DMA((n, N_DEV // 2)), pltpu.SemaphoreType.DMA((n, N_DEV // 2))],
    )(*arrays)


def _pair_sum(name, mine, theirs):
    _, r, c_ = mine.shape
    tr = _rtile(r, max(8, (1 << 19) // c_ // 8 * 8))

    def body(mine_ref, theirs_ref, o_ref):
        core = lax.axis_index("c")
        o_ref[0] = (mine_ref[0, core].astype(F32) + theirs_ref[0].astype(F32)).astype(o_ref.dtype)

    return pl.pallas_call(
        body, name=name, grid=(N_DEV // 2, r // tr),
        in_specs=[pl.BlockSpec((1, 2, tr, c_), lambda k, i: (k, 0, i, 0)),
                  pl.BlockSpec((1, tr, c_), lambda k, i: (k, i, 0))],
        out_specs=pl.BlockSpec((1, tr, c_), lambda k, i: (k, i, 0)),
        out_shape=jax.ShapeDtypeStruct((N_DEV // 2, r, c_), mine.dtype),
        compiler_params=_cparams(("parallel", "parallel")),
    )(mine.reshape(N_DEV // 2, 2, r, c_), theirs)


def _cols_to_shards(a):
    r, c = a.shape
    return a.reshape(r, N_DEV, c // N_DEV).transpose(1, 0, 2)


def _shards_to_cols(a):
    d, r, n = a.shape
    return a.transpose(1, 0, 2).reshape(r, d * n)


def kernel(x, c, ln_emb_g, ln_emb_b, rpb_table, w_mod, b_mod, w_in, mu_shift, w0, w_decay_up, a0, w_iclr_up, w_gate_up, k_k, k_a, r_k, lnx_g, lnx_b, attn_sinks, w_out, ln1_g, ln1_b, w_up, w_down, ln2_g, ln2_b, loss_target, m_ln_emb_g, m_ln_emb_b, m_rpb_table, m_w_mod, m_b_mod, m_w_in, m_mu_shift, m_w0, m_w_decay_up, m_a0, m_w_iclr_up, m_w_gate_up, m_k_k, m_k_a, m_r_k, m_lnx_g, m_lnx_b, m_attn_sinks, m_w_out, m_ln1_g, m_ln1_b, m_w_up, m_w_down, m_ln2_g, m_ln2_b, v_ln_emb_g, v_ln_emb_b, v_rpb_table, v_w_mod, v_b_mod, v_w_in, v_mu_shift, v_w0, v_w_decay_up, v_a0, v_w_iclr_up, v_w_gate_up, v_k_k, v_k_a, v_r_k, v_lnx_g, v_lnx_b, v_attn_sinks, v_w_out, v_ln1_g, v_ln1_b, v_w_up, v_w_down, v_ln2_g, v_ln2_b):
    names = ["ln_emb_g", "ln_emb_b", "rpb_table", "w_mod", "b_mod", "w_in", "mu_shift", "w0", "w_decay_up",
             "a0", "w_iclr_up", "w_gate_up", "k_k", "k_a", "r_k", "lnx_g", "lnx_b", "attn_sinks", "w_out",
             "ln1_g", "ln1_b", "w_up", "w_down", "ln2_g", "ln2_b"]
    env = dict(locals())
    weights = {nm: env[nm] for nm in names}
    mom_m = {nm: env["m_" + nm] for nm in names}
    mom_v = {nm: env["v_" + nm] for nm in names}

    t, d = x.shape[1], x.shape[2]
    da = d // 2
    h_a = da // HEAD
    hq = (d - da) // HEAD
    hkv = hq // GQA_RATIO
    l_w, l_a, l_g = w_decay_up.shape[1], w_iclr_up.shape[1], w_gate_up.shape[1]
    o_w, o_a, o_g = 3 * da, 3 * da + l_w, 3 * da + l_w + l_a
    n_rwkv = o_g + l_g
    o_kb, o_vb = n_rwkv + hq * HEAD, n_rwkv + hq * HEAD + hkv * HEAD
    me = 4 * lax.axis_index("x") + 2 * lax.axis_index("y") + lax.axis_index("c")

    x2d, tgt = x[0], loss_target[0]
    row = lambda a: a.reshape(1, -1)
    seg = (jnp.arange(da)[:, None] // HEAD == jnp.arange(h_a)[None, :]).astype(F32)
    seg_t = seg.T

    (c_all,) = _all_gather("gather_cond", [c])
    c_all = c_all.reshape(N_DEV, d)
    mod_rows, cond_all = _mod_fwd(c_all, w_mod[0])
    gathered = _all_gather("gather_weights", [mod_rows, w_decay_up[0], w_iclr_up[0], w_gate_up[0]])
    mod_all, wd_g, wa_g, wg_g = gathered
    win_src = [w_in[0].astype(BF16).T]
    win_send, win_recv, win_src, win_land, win_token = _exchange_start(
        "gather_w_in_start", win_src, [lax.empty((N_DEV,) + win_src[0].shape, BF16)], False, mod_all, halves=True)
    late = [(w[0] + win_token[0, 0]).astype(BF16) for w in (w_out, w_up, w_down)]
    n_up = w_up.shape[2]
    late_cols = [None, n_up, None]
    late_lands = [lax.empty((N_DEV,) + late[0].shape, BF16), lax.empty((d, N_DEV * n_up), BF16),
                  lax.empty((N_DEV,) + late[2].shape, BF16)]
    late_send, late_recv, late_src, late_land, late_token = _exchange_start(
        "gather_late_weights_start", late, late_lands, False, mod_all, late_cols, halves=True)
    mod = lax.dynamic_index_in_dim(mod_all, me, axis=1, keepdims=False).reshape(1, -1) + b_mod
    mod = mod + late_token[0, 0]
    sh1, sc1, g1, sh2, sc2, g2 = [mod[:, i * d:(i + 1) * d] for i in range(6)]
    wd_f, wa_f, wg_f = _shards_to_cols(wd_g), _shards_to_cols(wa_g), _shards_to_cols(wg_g)

    def late_weight(i, nm, after):
        land = _exchange_wait("gather_" + nm + "_wait", late_send[i], late_recv[i], late_src[i], late_land[i],
                              False, after, late_cols[i], halves=True)
        return _sibling_fill("gather_" + nm + "_fill", land, late_cols[i])

    tr = _rtile(t, 256)
    lng, lnb = row(ln_emb_g), row(ln_emb_b)

    def embed_fn(i, nb, xb, g, b, sc, sh):
        return _embed_math(xb, g, b, sc, sh), ()
    x0, u1 = _rowwise("embed_ln_mod", embed_fn, [x2d], [lng, lnb, sc1, sh1], [(d, F32), (d, BF16)], [], tr)
    win_g = _exchange_wait("gather_w_in_wait", win_send[0], win_recv[0], win_src[0], win_land[0], False, u1,
                           halves=True)
    w_in_t = _sibling_fill("gather_w_in_fill", win_g, None).reshape(-1, d)

    (p_rkv,) = _matmul("in_proj_rkv", u1, w_in_t[:o_w], "nt", [F32])
    (p_lora,) = _matmul("in_proj_lora", u1, w_in_t[o_w:n_rwkv], "nt", [F32])
    (q,) = _matmul("in_proj_q", u1, w_in_t[n_rwkv:o_kb], "nt", [F32])
    (p_kv,) = _matmul("in_proj_kv", u1, w_in_t[o_kb:], "nt", [F32])
    p_w, p_a, p_g = p_lora[:, :l_w], p_lora[:, l_w:l_w + l_a], p_lora[:, l_w + l_a:]
    kb, vb = p_kv[:, :hkv * HEAD], p_kv[:, hkv * HEAD:]
    mu_rkv, mu_w, mu_a, mu_g = (mu_shift[:, :o_w], mu_shift[:, o_w:o_a], mu_shift[:, o_a:o_g],
                                mu_shift[:, o_g:n_rwkv])
    pre_params = [w0, wd_f, a0, wa_f, wg_f, k_k, k_a, seg, seg_t]
    tr_pre = _rtile(t, 128)

    def shifted(i, blocks, halos, mus):
        return [xb + (_shift_prev(xb, hb, i) - xb) * mb for xb, hb, mb in zip(blocks, halos, mus)]

    def split3(a):
        return a[:, :da], a[:, da:2 * da], a[:, 2 * da:]

    def pre_fn(i, nb, b_rkv, b_w, b_a, b_g, h_rkv, h_w, h_a_, h_g, m_rkv, m_w, m_a, m_g, *params):
        s_rkv, s_w, s_a, s_g = shifted(i, [b_rkv, b_w, b_a, b_g], [h_rkv, h_w, h_a_, h_g],
                                       [m_rkv, m_w, m_a, m_g])
        return _rwkv_pre_math(*split3(s_rkv), s_w, s_a, s_g, *params, dot=_raw_dot), ()

    pre_rows = [p_rkv, p_w, p_a, p_g]
    pre_halos = [(a, "prev") for a in pre_rows]
    r_, lw_, k2_, v_, kk_, b_, gate_ = _rowwise(
        "rwkv_pre", pre_fn, pre_rows, [mu_rkv, mu_w, mu_a, mu_g] + pre_params,
        [(da, F32)] * 7, [], tr_pre, halos=pre_halos)

    y_rec, s0s, invs = _rec_fwd(r_, lw_, k2_, v_, kk_, b_)

    rk_flat = r_k.reshape(1, da)
    post_params = [lnx_g, lnx_b, rk_flat, seg, seg_t]

    def post_fn(i, nb, yb, rb, kb_, vb_, gb, *params):
        return (_rwkv_post_math(yb, rb, kb_, vb_, gb, *params, dot=_raw_dot),), ()
    (ya,) = _rowwise("rwkv_post", post_fn, [y_rec, r_, k2_, v_, gate_], post_params, [(da, BF16)], [], tr_pre)

    onehot = _bucket_onehot()
    bias = _small_dot("rpb_gather", rpb_table, onehot, 0, 0)
    bias = bias.reshape(hq, 2 * ATT_BLOCK, ATT_BLOCK)
    yb = _attn_fwd(q, kb, vb, bias, attn_sinks)

    mix_in = jnp.concatenate([ya, yb], axis=1)
    w_out_f = late_weight(0, "w_out", mix_in).reshape(d, d)
    (mix,) = _matmul("out_proj", mix_in, w_out_f, "nn", [F32])

    def post1_fn(i, nb, xin, yv, gate, g, b, sc, sh):
        return _post_math(xin, yv, gate, g, b, sc, sh), ()
    x1, u2 = _rowwise("ln1_mod", post1_fn, [x0, mix], [g1, ln1_g, ln1_b, sc2, sh2],
                      [(d, F32), (d, BF16)], [], tr)

    def relu2(acc):
        rl = jnp.maximum(acc, 0.0)
        return acc, rl * rl
    w_up_f = late_weight(1, "w_up", u2)
    hpre, hact = _matmul("mlp_up", u2, w_up_f, "nn", [F32, BF16], epilogue=relu2)
    w_down_f = late_weight(2, "w_down", hact).reshape(-1, d)
    (hmlp,) = _matmul("mlp_down", hact, w_down_f, "nn", [F32])

    def loss_fn(i, nb, xin, hv, tg, gate, g, b):
        val, vjp = jax.vjp(_loss_math, xin, hv, tg, gate, g, b)
        dxin, dh, _, dgate, dg, db = vjp(jnp.ones((), F32))
        return (dxin, dh), (val, dgate, dg, db)
    dx1, dh, loss_acc, dg2, dln2g, dln2b = _rowwise(
        "ln2_loss", loss_fn, [x1, hmlp, tgt], [g2, ln2_g, ln2_b], [(d, F32), (d, BF16)],
        [(1, 1), (1, d), (1, d), (1, d)], _rtile(t, 128))

    def drelu2(acc, hp):
        return (acc * 2.0 * jnp.maximum(hp, 0.0),)
    (dhpre,) = _matmul("mlp_down_dgrad", dh, w_down_f, "nt", [BF16], epilogue=drelu2, extras=[hpre])
    (gw_down,) = _matmul("mlp_down_wgrad", hact, dh, "tn", [BF16])
    (du2,) = _matmul("mlp_up_dgrad", dhpre, w_up_f, "nt", [F32])
    (gw_up,) = _matmul("mlp_up_wgrad", u2, dhpre, "tn", [BF16])

    def landing(src):
        return lax.empty(src.shape, src.dtype)
    mlp_src = [gw_down.reshape(N_DEV, -1, d), gw_up]
    mlp_cols = [None, n_up]
    mlp_send, mlp_recv, mlp_src, mlp_land, mlp_token = _exchange_start(
        "scatter_mlp_grads_start", mlp_src, [landing(mlp_src[0]), lax.empty((N_DEV, d, n_up), BF16)], True,
        gw_up, mlp_cols)

    def post1_bwd(i, nb, xin, yv, dx1v, du2v, gate, g, b, sc, sh):
        _, vjp = jax.vjp(_post_math, xin, yv, gate, g, b, sc, sh)
        dxin, dy, dgate, dg, db, dsc, dsh = vjp((dx1v, du2v))
        return (dxin, dy), (dgate, dg, db, dsc, dsh)
    dx0, dmix, dg1, dln1g, dln1b, dsc2, dsh2 = _rowwise(
        "ln1_mod_bwd", post1_bwd, [x0, mix, dx1, du2], [g1 + mlp_token[0, 0], ln1_g, ln1_b, sc2, sh2],
        [(d, F32), (d, BF16)], [(1, d)] * 5, _rtile(t, 128))

    (dmix_in,) = _matmul("out_proj_dgrad", dmix, w_out_f, "nt", [F32])
    (gw_out,) = _matmul("out_proj_wgrad", mix_in, dmix, "tn", [BF16])
    dya = dmix_in[:, :da]
    out_src = [gw_out.reshape(N_DEV, d // N_DEV, d)]
    out_send, out_recv, out_src, out_land, out_token = _exchange_start(
        "scatter_out_grad_start", out_src, [landing(a) for a in out_src], True, gw_out)
    post_params_bwd = [lnx_g, lnx_b, rk_flat + out_token[0, 0], seg, seg_t]

    def post_bwd(i, nb, yb_, rb, kb_, vb_, gb, dyab, *params):
        _, vjp = jax.vjp(_rwkv_post_math, yb_, rb, kb_, vb_, gb, *params)
        dy, dr, dk, dv, dg, dlg, dlb, drk, _, _ = vjp(dyab)
        return (dy, dr, dk, dv, dg), (dlg, dlb, drk)
    dy_rec, dr_e, dk_e, dv_e, dgate, dlnxg, dlnxb, drk = _rowwise(
        "rwkv_post_bwd", post_bwd, [y_rec, r_, k2_, v_, gate_, dya], post_params_bwd,
        [(da, F32)] * 5, [(1, da)] * 3, tr_pre)

    dr_r, dlw_r, dk_r, dv_r, dkk_r, db_r = _rec_bwd(r_, lw_, k2_, v_, kk_, b_, s0s, invs, dy_rec)

    def pre_bwd(i, nb, b_rkv, b_w, b_a, b_g, dr1, dr2, dlw, dk1, dk2, dv1, dv2, dkk, dbb, dgt,
                h_rkv, h_w, h_a_, h_g, m_rkv, m_w, m_a, m_g, *params):
        blocks = [b_rkv, b_w, b_a, b_g]
        prevs = [_shift_prev(xb, hb, i) for xb, hb in zip(blocks, [h_rkv, h_w, h_a_, h_g])]
        mus = [m_rkv, m_w, m_a, m_g]
        s_rkv, s_w, s_a, s_g = [xb + (pb - xb) * mb for xb, pb, mb in zip(blocks, prevs, mus)]
        _, vjp = jax.vjp(_rwkv_pre_math, *split3(s_rkv), s_w, s_a, s_g, *params)
        grads = vjp((dr1 + dr2, dlw, dk1 + dk2, dv1 + dv2, dkk, dbb, dgt))
        g_rkv = jnp.concatenate(grads[:3], axis=1)
        g_w, g_a, g_g = grads[3:6]
        dmu = [jnp.sum(gs * (pb - xb), axis=0, keepdims=True)
               for gs, pb, xb in zip([g_rkv, g_w, g_a, g_g], prevs, blocks)]
        dw0, dwd, da0, dwa, dwg, dkk_, dka = grads[6:13]
        return (g_rkv, g_w, g_a, g_g), (*dmu, dw0, dwd, da0, dwa, dwg, dkk_, dka)

    pre_out = _rowwise(
        "rwkv_pre_bwd", pre_bwd,
        pre_rows + [dr_r, dr_e, dlw_r, dk_r, dk_e, dv_r, dv_e, dkk_r, db_r, dgate],
        [mu_rkv, mu_w, mu_a, mu_g] + pre_params,
        [(o_w, F32), (l_w, F32), (l_a, F32), (l_g, F32)],
        [(1, o_w), (1, l_w), (1, l_a), (1, l_g), (1, da), (l_w, da), (1, da), (l_a, da), (l_g, da),
         (1, da), (1, da)],
        _rtile(t, 64), halos=pre_halos)
    gs_rkv, gs_w, gs_a, gs_g = pre_out[:4]
    dmu_parts = pre_out[4:8]
    dw0, dwd, da0, dwa, dwg, dk_k, dk_a = pre_out[8:]

    def unshift_fn(i, nb, a1, a2, a3, a4, n1, n2, n3, n4, m1, m2, m3, m4):
        outs = [gs * (1.0 - mb) + _shift_next(gs * mb, hb * mb, i, nb)
                for gs, hb, mb in zip([a1, a2, a3, a4], [n1, n2, n3, n4], [m1, m2, m3, m4])]
        return outs, ()
    gs_list = [gs_rkv, gs_w, gs_a, gs_g]
    dp_rkv, dp_w, dp_a, dp_g = _rowwise(
        "token_shift_bwd", unshift_fn, gs_list, [mu_rkv, mu_w, mu_a, mu_g],
        [(o_w, BF16), (l_w, BF16), (l_a, BF16), (l_g, BF16)], [], tr_pre,
        halos=[(a, "next") for a in gs_list])

    dq, dkp, dkc, dvp, dvc, dbias, dsinks = _attn_bwd(q, kb, vb, bias, attn_sinks, dmix_in, 1)
    zpad = jnp.zeros((ATT_BLOCK, kb.shape[1]), F32)
    dkb = (dkc + jnp.concatenate([dkp[ATT_BLOCK:], zpad], axis=0)).astype(BF16)
    dvb = (dvc + jnp.concatenate([dvp[ATT_BLOCK:], zpad], axis=0)).astype(BF16)
    d_rpb = _small_dot("rpb_scatter", onehot, dbias.reshape(hq, -1), 1, 1)

    dp = jnp.concatenate([dp_rkv, dp_w, dp_a, dp_g, dq, dkb, dvb], axis=1)
    (gw_in_t,) = _matmul("in_proj_wgrad", dp, u1, "tn", [BF16])
    in_names = ["w_in", "w_decay_up", "w_iclr_up", "w_gate_up"]
    in_parts = [gw_in_t.reshape(N_DEV, -1, d), _cols_to_shards(dwd), _cols_to_shards(dwa), _cols_to_shards(dwg)]
    in_sibling = _pair_exchange("pair_in_grads", in_parts)
    in_src = [_pair_sum("pair_sum_" + nm, a, b) for nm, a, b in zip(in_names, in_parts, in_sibling)]
    cut = in_src[0].shape[1] * 3 // 64 * 16
    second_src = [in_src[0][:, cut:]]
    in_src[0] = in_src[0][:, :cut]
    in_send, in_recv, in_src, in_land, in_token = _exchange_start(
        "scatter_in_grads_start", in_src, [landing(a) for a in in_src], "chips", in_src[0])
    (du1,) = _matmul("in_proj_dgrad", dp, w_in_t, "nn", [F32], after=in_token)

    def embed_bwd(i, nb, xb, dx0v, du1v, g, b, sc, sh):
        _, vjp = jax.vjp(_embed_math, xb, g, b, sc, sh)
        dxv, dg, db, dsc, dsh = vjp((dx0v, du1v))
        return (dxv,), (dg, db, dsc, dsh)
    grad_x, dlng, dlnb, dsc1, dsh1 = _rowwise(
        "embed_ln_mod_bwd", embed_bwd, [x2d, dx0, du1], [lng, lnb, sc1, sh1], [(d, F32)], [(1, d)] * 4,
        _rtile(t, 128))

    dmod = jnp.concatenate([dsh1, dsc1, dg1, dsh2, dsc2, dg2], axis=1)
    small = {"ln_emb_g": dlng, "ln_emb_b": dlnb, "rpb_table": d_rpb, "b_mod": dmod,
             "mu_shift": jnp.concatenate(dmu_parts, axis=1), "w0": dw0, "a0": da0, "k_k": dk_k, "k_a": dk_a,
             "r_k": drk, "lnx_g": dlnxg, "lnx_b": dlnxb, "attn_sinks": dsinks, "ln1_g": dln1g, "ln1_b": dln1b,
             "ln2_g": dln2g, "ln2_b": dln2b}
    small_names = list(small)
    packed = jnp.concatenate([small[nm].reshape(1, -1) for nm in small_names], axis=1)
    sm_send, sm_recv, sm_src, sm_land, sm_token = _exchange_start(
        "gather_small_grads_start", [packed], [lax.empty((N_DEV,) + packed.shape, F32)], False, grad_x)
    second_send, second_recv, second_src, second_land, second_token = _exchange_start(
        "scatter_in_grads_second_start", second_src, [landing(a) for a in second_src], "chips", sm_token)

    grads, deltas, new_m, new_v = {}, {}, {}, {}

    def put(nm, res):
        shape = weights[nm].shape
        grads[nm], deltas[nm], new_m[nm], new_v[nm] = [a.reshape(shape) for a in res]

    def big_update(nm, parts):
        put(nm, _adamw("adamw_" + nm, weights[nm][0], mom_m[nm][0], mom_v[nm][0], parts))

    behind = second_token
    big_update("w_down", _exchange_wait("scatter_w_down_wait", mlp_send[0], mlp_recv[0], mlp_src[0], mlp_land[0],
                                        True, behind, mlp_cols[0]))
    big_update("w_up", _exchange_wait("scatter_w_up_wait", mlp_send[1], mlp_recv[1], mlp_src[1], mlp_land[1],
                                      True, behind, mlp_cols[1]))
    big_update("w_out", _exchange_wait("scatter_w_out_wait", out_send[0], out_recv[0], out_src[0], out_land[0],
                                       True, behind))

    packed_all = _exchange_wait("gather_small_grads_wait", sm_send[0], sm_recv[0], sm_src[0], sm_land[0], False,
                                deltas["w_out"]).reshape(N_DEV, -1)
    n_mod = w_mod.shape[2]
    dmod_cols = lax.dynamic_slice_in_dim(packed_all[:, _offset(small, small_names, "b_mod"):], me * n_mod, n_mod,
                                         axis=1)
    put("w_mod", _adamw_outer("adamw_w_mod", w_mod[0], m_w_mod[0], v_w_mod[0], cond_all.T, dmod_cols))

    off = 0
    for nm in small_names:
        size = small[nm].size
        wshape = weights[nm].shape
        two_d = (1, size) if nm != "rpb_table" else wshape
        parts = packed_all[:, off:off + size].reshape((N_DEV,) + two_d)
        off += size
        put(nm, _adamw("adamw_" + nm, weights[nm].reshape(two_d), mom_m[nm].reshape(two_d),
                       mom_v[nm].reshape(two_d), parts))

    behind = deltas[small_names[-1]]
    for i, nm in enumerate(in_names):
        parts = _exchange_wait("scatter_" + nm + "_wait", in_send[i], in_recv[i], in_src[i], in_land[i],
                               "chips", behind)
        if nm == "w_in":
            second = _exchange_wait("scatter_w_in_second_wait", second_send[0], second_recv[0], second_src[0],
                                    second_land[0], "chips", behind)
            parts = jnp.swapaxes(jnp.concatenate([parts, second], axis=1), 1, 2)
        big_update(nm, parts)

    loss = lax.psum(loss_acc[0, 0], MESH_AXES)
    return (loss, grad_x[None], *[grads[nm] for nm in names], *[deltas[nm] for nm in names],
            *[new_m[nm] for nm in names], *[new_v[nm] for nm in names])


def _offset(small, small_names, name):
    off = 0
    for nm in small_names:
        if nm == name:
            return off
        off += small[nm].size
    raise KeyError(name)
```
